```python
import math
import functools
import jax
import jax.numpy as jnp
from jax import lax
import numpy as np

D_MODEL = 2048
BATCH = 8
SEQ = 2048
DEPTH = 2
DEC_BATCH = 8
DEC_SEQ = 32
PAST_LEN = 4096

CHUNK = 64
Q_BLOCK = 128
A_Q_BLOCK = 64
ROPE_THETA = 10000.0
HEAD_DIM = 128
A_HEADS = 8
A_KV_HEADS = 2
IDX_HEADS = 16
IDX_DIM = 128
A_TOPK_MAX = 256
B_HEADS = 8
C_HEADS = 8
C_QK_DIM = 64
C_V_DIM = 128
D_HEADS = 8
D_NOPE = 128
D_ROPE = 64
D_V = 128
D_Q_LORA = 512
D_KV_LORA = 256
N_EXPERTS = 64
N_GROUPS = 8
TOPK_GROUPS = 4
TOP_K = 8
MOE_FF = 512
ROUTED_SCALE = 2.5
NORM_EPS = 1e-6
DN_ALPHA = (2 * DEPTH) ** 0.25
DN_BETA = (8 * DEPTH) ** -0.25
N_EVEN = (DEPTH + 1) // 2
N_ODD = DEPTH // 2
EVEN_SIZES = (A_HEADS * HEAD_DIM, A_KV_HEADS * HEAD_DIM, A_KV_HEADS * HEAD_DIM,
              IDX_HEADS * IDX_DIM, IDX_DIM, IDX_HEADS,
              B_HEADS * HEAD_DIM, B_HEADS * HEAD_DIM, B_HEADS * HEAD_DIM, B_HEADS)
EVEN_IN = sum(EVEN_SIZES)
EVEN_OUT = A_HEADS * HEAD_DIM + B_HEADS * HEAD_DIM
ODD_SIZES = (C_HEADS * 2 * C_QK_DIM, C_HEADS * 2 * C_QK_DIM, C_HEADS * C_V_DIM,
             D_Q_LORA, D_KV_LORA, D_ROPE)
ODD_IN = sum(ODD_SIZES)
ODD_OUT = C_HEADS * C_V_DIM + D_HEADS * D_V

kernel_name = 'hybrid_streaming_encoder_step'


def split_cols(p, sizes):
    return jnp.split(p, np.cumsum(sizes)[:-1].tolist(), axis=-1)


def rms_norm(x, g):
    xf = x.astype(jnp.float32)
    y = xf * lax.rsqrt(jnp.mean(xf * xf, axis=-1, keepdims=True) + NORM_EPS)
    return (y * g.astype(jnp.float32)).astype(x.dtype)


def layer_norm(x, g, b):
    xf = x.astype(jnp.float32)
    xc = xf - jnp.mean(xf, axis=-1, keepdims=True)
    var = jnp.mean(xc * xc, axis=-1, keepdims=True)
    return (xc * lax.rsqrt(var + NORM_EPS) * g.astype(jnp.float32) + b.astype(jnp.float32)).astype(x.dtype)


def rope(x, pos):
    half = x.shape[-1] // 2
    inv_freq = jnp.power(ROPE_THETA, -jnp.arange(half, dtype=jnp.float32) / half)
    ang = pos.astype(jnp.float32)[:, None] * inv_freq[None, :]
    shape = (pos.shape[0],) + (1,) * (x.ndim - 3) + (half,)
    cos = jnp.cos(ang).reshape(shape)
    sin = jnp.sin(ang).reshape(shape)
    xf = x.astype(jnp.float32)
    x1, x2 = xf[..., :half], xf[..., half:]
    return jnp.concatenate([x1 * cos - x2 * sin, x2 * cos + x1 * sin], axis=-1).astype(x.dtype)


def chunk_causal(qp, kp):
    return (kp[None, :] // CHUNK) <= (qp[:, None] // CHUNK)


def sweep(fn, qpos, q_arrays, block):
    L = qpos.shape[0]
    qb = min(block, L)
    n = L // qb

    def split(a):
        return jnp.moveaxis(a.reshape((a.shape[0], n, qb) + a.shape[2:]), 1, 0)

    out = lax.map(lambda args: fn(*args), (qpos.reshape(n, qb),) + tuple(split(a) for a in q_arrays))
    out = jnp.moveaxis(out, 0, 1)
    return out.reshape((out.shape[0], L) + out.shape[3:])


def dsa_attend(qp, q, qi, wi, kpos, k, v, ki, topk):
    B, qb = q.shape[0], q.shape[1]
    adm = chunk_causal(qp, kpos)
    s = jnp.einsum('bqhd,bkd->bqhk', qi, ki, preferred_element_type=jnp.float32)
    score = jnp.einsum('bqhk,bqh->bqk', jax.nn.relu(s), wi.astype(jnp.float32))
    score = jnp.where(adm[None], score, -jnp.inf)
    _, idx = lax.top_k(score, topk)
    valid = jnp.take_along_axis(jnp.broadcast_to(adm[None], score.shape), idx, axis=-1)
    take = jax.vmap(lambda rows, ids: rows[ids])
    kg = take(k, idx)
    vg = take(v, idx)
    qg = q.reshape(B, qb, A_KV_HEADS, A_HEADS // A_KV_HEADS, HEAD_DIM)
    logits = jnp.einsum('bqgrd,bqkgd->bqgrk', qg, kg, preferred_element_type=jnp.float32) * HEAD_DIM ** -0.5
    logits = jnp.where(valid[:, :, None, None, :], logits, -jnp.inf)
    p = jax.nn.softmax(logits, axis=-1).astype(v.dtype)
    out = jnp.einsum('bqgrk,bqkgd->bqgrd', p, vg)
    return out.reshape(B, qb, A_HEADS * HEAD_DIM)


def fox_attend(qp, q, fq, kpos, k, v, fk):
    B, qb = q.shape[0], q.shape[1]
    logits = jnp.einsum('bqhd,bkhd->bhqk', q, k, preferred_element_type=jnp.float32) * HEAD_DIM ** -0.5
    logits = logits + jnp.moveaxis(fq, 1, 2)[..., :, None] - jnp.moveaxis(fk, 1, 2)[..., None, :]
    mask = kpos[None, :] <= qp[:, None]
    p = jax.nn.softmax(jnp.where(mask, logits, -jnp.inf), axis=-1).astype(v.dtype)
    out = jnp.einsum('bhqk,bkhd->bqhd', p, v)
    return out.reshape(B, qb, B_HEADS * HEAD_DIM)


def diff_attend(qp, q, kpos, k, v, lam):
    logits = jnp.einsum('bqhid,bkhid->bhiqk', q, k, preferred_element_type=jnp.float32) * C_QK_DIM ** -0.5
    p = jax.nn.softmax(jnp.where(chunk_causal(qp, kpos), logits, -jnp.inf), axis=-1)
    p_diff = (p[:, :, 0] - lam * p[:, :, 1]).astype(v.dtype)
    return jnp.einsum('bhqk,bkhd->bqhd', p_diff, v)


def mla_attend(qp, q_abs, q_rope, kpos, lat, k_rope):
    logits = (jnp.einsum('bqhr,bkr->bhqk', q_abs, lat, preferred_element_type=jnp.float32)
              + jnp.einsum('bqhd,bkd->bhqk', q_rope, k_rope, preferred_element_type=jnp.float32))
    logits = logits * (D_NOPE + D_ROPE) ** -0.5
    p = jax.nn.softmax(jnp.where(chunk_causal(qp, kpos), logits, -jnp.inf), axis=-1).astype(lat.dtype)
    return jnp.einsum('bhqk,bkr->bqhr', p, lat)


def join_past(past, new_rows):
    if past is None:
        return new_rows
    return tuple(jnp.concatenate([pc, nr.astype(pc.dtype)], axis=1) for pc, nr in zip(past, new_rows))


def even_mixer(h, past, w_in, b_f, w_out):
    B, L, _ = h.shape
    Lp = 0 if past is None else past[0].shape[1]
    qpos = jnp.arange(Lp, Lp + L)
    qa, ka, va, qi, ki, wi, qf, kf, vf, fl = split_cols(h @ w_in, EVEN_SIZES)
    qa = rope(qa.reshape(B, L, A_HEADS, HEAD_DIM), qpos)
    ka = rope(ka.reshape(B, L, A_KV_HEADS, HEAD_DIM), qpos)
    va = va.reshape(B, L, A_KV_HEADS, HEAD_DIM)
    qi = rope(qi.reshape(B, L, IDX_HEADS, IDX_DIM), qpos)
    ki = rope(ki[:, :, None, :], qpos)[:, :, 0]
    wi = wi * (IDX_HEADS * IDX_DIM) ** -0.5
    qf = qf.reshape(B, L, B_HEADS, HEAD_DIM)
    kf = kf.reshape(B, L, B_HEADS, HEAD_DIM)
    vf = vf.reshape(B, L, B_HEADS, HEAD_DIM)
    logf = jax.nn.log_sigmoid((fl + b_f).astype(jnp.float32)).astype(h.dtype)
    new_rows = (ka, va, ki, kf, vf, logf)
    k_a, v_a, ki_all, k_f, v_f, logf_all = join_past(past, new_rows)
    Lk = Lp + L
    kpos = jnp.arange(Lk)
    topk = min(A_TOPK_MAX, Lk // 4)
    cum = jnp.cumsum(logf_all.astype(jnp.float32), axis=1)
    out_a = sweep(lambda qp, q, qix, wix: dsa_attend(qp, q, qix, wix, kpos, k_a, v_a, ki_all, topk),
                  qpos, (qa, qi, wi), A_Q_BLOCK)
    out_b = sweep(lambda qp, q, fq: fox_attend(qp, q, fq, kpos, k_f, v_f, cum),
                  qpos, (qf, cum[:, Lp:]), Q_BLOCK)
    y = jnp.concatenate([out_a, out_b], axis=-1) @ w_out
    return y, new_rows


def odd_mixer(h, past, w_in, lam_p, subln, q_norm, w_uq, kv_norm, w_uk, w_uv, w_out, lam_init):
    B, L, _ = h.shape
    Lp = 0 if past is None else past[0].shape[1]
    qpos = jnp.arange(Lp, Lp + L)
    qc, kc, vc, qd, ckv, kr = split_cols(h @ w_in, ODD_SIZES)
    qc = rope(qc.reshape(B, L, C_HEADS, 2, C_QK_DIM), qpos)
    kc = rope(kc.reshape(B, L, C_HEADS, 2, C_QK_DIM), qpos)
    vc = vc.reshape(B, L, C_HEADS, C_V_DIM)
    qd = (rms_norm(qd, q_norm) @ w_uq).reshape(B, L, D_HEADS, D_NOPE + D_ROPE)
    q_abs = jnp.einsum('bqhn,rhn->bqhr', qd[..., :D_NOPE], w_uk)
    q_rope = rope(qd[..., D_NOPE:], qpos)
    c_kv = rms_norm(ckv, kv_norm)
    k_rope = rope(kr[:, :, None, :], qpos)[:, :, 0]
    new_rows = (kc, vc, c_kv, k_rope)
    k_c, v_c, lat, kr_all = join_past(past, new_rows)
    kpos = jnp.arange(Lp + L)
    lam_f = lam_p.astype(jnp.float32)
    lam = jnp.exp(jnp.sum(lam_f[0] * lam_f[1])) - jnp.exp(jnp.sum(lam_f[2] * lam_f[3])) + lam_init
    out_c = sweep(lambda qp, q: diff_attend(qp, q, kpos, k_c, v_c, lam), qpos, (qc,), Q_BLOCK)
    out_c = (rms_norm(out_c, subln) * (1.0 - lam_init)).reshape(B, L, C_HEADS * C_V_DIM)
    out_lat = sweep(lambda qp, qa_, qr_: mla_attend(qp, qa_, qr_, kpos, lat, kr_all),
                    qpos, (q_abs, q_rope), Q_BLOCK)
    out_d = jnp.einsum('bqhr,rhv->bqhv', out_lat, w_uv).reshape(B, L, D_HEADS * D_V)
    y = jnp.concatenate([out_c, out_d], axis=-1) @ w_out
    return y, new_rows


def moe(h, w_router, r_bias, w1, w3, w2, ws1, ws3, ws2):
    B, L, D = h.shape
    t = h.reshape(B * L, D)
    scores = jax.nn.sigmoid(jnp.einsum('td,de->te', t, w_router, preferred_element_type=jnp.float32))
    biased = scores + r_bias.astype(jnp.float32)
    grouped = biased.reshape(B * L, N_GROUPS, N_EXPERTS // N_GROUPS)
    group_score = jnp.sum(lax.top_k(grouped, 2)[0], axis=-1)
    _, gidx = lax.top_k(group_score, TOPK_GROUPS)
    gmask = jnp.sum(jax.nn.one_hot(gidx, N_GROUPS, dtype=jnp.float32), axis=1) > 0
    emask = jnp.repeat(gmask, N_EXPERTS // N_GROUPS, axis=1)
    _, eidx = lax.top_k(jnp.where(emask, biased, -jnp.inf), TOP_K)
    wsel = jnp.take_along_axis(scores, eidx, axis=-1)
    wsel = wsel / jnp.sum(wsel, axis=-1, keepdims=True) * ROUTED_SCALE
    gates = jnp.einsum('tk,tke->te', wsel, jax.nn.one_hot(eidx, N_EXPERTS, dtype=jnp.float32)).astype(h.dtype)
    y = (jax.nn.silu(t @ ws1) * (t @ ws3)) @ ws2
    for e in range(N_EXPERTS):
        y = y + gates[:, e:e + 1] * ((jax.nn.silu(t @ w1[e]) * (t @ w3[e])) @ w2[e])
    return y.reshape(B, L, D)


def residual_block(x, c, ada_w, ada_b, ln_g, ln_b, sublayer):
    mod = jax.nn.silu(c) @ ada_w + ada_b
    shift, scale, gate = jnp.split(mod[:, None, :], 3, axis=-1)
    y, rows = sublayer(x * (1 + scale) + shift)
    return layer_norm(DN_ALPHA * x + (1 + gate) * y, ln_g, ln_b), rows


def stack_rows(rows, idx):
    return jnp.stack([r[idx] for r in rows])


def _nrm(key, shape, scale):
    return scale * jax.random.normal(key, shape, jnp.float32)


def setup_inputs(seed: int = 0) -> dict:
    key = jax.random.key(seed)
    ks = iter(jax.random.split(key, 64))
    D = D_MODEL
    inp = {}
    inp['x_prompt'] = _nrm(next(ks), (BATCH, SEQ, D), 1.0)
    inp['x_sample'] = _nrm(next(ks), (DEC_BATCH, DEC_SEQ, D), 1.0)
    inp['c_prompt'] = _nrm(next(ks), (BATCH, D), 1.0)
    inp['c_sample'] = _nrm(next(ks), (DEC_BATCH, D), 1.0)
    inp['cache_a_k'] = _nrm(next(ks), (N_EVEN, DEC_BATCH, PAST_LEN, A_KV_HEADS, HEAD_DIM), 1.0)
    inp['cache_a_v'] = _nrm(next(ks), (N_EVEN, DEC_BATCH, PAST_LEN, A_KV_HEADS, HEAD_DIM), 1.0)
    inp['cache_a_kidx'] = _nrm(next(ks), (N_EVEN, DEC_BATCH, PAST_LEN, IDX_DIM), 1.0)
    inp['cache_b_k'] = _nrm(next(ks), (N_EVEN, DEC_BATCH, PAST_LEN, B_HEADS, HEAD_DIM), 1.0)
    inp['cache_b_v'] = _nrm(next(ks), (N_EVEN, DEC_BATCH, PAST_LEN, B_HEADS, HEAD_DIM), 1.0)
    inp['cache_b_logf'] = jax.nn.log_sigmoid(3.0 + _nrm(next(ks), (N_EVEN, DEC_BATCH, PAST_LEN, B_HEADS), 1.0))
    inp['cache_c_k'] = _nrm(next(ks), (N_ODD, DEC_BATCH, PAST_LEN, C_HEADS, 2, C_QK_DIM), 1.0)
    inp['cache_c_v'] = _nrm(next(ks), (N_ODD, DEC_BATCH, PAST_LEN, C_HEADS, C_V_DIM), 1.0)
    inp['cache_d_latent'] = _nrm(next(ks), (N_ODD, DEC_BATCH, PAST_LEN, D_KV_LORA), 1.0)
    inp['cache_d_krope'] = _nrm(next(ks), (N_ODD, DEC_BATCH, PAST_LEN, D_ROPE), 1.0)
    inp['w_in_even'] = _nrm(next(ks), (N_EVEN, D, EVEN_IN), D ** -0.5)
    inp['b_forget'] = 3.0 + _nrm(next(ks), (N_EVEN, B_HEADS), 0.1)
    inp['w_out_even'] = _nrm(next(ks), (N_EVEN, EVEN_OUT, D), EVEN_OUT ** -0.5 * DN_BETA)
    inp['w_in_odd'] = _nrm(next(ks), (N_ODD, D, ODD_IN), D ** -0.5)
    inp['c_lambda'] = _nrm(next(ks), (N_ODD, 4, C_QK_DIM), 0.1)
    inp['c_subln'] = 1.0 + _nrm(next(ks), (N_ODD, C_V_DIM), 0.01)
    inp['d_q_norm'] = 1.0 + _nrm(next(ks), (N_ODD, D_Q_LORA), 0.01)
    inp['d_w_uq'] = _nrm(next(ks), (N_ODD, D_Q_LORA, D_HEADS * (D_NOPE + D_ROPE)), D_Q_LORA ** -0.5)
    inp['d_kv_norm'] = 1.0 + _nrm(next(ks), (N_ODD, D_KV_LORA), 0.01)
    inp['d_w_uk'] = _nrm(next(ks), (N_ODD, D_KV_LORA, D_HEADS, D_NOPE), D_KV_LORA ** -0.5)
    inp['d_w_uv'] = _nrm(next(ks), (N_ODD, D_KV_LORA, D_HEADS, D_V), D_KV_LORA ** -0.5)
    inp['w_out_odd'] = _nrm(next(ks), (N_ODD, ODD_OUT, D), ODD_OUT ** -0.5 * DN_BETA)
    inp['ada_mix_w'] = _nrm(next(ks), (DEPTH, D, 3 * D), 0.1 * D ** -0.5)
    inp['ada_mix_b'] = _nrm(next(ks), (DEPTH, 3 * D), 0.01)
    inp['ln_mix_g'] = 1.0 + _nrm(next(ks), (DEPTH, D), 0.01)
    inp['ln_mix_b'] = _nrm(next(ks), (DEPTH, D), 0.01)
    inp['ada_ffn_w'] = _nrm(next(ks), (DEPTH, D, 3 * D), 0.1 * D ** -0.5)
    inp['ada_ffn_b'] = _nrm(next(ks), (DEPTH, 3 * D), 0.01)
    inp['ln_ffn_g'] = 1.0 + _nrm(next(ks), (DEPTH, D), 0.01)
    inp['ln_ffn_b'] = _nrm(next(ks), (DEPTH, D), 0.01)
    inp['router_w'] = _nrm(next(ks), (DEPTH, D, N_EXPERTS), D ** -0.5)
    inp['router_bias'] = _nrm(next(ks), (DEPTH, N_EXPERTS), 0.01)
    inp['moe_w1'] = _nrm(next(ks), (DEPTH, N_EXPERTS, D, MOE_FF), D ** -0.5)
    inp['moe_w3'] = _nrm(next(ks), (DEPTH, N_EXPERTS, D, MOE_FF), D ** -0.5)
    inp['moe_w2'] = _nrm(next(ks), (DEPTH, N_EXPERTS, MOE_FF, D), MOE_FF ** -0.5 * DN_BETA)
    inp['shared_w1'] = _nrm(next(ks), (DEPTH, D, MOE_FF), D ** -0.5)
    inp['shared_w3'] = _nrm(next(ks), (DEPTH, D, MOE_FF), D ** -0.5)
    inp['shared_w2'] = _nrm(next(ks), (DEPTH, MOE_FF, D), MOE_FF ** -0.5 * DN_BETA)
    return inp


def reference(x_prompt, x_sample, c_prompt, c_sample,
              cache_a_k, cache_a_v, cache_a_kidx, cache_b_k, cache_b_v, cache_b_logf,
              cache_c_k, cache_c_v, cache_d_latent, cache_d_krope,
              w_in_even, b_forget, w_out_even,
              w_in_odd, c_lambda, c_subln, d_q_norm, d_w_uq, d_kv_norm, d_w_uk, d_w_uv, w_out_odd,
              ada_mix_w, ada_mix_b, ln_mix_g, ln_mix_b,
              ada_ffn_w, ada_ffn_b, ln_ffn_g, ln_ffn_b,
              router_w, router_bias, moe_w1, moe_w3, moe_w2, shared_w1, shared_w3, shared_w2):
    yp, ys = x_prompt, x_sample
    ev_p, ev_s, od_p, od_s = [], [], [], []
    for i in range(DEPTH):
        j = i // 2
        if i % 2 == 0:
            mixer = functools.partial(even_mixer, w_in=w_in_even[j], b_f=b_forget[j], w_out=w_out_even[j])
            past = (cache_a_k[j], cache_a_v[j], cache_a_kidx[j], cache_b_k[j], cache_b_v[j], cache_b_logf[j])
            store_p, store_s = ev_p, ev_s
        else:
            lam_init = 0.8 - 0.6 * math.exp(-0.3 * i)
            mixer = functools.partial(odd_mixer, w_in=w_in_odd[j], lam_p=c_lambda[j], subln=c_subln[j],
                                      q_norm=d_q_norm[j], w_uq=d_w_uq[j], kv_norm=d_kv_norm[j],
                                      w_uk=d_w_uk[j], w_uv=d_w_uv[j], w_out=w_out_odd[j], lam_init=lam_init)
            past = (cache_c_k[j], cache_c_v[j], cache_d_latent[j], cache_d_krope[j])
            store_p, store_s = od_p, od_s
        mix_p = (ada_mix_w[i], ada_mix_b[i], ln_mix_g[i], ln_mix_b[i])
        yp, rows = residual_block(yp, c_prompt, *mix_p, lambda h: mixer(h, None))
        store_p.append(rows)
        ys, rows = residual_block(ys, c_sample, *mix_p, lambda h: mixer(h, past))
        store_s.append(rows)
        ffn = functools.partial(moe, w_router=router_w[i], r_bias=router_bias[i], w1=moe_w1[i], w3=moe_w3[i],
                                w2=moe_w2[i], ws1=shared_w1[i], ws3=shared_w3[i], ws2=shared_w2[i])
        ffn_p = (ada_ffn_w[i], ada_ffn_b[i], ln_ffn_g[i], ln_ffn_b[i])
        yp, _ = residual_block(yp, c_prompt, *ffn_p, lambda h: (ffn(h), None))
        ys, _ = residual_block(ys, c_sample, *ffn_p, lambda h: (ffn(h), None))
    y_prompt, y_sample = yp, ys
    a_k_p, a_k_s = stack_rows(ev_p, 0), stack_rows(ev_s, 0)
    a_v_p, a_v_s = stack_rows(ev_p, 1), stack_rows(ev_s, 1)
    a_kidx_p, a_kidx_s = stack_rows(ev_p, 2), stack_rows(ev_s, 2)
    b_k_p, b_k_s = stack_rows(ev_p, 3), stack_rows(ev_s, 3)
    b_v_p, b_v_s = stack_rows(ev_p, 4), stack_rows(ev_s, 4)
    b_logf_p, b_logf_s = stack_rows(ev_p, 5), stack_rows(ev_s, 5)
    c_k_p, c_k_s = stack_rows(od_p, 0), stack_rows(od_s, 0)
    c_v_p, c_v_s = stack_rows(od_p, 1), stack_rows(od_s, 1)
    d_lat_p, d_lat_s = stack_rows(od_p, 2), stack_rows(od_s, 2)
    d_kr_p, d_kr_s = stack_rows(od_p, 3), stack_rows(od_s, 3)
    return (y_prompt, y_sample, a_k_p, a_k_s, a_v_p, a_v_s, a_kidx_p, a_kidx_s, b_k_p, b_k_s, b_v_p, b_v_s,
            b_logf_p, b_logf_s, c_k_p, c_k_s, c_v_p, c_v_s, d_lat_p, d_lat_s, d_kr_p, d_kr_s)
```

```python
import functools
import math

import jax
import jax.numpy as jnp
from jax import lax
from jax.experimental import pallas as pl
from jax.experimental.pallas import tpu as pltpu

CHUNK = 64
ROPE_THETA = 10000.0
HEAD_DIM = 128
A_HEADS = 8
A_KV_HEADS = 2
IDX_HEADS = 16
IDX_DIM = 128
A_TOPK_MAX = 256
B_HEADS = 8
C_HEADS = 8
C_QK_DIM = 64
C_V_DIM = 128
D_HEADS = 8
D_NOPE = 128
D_ROPE = 64
D_V = 128
N_GROUPS = 8
TOPK_GROUPS = 4
TOP_K = 8
ROUTED_SCALE = 2.5
NORM_EPS = 1e-6

EVEN_SIZES = (A_HEADS * HEAD_DIM, A_KV_HEADS * HEAD_DIM, A_KV_HEADS * HEAD_DIM,
              IDX_HEADS * IDX_DIM, IDX_DIM, IDX_HEADS,
              B_HEADS * HEAD_DIM, B_HEADS * HEAD_DIM, B_HEADS * HEAD_DIM, B_HEADS)

LANES = 128
SUBLANES = 8
VMEM_LIMIT_BYTES = 56 * 2**20
MXU_DTYPE = jnp.bfloat16

F32 = jnp.float32
NEG = -1e30
INT_MIN = -2**31


def _pick(n, target, mult):
    best = None
    for d in range(mult, min(n, target) + 1, mult):
        if n % d == 0:
            best = d
    return n if best is None else best


def _params(n_axes):
    return pltpu.CompilerParams(dimension_semantics=("arbitrary",) * n_axes,
                                vmem_limit_bytes=VMEM_LIMIT_BYTES)


def _dot(a, b):
    return jnp.dot(a.astype(MXU_DTYPE), b.astype(MXU_DTYPE), preferred_element_type=F32)


def _dot_nt(a, b):
    return lax.dot_general(a.astype(MXU_DTYPE), b.astype(MXU_DTYPE), (((1,), (1,)), ((), ())),
                           preferred_element_type=F32)


def _mm_body(a_ref, b_ref, o_ref, *, precise):
    if precise:
        o = jnp.dot(a_ref[...], b_ref[...], preferred_element_type=F32, precision=lax.Precision.HIGHEST)
    else:
        o = _dot(a_ref[...], b_ref[...])
    o_ref[...] = o.astype(o_ref.dtype)


def matmul(a, b, out_dtype=F32, precise=False, tm_target=1024, tn_target=512):
    M, K = a.shape
    N = b.shape[1]
    tm = _pick(M, tm_target, SUBLANES)
    tn = _pick(N, tn_target, LANES)
    return pl.pallas_call(
        functools.partial(_mm_body, precise=precise),
        grid=(M // tm, N // tn),
        in_specs=[pl.BlockSpec((tm, K), lambda i, j: (i, 0)),
                  pl.BlockSpec((K, tn), lambda i, j: (0, j))],
        out_specs=pl.BlockSpec((tm, tn), lambda i, j: (i, j)),
        out_shape=jax.ShapeDtypeStruct((M, N), out_dtype),
        compiler_params=_params(2),
        name="matmul",
    )(a, b)


def _bmm_body(a_ref, b_ref, o_ref):
    o_ref[0] = _dot(a_ref[0], b_ref[0]).astype(o_ref.dtype)


def bmm(a, b, out_dtype=F32, tm_target=1024):
    H, M, K = a.shape
    N = b.shape[2]
    tm = _pick(M, tm_target, SUBLANES)
    return pl.pallas_call(
        _bmm_body,
        grid=(H, M // tm),
        in_specs=[pl.BlockSpec((1, tm, K), lambda h, i: (h, i, 0)),
                  pl.BlockSpec((1, K, N), lambda h, i: (h, 0, 0))],
        out_specs=pl.BlockSpec((1, tm, N), lambda h, i: (h, i, 0)),
        out_shape=jax.ShapeDtypeStruct((H, M, N), out_dtype),
        compiler_params=_params(2),
        name="bmm",
    )(a, b)


def _online_step(s, v, carry):
    m, l, acc = carry
    m_new = jnp.maximum(m, jnp.max(s, axis=-1, keepdims=True))
    p = jnp.exp(s - m_new)
    alpha = jnp.exp(m - m_new)
    l = alpha * l + jnp.sum(p, axis=-1, keepdims=True)
    acc = alpha * acc + _dot(p, v)
    return m_new, l, acc


def _init_carry(tq, dv):
    return (jnp.full((tq, 1), NEG, F32), jnp.zeros((tq, 1), F32), jnp.zeros((tq, dv), F32))


def _n_key_blocks(last_key, tk):
    return last_key // tk + 1


def _fox_body(q_ref, k_ref, v_ref, fq_ref, fk_ref, o_ref, *, tq, tk, lp, lk):
    i = pl.program_id(2)
    q = q_ref[0]
    fq = fq_ref[0]
    qpos = lp + i * tq + lax.broadcasted_iota(jnp.int32, (tq, tk), 0)
    last = jnp.minimum(lp + (i + 1) * tq - 1, lk - 1)

    def body(j, carry):
        off = pl.multiple_of(j * tk, tk)
        k = k_ref[0, pl.ds(off, tk), :]
        v = v_ref[0, pl.ds(off, tk), :]
        s = _dot_nt(q, k) + fq - fk_ref[0, j]
        kpos = off + lax.broadcasted_iota(jnp.int32, (tq, tk), 1)
        s = jnp.where(kpos <= qpos, s, NEG)
        return _online_step(s, v, carry)

    m, l, acc = lax.fori_loop(0, _n_key_blocks(last, tk), body, _init_carry(tq, HEAD_DIM))
    o_ref[0] = (acc / l).astype(o_ref.dtype)


def fox_attention(q, k, v, fq, fk, lp, lk, tq, tk):
    B, Lq, _ = q.shape
    Lkp = k.shape[1]
    H = B_HEADS
    return pl.pallas_call(
        functools.partial(_fox_body, tq=tq, tk=tk, lp=lp, lk=lk),
        grid=(B, H, Lq // tq),
        in_specs=[pl.BlockSpec((1, tq, HEAD_DIM), lambda b, h, i: (b, i, h)),
                  pl.BlockSpec((1, Lkp, HEAD_DIM), lambda b, h, i: (b, 0, h)),
                  pl.BlockSpec((1, Lkp, HEAD_DIM), lambda b, h, i: (b, 0, h)),
                  pl.BlockSpec((1, tq, 1), lambda b, h, i: (b * H + h, i, 0)),
                  pl.BlockSpec((1, Lkp // tk, 1, tk), lambda b, h, i: (b * H + h, 0, 0, 0))],
        out_specs=pl.BlockSpec((1, tq, HEAD_DIM), lambda b, h, i: (b, i, h)),
        out_shape=jax.ShapeDtypeStruct((B, Lq, H * HEAD_DIM), MXU_DTYPE),
        compiler_params=_params(3),
        name="fox_attention",
    )(q, k, v, fq, fk)


def _chunk_last_key(lp, i, tq, lk):
    last_q = lp + (i + 1) * tq - 1
    return jnp.minimum((last_q // CHUNK + 1) * CHUNK - 1, lk - 1)


def _diff_body(lam_ref, q_ref, k_ref, v_ref, g_ref, o_ref, *, tq, tk, lp, lk, out_scale):
    i = pl.program_id(2)
    q = q_ref[0].astype(F32)
    lane = lax.broadcasted_iota(jnp.int32, q.shape, 1)
    q0 = jnp.where(lane < C_QK_DIM, q, 0.0).astype(MXU_DTYPE)
    q1 = jnp.where(lane >= C_QK_DIM, q, 0.0).astype(MXU_DTYPE)
    qchunk = (lp + i * tq + lax.broadcasted_iota(jnp.int32, (tq, tk), 0)) // CHUNK

    def body(j, carry):
        c0, c1 = carry
        off = pl.multiple_of(j * tk, tk)
        k = k_ref[0, pl.ds(off, tk), :]
        v = v_ref[0, pl.ds(off, tk), :]
        kpos = off + lax.broadcasted_iota(jnp.int32, (tq, tk), 1)
        ok = (kpos // CHUNK <= qchunk) & (kpos < lk)
        s0 = jnp.where(ok, _dot_nt(q0, k), NEG)
        s1 = jnp.where(ok, _dot_nt(q1, k), NEG)
        return _online_step(s0, v, c0), _online_step(s1, v, c1)

    nb = _n_key_blocks(_chunk_last_key(lp, i, tq, lk), tk)
    (m0, l0, a0), (m1, l1, a1) = lax.fori_loop(
        0, nb, body, (_init_carry(tq, C_V_DIM), _init_carry(tq, C_V_DIM)))
    o = a0 / l0 - lam_ref[0] * (a1 / l1)
    o = o * lax.rsqrt(jnp.mean(o * o, axis=-1, keepdims=True) + NORM_EPS)
    o_ref[0] = (o * g_ref[...] * out_scale).astype(o_ref.dtype)


def diff_attention(lam, q, k, v, subln, out_scale, lp, lk, tq, tk):
    B, Lq, _ = q.shape
    Lkp = k.shape[1]
    H = C_HEADS
    W = 2 * C_QK_DIM
    return pl.pallas_call(
        functools.partial(_diff_body, tq=tq, tk=tk, lp=lp, lk=lk, out_scale=out_scale),
        grid=(B, H, Lq // tq),
        in_specs=[pl.BlockSpec(memory_space=pltpu.SMEM),
                  pl.BlockSpec((1, tq, W), lambda b, h, i: (b, i, h)),
                  pl.BlockSpec((1, Lkp, W), lambda b, h, i: (b, 0, h)),
                  pl.BlockSpec((1, Lkp, C_V_DIM), lambda b, h, i: (b, 0, h)),
                  pl.BlockSpec((1, C_V_DIM), lambda b, h, i: (0, 0))],
        out_specs=pl.BlockSpec((1, tq, C_V_DIM), lambda b, h, i: (b, i, h)),
        out_shape=jax.ShapeDtypeStruct((B, Lq, H * C_V_DIM), MXU_DTYPE),
        compiler_params=_params(3),
        name="diff_attention",
    )(lam, q, k, v, subln)


def _mla_body(qa_ref, qr_ref, lat_ref, kr_ref, o_ref, *, tq, tk, lp, lk):
    i = pl.program_id(2)
    qa = qa_ref[0, 0]
    qr = qr_ref[0, 0]
    qchunk = (lp + i * tq + lax.broadcasted_iota(jnp.int32, (tq, tk), 0)) // CHUNK

    def body(j, carry):
        off = pl.multiple_of(j * tk, tk)
        lat = lat_ref[0, pl.ds(off, tk), :]
        kr = kr_ref[0, pl.ds(off, tk), :]
        kpos = off + lax.broadcasted_iota(jnp.int32, (tq, tk), 1)
        ok = (kpos // CHUNK <= qchunk) & (kpos < lk)
        s = jnp.where(ok, _dot_nt(qa, lat) + _dot_nt(qr, kr), NEG)
        return _online_step(s, lat, carry)

    nb = _n_key_blocks(_chunk_last_key(lp, i, tq, lk), tk)
    m, l, acc = lax.fori_loop(0, nb, body, _init_carry(tq, lat_ref.shape[-1]))
    o_ref[0, 0] = (acc / l).astype(o_ref.dtype)


def mla_attention(qa, qr, lat, kr, lp, lk, tq, tk):
    B, H, Lq, R = qa.shape
    dr = qr.shape[-1]
    Lkp = lat.shape[1]
    return pl.pallas_call(
        functools.partial(_mla_body, tq=tq, tk=tk, lp=lp, lk=lk),
        grid=(B, H, Lq // tq),
        in_specs=[pl.BlockSpec((1, 1, tq, R), lambda b, h, i: (b, h, i, 0)),
                  pl.BlockSpec((1, 1, tq, dr), lambda b, h, i: (b, h, i, 0)),
                  pl.BlockSpec((1, Lkp, R), lambda b, h, i: (b, 0, 0)),
                  pl.BlockSpec((1, Lkp, dr), lambda b, h, i: (b, 0, 0))],
        out_specs=pl.BlockSpec((1, 1, tq, R), lambda b, h, i: (b, h, i, 0)),
        out_shape=jax.ShapeDtypeStruct((B, H, Lq, R), MXU_DTYPE),
        compiler_params=_params(3),
        name="mla_attention",
    )(qa, qr, lat, kr)


def _dsa_body(qi_ref, wi_ref, ki_ref, q_ref, k_ref, v_ref, o_ref, *, tq, lp, lk, topk):
    i = pl.program_id(1)
    lkp = ki_ref.shape[1]
    ki = ki_ref[0]
    wi = wi_ref[0]

    score = jnp.zeros((tq, lkp), F32)
    for h in range(IDX_HEADS):
        s = _dot_nt(qi_ref[0, :, h * IDX_DIM:(h + 1) * IDX_DIM], ki)
        score = score + jnp.maximum(s, 0.0) * wi[:, h:h + 1]

    qchunk = (lp + i * tq + lax.broadcasted_iota(jnp.int32, (tq, lkp), 0)) // CHUNK
    kpos = lax.broadcasted_iota(jnp.int32, (tq, lkp), 1)
    adm = (kpos // CHUNK <= qchunk) & (kpos < lk)

    bits = pltpu.bitcast(score, jnp.int32)
    key = jnp.where(bits < 0, bits ^ jnp.int32(0x7FFFFFFF), bits)
    key = jnp.where(adm, key, INT_MIN)

    def count_ge(cand):
        return jnp.sum(jnp.where(key >= cand, 1.0, 0.0), axis=-1, keepdims=True)

    kf = float(topk)
    t = jnp.where(count_ge(jnp.zeros((tq, 1), jnp.int32)) >= kf, 0, INT_MIN).astype(jnp.int32)

    def bit_step(it, t):
        cand = t | jnp.left_shift(jnp.int32(1), 30 - it)
        return jnp.where(count_ge(cand) >= kf, cand, t)

    t = lax.fori_loop(0, 31, bit_step, t)
    sel = adm & (key >= t)

    rep = A_HEADS // A_KV_HEADS
    for g in range(A_KV_HEADS):
        kg = k_ref[0, :, g * HEAD_DIM:(g + 1) * HEAD_DIM]
        vg = v_ref[0, :, g * HEAD_DIM:(g + 1) * HEAD_DIM]
        for r in range(rep):
            h = g * rep + r
            s = jnp.where(sel, _dot_nt(q_ref[0, :, h * HEAD_DIM:(h + 1) * HEAD_DIM], kg), NEG)
            p = jnp.exp(s - jnp.max(s, axis=-1, keepdims=True))
            l = jnp.sum(p, axis=-1, keepdims=True)
            o_ref[0, :, h * HEAD_DIM:(h + 1) * HEAD_DIM] = (_dot(p, vg) / l).astype(o_ref.dtype)


def dsa_attention(qi, wi, ki, q, k, v, lp, lk, topk, tq):
    B, Lq, _ = q.shape
    Lkp = k.shape[1]
    row = lambda b, i: (b, i, 0)
    full = lambda b, i: (b, 0, 0)
    return pl.pallas_call(
        functools.partial(_dsa_body, tq=tq, lp=lp, lk=lk, topk=topk),
        grid=(B, Lq // tq),
        in_specs=[pl.BlockSpec((1, tq, IDX_HEADS * IDX_DIM), row),
                  pl.BlockSpec((1, tq, IDX_HEADS), row),
                  pl.BlockSpec((1, Lkp, IDX_DIM), full),
                  pl.BlockSpec((1, tq, A_HEADS * HEAD_DIM), row),
                  pl.BlockSpec((1, Lkp, A_KV_HEADS * HEAD_DIM), full),
                  pl.BlockSpec((1, Lkp, A_KV_HEADS * HEAD_DIM), full)],
        out_specs=pl.BlockSpec((1, tq, A_HEADS * HEAD_DIM), row),
        out_shape=jax.ShapeDtypeStruct((B, Lq, A_HEADS * HEAD_DIM), MXU_DTYPE),
        compiler_params=_params(2),
        name="dsa_attention",
    )(qi, wi, ki, q, k, v)


def _gmm_body(te_ref, nt_ref, x_ref, g_ref, w1_ref, w3_ref, w2_ref, o_ref):
    i = pl.program_id(0)

    @pl.when(i < nt_ref[0])
    def _():
        x = x_ref[...]
        a = _dot(x, w1_ref[0])
        b = _dot(x, w3_ref[0])
        act = a * jax.nn.sigmoid(a) * b * g_ref[...]
        o_ref[...] = _dot(act, w2_ref[0]).astype(o_ref.dtype)

    @pl.when(i >= nt_ref[0])
    def _():
        o_ref[...] = jnp.zeros(o_ref.shape, o_ref.dtype)


def grouped_ffn(tile_expert, n_tiles, x, gate, w1, w3, w2, tm):
    P, D = x.shape
    F = w1.shape[2]
    row = lambda i, te, nt: (jnp.minimum(i, nt[0] - 1), 0)
    wsel = lambda i, te, nt: (te[i], 0, 0)
    return pl.pallas_call(
        _gmm_body,
        grid_spec=pltpu.PrefetchScalarGridSpec(
            num_scalar_prefetch=2,
            grid=(P // tm,),
            in_specs=[pl.BlockSpec((tm, D), row),
                      pl.BlockSpec((tm, 1), row),
                      pl.BlockSpec((1, D, F), wsel),
                      pl.BlockSpec((1, D, F), wsel),
                      pl.BlockSpec((1, F, D), wsel)],
            out_specs=pl.BlockSpec((tm, D), lambda i, te, nt: (i, 0))),
        out_shape=jax.ShapeDtypeStruct((P, D), F32),
        compiler_params=_params(1),
        name="grouped_ffn",
    )(tile_expert, n_tiles, x, gate, w1, w3, w2)


def _rope_tables(pos, half):
    inv_freq = jnp.power(ROPE_THETA, -jnp.arange(half, dtype=F32) / half)
    ang = pos.astype(F32)[:, None] * inv_freq[None, :]
    return jnp.cos(ang), jnp.sin(ang)


def _rope(x, pos):
    half = x.shape[-1] // 2
    cos, sin = _rope_tables(pos, half)
    shape = (pos.shape[0],) + (1,) * (x.ndim - 3) + (half,)
    cos, sin = cos.reshape(shape), sin.reshape(shape)
    x1, x2 = x[..., :half], x[..., half:]
    return jnp.concatenate([x1 * cos - x2 * sin, x2 * cos + x1 * sin], axis=-1)


def _rms_norm(x, g):
    return x * lax.rsqrt(jnp.mean(x * x, axis=-1, keepdims=True) + NORM_EPS) * g


def _layer_norm(x, g, b):
    xc = x - jnp.mean(x, axis=-1, keepdims=True)
    var = jnp.mean(xc * xc, axis=-1, keepdims=True)
    return xc * lax.rsqrt(var + NORM_EPS) * g + b


def _split_cols(p, sizes):
    out, o = [], 0
    for s in sizes:
        out.append(p[..., o:o + s])
        o += s
    return out


def _pad_keys(x, lkp):
    return jnp.pad(x, ((0, 0), (0, lkp - x.shape[1])) + ((0, 0),) * (x.ndim - 2))


def _join(past, new, lkp):
    new = new.reshape(new.shape[0], new.shape[1], -1)
    if past is not None:
        new = jnp.concatenate([past.reshape(past.shape[0], past.shape[1], -1), new], axis=1)
    return _pad_keys(new, lkp).astype(MXU_DTYPE)


class _Group:
    def __init__(self, batch, lq, lp):
        self.batch, self.lq, self.lp = batch, lq, lp
        self.lk = lp + lq
        self.tk = _pick(-(-self.lk // LANES) * LANES, 512, LANES)
        self.lkp = -(-self.lk // self.tk) * self.tk
        self.tq = _pick(lq, 256, SUBLANES)
        self.tq_dsa = _pick(lq, 128, SUBLANES)
        self.topk = min(A_TOPK_MAX, self.lk // 4)


def _even_attend(grp, p, past, b_f):
    B, L = grp.batch, grp.lq
    qpos = jnp.arange(grp.lp, grp.lp + L)
    qa, ka, va, qi, ki, wi, qf, kf, vf, fl = _split_cols(p, EVEN_SIZES)
    qa = _rope(qa.reshape(B, L, A_HEADS, HEAD_DIM), qpos)
    ka = _rope(ka.reshape(B, L, A_KV_HEADS, HEAD_DIM), qpos)
    va = va.reshape(B, L, A_KV_HEADS, HEAD_DIM)
    qi = _rope(qi.reshape(B, L, IDX_HEADS, IDX_DIM), qpos)
    ki = _rope(ki[:, :, None, :], qpos)[:, :, 0]
    wi = wi * (IDX_HEADS * IDX_DIM) ** -0.5
    kf = kf.reshape(B, L, B_HEADS, HEAD_DIM)
    vf = vf.reshape(B, L, B_HEADS, HEAD_DIM)
    logf = jax.nn.log_sigmoid(fl + b_f)
    new_rows = (ka, va, ki, kf, vf, logf)
    pa = (None,) * 6 if past is None else past
    lkp, tk = grp.lkp, grp.tk

    logf_all = logf if past is None else jnp.concatenate([pa[5], logf], axis=1)
    cum = jnp.cumsum(logf_all, axis=1)
    fq = jnp.moveaxis(cum[:, grp.lp:], 1, 2).reshape(B * B_HEADS, L, 1)
    fk = jnp.moveaxis(_pad_keys(cum, lkp), 1, 2).reshape(B * B_HEADS, lkp // tk, 1, tk)

    scale = HEAD_DIM ** -0.5
    out_a = dsa_attention(
        qi.reshape(B, L, -1).astype(MXU_DTYPE), wi, _join(pa[2], ki, lkp),
        (qa * scale).reshape(B, L, -1).astype(MXU_DTYPE), _join(pa[0], ka, lkp), _join(pa[1], va, lkp),
        grp.lp, grp.lk, grp.topk, grp.tq_dsa)
    out_b = fox_attention(
        (qf * scale).astype(MXU_DTYPE), _join(pa[3], kf, lkp), _join(pa[4], vf, lkp), fq, fk,
        grp.lp, grp.lk, grp.tq, tk)
    out = jnp.concatenate([out_a, out_b], axis=-1).reshape(B * L, -1)
    return out, new_rows


def _odd_attend(grp, p, past, lam, subln, q_norm, w_uq, kv_norm, w_uk, w_uv, lam_init):
    B, L = grp.batch, grp.lq
    qpos = jnp.arange(grp.lp, grp.lp + L)
    R = kv_norm.shape[0]
    q_lora = q_norm.shape[0]
    sizes = (C_HEADS * 2 * C_QK_DIM, C_HEADS * 2 * C_QK_DIM, C_HEADS * C_V_DIM, q_lora, R, D_ROPE)
    qc, kc, vc, qd, ckv, kr = _split_cols(p, sizes)
    qc = _rope(qc.reshape(B, L, C_HEADS, 2, C_QK_DIM), qpos)
    kc = _rope(kc.reshape(B, L, C_HEADS, 2, C_QK_DIM), qpos)
    vc = vc.reshape(B, L, C_HEADS, C_V_DIM)
    qd = matmul(_rms_norm(qd, q_norm).reshape(B * L, q_lora).astype(MXU_DTYPE), w_uq.astype(MXU_DTYPE))
    qd = qd.reshape(B, L, D_HEADS, D_NOPE + D_ROPE)
    q_nope = jnp.moveaxis(qd[..., :D_NOPE], 2, 0).reshape(D_HEADS, B * L, D_NOPE)
    q_abs = bmm(q_nope.astype(MXU_DTYPE), jnp.transpose(w_uk, (1, 2, 0)).astype(MXU_DTYPE))
    q_rope = _rope(qd[..., D_NOPE:], qpos)
    c_kv = _rms_norm(ckv, kv_norm)
    k_rope = _rope(kr[:, :, None, :], qpos)[:, :, 0]
    new_rows = (kc, vc, c_kv, k_rope)
    pa = (None,) * 4 if past is None else past
    lkp, tk = grp.lkp, grp.tk

    out_c = diff_attention(
        lam, (qc * C_QK_DIM ** -0.5).reshape(B, L, -1).astype(MXU_DTYPE), _join(pa[0], kc, lkp),
        _join(pa[1], vc, lkp), subln.reshape(1, C_V_DIM), 1.0 - lam_init, grp.lp, grp.lk, grp.tq, tk)

    scale = (D_NOPE + D_ROPE) ** -0.5
    qa_ = jnp.moveaxis((q_abs * scale).reshape(D_HEADS, B, L, R), 0, 1).astype(MXU_DTYPE)
    qr_ = jnp.moveaxis(q_rope * scale, 2, 1).astype(MXU_DTYPE)
    out_lat = mla_attention(qa_, qr_, _join(pa[2], c_kv, lkp), _join(pa[3], k_rope, lkp),
                            grp.lp, grp.lk, grp.tq, tk)
    out_lat = jnp.moveaxis(out_lat, 1, 0).reshape(D_HEADS, B * L, R)
    out_d = bmm(out_lat, jnp.transpose(w_uv, (1, 0, 2)).astype(MXU_DTYPE), out_dtype=MXU_DTYPE)
    out_d = jnp.moveaxis(out_d, 0, 1).reshape(B * L, D_HEADS * D_V)
    out = jnp.concatenate([out_c.reshape(B * L, -1), out_d], axis=-1)
    return out, new_rows


def _route(h, w_router, r_bias):
    T = h.shape[0]
    E = w_router.shape[1]
    scores = jax.nn.sigmoid(matmul(h, w_router, precise=True, tn_target=E))
    biased = scores + r_bias
    grouped = biased.reshape(T, N_GROUPS, E // N_GROUPS)
    group_score = jnp.sum(lax.top_k(grouped, 2)[0], axis=-1)
    _, gidx = lax.top_k(group_score, TOPK_GROUPS)
    gmask = jnp.sum(jax.nn.one_hot(gidx, N_GROUPS, dtype=F32), axis=1) > 0
    emask = jnp.repeat(gmask, E // N_GROUPS, axis=1)
    _, eidx = lax.top_k(jnp.where(emask, biased, -jnp.inf), TOP_K)
    wsel = jnp.take_along_axis(scores, eidx, axis=-1)
    wsel = wsel / jnp.sum(wsel, axis=-1, keepdims=True) * ROUTED_SCALE
    return eidx, wsel


def _moe(h, w_router, r_bias, w1, w3, w2, ws1, ws3, ws2):
    T, D = h.shape
    E = w1.shape[0]
    tm = 256
    eidx, wsel = _route(h, w_router, r_bias)
    hb = h.astype(MXU_DTYPE)

    onehot = jnp.sum(jax.nn.one_hot(eidx, E, dtype=jnp.int32), axis=1)
    before = jnp.cumsum(onehot, axis=0) - onehot
    counts = jnp.sum(onehot, axis=0)
    tiles_per = (counts + tm - 1) // tm
    tile_end = jnp.cumsum(tiles_per)
    pad_off = (tile_end - tiles_per) * tm
    dest = pad_off[eidx] + jnp.take_along_axis(before, eidx, axis=1)
    n_tiles = (T * TOP_K) // tm + E
    P = n_tiles * tm
    tok = jnp.broadcast_to(jnp.arange(T, dtype=jnp.int32)[:, None], eidx.shape)
    row_token = jnp.zeros((P,), jnp.int32).at[dest.reshape(-1)].set(tok.reshape(-1))
    row_gate = jnp.zeros((P,), F32).at[dest.reshape(-1)].set(wsel.reshape(-1))
    used = tile_end[-1].astype(jnp.int32)
    tile_ids = jnp.minimum(jnp.arange(n_tiles, dtype=jnp.int32), used - 1)
    tile_expert = jnp.searchsorted(tile_end, tile_ids, side="right").astype(jnp.int32)

    y_rows = grouped_ffn(tile_expert, used.reshape(1), hb[row_token], row_gate[:, None],
                         w1.astype(MXU_DTYPE), w3.astype(MXU_DTYPE), w2.astype(MXU_DTYPE), tm)
    y = jnp.sum(y_rows[dest], axis=1)

    ts = _pick(T, 1024, SUBLANES)
    y_shared = grouped_ffn(jnp.zeros((T // ts,), jnp.int32), jnp.full((1,), T // ts, jnp.int32), hb,
                           jnp.ones((T, 1), F32), ws1[None].astype(MXU_DTYPE), ws3[None].astype(MXU_DTYPE),
                           ws2[None].astype(MXU_DTYPE), ts)
    return y_shared + y


def kernel(x_prompt, x_sample, c_prompt, c_sample, cache_a_k, cache_a_v, cache_a_kidx, cache_b_k, cache_b_v, cache_b_logf, cache_c_k, cache_c_v, cache_d_latent, cache_d_krope, w_in_even, b_forget, w_out_even, w_in_odd, c_lambda, c_subln, d_q_norm, d_w_uq, d_kv_norm, d_w_uk, d_w_uv, w_out_odd, ada_mix_w, ada_mix_b, ln_mix_g, ln_mix_b, ada_ffn_w, ada_ffn_b, ln_ffn_g, ln_ffn_b, router_w, router_bias, moe_w1, moe_w3, moe_w2, shared_w1, shared_w3, shared_w2):
    B, L, D = x_prompt.shape
    Bs, Ls, _ = x_sample.shape
    depth = ada_mix_w.shape[0]
    alpha = (2 * depth) ** 0.25
    past_len = cache_a_k.shape[2]
    grp_p = _Group(B, L, 0)
    grp_s = _Group(Bs, Ls, past_len)
    Tp = B * L
    c_act = jax.nn.silu(jnp.concatenate([c_prompt, c_sample], axis=0))

    def modulate(xp, xs, ada_w, ada_b):
        mod = matmul(c_act, ada_w) + ada_b
        shift, scale, gate = jnp.split(mod, 3, axis=-1)
        per_tok = lambda m: (m[:B, None, :], m[B:, None, :])
        (sh_p, sh_s), (sc_p, sc_s), gates = per_tok(shift), per_tok(scale), per_tok(gate)
        h = jnp.concatenate([(xp * (1 + sc_p) + sh_p).reshape(Tp, D),
                             (xs * (1 + sc_s) + sh_s).reshape(Bs * Ls, D)], axis=0)
        return h, gates

    def close(xp, xs, y, gates, g, b):
        yp = y[:Tp].reshape(B, L, D)
        ys = y[Tp:].reshape(Bs, Ls, D)
        return (_layer_norm(alpha * xp + (1 + gates[0]) * yp, g, b),
                _layer_norm(alpha * xs + (1 + gates[1]) * ys, g, b))

    xp, xs = x_prompt, x_sample
    ev_p, ev_s, od_p, od_s = [], [], [], []
    for i in range(depth):
        j = i // 2
        h, gates = modulate(xp, xs, ada_mix_w[i], ada_mix_b[i])
        hb = h.astype(MXU_DTYPE)
        if i % 2 == 0:
            n_in = w_in_even.shape[2]
            n_pad = -(-n_in // LANES) * LANES
            w_in = jnp.pad(w_in_even[j], ((0, 0), (0, n_pad - n_in))).astype(MXU_DTYPE)
            proj = matmul(hb, w_in)
            past = (cache_a_k[j], cache_a_v[j], cache_a_kidx[j], cache_b_k[j], cache_b_v[j], cache_b_logf[j])
            out_p, rows_p = _even_attend(grp_p, proj[:Tp].reshape(B, L, n_pad), None, b_forget[j])
            out_s, rows_s = _even_attend(grp_s, proj[Tp:].reshape(Bs, Ls, n_pad), past, b_forget[j])
            ev_p.append(rows_p)
            ev_s.append(rows_s)
            w_out = w_out_even[j]
        else:
            lam_init = 0.8 - 0.6 * math.exp(-0.3 * i)
            lam_f = c_lambda[j]
            lam = (jnp.exp(jnp.sum(lam_f[0] * lam_f[1])) - jnp.exp(jnp.sum(lam_f[2] * lam_f[3])) + lam_init).reshape(1)
            n_in = w_in_odd.shape[2]
            proj = matmul(hb, w_in_odd[j].astype(MXU_DTYPE))
            past = (cache_c_k[j], cache_c_v[j], cache_d_latent[j], cache_d_krope[j])
            args = (lam, c_subln[j], d_q_norm[j], d_w_uq[j], d_kv_norm[j], d_w_uk[j], d_w_uv[j], lam_init)
            out_p, rows_p = _odd_attend(grp_p, proj[:Tp].reshape(B, L, n_in), None, *args)
            out_s, rows_s = _odd_attend(grp_s, proj[Tp:].reshape(Bs, Ls, n_in), past, *args)
            od_p.append(rows_p)
            od_s.append(rows_s)
            w_out = w_out_odd[j]
        y = matmul(jnp.concatenate([out_p, out_s], axis=0), w_out.astype(MXU_DTYPE))
        xp, xs = close(xp, xs, y, gates, ln_mix_g[i], ln_mix_b[i])

        h, gates = modulate(xp, xs, ada_ffn_w[i], ada_ffn_b[i])
        y = _moe(h, router_w[i], router_bias[i], moe_w1[i], moe_w3[i], moe_w2[i],
                 shared_w1[i], shared_w3[i], shared_w2[i])
        xp, xs = close(xp, xs, y, gates, ln_ffn_g[i], ln_ffn_b[i])

    stack = lambda rows, idx: jnp.stack([r[idx] for r in rows])
    outs = [xp, xs]
    for idx in range(6):
        outs += [stack(ev_p, idx), stack(ev_s, idx)]
    for idx in range(4):
        outs += [stack(od_p, idx), stack(od_s, idx)]
    return tuple(outs)
```

```python
import functools
import math

import jax
import jax.numpy as jnp
from jax import lax
from jax.experimental import pallas as pl
from jax.experimental.pallas import tpu as pltpu

CHUNK = 64
ROPE_THETA = 10000.0
HEAD_DIM = 128
A_HEADS = 8
A_KV_HEADS = 2
IDX_HEADS = 16
IDX_DIM = 128
A_TOPK_MAX = 256
B_HEADS = 8
C_HEADS = 8
C_QK_DIM = 64
C_V_DIM = 128
D_HEADS = 8
D_NOPE = 128
D_ROPE = 64
D_V = 128
N_GROUPS = 8
TOPK_GROUPS = 4
TOP_K = 8
ROUTED_SCALE = 2.5
NORM_EPS = 1e-6

LANES = 128
SUBLANES = 8
MXU_WIDTH = 256
VMEM_LIMIT_BYTES = 56 * 2**20
MXU_DTYPE = jnp.bfloat16

F32 = jnp.float32
I32 = jnp.int32
NEG = -1e30
INT_MIN = -2**31
KEY_BLOCK = 512

EVEN_COLS = dict(qi=0, qa=2048, ka=3072, ki=3328, va=3584, qf=3840, kf=4864, vf=5888, misc=6912)
EVEN_WIDTH = 7168
EVEN_ROPE_COLS = 3584
MISC_WI, MISC_FL = 0, IDX_HEADS
ODD_COLS = dict(qc=0, kc=1024, kr=2048, vc=2304, ckv=3328, qd=3584)
ODD_WIDTH = 4096
ODD_ROPE_COLS = 2304


def _pick(n, target, mult):
    best = None
    for d in range(mult, min(n, target) + 1, mult):
        if n % d == 0:
            best = d
    return n if best is None else best


def _params(n_axes):
    return pltpu.CompilerParams(dimension_semantics=("arbitrary",) * n_axes,
                                vmem_limit_bytes=VMEM_LIMIT_BYTES)


def _mx(a):
    return a.astype(MXU_DTYPE)


def _dot(a, b):
    return jnp.dot(_mx(a), _mx(b), preferred_element_type=F32)


def _dot_nt(a, b, precision=None):
    if precision is None:
        a, b = _mx(a), _mx(b)
    return lax.dot_general(a, b, (((1,), (1,)), ((), ())), preferred_element_type=F32, precision=precision)


def _mm_body(a_ref, b_ref, o_ref, *, precise):
    if precise:
        o = jnp.dot(a_ref[...], b_ref[...], preferred_element_type=F32, precision=lax.Precision.HIGHEST)
    else:
        o = _dot(a_ref[...], b_ref[...])
    o_ref[...] = o.astype(o_ref.dtype)


def matmul(a, b, out_dtype=F32, precise=False, tm_target=1024, tn_target=512):
    M, K = a.shape
    N = b.shape[1]
    tm = _pick(M, tm_target, SUBLANES)
    tn = _pick(N, tn_target, LANES)
    return pl.pallas_call(
        functools.partial(_mm_body, precise=precise),
        grid=(M // tm, N // tn),
        in_specs=[pl.BlockSpec((tm, K), lambda i, j: (i, 0)),
                  pl.BlockSpec((K, tn), lambda i, j: (0, j))],
        out_specs=pl.BlockSpec((tm, tn), lambda i, j: (i, j)),
        out_shape=jax.ShapeDtypeStruct((M, N), out_dtype),
        compiler_params=_params(2),
        name="matmul",
    )(a, b)


def _bmm_body(a_ref, b_ref, o_ref):
    o_ref[0] = jnp.dot(a_ref[0], b_ref[0], preferred_element_type=F32,
                       precision=lax.Precision.HIGHEST).astype(o_ref.dtype)


def bmm_precise(a, b):
    H, M, K = a.shape
    N = b.shape[2]
    return pl.pallas_call(
        _bmm_body,
        grid=(H,),
        in_specs=[pl.BlockSpec((1, M, K), lambda h: (h, 0, 0)),
                  pl.BlockSpec((1, K, N), lambda h: (h, 0, 0))],
        out_specs=pl.BlockSpec((1, M, N), lambda h: (h, 0, 0)),
        out_shape=jax.ShapeDtypeStruct((H, M, N), F32),
        compiler_params=_params(1),
        name="bmm_precise",
    )(a, b)


def _per_tile(table, tm, group):
    return table.reshape(table.shape[0] * group // tm, tm // group, table.shape[1])


def _modulated(x_ref, sc_ref, sh_ref, group):
    parts = []
    for g in range(x_ref.shape[0] // group):
        rows = slice(g * group, (g + 1) * group)
        parts.append(x_ref[rows, :] * (1.0 + sc_ref[0, g:g + 1, :]) + sh_ref[0, g:g + 1, :])
    return parts


def _rope_lanes(x, cos, sin, half):
    lane = lax.broadcasted_iota(I32, x.shape, 1)
    lower = (lane % (2 * half)) < half
    partner = jnp.where(lower, pltpu.roll(x, LANES - half, axis=1), pltpu.roll(x, half, axis=1))
    return x * cos + partner * sin


def _layer_norm_rows(r, g, b):
    rc = r - jnp.mean(r, axis=-1, keepdims=True)
    var = jnp.mean(rc * rc, axis=-1, keepdims=True)
    return rc * lax.rsqrt(var + NORM_EPS) * g + b


def _gated_residual(x_ref, y, gate_ref, group, alpha):
    parts = []
    for g in range(x_ref.shape[0] // group):
        rows = slice(g * group, (g + 1) * group)
        parts.append(alpha * x_ref[rows, :] + (1.0 + gate_ref[0, g:g + 1, :]) * y[rows, :])
    return jnp.concatenate(parts, axis=0)


def _proj_body(kind_ref, x_ref, sc_ref, sh_ref, w_ref, cos_ref, sin_ref, o_ref, h_ref, *, group, half):
    j = pl.program_id(1)

    @pl.when(j == 0)
    def _():
        for g, part in enumerate(_modulated(x_ref, sc_ref, sh_ref, group)):
            h_ref[g * group:(g + 1) * group, :] = part.astype(h_ref.dtype)

    acc = jnp.dot(h_ref[...], w_ref[...], preferred_element_type=F32)
    n_sub = acc.shape[1] // MXU_WIDTH
    for c in range(n_sub):
        cols = slice(c * MXU_WIDTH, (c + 1) * MXU_WIDTH)
        rotate = kind_ref[j * n_sub + c] == 1

        @pl.when(rotate)
        def _():
            for p in range(MXU_WIDTH // LANES):
                lanes = slice(c * MXU_WIDTH + p * LANES, c * MXU_WIDTH + (p + 1) * LANES)
                o_ref[:, lanes] = _rope_lanes(acc[:, lanes], cos_ref[...], sin_ref[...], half)

        @pl.when(jnp.logical_not(rotate))
        def _():
            o_ref[:, cols] = acc[:, cols]


def project(x, scale_g, shift_g, w, cos, sin, rope_cols, half, group, tm):
    T, D = x.shape
    N = w.shape[1]
    tn = 2 * MXU_WIDTH
    kinds = (jnp.arange(N // MXU_WIDTH) * MXU_WIDTH < rope_cols).astype(I32)
    row = lambda i, j, k: (i, 0)
    return pl.pallas_call(
        functools.partial(_proj_body, group=group, half=half),
        grid_spec=pltpu.PrefetchScalarGridSpec(
            num_scalar_prefetch=1,
            grid=(T // tm, N // tn),
            in_specs=[pl.BlockSpec((tm, D), row),
                      pl.BlockSpec((1, tm // group, D), lambda i, j, k: (i, 0, 0)),
                      pl.BlockSpec((1, tm // group, D), lambda i, j, k: (i, 0, 0)),
                      pl.BlockSpec((D, tn), lambda i, j, k: (0, j)),
                      pl.BlockSpec((tm, LANES), row),
                      pl.BlockSpec((tm, LANES), row)],
            out_specs=pl.BlockSpec((tm, tn), lambda i, j, k: (i, j)),
            scratch_shapes=[pltpu.VMEM((tm, D), MXU_DTYPE)]),
        out_shape=jax.ShapeDtypeStruct((T, N), F32),
        compiler_params=_params(2),
        name="project",
    )(kinds, x, _per_tile(scale_g, tm, group), _per_tile(shift_g, tm, group), w, cos, sin)


def _close_mix_body(a1_ref, a2_ref, w1_ref, w2_ref, x_ref, gate_ref, g_ref, b_ref, o_ref, *, group, alpha):
    y = _dot(a1_ref[...], w1_ref[...]) + _dot(a2_ref[...], w2_ref[...])
    r = _gated_residual(x_ref, y, gate_ref, group, alpha)
    o_ref[...] = _layer_norm_rows(r, g_ref[...], b_ref[...])


def close_mixer(a1, a2, w1, w2, x, gate_g, ln_g, ln_b, group, alpha, tm):
    T, D = x.shape
    row = lambda i: (i, 0)
    fixed = lambda i: (0, 0)
    return pl.pallas_call(
        functools.partial(_close_mix_body, group=group, alpha=alpha),
        grid=(T // tm,),
        in_specs=[pl.BlockSpec((tm, a1.shape[1]), row),
                  pl.BlockSpec((tm, a2.shape[1]), row),
                  pl.BlockSpec(w1.shape, fixed),
                  pl.BlockSpec(w2.shape, fixed),
                  pl.BlockSpec((tm, D), row),
                  pl.BlockSpec((1, tm // group, D), lambda i: (i, 0, 0)),
                  pl.BlockSpec((1, D), fixed),
                  pl.BlockSpec((1, D), fixed)],
        out_specs=pl.BlockSpec((tm, D), row),
        out_shape=jax.ShapeDtypeStruct((T, D), F32),
        compiler_params=_params(1),
        name="close_mixer",
    )(a1, a2, w1, w2, x, _per_tile(gate_g, tm, group), ln_g, ln_b)


def _online_step(s, v, carry):
    m, l, acc = carry
    m_new = jnp.maximum(m, jnp.max(s, axis=-1, keepdims=True))
    p = jnp.exp(s - m_new)
    alpha = jnp.exp(m - m_new)
    l = alpha * l + jnp.sum(p, axis=-1, keepdims=True)
    acc = alpha * acc + _dot(p, v)
    return m_new, l, acc


def _init_carry(tq, dv):
    return (jnp.full((tq, 1), NEG, F32), jnp.zeros((tq, 1), F32), jnp.zeros((tq, dv), F32))


def _chunk_last_key(i, tq):
    return ((i + 1) * tq - 1) // CHUNK * CHUNK + CHUNK - 1


class _Group:
    def __init__(self, batch, lq, lp, row0):
        self.batch, self.lq, self.lp, self.row0 = batch, lq, lp, row0
        self.lk = lp + lq
        self.tq = _pick(lq, 256, SUBLANES)
        self.tq_dsa = _pick(lq, 128, SUBLANES)
        self.tk = _pick(lq, KEY_BLOCK, SUBLANES)
        self.tkp = _pick(lp, KEY_BLOCK, LANES) if lp else 0
        self.n_past_blocks = lp // self.tkp if lp else 0
        self.topk = min(A_TOPK_MAX, self.lk // 4)
        assert lp % CHUNK == 0 and row0 % lq == 0 and lq % self.tk == 0

    def qrow(self, tq):
        base, per = self.row0 // tq, self.lq // tq
        return lambda b, i: base + b * per + i

    def krow(self):
        base = self.row0 // self.lq
        return lambda b: base + b


def _new_key_blocks(i, tq, grp_lq, tk, causal_last):
    return jnp.minimum(causal_last, grp_lq - 1) // tk + 1


def _fox_body(*refs, tq, tk, tkp, lp, lq, n_past):
    if n_past:
        q_ref, k_ref, v_ref, pk_ref, pv_ref, fq_ref, fk_ref, o_ref = refs
    else:
        q_ref, k_ref, v_ref, fq_ref, fk_ref, o_ref = refs
    i = pl.program_id(2)
    q = _mx(q_ref[...] * HEAD_DIM ** -0.5)
    fq = fq_ref[0]
    carry = _init_carry(tq, HEAD_DIM)

    if n_past:
        def past_step(j, c):
            off = pl.multiple_of(j * tkp, tkp)
            s = _dot_nt(q, pk_ref[0, pl.ds(off, tkp), :]) + fq - fk_ref[0, j][:, :tkp]
            return _online_step(s, pv_ref[0, pl.ds(off, tkp), :], c)
        carry = lax.fori_loop(0, n_past, past_step, carry)

    qpos = i * tq + lax.broadcasted_iota(I32, (tq, tk), 0)

    def new_step(j, c):
        off = pl.multiple_of(j * tk, tk)
        s = _dot_nt(q, k_ref[pl.ds(off, tk), :]) + fq - fk_ref[0, n_past + j][:, :tk]
        kpos = off + lax.broadcasted_iota(I32, (tq, tk), 1)
        s = jnp.where(kpos <= qpos, s, NEG)
        return _online_step(s, v_ref[pl.ds(off, tk), :], c)

    nb = _new_key_blocks(i, tq, lq, tk, (i + 1) * tq - 1)
    m, l, acc = lax.fori_loop(0, nb, new_step, carry)
    o_ref[...] = (acc / l).astype(o_ref.dtype)


def fox_attention(grp, proj, fq, fk, past_k, past_v):
    B, H, d = grp.batch, B_HEADS, HEAD_DIM
    tq = grp.tq
    qrow, krow = grp.qrow(tq), grp.krow()
    cq, ck, cv = (EVEN_COLS[n] // d for n in ("qf", "kf", "vf"))
    in_specs = [pl.BlockSpec((tq, d), lambda b, h, i: (qrow(b, i), cq + h)),
                pl.BlockSpec((grp.lq, d), lambda b, h, i: (krow(b), ck + h)),
                pl.BlockSpec((grp.lq, d), lambda b, h, i: (krow(b), cv + h))]
    args = [proj, proj, proj]
    if grp.lp:
        in_specs += [pl.BlockSpec((1, grp.lp, d), lambda b, h, i: (b, 0, h))] * 2
        args += [past_k, past_v]
    in_specs += [pl.BlockSpec((1, tq, 1), lambda b, h, i: (b * H + h, i, 0)),
                 pl.BlockSpec((1,) + fk.shape[1:], lambda b, h, i: (b * H + h, 0, 0, 0))]
    args += [fq, fk]
    return pl.pallas_call(
        functools.partial(_fox_body, tq=tq, tk=grp.tk, tkp=grp.tkp, lp=grp.lp, lq=grp.lq,
                          n_past=grp.n_past_blocks),
        grid=(B, H, grp.lq // tq),
        in_specs=in_specs,
        out_specs=pl.BlockSpec((tq, d), lambda b, h, i: (b * (grp.lq // tq) + i, h)),
        out_shape=jax.ShapeDtypeStruct((B * grp.lq, H * d), MXU_DTYPE),
        compiler_params=_params(3),
        name="fox_attention",
    )(*args)


def _diff_body(*refs, tq, tk, tkp, lp, lq, n_past, out_scale):
    if n_past:
        lam_ref, q_ref, k_ref, v_ref, pk_ref, pv_ref, g_ref, o_ref = refs
    else:
        lam_ref, q_ref, k_ref, v_ref, g_ref, o_ref = refs
    i = pl.program_id(2)
    q = q_ref[...] * C_QK_DIM ** -0.5
    lane = lax.broadcasted_iota(I32, q.shape, 1)
    q0 = _mx(jnp.where(lane < C_QK_DIM, q, 0.0))
    q1 = _mx(jnp.where(lane >= C_QK_DIM, q, 0.0))
    c0 = c1 = _init_carry(tq, C_V_DIM)

    if n_past:
        def past_step(j, cc):
            off = pl.multiple_of(j * tkp, tkp)
            k = _mx(pk_ref[0, pl.ds(off, tkp), :])
            v = _mx(pv_ref[0, pl.ds(off, tkp), :])
            return _online_step(_dot_nt(q0, k), v, cc[0]), _online_step(_dot_nt(q1, k), v, cc[1])
        c0, c1 = lax.fori_loop(0, n_past, past_step, (c0, c1))

    qchunk = (i * tq + lax.broadcasted_iota(I32, (tq, tk), 0)) // CHUNK

    def new_step(j, cc):
        off = pl.multiple_of(j * tk, tk)
        k = _mx(k_ref[pl.ds(off, tk), :])
        v = _mx(v_ref[pl.ds(off, tk), :])
        ok = (off + lax.broadcasted_iota(I32, (tq, tk), 1)) // CHUNK <= qchunk
        s0 = jnp.where(ok, _dot_nt(q0, k), NEG)
        s1 = jnp.where(ok, _dot_nt(q1, k), NEG)
        return _online_step(s0, v, cc[0]), _online_step(s1, v, cc[1])

    nb = _new_key_blocks(i, tq, lq, tk, _chunk_last_key(i, tq))
    (m0, l0, a0), (m1, l1, a1) = lax.fori_loop(0, nb, new_step, (c0, c1))
    o = a0 / l0 - lam_ref[0] * (a1 / l1)
    o = o * lax.rsqrt(jnp.mean(o * o, axis=-1, keepdims=True) + NORM_EPS)
    o_ref[...] = (o * g_ref[...] * out_scale).astype(o_ref.dtype)


def diff_attention(grp, proj, lam, subln, out_scale, past_k, past_v):
    B, H, d = grp.batch, C_HEADS, C_V_DIM
    tq = grp.tq
    qrow, krow = grp.qrow(tq), grp.krow()
    cq, ck, cv = (ODD_COLS[n] // d for n in ("qc", "kc", "vc"))
    in_specs = [pl.BlockSpec(memory_space=pltpu.SMEM),
                pl.BlockSpec((tq, d), lambda b, h, i: (qrow(b, i), cq + h)),
                pl.BlockSpec((grp.lq, d), lambda b, h, i: (krow(b), ck + h)),
                pl.BlockSpec((grp.lq, d), lambda b, h, i: (krow(b), cv + h))]
    args = [lam, proj, proj, proj]
    if grp.lp:
        in_specs += [pl.BlockSpec((1, grp.lp, d), lambda b, h, i: (b, 0, h))] * 2
        args += [past_k, past_v]
    in_specs += [pl.BlockSpec((1, d), lambda b, h, i: (0, 0))]
    args += [subln]
    return pl.pallas_call(
        functools.partial(_diff_body, tq=tq, tk=grp.tk, tkp=grp.tkp, lp=grp.lp, lq=grp.lq,
                          n_past=grp.n_past_blocks, out_scale=out_scale),
        grid=(B, H, grp.lq // tq),
        in_specs=in_specs,
        out_specs=pl.BlockSpec((tq, d), lambda b, h, i: (b * (grp.lq // tq) + i, h)),
        out_shape=jax.ShapeDtypeStruct((B * grp.lq, H * d), MXU_DTYPE),
        compiler_params=_params(3),
        name="diff_attention",
    )(*args)


def _mla_prep_body(qd_ref, ckv_ref, qg_ref, kg_ref, w_ref, cos_ref, sin_ref, qa_ref, qr_ref, lat_ref, *, n_abs):
    qd = qd_ref[...]
    qn = qd * lax.rsqrt(jnp.mean(qd * qd, axis=-1, keepdims=True) + NORM_EPS) * qg_ref[...]
    q = _dot(qn, w_ref[...]) * (D_NOPE + D_ROPE) ** -0.5
    qa_ref[...] = q[:, :n_abs].astype(qa_ref.dtype)
    for h in range((q.shape[1] - n_abs) // LANES):
        lanes = slice(n_abs + h * LANES, n_abs + (h + 1) * LANES)
        qr_ref[:, h * LANES:(h + 1) * LANES] = _rope_lanes(
            q[:, lanes], cos_ref[...], sin_ref[...], D_ROPE // 2).astype(qr_ref.dtype)
    ckv = ckv_ref[...]
    lat_ref[...] = ckv * lax.rsqrt(jnp.mean(ckv * ckv, axis=-1, keepdims=True) + NORM_EPS) * kg_ref[...]


def mla_prepare(proj, q_norm, kv_norm, w_q, cos, sin, tm):
    T = proj.shape[0]
    q_lora, R = q_norm.shape[1], kv_norm.shape[1]
    n_abs = D_HEADS * R
    n_rope = D_HEADS * LANES
    row = lambda i: (i, 0)
    fixed = lambda i: (0, 0)
    return pl.pallas_call(
        functools.partial(_mla_prep_body, n_abs=n_abs),
        grid=(T // tm,),
        in_specs=[pl.BlockSpec((tm, q_lora), lambda i: (i, ODD_COLS["qd"] // q_lora)),
                  pl.BlockSpec((tm, R), lambda i: (i, ODD_COLS["ckv"] // R)),
                  pl.BlockSpec((1, q_lora), fixed),
                  pl.BlockSpec((1, R), fixed),
                  pl.BlockSpec(w_q.shape, fixed),
                  pl.BlockSpec((tm, LANES), row),
                  pl.BlockSpec((tm, LANES), row)],
        out_specs=[pl.BlockSpec((tm, n_abs), row), pl.BlockSpec((tm, n_rope), row), pl.BlockSpec((tm, R), row)],
        out_shape=[jax.ShapeDtypeStruct((T, n_abs), MXU_DTYPE), jax.ShapeDtypeStruct((T, n_rope), MXU_DTYPE),
                   jax.ShapeDtypeStruct((T, R), F32)],
        compiler_params=_params(1),
        name="mla_prepare",
    )(proj, proj, q_norm, kv_norm, w_q, cos, sin)


def _mla_body(*refs, tq, tk, tkp, lp, lq, n_past):
    if n_past:
        qa_ref, qr_ref, lat_ref, kr_ref, plat_ref, pkr_ref, o_ref = refs
    else:
        qa_ref, qr_ref, lat_ref, kr_ref, o_ref = refs
    i = pl.program_id(2)
    qa = qa_ref[...]
    qr = qr_ref[...]
    carry = _init_carry(tq, lat_ref.shape[-1])

    if n_past:
        def past_step(j, c):
            off = pl.multiple_of(j * tkp, tkp)
            lat = _mx(plat_ref[0, pl.ds(off, tkp), :])
            s = _dot_nt(qa, lat) + _dot_nt(qr[:, :D_ROPE], pkr_ref[0, pl.ds(off, tkp), :])
            return _online_step(s, lat, c)
        carry = lax.fori_loop(0, n_past, past_step, carry)

    qchunk = (i * tq + lax.broadcasted_iota(I32, (tq, tk), 0)) // CHUNK

    def new_step(j, c):
        off = pl.multiple_of(j * tk, tk)
        lat = _mx(lat_ref[pl.ds(off, tk), :])
        ok = (off + lax.broadcasted_iota(I32, (tq, tk), 1)) // CHUNK <= qchunk
        s = jnp.where(ok, _dot_nt(qa, lat) + _dot_nt(qr, kr_ref[pl.ds(off, tk), :]), NEG)
        return _online_step(s, lat, c)

    nb = _new_key_blocks(i, tq, lq, tk, _chunk_last_key(i, tq))
    m, l, acc = lax.fori_loop(0, nb, new_step, carry)
    o_ref[...] = (acc / l).astype(o_ref.dtype)


def mla_attention(grp, q_abs, q_rope, lat, proj, past_lat, past_kr):
    B, H = grp.batch, D_HEADS
    R = lat.shape[1]
    tq = grp.tq
    qrow, krow = grp.qrow(tq), grp.krow()
    in_specs = [pl.BlockSpec((tq, R), lambda b, h, i: (qrow(b, i), h)),
                pl.BlockSpec((tq, LANES), lambda b, h, i: (qrow(b, i), h)),
                pl.BlockSpec((grp.lq, R), lambda b, h, i: (krow(b), 0)),
                pl.BlockSpec((grp.lq, LANES), lambda b, h, i: (krow(b), ODD_COLS["kr"] // LANES))]
    args = [q_abs, q_rope, lat, proj]
    if grp.lp:
        in_specs += [pl.BlockSpec((1, grp.lp, R), lambda b, h, i: (b, 0, 0)),
                     pl.BlockSpec((1, grp.lp, D_ROPE), lambda b, h, i: (b, 0, 0))]
        args += [past_lat, past_kr]
    return pl.pallas_call(
        functools.partial(_mla_body, tq=tq, tk=grp.tk, tkp=grp.tkp, lp=grp.lp, lq=grp.lq, n_past=grp.n_past_blocks),
        grid=(B, H, grp.lq // tq),
        in_specs=in_specs,
        out_specs=pl.BlockSpec((tq, R), lambda b, h, i: (b * (grp.lq // tq) + i, h)),
        out_shape=jax.ShapeDtypeStruct((B * grp.lq, H * R), MXU_DTYPE),
        compiler_params=_params(3),
        name="mla_attention",
    )(*args)


def _sortable(score):
    bits = pltpu.bitcast(score, I32)
    return jnp.where(bits < 0, bits ^ jnp.int32(0x7FFFFFFF), bits)


def _dsa_body(*refs, tq, lp, lq, topk, has_past):
    if has_past:
        qi_ref, misc_ref, q_ref, ki_ref, k_ref, v_ref, pki_ref, pk_ref, pv_ref, o_ref = refs
    else:
        qi_ref, misc_ref, q_ref, ki_ref, k_ref, v_ref, o_ref = refs
    i = pl.program_id(1)
    wi = misc_ref[:, MISC_WI:MISC_WI + IDX_HEADS] * (IDX_HEADS * IDX_DIM) ** -0.5

    qchunk = (i * tq + lax.broadcasted_iota(I32, (tq, lq), 0)) // CHUNK
    adm_new = lax.broadcasted_iota(I32, (tq, lq), 1) // CHUNK <= qchunk
    segs = [(_mx(ki_ref[...]), k_ref, v_ref, adm_new)]
    if has_past:
        segs.insert(0, (_mx(pki_ref[0]), pk_ref.at[0], pv_ref.at[0], None))

    keys = []
    for ki, _, _, adm in segs:
        score = jnp.zeros((tq, ki.shape[0]), F32)
        for h in range(IDX_HEADS):
            s = _dot_nt(qi_ref[:, h * IDX_DIM:(h + 1) * IDX_DIM], ki)
            score = score + jnp.maximum(s, 0.0) * wi[:, h:h + 1]
        key = _sortable(score)
        keys.append(key if adm is None else jnp.where(adm, key, INT_MIN))

    def count_ge(cand):
        return sum(jnp.sum(jnp.where(key >= cand, 1.0, 0.0), axis=-1, keepdims=True) for key in keys)

    kf = float(topk)
    t = jnp.where(count_ge(jnp.zeros((tq, 1), I32)) >= kf, 0, INT_MIN).astype(I32)

    def bit_step(it, t):
        cand = t | jnp.left_shift(jnp.int32(1), 30 - it)
        return jnp.where(count_ge(cand) >= kf, cand, t)

    t = lax.fori_loop(0, 31, bit_step, t)
    sels = [(key >= t) if adm is None else (adm & (key >= t)) for key, (_, _, _, adm) in zip(keys, segs)]

    rep = A_HEADS // A_KV_HEADS
    for g in range(A_KV_HEADS):
        kv = [(_mx(k[:, g * HEAD_DIM:(g + 1) * HEAD_DIM]), _mx(v[:, g * HEAD_DIM:(g + 1) * HEAD_DIM]))
              for _, k, v, _ in segs]
        for r in range(rep):
            h = g * rep + r
            q = _mx(q_ref[:, h * HEAD_DIM:(h + 1) * HEAD_DIM] * HEAD_DIM ** -0.5)
            ss = [jnp.where(sel, _dot_nt(q, k), NEG) for sel, (k, _) in zip(sels, kv)]
            m = functools.reduce(jnp.maximum, [jnp.max(s, axis=-1, keepdims=True) for s in ss])
            ps = [jnp.exp(s - m) for s in ss]
            l = sum(jnp.sum(p, axis=-1, keepdims=True) for p in ps)
            o = sum(_dot(p, v) for p, (_, v) in zip(ps, kv))
            o_ref[:, h * HEAD_DIM:(h + 1) * HEAD_DIM] = (o / l).astype(o_ref.dtype)


def dsa_attention(grp, proj, past_ki, past_k, past_v):
    B = grp.batch
    tq = grp.tq_dsa
    qrow, krow = grp.qrow(tq), grp.krow()
    wq, wkv = A_HEADS * HEAD_DIM, A_KV_HEADS * HEAD_DIM
    wqi = IDX_HEADS * IDX_DIM
    c = EVEN_COLS
    in_specs = [pl.BlockSpec((tq, wqi), lambda b, i: (qrow(b, i), c["qi"] // wqi)),
                pl.BlockSpec((tq, LANES), lambda b, i: (qrow(b, i), c["misc"] // LANES)),
                pl.BlockSpec((tq, wq), lambda b, i: (qrow(b, i), c["qa"] // wq)),
                pl.BlockSpec((grp.lq, IDX_DIM), lambda b, i: (krow(b), c["ki"] // IDX_DIM)),
                pl.BlockSpec((grp.lq, wkv), lambda b, i: (krow(b), c["ka"] // wkv)),
                pl.BlockSpec((grp.lq, wkv), lambda b, i: (krow(b), c["va"] // wkv))]
    args = [proj] * 6
    if grp.lp:
        in_specs += [pl.BlockSpec((1, grp.lp, IDX_DIM), lambda b, i: (b, 0, 0)),
                     pl.BlockSpec((1, grp.lp, wkv), lambda b, i: (b, 0, 0)),
                     pl.BlockSpec((1, grp.lp, wkv), lambda b, i: (b, 0, 0))]
        args += [past_ki, past_k, past_v]
    return pl.pallas_call(
        functools.partial(_dsa_body, tq=tq, lp=grp.lp, lq=grp.lq, topk=grp.topk, has_past=bool(grp.lp)),
        grid=(B, grp.lq // tq),
        in_specs=in_specs,
        out_specs=pl.BlockSpec((tq, wq), lambda b, i: (b * (grp.lq // tq) + i, 0)),
        out_shape=jax.ShapeDtypeStruct((B * grp.lq, wq), MXU_DTYPE),
        compiler_params=_params(2),
        name="dsa_attention",
    )(*args)


def _first_index_of_max(x, iota, n):
    mx = jnp.max(x, axis=0, keepdims=True)
    return mx, jnp.min(jnp.where(x == mx, iota, n), axis=0, keepdims=True)


def _route_body(x_ref, sc_ref, sh_ref, wr_ref, bias_ref, hb_ref, gate_ref, rank_ref, cnt_ref, h32_ref, carry_ref,
                *, group):
    step = pl.program_id(0)
    tm = x_ref.shape[0]
    E = wr_ref.shape[0]
    per = E // N_GROUPS
    for g, part in enumerate(_modulated(x_ref, sc_ref, sh_ref, group)):
        h32_ref[g * group:(g + 1) * group, :] = part
    h = h32_ref[...]
    hb_ref[...] = h.astype(hb_ref.dtype)

    scores = jax.nn.sigmoid(_dot_nt(wr_ref[...], h, precision=lax.Precision.HIGHEST))
    biased = scores + bias_ref[...]
    member = lax.broadcasted_iota(I32, (per, tm), 0).astype(F32)
    gscore = []
    for g in range(N_GROUPS):
        blk = biased[g * per:(g + 1) * per, :]
        m1, i1 = _first_index_of_max(blk, member, per)
        m2 = jnp.max(jnp.where(member == i1, -jnp.inf, blk), axis=0, keepdims=True)
        gscore.append(m1 + m2)
    gscore = jnp.concatenate(gscore, axis=0)
    giota = lax.broadcasted_iota(I32, (N_GROUPS, tm), 0).astype(F32)
    gsel = jnp.zeros((N_GROUPS, tm), F32)
    for _ in range(TOPK_GROUPS):
        _, gi = _first_index_of_max(gscore, giota, N_GROUPS)
        hit = giota == gi
        gsel = jnp.where(hit, 1.0, gsel)
        gscore = jnp.where(hit, -jnp.inf, gscore)
    emask = jnp.concatenate([jnp.broadcast_to(gsel[g:g + 1, :], (per, tm)) for g in range(N_GROUPS)], axis=0)
    masked = jnp.where(emask > 0.0, biased, -jnp.inf)
    eiota = lax.broadcasted_iota(I32, (E, tm), 0).astype(F32)
    sel = jnp.zeros((E, tm), jnp.bool_)
    for _ in range(TOP_K):
        _, ei = _first_index_of_max(masked, eiota, E)
        hit = eiota == ei
        sel = sel | hit
        masked = jnp.where(hit, -jnp.inf, masked)
    w = jnp.where(sel, scores, 0.0)
    gate_ref[...] = w / jnp.sum(w, axis=0, keepdims=True) * ROUTED_SCALE

    @pl.when(step == 0)
    def _():
        carry_ref[...] = jnp.zeros(carry_ref.shape, F32)
    upper = (lax.broadcasted_iota(I32, (tm, tm), 0) <= lax.broadcasted_iota(I32, (tm, tm), 1))
    self = jnp.where(sel, 1.0, 0.0)
    incl = _dot(self, jnp.where(upper, 1.0, 0.0)) + carry_ref[...]
    rank_ref[...] = jnp.where(sel, incl - 1.0, -1.0).astype(I32)
    carry_ref[...] = incl[:, tm - 1:tm]
    cnt_ref[...] = incl[:, tm - 1:tm].astype(I32)


def route(x, scale_g, shift_g, w_router_t, r_bias, group, tm):
    T, D = x.shape
    E = w_router_t.shape[0]
    row = lambda i: (i, 0)
    col = lambda i: (0, i)
    fixed = lambda i: (0, 0)
    return pl.pallas_call(
        functools.partial(_route_body, group=group),
        grid=(T // tm,),
        in_specs=[pl.BlockSpec((tm, D), row),
                  pl.BlockSpec((1, tm // group, D), lambda i: (i, 0, 0)),
                  pl.BlockSpec((1, tm // group, D), lambda i: (i, 0, 0)),
                  pl.BlockSpec((E, D), fixed),
                  pl.BlockSpec((E, 1), fixed)],
        out_specs=[pl.BlockSpec((tm, D), row), pl.BlockSpec((E, tm), col), pl.BlockSpec((E, tm), col),
                   pl.BlockSpec((E, 1), fixed)],
        out_shape=[jax.ShapeDtypeStruct((T, D), MXU_DTYPE), jax.ShapeDtypeStruct((E, T), F32),
                   jax.ShapeDtypeStruct((E, T), I32), jax.ShapeDtypeStruct((E, 1), I32)],
        scratch_shapes=[pltpu.VMEM((tm, D), F32), pltpu.VMEM((E, 1), F32)],
        compiler_params=_params(1),
        name="route",
    )(x, _per_tile(scale_g, tm, group), _per_tile(shift_g, tm, group), w_router_t, r_bias)


def _compact_body(gate_ref, rank_ref, off_ref, dest_ref, g8_ref):
    E, tm = gate_ref.shape
    rank = rank_ref[...]
    sel = rank >= 0
    dest = (rank + off_ref[...]).astype(F32)
    gate = gate_ref[...]
    eiota = lax.broadcasted_iota(I32, (E, tm), 0).astype(F32)
    dests, gates = [], []
    for _ in range(TOP_K):
        ei = jnp.min(jnp.where(sel, eiota, E), axis=0, keepdims=True)
        hit = eiota == ei
        dests.append(jnp.sum(jnp.where(hit, dest, 0.0), axis=0, keepdims=True))
        gates.append(jnp.sum(jnp.where(hit, gate, 0.0), axis=0, keepdims=True))
        sel = sel & jnp.logical_not(hit)
    dest_ref[...] = jnp.concatenate(dests, axis=0).astype(I32)
    g8_ref[...] = jnp.concatenate(gates, axis=0)


def compact_routes(gate, rank, seg_off, tm):
    E, T = gate.shape
    col = lambda i: (0, i)
    return pl.pallas_call(
        _compact_body,
        grid=(T // tm,),
        in_specs=[pl.BlockSpec((E, tm), col), pl.BlockSpec((E, tm), col), pl.BlockSpec((E, 1), lambda i: (0, 0))],
        out_specs=[pl.BlockSpec((TOP_K, tm), col), pl.BlockSpec((TOP_K, tm), col)],
        out_shape=[jax.ShapeDtypeStruct((TOP_K, T), I32), jax.ShapeDtypeStruct((TOP_K, T), F32)],
        compiler_params=_params(1),
        name="compact_routes",
    )(gate, rank, seg_off)


def _ffn(x, w1, w3, w2):
    a = _dot(x, w1)
    return _dot(a * jax.nn.sigmoid(a) * _dot(x, w3), w2)


def _gmm_body(te_ref, nt_ref, x_ref, w1_ref, w3_ref, w2_ref, o_ref, w1b, w3b, w2b):
    i = pl.program_id(0)
    changed = jnp.logical_or(i == 0, te_ref[i] != te_ref[jnp.maximum(i - 1, 0)])

    @pl.when(changed)
    def _():
        w1b[...] = _mx(w1_ref[0])
        w3b[...] = _mx(w3_ref[0])
        w2b[...] = _mx(w2_ref[0])

    @pl.when(i < nt_ref[0])
    def _():
        o_ref[...] = _ffn(x_ref[...], w1b[...], w3b[...], w2b[...]).astype(o_ref.dtype)

    @pl.when(i >= nt_ref[0])
    def _():
        o_ref[...] = jnp.zeros(o_ref.shape, o_ref.dtype)


def grouped_ffn(tile_expert, n_tiles, x, w1, w3, w2, tm):
    P, D = x.shape
    F = w1.shape[2]
    row = lambda i, te, nt: (jnp.minimum(i, nt[0] - 1), 0)
    wsel = lambda i, te, nt: (te[i], 0, 0)
    return pl.pallas_call(
        _gmm_body,
        grid_spec=pltpu.PrefetchScalarGridSpec(
            num_scalar_prefetch=2,
            grid=(P // tm,),
            in_specs=[pl.BlockSpec((tm, D), row),
                      pl.BlockSpec((1, D, F), wsel),
                      pl.BlockSpec((1, D, F), wsel),
                      pl.BlockSpec((1, F, D), wsel)],
            out_specs=pl.BlockSpec((tm, D), lambda i, te, nt: (i, 0)),
            scratch_shapes=[pltpu.VMEM((D, F), MXU_DTYPE), pltpu.VMEM((D, F), MXU_DTYPE),
                            pltpu.VMEM((F, D), MXU_DTYPE)]),
        out_shape=jax.ShapeDtypeStruct((P, D), F32),
        compiler_params=_params(1),
        name="grouped_ffn",
    )(tile_expert, n_tiles, x, w1, w3, w2)


def _close_moe_body(hb_ref, yk_ref, g8_ref, ws1_ref, ws3_ref, ws2_ref, x_ref, gate_ref, g_ref, b_ref, o_ref,
                    *, group, alpha):
    y = _ffn(hb_ref[...], ws1_ref[...], ws3_ref[...], ws2_ref[...])
    g8 = g8_ref[...]
    for k in range(yk_ref.shape[0]):
        y = y + g8[:, k:k + 1] * yk_ref[k]
    r = _gated_residual(x_ref, y, gate_ref, group, alpha)
    o_ref[...] = _layer_norm_rows(r, g_ref[...], b_ref[...])


def close_moe(hb, y_k, g8, ws1, ws3, ws2, x, gate_g, ln_g, ln_b, group, alpha, tm):
    T, D = x.shape
    K = y_k.shape[0]
    row = lambda i: (i, 0)
    fixed = lambda i: (0, 0)
    return pl.pallas_call(
        functools.partial(_close_moe_body, group=group, alpha=alpha),
        grid=(T // tm,),
        in_specs=[pl.BlockSpec((tm, D), row),
                  pl.BlockSpec((K, tm, D), lambda i: (0, i, 0)),
                  pl.BlockSpec((tm, K), row),
                  pl.BlockSpec(ws1.shape, fixed),
                  pl.BlockSpec(ws3.shape, fixed),
                  pl.BlockSpec(ws2.shape, fixed),
                  pl.BlockSpec((tm, D), row),
                  pl.BlockSpec((1, tm // group, D), lambda i: (i, 0, 0)),
                  pl.BlockSpec((1, D), fixed),
                  pl.BlockSpec((1, D), fixed)],
        out_specs=pl.BlockSpec((tm, D), row),
        out_shape=jax.ShapeDtypeStruct((T, D), F32),
        compiler_params=_params(1),
        name="close_moe",
    )(hb, y_k, g8, ws1, ws3, ws2, x, _per_tile(gate_g, tm, group), ln_g, ln_b)


def _rope_tables(pos, half):
    inv_freq = jnp.power(ROPE_THETA, -jnp.arange(half, dtype=F32) / half)
    ang = pos.astype(F32)[:, None] * inv_freq[None, :]
    cos, sin = jnp.cos(ang), jnp.sin(ang)
    reps = LANES // (2 * half)
    return jnp.tile(jnp.concatenate([cos, cos], axis=1), (1, reps)), jnp.tile(jnp.concatenate([-sin, sin], axis=1), (1, reps))


def _place_cols(w, sizes, names, cols, width):
    out = jnp.zeros((w.shape[0], width), w.dtype)
    o = 0
    for size, name in zip(sizes, names):
        if name is not None:
            dst, sub = name
            out = lax.dynamic_update_slice(out, w[:, o:o + size], (0, cols[dst] + sub))
        o += size
    return out


def _moe(x, scale_g, shift_g, gate_g, ln_g, ln_b, w_router, r_bias, w1, w3, w2, ws1, ws3, ws2, group, alpha):
    T, D = x.shape
    E = w1.shape[0]
    tm = 256
    t_route = _pick(T, 768, math.lcm(LANES, group))
    hb, gate, rank, counts = route(x, scale_g, shift_g, w_router.T, r_bias.reshape(E, 1), group, t_route)

    counts = counts[:, 0]
    tiles_per = (counts + tm - 1) // tm
    tile_end = jnp.cumsum(tiles_per)
    seg_off = ((tile_end - tiles_per) * tm).astype(I32)
    n_tiles = (T * TOP_K) // tm + E
    used = tile_end[-1].astype(I32)
    tile_ids = jnp.minimum(jnp.arange(n_tiles, dtype=I32), used - 1)
    tile_expert = jnp.sum((tile_end[None, :] <= tile_ids[:, None]).astype(I32), axis=1)

    dest, g8 = compact_routes(gate, rank, seg_off.reshape(E, 1), t_route)
    tok = jnp.broadcast_to(jnp.arange(T, dtype=I32)[None, :], dest.shape)
    row_token = jnp.zeros((n_tiles * tm,), I32).at[dest.reshape(-1)].set(tok.reshape(-1))
    y_rows = grouped_ffn(tile_expert, used.reshape(1), hb[row_token], w1, w3, w2, tm)
    return close_moe(hb, y_rows[dest], g8.T, _mx(ws1), _mx(ws3), _mx(ws2), x, gate_g, ln_g, ln_b, group, alpha,
                     _pick(T, 128, math.lcm(2 * SUBLANES, group)))


def _fox_gates(grp, proj, b_f, past_logf):
    B, H = grp.batch, B_HEADS
    c0 = EVEN_COLS["misc"] + MISC_FL
    fl = proj[grp.row0:grp.row0 + B * grp.lq, c0:c0 + H].reshape(B, grp.lq, H)
    logf = jax.nn.log_sigmoid(fl + b_f)
    logf_all = logf if past_logf is None else jnp.concatenate([past_logf, logf], axis=1)
    cum = jnp.moveaxis(jnp.cumsum(logf_all, axis=1), 1, 2).reshape(B * H, grp.lk)
    fq = cum[:, grp.lp:, None]
    new = cum[:, grp.lp:].reshape(B * H, grp.lq // grp.tk, 1, grp.tk)
    new = jnp.pad(new, ((0, 0), (0, 0), (0, 0), (0, KEY_BLOCK - grp.tk)))
    if grp.lp:
        past = cum[:, :grp.lp].reshape(B * H, grp.n_past_blocks, 1, grp.tkp)
        past = jnp.pad(past, ((0, 0), (0, 0), (0, 0), (0, KEY_BLOCK - grp.tkp)))
        new = jnp.concatenate([past, new], axis=1)
    return logf, fq, new


def kernel(x_prompt, x_sample, c_prompt, c_sample, cache_a_k, cache_a_v, cache_a_kidx, cache_b_k, cache_b_v, cache_b_logf, cache_c_k, cache_c_v, cache_d_latent, cache_d_krope, w_in_even, b_forget, w_out_even, w_in_odd, c_lambda, c_subln, d_q_norm, d_w_uq, d_kv_norm, d_w_uk, d_w_uv, w_out_odd, ada_mix_w, ada_mix_b, ln_mix_g, ln_mix_b, ada_ffn_w, ada_ffn_b, ln_ffn_g, ln_ffn_b, router_w, router_bias, moe_w1, moe_w3, moe_w2, shared_w1, shared_w3, shared_w2):
    B, L, D = x_prompt.shape
    Bs, Ls, _ = x_sample.shape
    depth = ada_mix_w.shape[0]
    alpha = (2 * depth) ** 0.25
    past_len = cache_a_k.shape[2]
    Tp = B * L
    T = Tp + Bs * Ls
    grp_p = _Group(B, L, 0, 0)
    grp_s = _Group(Bs, Ls, past_len, Tp)
    group = math.gcd(L, Ls)
    assert group % (2 * SUBLANES) == 0
    tm_tok = _pick(T, 768, math.lcm(LANES, group))
    tm_close = _pick(T, 512, math.lcm(2 * SUBLANES, group))
    pos = jnp.concatenate([jnp.tile(jnp.arange(L), B), jnp.tile(past_len + jnp.arange(Ls), Bs)])
    c_act = jax.nn.silu(jnp.concatenate([c_prompt, c_sample], axis=0))

    def conditioning(ada_w, ada_b):
        mod = matmul(c_act, ada_w) + ada_b
        per_group = jnp.concatenate([jnp.repeat(mod[:B], L // group, axis=0),
                                     jnp.repeat(mod[B:], Ls // group, axis=0)], axis=0)
        return jnp.split(per_group, 3, axis=-1)

    def flat_past(c):
        return c.reshape(c.shape[0], c.shape[1], -1)

    x = jnp.concatenate([x_prompt.reshape(Tp, D), x_sample.reshape(Bs * Ls, D)], axis=0)
    ev_p, ev_s, od_p, od_s = [], [], [], []
    for i in range(depth):
        j = i // 2
        shift_g, scale_g, gate_g = conditioning(ada_mix_w[i], ada_mix_b[i])
        if i % 2 == 0:
            sizes = (A_HEADS * HEAD_DIM, A_KV_HEADS * HEAD_DIM, A_KV_HEADS * HEAD_DIM, IDX_HEADS * IDX_DIM, IDX_DIM,
                     IDX_HEADS, B_HEADS * HEAD_DIM, B_HEADS * HEAD_DIM, B_HEADS * HEAD_DIM, B_HEADS)
            names = (("qa", 0), ("ka", 0), ("va", 0), ("qi", 0), ("ki", 0), ("misc", MISC_WI),
                     ("qf", 0), ("kf", 0), ("vf", 0), ("misc", MISC_FL))
            w_in = _mx(_place_cols(w_in_even[j], sizes, names, EVEN_COLS, EVEN_WIDTH))
            cos, sin = _rope_tables(pos, HEAD_DIM // 2)
            proj = project(x, scale_g, shift_g, w_in, cos, sin, EVEN_ROPE_COLS, HEAD_DIM // 2, group, tm_tok)
            outs_a, outs_b = [], []
            for grp, store, past in ((grp_p, ev_p, None),
                                     (grp_s, ev_s, (cache_a_k[j], cache_a_v[j], cache_a_kidx[j], cache_b_k[j],
                                                    cache_b_v[j], cache_b_logf[j]))):
                pa = (None,) * 6 if past is None else tuple(flat_past(c) for c in past)
                logf, fq, fk = _fox_gates(grp, proj, b_forget[j], pa[5])
                outs_a.append(dsa_attention(grp, proj, pa[2], pa[0], pa[1]))
                outs_b.append(fox_attention(grp, proj, fq, fk, pa[3], pa[4]))
                rows = proj[grp.row0:grp.row0 + grp.batch * grp.lq].reshape(grp.batch, grp.lq, EVEN_WIDTH)
                cut = lambda name, heads, d: rows[:, :, EVEN_COLS[name]:EVEN_COLS[name] + heads * d].reshape(
                    (grp.batch, grp.lq) + ((heads, d) if heads > 1 else (d,)))
                store.append((cut("ka", A_KV_HEADS, HEAD_DIM), cut("va", A_KV_HEADS, HEAD_DIM), cut("ki", 1, IDX_DIM),
                              cut("kf", B_HEADS, HEAD_DIM), cut("vf", B_HEADS, HEAD_DIM), logf))
            a1, a2 = jnp.concatenate(outs_a, axis=0), jnp.concatenate(outs_b, axis=0)
            n1 = A_HEADS * HEAD_DIM
            w1o, w2o = _mx(w_out_even[j][:n1]), _mx(w_out_even[j][n1:])
        else:
            lam_init = 0.8 - 0.6 * math.exp(-0.3 * i)
            lam_f = c_lambda[j]
            lam = (jnp.exp(jnp.sum(lam_f[0] * lam_f[1])) - jnp.exp(jnp.sum(lam_f[2] * lam_f[3])) + lam_init).reshape(1)
            q_lora, R = d_q_norm.shape[1], d_kv_norm.shape[1]
            sizes = (C_HEADS * 2 * C_QK_DIM, C_HEADS * 2 * C_QK_DIM, C_HEADS * C_V_DIM, q_lora, R, D_ROPE)
            names = (("qc", 0), ("kc", 0), ("vc", 0), ("qd", 0), ("ckv", 0), ("kr", 0))
            w_in = _mx(_place_cols(w_in_odd[j], sizes, names, ODD_COLS, ODD_WIDTH))
            cos, sin = _rope_tables(pos, C_QK_DIM // 2)
            proj = project(x, scale_g, shift_g, w_in, cos, sin, ODD_ROPE_COLS, C_QK_DIM // 2, group, tm_tok)

            w_uq = d_w_uq[j].reshape(q_lora, D_HEADS, D_NOPE + D_ROPE)
            w_abs = bmm_precise(jnp.moveaxis(w_uq[:, :, :D_NOPE], 1, 0), jnp.transpose(d_w_uk[j], (1, 2, 0)))
            w_abs = jnp.moveaxis(w_abs, 0, 1).reshape(q_lora, D_HEADS * R)
            w_rope = jnp.pad(w_uq[:, :, D_NOPE:], ((0, 0), (0, 0), (0, LANES - D_ROPE))).reshape(q_lora, D_HEADS * LANES)
            q_abs, q_rope, lat = mla_prepare(proj, d_q_norm[j][None], d_kv_norm[j][None],
                                             _mx(jnp.concatenate([w_abs, w_rope], axis=1)), cos, sin, tm_close)
            n1 = C_HEADS * C_V_DIM
            w_od = w_out_odd[j][n1:].reshape(D_HEADS, D_V, D)
            w_lat_out = bmm_precise(jnp.transpose(d_w_uv[j], (1, 0, 2)), w_od).reshape(D_HEADS * R, D)

            outs_a, outs_b = [], []
            for grp, store, past in ((grp_p, od_p, None),
                                     (grp_s, od_s, (cache_c_k[j], cache_c_v[j], cache_d_latent[j], cache_d_krope[j]))):
                pa = (None,) * 4 if past is None else tuple(flat_past(c) for c in past)
                outs_a.append(diff_attention(grp, proj, lam, c_subln[j][None], 1.0 - lam_init, pa[0], pa[1]))
                outs_b.append(mla_attention(grp, q_abs, q_rope, lat, proj, pa[2], pa[3]))
                r0, r1 = grp.row0, grp.row0 + grp.batch * grp.lq
                shp = (grp.batch, grp.lq)
                oc = ODD_COLS
                store.append((proj[r0:r1, oc["kc"]:oc["kc"] + C_HEADS * 2 * C_QK_DIM].reshape(shp + (C_HEADS, 2, C_QK_DIM)),
                              proj[r0:r1, oc["vc"]:oc["vc"] + n1].reshape(shp + (C_HEADS, C_V_DIM)),
                              lat[r0:r1].reshape(shp + (R,)),
                              proj[r0:r1, oc["kr"]:oc["kr"] + D_ROPE].reshape(shp + (D_ROPE,))))
            a1, a2 = jnp.concatenate(outs_a, axis=0), jnp.concatenate(outs_b, axis=0)
            w1o, w2o = _mx(w_out_odd[j][:n1]), _mx(w_lat_out)
        x = close_mixer(a1, a2, w1o, w2o, x, gate_g, ln_mix_g[i][None], ln_mix_b[i][None], group, alpha, tm_close)

        shift_g, scale_g, gate_g = conditioning(ada_ffn_w[i], ada_ffn_b[i])
        x = _moe(x, scale_g, shift_g, gate_g, ln_ffn_g[i][None], ln_ffn_b[i][None], router_w[i], router_bias[i],
                 moe_w1[i], moe_w3[i], moe_w2[i], shared_w1[i], shared_w3[i], shared_w2[i], group, alpha)

    stack = lambda rows, idx: jnp.stack([r[idx] for r in rows])
    outs = [x[:Tp].reshape(B, L, D), x[Tp:].reshape(Bs, Ls, D)]
    for idx in range(6):
        outs += [stack(ev_p, idx), stack(ev_s, idx)]
    for idx in range(4):
        outs += [stack(od_p, idx), stack(od_s, idx)]
    return tuple(outs)
```

```python
import functools
import math

import jax
import jax.numpy as jnp
from jax import lax
from jax.experimental import pallas as pl
from jax.experimental.pallas import tpu as pltpu

CHUNK = 64
ROPE_THETA = 10000.0
HEAD_DIM = 128
A_HEADS = 8
A_KV_HEADS = 2
IDX_HEADS = 16
IDX_DIM = 128
A_TOPK_MAX = 256
B_HEADS = 8
C_HEADS = 8
C_QK_DIM = 64
C_V_DIM = 128
D_HEADS = 8
D_NOPE = 128
D_ROPE = 64
D_V = 128
N_GROUPS = 8
TOPK_GROUPS = 4
TOP_K = 8
ROUTED_SCALE = 2.5
NORM_EPS = 1e-6

LANES = 128
SUBLANES = 8
MXU_WIDTH = 256
VMEM_LIMIT_BYTES = 56 * 2**20
MXU_DTYPE = jnp.bfloat16

F32 = jnp.float32
I32 = jnp.int32
NEG = -1e30
INT_MIN = -2**31
KEY_BLOCK = 512
QUERY_BLOCK = 512
HEADS_PER_STEP = 2

EVEN_COLS = dict(qi=0, qa=2048, qf=3072, kf=4096, vf=5120, ka=6144, va=6400, ki=6656, misc=6912)
EVEN_WIDTH = 7168
EVEN_ROPE_RANGES = ((0, 3072), (6144, 6400), (6656, 6912))
MISC_WI, MISC_FL = 0, IDX_HEADS
ODD_COLS = dict(qc=0, kc=1024, vc=2048, qd=3072, ckv=3584, kr=3840)
ODD_WIDTH = 4096
ODD_ROPE_RANGES = ((0, 2048), (3840, 4096))


def _pick(n, target, mult):
    best = None
    for d in range(mult, min(n, target) + 1, mult):
        if n % d == 0:
            best = d
    return n if best is None else best


def _params(n_axes):
    return pltpu.CompilerParams(dimension_semantics=("arbitrary",) * n_axes,
                                vmem_limit_bytes=VMEM_LIMIT_BYTES)


def _mx(a):
    return a.astype(MXU_DTYPE)


def _dot(a, b):
    return jnp.dot(_mx(a), _mx(b), preferred_element_type=F32)


def _dot_nt(a, b, precision=None):
    if precision is None:
        a, b = _mx(a), _mx(b)
    return lax.dot_general(a, b, (((1,), (1,)), ((), ())), preferred_element_type=F32, precision=precision)


def _mm_body(a_ref, b_ref, o_ref, *, precise):
    if precise:
        o = jnp.dot(a_ref[...], b_ref[...], preferred_element_type=F32, precision=lax.Precision.HIGHEST)
    else:
        o = _dot(a_ref[...], b_ref[...])
    o_ref[...] = o.astype(o_ref.dtype)


def matmul(a, b, out_dtype=F32, precise=False, tm_target=1024, tn_target=512, layer=None):
    M, K = a.shape
    N = b.shape[-1]
    tm = _pick(M, tm_target, SUBLANES)
    tn = _pick(N, tn_target, LANES)
    if layer is None:
        b_spec = pl.BlockSpec((K, tn), lambda i, j: (0, j))
    else:
        b_spec = pl.BlockSpec((None, K, tn), lambda i, j: (layer, 0, j))
    return pl.pallas_call(
        functools.partial(_mm_body, precise=precise),
        grid=(M // tm, N // tn),
        in_specs=[pl.BlockSpec((tm, K), lambda i, j: (i, 0)), b_spec],
        out_specs=pl.BlockSpec((tm, tn), lambda i, j: (i, j)),
        out_shape=jax.ShapeDtypeStruct((M, N), out_dtype),
        compiler_params=_params(2),
        name="matmul",
    )(a, b)


def _bmm_body(a_ref, b_ref, o_ref):
    o_ref[0] = jnp.dot(a_ref[0], b_ref[0], preferred_element_type=F32,
                       precision=lax.Precision.HIGHEST).astype(o_ref.dtype)


def bmm_precise(a, b):
    H, M, K = a.shape
    N = b.shape[2]
    return pl.pallas_call(
        _bmm_body,
        grid=(H,),
        in_specs=[pl.BlockSpec((1, M, K), lambda h: (h, 0, 0)),
                  pl.BlockSpec((1, K, N), lambda h: (h, 0, 0))],
        out_specs=pl.BlockSpec((1, M, N), lambda h: (h, 0, 0)),
        out_shape=jax.ShapeDtypeStruct((H, M, N), F32),
        compiler_params=_params(1),
        name="bmm_precise",
    )(a, b)


def _per_tile(table, tm, group):
    return table.reshape(table.shape[0] * group // tm, tm // group, table.shape[1])


def _modulated(x_ref, sc_ref, sh_ref, group):
    parts = []
    for g in range(x_ref.shape[0] // group):
        rows = slice(g * group, (g + 1) * group)
        parts.append(x_ref[rows, :] * (1.0 + sc_ref[0, g:g + 1, :]) + sh_ref[0, g:g + 1, :])
    return parts


def _rope_lanes(x, cos, sin, half):
    if 2 * half == LANES:
        partner = pltpu.roll(x, half, axis=1)
    else:
        lane = lax.broadcasted_iota(I32, x.shape, 1)
        lower = (lane % (2 * half)) < half
        partner = jnp.where(lower, pltpu.roll(x, LANES - half, axis=1), pltpu.roll(x, half, axis=1))
    return x * cos + partner * sin


def _layer_norm_rows(r, g, b):
    rc = r - jnp.mean(r, axis=-1, keepdims=True)
    var = jnp.mean(rc * rc, axis=-1, keepdims=True)
    return rc * lax.rsqrt(var + NORM_EPS) * g + b


def _gated_residual(x_ref, y, gate_ref, group, alpha):
    parts = []
    for g in range(x_ref.shape[0] // group):
        rows = slice(g * group, (g + 1) * group)
        parts.append(alpha * x_ref[rows, :] + (1.0 + gate_ref[0, g:g + 1, :]) * y[rows, :])
    return jnp.concatenate(parts, axis=0)


def _proj_body(kind_ref, x_ref, sc_ref, sh_ref, w_ref, *rest, group, half, n_sub):
    tables, (o_ref, h_ref) = rest[:2 * n_sub], rest[2 * n_sub:]
    j = pl.program_id(1)

    @pl.when(j == 0)
    def _():
        for g, part in enumerate(_modulated(x_ref, sc_ref, sh_ref, group)):
            h_ref[g * group:(g + 1) * group, :] = part.astype(h_ref.dtype)

    for c in range(n_sub):
        cos, sin = tables[2 * c][...], tables[2 * c + 1][...]
        acc = jnp.dot(h_ref[...], w_ref[:, c * MXU_WIDTH:(c + 1) * MXU_WIDTH], preferred_element_type=F32)
        for p in range(MXU_WIDTH // LANES):
            o_ref[:, c * MXU_WIDTH + p * LANES:c * MXU_WIDTH + (p + 1) * LANES] = _rope_lanes(
                acc[:, p * LANES:(p + 1) * LANES], cos, sin, half)


def project(x, scale_g, shift_g, w, cos, sin, rope_ranges, half, group, tm):
    T, D = x.shape
    N = w.shape[1]
    n_sub = 2
    tn = n_sub * MXU_WIDTH
    start = jnp.arange(N // MXU_WIDTH) * MXU_WIDTH
    kinds = sum(((start >= lo) & (start < hi)).astype(I32) for lo, hi in rope_ranges)
    row = lambda i, j, k: (i, 0)
    tile = lambda i, j, k: (i, 0, 0)
    table_specs, table_args = [], []
    for c in range(n_sub):
        pick = lambda i, j, k, c=c: (k[j * n_sub + c], i, 0)
        table_specs += [pl.BlockSpec((None, tm, LANES), pick)] * 2
        table_args += [cos, sin]
    return pl.pallas_call(
        functools.partial(_proj_body, group=group, half=half, n_sub=n_sub),
        grid_spec=pltpu.PrefetchScalarGridSpec(
            num_scalar_prefetch=1,
            grid=(T // tm, N // tn),
            in_specs=[pl.BlockSpec((tm, D), row),
                      pl.BlockSpec((1, tm // group, D), tile),
                      pl.BlockSpec((1, tm // group, D), tile),
                      pl.BlockSpec((D, tn), lambda i, j, k: (0, j))] + table_specs,
            out_specs=pl.BlockSpec((tm, tn), lambda i, j, k: (i, j)),
            scratch_shapes=[pltpu.VMEM((tm, D), MXU_DTYPE)]),
        out_shape=jax.ShapeDtypeStruct((T, N), F32),
        compiler_params=_params(2),
        name="project",
    )(kinds, x, _per_tile(scale_g, tm, group), _per_tile(shift_g, tm, group), w, *table_args)


def _close_mix_body(a1_ref, a2_ref, w1_ref, w2_ref, x_ref, gate_ref, g_ref, b_ref, o_ref, *, group, alpha):
    y = _dot(a1_ref[...], w1_ref[...]) + _dot(a2_ref[...], w2_ref[...])
    r = _gated_residual(x_ref, y, gate_ref, group, alpha)
    o_ref[...] = _layer_norm_rows(r, g_ref[...], b_ref[...])


def close_mixer(a1, a2, w1, w2, x, gate_g, ln_g, ln_b, group, alpha, tm):
    T, D = x.shape
    row = lambda i: (i, 0)
    fixed = lambda i: (0, 0)
    return pl.pallas_call(
        functools.partial(_close_mix_body, group=group, alpha=alpha),
        grid=(T // tm,),
        in_specs=[pl.BlockSpec((tm, a1.shape[1]), row),
                  pl.BlockSpec((tm, a2.shape[1]), row),
                  pl.BlockSpec(w1.shape, fixed),
                  pl.BlockSpec(w2.shape, fixed),
                  pl.BlockSpec((tm, D), row),
                  pl.BlockSpec((1, tm // group, D), lambda i: (i, 0, 0)),
                  pl.BlockSpec((1, D), fixed),
                  pl.BlockSpec((1, D), fixed)],
        out_specs=pl.BlockSpec((tm, D), row),
        out_shape=jax.ShapeDtypeStruct((T, D), F32),
        compiler_params=_params(1),
        name="close_mixer",
    )(a1, a2, w1, w2, x, _per_tile(gate_g, tm, group), ln_g, ln_b)


def _online_step(s, v, carry):
    m, l, acc = carry
    m_new = jnp.maximum(m, jnp.max(s, axis=-1, keepdims=True))
    p = jnp.exp(s - m_new)
    alpha = jnp.exp(m - m_new)
    l = alpha * l + jnp.sum(p, axis=-1, keepdims=True)
    acc = alpha * acc + _dot(p, v)
    return m_new, l, acc


def _init_carry(tq, dv):
    return (jnp.full((tq, 1), NEG, F32), jnp.zeros((tq, 1), F32), jnp.zeros((tq, dv), F32))


def _chunk_last_key(i, tq):
    return ((i + 1) * tq - 1) // CHUNK * CHUNK + CHUNK - 1


class _Group:
    def __init__(self, batch, lq, lp, row0):
        self.batch, self.lq, self.lp, self.row0 = batch, lq, lp, row0
        self.lk = lp + lq
        self.tq = _pick(lq, QUERY_BLOCK, SUBLANES)
        self.tq_dsa = _pick(lq, 128, SUBLANES)
        self.tk = _pick(lq, KEY_BLOCK, SUBLANES)
        self.tkp = _pick(lp, KEY_BLOCK, LANES) if lp else 0
        self.n_past_blocks = lp // self.tkp if lp else 0
        self.topk = min(A_TOPK_MAX, self.lk // 4)
        assert lp % CHUNK == 0 and row0 % lq == 0 and lq % self.tk == 0

    def qrow(self, tq):
        base, per = self.row0 // tq, self.lq // tq
        return lambda b, i: base + b * per + i

    def krow(self):
        base = self.row0 // self.lq
        return lambda b: base + b


def _new_key_blocks(i, tq, grp_lq, tk, causal_last):
    return jnp.minimum(causal_last, grp_lq - 1) // tk + 1


def _fox_body(q_ref, k_ref, v_ref, fq_ref, fk_ref, o_ref, *, tq, tk, lq, hp):
    i = pl.program_id(2)
    d = HEAD_DIM
    head = lambda x, u: x[:, u * d:(u + 1) * d]
    q = _mx(q_ref[...] * d ** -0.5)
    carry = tuple(_init_carry(tq, d) for _ in range(hp))
    qpos = i * tq + lax.broadcasted_iota(I32, (tq, tk), 0)

    def new_step(j, c):
        off = pl.multiple_of(j * tk, tk)
        k, v = k_ref[pl.ds(off, tk), :], v_ref[pl.ds(off, tk), :]
        causal = off + lax.broadcasted_iota(I32, (tq, tk), 1) <= qpos
        out = []
        for u in range(hp):
            s = _dot_nt(head(q, u), head(k, u)) + fq_ref[u] - fk_ref[u, j][:, :tk]
            out.append(_online_step(jnp.where(causal, s, NEG), head(v, u), c[u]))
        return tuple(out)

    nb = _new_key_blocks(i, tq, lq, tk, (i + 1) * tq - 1)
    carry = lax.fori_loop(0, nb, new_step, carry)
    for u, (m, l, acc) in enumerate(carry):
        o_ref[:, u * d:(u + 1) * d] = (acc / l).astype(o_ref.dtype)


def fox_attention(grp, proj, fq, fk):
    assert grp.lp == 0
    B, H, hp = grp.batch, B_HEADS, HEADS_PER_STEP
    w = hp * HEAD_DIM
    tq = grp.tq
    qrow, krow = grp.qrow(tq), grp.krow()
    cq, ck, cv = (EVEN_COLS[n] // w for n in ("qf", "kf", "vf"))
    in_specs = [pl.BlockSpec((tq, w), lambda b, h, i: (qrow(b, i), cq + h)),
                pl.BlockSpec((grp.lq, w), lambda b, h, i: (krow(b), ck + h)),
                pl.BlockSpec((grp.lq, w), lambda b, h, i: (krow(b), cv + h))]
    in_specs += [pl.BlockSpec((hp, tq, 1), lambda b, h, i: (b * (H // hp) + h, i, 0)),
                 pl.BlockSpec((hp,) + fk.shape[1:], lambda b, h, i: (b * (H // hp) + h, 0, 0, 0))]
    return pl.pallas_call(
        functools.partial(_fox_body, tq=tq, tk=grp.tk, lq=grp.lq, hp=hp),
        grid=(B, H // hp, grp.lq // tq),
        in_specs=in_specs,
        out_specs=pl.BlockSpec((tq, w), lambda b, h, i: (b * (grp.lq // tq) + i, h)),
        out_shape=jax.ShapeDtypeStruct((B * grp.lq, H * HEAD_DIM), MXU_DTYPE),
        compiler_params=_params(3),
        name="fox_attention",
    )(proj, proj, proj, fq, fk)


def _diff_body(lam_ref, q_ref, k_ref, v_ref, g_ref, o_ref, *, tq, tk, lq, hp, out_scale):
    i = pl.program_id(2)
    d = C_V_DIM
    head = lambda x, u: x[:, u * d:(u + 1) * d]
    q = q_ref[...] * C_QK_DIM ** -0.5
    lane = lax.broadcasted_iota(I32, q.shape, 1) % (2 * C_QK_DIM)
    qs = (_mx(jnp.where(lane < C_QK_DIM, q, 0.0)), _mx(jnp.where(lane >= C_QK_DIM, q, 0.0)))
    carry = tuple(_init_carry(tq, d) for _ in range(2 * hp))

    def step(k, v, ok, c):
        out = []
        for u in range(hp):
            ku, vu = _mx(head(k, u)), _mx(head(v, u))
            for t in range(2):
                s = _dot_nt(head(qs[t], u), ku)
                out.append(_online_step(jnp.where(ok, s, NEG), vu, c[2 * u + t]))
        return tuple(out)

    qchunk = (i * tq + lax.broadcasted_iota(I32, (tq, tk), 0)) // CHUNK

    def new_step(j, c):
        off = pl.multiple_of(j * tk, tk)
        ok = (off + lax.broadcasted_iota(I32, (tq, tk), 1)) // CHUNK <= qchunk
        return step(k_ref[pl.ds(off, tk), :], v_ref[pl.ds(off, tk), :], ok, c)

    nb = _new_key_blocks(i, tq, lq, tk, _chunk_last_key(i, tq))
    carry = lax.fori_loop(0, nb, new_step, carry)
    for u in range(hp):
        (_, l0, a0), (_, l1, a1) = carry[2 * u], carry[2 * u + 1]
        o = a0 / l0 - lam_ref[0] * (a1 / l1)
        o = o * lax.rsqrt(jnp.mean(o * o, axis=-1, keepdims=True) + NORM_EPS)
        o_ref[:, u * d:(u + 1) * d] = (o * g_ref[...] * out_scale).astype(o_ref.dtype)


def diff_attention(grp, proj, lam, subln, out_scale):
    assert grp.lp == 0
    B, H, hp = grp.batch, C_HEADS, HEADS_PER_STEP
    w = hp * C_V_DIM
    tq = grp.tq
    qrow, krow = grp.qrow(tq), grp.krow()
    cq, ck, cv = (ODD_COLS[n] // w for n in ("qc", "kc", "vc"))
    in_specs = [pl.BlockSpec(memory_space=pltpu.SMEM),
                pl.BlockSpec((tq, w), lambda b, h, i: (qrow(b, i), cq + h)),
                pl.BlockSpec((grp.lq, w), lambda b, h, i: (krow(b), ck + h)),
                pl.BlockSpec((grp.lq, w), lambda b, h, i: (krow(b), cv + h))]
    in_specs += [pl.BlockSpec((1, C_V_DIM), lambda b, h, i: (0, 0))]
    return pl.pallas_call(
        functools.partial(_diff_body, tq=tq, tk=grp.tk, lq=grp.lq, hp=hp, out_scale=out_scale),
        grid=(B, H // hp, grp.lq // tq),
        in_specs=in_specs,
        out_specs=pl.BlockSpec((tq, w), lambda b, h, i: (b * (grp.lq // tq) + i, h)),
        out_shape=jax.ShapeDtypeStruct((B * grp.lq, H * C_V_DIM), MXU_DTYPE),
        compiler_params=_params(3),
        name="diff_attention",
    )(lam, proj, proj, proj, subln)


def _mla_prep_body(qd_ref, ckv_ref, qg_ref, kg_ref, w_ref, cos_ref, sin_ref, qa_ref, qr_ref, lat_ref, *, n_abs):
    qd = qd_ref[...]
    qn = qd * lax.rsqrt(jnp.mean(qd * qd, axis=-1, keepdims=True) + NORM_EPS) * qg_ref[...]
    q = _dot(qn, w_ref[...]) * (D_NOPE + D_ROPE) ** -0.5
    qa_ref[...] = q[:, :n_abs].astype(qa_ref.dtype)
    for h in range((q.shape[1] - n_abs) // LANES):
        lanes = slice(n_abs + h * LANES, n_abs + (h + 1) * LANES)
        qr_ref[:, h * LANES:(h + 1) * LANES] = _rope_lanes(
            q[:, lanes], cos_ref[...], sin_ref[...], D_ROPE // 2).astype(qr_ref.dtype)
    ckv = ckv_ref[...]
    lat_ref[...] = ckv * lax.rsqrt(jnp.mean(ckv * ckv, axis=-1, keepdims=True) + NORM_EPS) * kg_ref[...]


def mla_prepare(proj, q_norm, kv_norm, w_q, cos, sin, tm):
    T = proj.shape[0]
    q_lora, R = q_norm.shape[1], kv_norm.shape[1]
    n_abs = D_HEADS * R
    n_rope = D_HEADS * LANES
    row = lambda i: (i, 0)
    fixed = lambda i: (0, 0)
    return pl.pallas_call(
        functools.partial(_mla_prep_body, n_abs=n_abs),
        grid=(T // tm,),
        in_specs=[pl.BlockSpec((tm, q_lora), lambda i: (i, ODD_COLS["qd"] // q_lora)),
                  pl.BlockSpec((tm, R), lambda i: (i, ODD_COLS["ckv"] // R)),
                  pl.BlockSpec((1, q_lora), fixed),
                  pl.BlockSpec((1, R), fixed),
                  pl.BlockSpec(w_q.shape, fixed),
                  pl.BlockSpec((None, tm, LANES), lambda i: (1, i, 0)),
                  pl.BlockSpec((None, tm, LANES), lambda i: (1, i, 0))],
        out_specs=[pl.BlockSpec((tm, n_abs), row), pl.BlockSpec((tm, n_rope), row), pl.BlockSpec((tm, R), row)],
        out_shape=[jax.ShapeDtypeStruct((T, n_abs), MXU_DTYPE), jax.ShapeDtypeStruct((T, n_rope), MXU_DTYPE),
                   jax.ShapeDtypeStruct((T, R), F32)],
        compiler_params=_params(1),
        name="mla_prepare",
    )(proj, proj, q_norm, kv_norm, w_q, cos, sin)


def _mla_body(qa_ref, qr_ref, lat_ref, kr_ref, o_ref, *, tq, tk, lq, hp):
    i = pl.program_id(2)
    R = lat_ref.shape[-1]
    qa = [qa_ref[:, u * R:(u + 1) * R] for u in range(hp)]
    qr = [qr_ref[:, u * LANES:(u + 1) * LANES] for u in range(hp)]
    carry = tuple(_init_carry(tq, R) for _ in range(hp))

    qchunk = (i * tq + lax.broadcasted_iota(I32, (tq, tk), 0)) // CHUNK

    def new_step(j, c):
        off = pl.multiple_of(j * tk, tk)
        lat = _mx(lat_ref[pl.ds(off, tk), :])
        kr = _mx(kr_ref[pl.ds(off, tk), :])
        ok = (off + lax.broadcasted_iota(I32, (tq, tk), 1)) // CHUNK <= qchunk
        return tuple(_online_step(jnp.where(ok, _dot_nt(qa[u], lat) + _dot_nt(qr[u], kr), NEG), lat, c[u])
                     for u in range(hp))

    nb = _new_key_blocks(i, tq, lq, tk, _chunk_last_key(i, tq))
    carry = lax.fori_loop(0, nb, new_step, carry)
    for u, (m, l, acc) in enumerate(carry):
        o_ref[:, u * R:(u + 1) * R] = (acc / l).astype(o_ref.dtype)


def mla_attention(grp, q_abs, q_rope, lat, proj):
    assert grp.lp == 0
    B, H, hp = grp.batch, D_HEADS, HEADS_PER_STEP
    R = lat.shape[1]
    tq = grp.tq
    qrow, krow = grp.qrow(tq), grp.krow()
    in_specs = [pl.BlockSpec((tq, hp * R), lambda b, h, i: (qrow(b, i), h)),
                pl.BlockSpec((tq, hp * LANES), lambda b, h, i: (qrow(b, i), h)),
                pl.BlockSpec((grp.lq, R), lambda b, h, i: (krow(b), 0)),
                pl.BlockSpec((grp.lq, LANES), lambda b, h, i: (krow(b), ODD_COLS["kr"] // LANES))]
    return pl.pallas_call(
        functools.partial(_mla_body, tq=tq, tk=grp.tk, lq=grp.lq, hp=hp),
        grid=(B, H // hp, grp.lq // tq),
        in_specs=in_specs,
        out_specs=pl.BlockSpec((tq, hp * R), lambda b, h, i: (b * (grp.lq // tq) + i, h)),
        out_shape=jax.ShapeDtypeStruct((B * grp.lq, H * R), MXU_DTYPE),
        compiler_params=_params(3),
        name="mla_attention",
    )(q_abs, q_rope, lat, proj)


def _head_cols(ref, h, d):
    return ref[:, h, :] if len(ref.shape) == 3 else ref[:, h * d:(h + 1) * d]


def _chain_reset(j, m_ref, l_ref, acc_ref):
    @pl.when(j == 0)
    def _():
        m_ref[...] = jnp.full(m_ref.shape, NEG, F32)
        l_ref[...] = jnp.zeros(l_ref.shape, F32)
        acc_ref[...] = jnp.zeros(acc_ref.shape, F32)


def _chain_advance(c, s, v, m_ref, l_ref, acc_ref):
    m, l, acc = _online_step(s, v, (m_ref[c], l_ref[c], acc_ref[c]))
    m_ref[c], l_ref[c], acc_ref[c] = m, l, acc
    return l, acc


def _past_key_spec(cache, layer, tkp, n_past):
    blk = lambda j: jnp.minimum(j, n_past - 1)
    tail = cache.shape[3:] if layer is not None else cache.shape[2:]
    zeros = (0,) * len(tail)
    if layer is None:
        return pl.BlockSpec((None, tkp) + tail, lambda b, j: (b, blk(j)) + zeros)
    return pl.BlockSpec((None, None, tkp) + tail, lambda b, j: (layer, b, blk(j)) + zeros)


def _past_call(body, grp, heads, dv, chains, in_specs, args, name):
    lq = grp.lq
    assert grp.lq == grp.tk
    return pl.pallas_call(
        body,
        grid=(grp.batch, grp.n_past_blocks + 1),
        in_specs=in_specs,
        out_specs=pl.BlockSpec((lq, heads * dv), lambda b, j: (b, 0)),
        out_shape=jax.ShapeDtypeStruct((grp.batch * lq, heads * dv), MXU_DTYPE),
        scratch_shapes=[pltpu.VMEM((chains, lq, 1), F32), pltpu.VMEM((chains, lq, 1), F32),
                        pltpu.VMEM((chains, lq, dv), F32)],
        compiler_params=_params(2),
        name=name,
    )(*args)


def _fox_past_body(q_ref, k_ref, v_ref, pk_ref, pv_ref, fq_ref, fk_ref, o_ref, m_ref, l_ref, acc_ref, *, n_past):
    j = pl.program_id(1)
    d, lq = HEAD_DIM, q_ref.shape[0]
    _chain_reset(j, m_ref, l_ref, acc_ref)
    q = lambda h: _mx(q_ref[:, h * d:(h + 1) * d] * d ** -0.5)

    @pl.when(j < n_past)
    def _():
        for h in range(B_HEADS):
            s = _dot_nt(q(h), _head_cols(pk_ref, h, d)) + fq_ref[h] - fk_ref[h, 0]
            _chain_advance(h, s, _head_cols(pv_ref, h, d), m_ref, l_ref, acc_ref)

    @pl.when(j == n_past)
    def _():
        causal = lax.broadcasted_iota(I32, (lq, lq), 1) <= lax.broadcasted_iota(I32, (lq, lq), 0)
        for h in range(B_HEADS):
            s = _dot_nt(q(h), k_ref[:, h * d:(h + 1) * d]) + fq_ref[h] - fk_ref[h, 0][:, :lq]
            l, acc = _chain_advance(h, jnp.where(causal, s, NEG), v_ref[:, h * d:(h + 1) * d], m_ref, l_ref, acc_ref)
            o_ref[:, h * d:(h + 1) * d] = (acc / l).astype(o_ref.dtype)


def fox_attention_past(grp, proj, fq, fk, cache_k, cache_v, layer):
    H, d, lq = B_HEADS, HEAD_DIM, grp.lq
    row = grp.krow()
    w = H * d
    in_specs = [pl.BlockSpec((lq, w), lambda b, j, n=n: (row(b), EVEN_COLS[n] // w)) for n in ("qf", "kf", "vf")]
    in_specs += [_past_key_spec(cache_k, layer, grp.tkp, grp.n_past_blocks),
                 _past_key_spec(cache_v, layer, grp.tkp, grp.n_past_blocks),
                 pl.BlockSpec((H, lq, 1), lambda b, j: (b, 0, 0)),
                 pl.BlockSpec((H, 1, 1, fk.shape[-1]), lambda b, j: (b, j, 0, 0))]
    return _past_call(functools.partial(_fox_past_body, n_past=grp.n_past_blocks), grp, H, d, H, in_specs,
                      [proj, proj, proj, cache_k, cache_v, fq, fk], "fox_attention_past")


def _diff_past_body(lam_ref, q_ref, k_ref, v_ref, pk_ref, pv_ref, g_ref, o_ref, m_ref, l_ref, acc_ref,
                    *, n_past, out_scale):
    j = pl.program_id(1)
    d = C_V_DIM
    _chain_reset(j, m_ref, l_ref, acc_ref)
    lane = lax.broadcasted_iota(I32, (q_ref.shape[0], d), 1)

    def maps(h):
        q = q_ref[:, h * d:(h + 1) * d] * C_QK_DIM ** -0.5
        return _mx(jnp.where(lane < C_QK_DIM, q, 0.0)), _mx(jnp.where(lane >= C_QK_DIM, q, 0.0))

    @pl.when(j < n_past)
    def _():
        for h in range(C_HEADS):
            k, v = _mx(_head_cols(pk_ref, h, d)), _mx(_head_cols(pv_ref, h, d))
            for t, qt in enumerate(maps(h)):
                _chain_advance(2 * h + t, _dot_nt(qt, k), v, m_ref, l_ref, acc_ref)

    @pl.when(j == n_past)
    def _():
        for h in range(C_HEADS):
            k, v = _mx(k_ref[:, h * d:(h + 1) * d]), _mx(v_ref[:, h * d:(h + 1) * d])
            (l0, a0), (l1, a1) = [_chain_advance(2 * h + t, _dot_nt(qt, k), v, m_ref, l_ref, acc_ref)
                                  for t, qt in enumerate(maps(h))]
            o = a0 / l0 - lam_ref[0] * (a1 / l1)
            o = o * lax.rsqrt(jnp.mean(o * o, axis=-1, keepdims=True) + NORM_EPS)
            o_ref[:, h * d:(h + 1) * d] = (o * g_ref[...] * out_scale).astype(o_ref.dtype)


def diff_attention_past(grp, proj, lam, subln, out_scale, cache_k, cache_v, layer):
    H, d, lq = C_HEADS, C_V_DIM, grp.lq
    assert lq <= CHUNK and grp.lp % CHUNK == 0
    row = grp.krow()
    w = H * d
    in_specs = [pl.BlockSpec(memory_space=pltpu.SMEM)]
    in_specs += [pl.BlockSpec((lq, w), lambda b, j, n=n: (row(b), ODD_COLS[n] // w)) for n in ("qc", "kc", "vc")]
    in_specs += [_past_key_spec(cache_k, None, grp.tkp, grp.n_past_blocks),
                 _past_key_spec(cache_v, layer, grp.tkp, grp.n_past_blocks),
                 pl.BlockSpec((1, d), lambda b, j: (0, 0))]
    return _past_call(functools.partial(_diff_past_body, n_past=grp.n_past_blocks, out_scale=out_scale), grp, H, d,
                      2 * H, in_specs, [lam, proj, proj, proj, cache_k, cache_v, subln], "diff_attention_past")


def _mla_past_body(qa_ref, qr_ref, lat_ref, kr_ref, plat_ref, pkr_ref, o_ref, m_ref, l_ref, acc_ref, *, n_past):
    j = pl.program_id(1)
    R = lat_ref.shape[-1]
    _chain_reset(j, m_ref, l_ref, acc_ref)

    @pl.when(j < n_past)
    def _():
        lat, kr = _mx(plat_ref[...]), _mx(pkr_ref[...])
        for h in range(D_HEADS):
            s = _dot_nt(qa_ref[:, h * R:(h + 1) * R], lat) + _dot_nt(qr_ref[:, h * LANES:h * LANES + D_ROPE], kr)
            _chain_advance(h, s, lat, m_ref, l_ref, acc_ref)

    @pl.when(j == n_past)
    def _():
        lat, kr = _mx(lat_ref[...]), _mx(kr_ref[...])
        for h in range(D_HEADS):
            s = _dot_nt(qa_ref[:, h * R:(h + 1) * R], lat) + _dot_nt(qr_ref[:, h * LANES:(h + 1) * LANES], kr)
            l, acc = _chain_advance(h, s, lat, m_ref, l_ref, acc_ref)
            o_ref[:, h * R:(h + 1) * R] = (acc / l).astype(o_ref.dtype)


def mla_attention_past(grp, q_abs, q_rope, lat, proj, cache_lat, cache_kr, layer):
    H, lq = D_HEADS, grp.lq
    assert lq <= CHUNK and grp.lp % CHUNK == 0
    R = lat.shape[1]
    row = grp.krow()
    in_specs = [pl.BlockSpec((lq, H * R), lambda b, j: (row(b), 0)),
                pl.BlockSpec((lq, H * LANES), lambda b, j: (row(b), 0)),
                pl.BlockSpec((lq, R), lambda b, j: (row(b), 0)),
                pl.BlockSpec((lq, LANES), lambda b, j: (row(b), ODD_COLS["kr"] // LANES)),
                _past_key_spec(cache_lat, layer, grp.tkp, grp.n_past_blocks),
                _past_key_spec(cache_kr, layer, grp.tkp, grp.n_past_blocks)]
    return _past_call(functools.partial(_mla_past_body, n_past=grp.n_past_blocks), grp, H, R, H, in_specs,
                      [q_abs, q_rope, lat, proj, cache_lat, cache_kr], "mla_attention_past")


def _sortable(score):
    bits = pltpu.bitcast(score, I32)
    return jnp.where(bits < 0, bits ^ jnp.int32(0x7FFFFFFF), bits)


def _dsa_body(*refs, tq, i0, nk, topk, has_past):
    if has_past:
        qi_ref, misc_ref, q_ref, ki_ref, k_ref, v_ref, pki_ref, pk_ref, pv_ref, o_ref = refs
    else:
        qi_ref, misc_ref, q_ref, ki_ref, k_ref, v_ref, o_ref = refs
    i = i0 + pl.program_id(1)
    wi = misc_ref[:, MISC_WI:MISC_WI + IDX_HEADS] * (IDX_HEADS * IDX_DIM) ** -0.5

    qchunk = (i * tq + lax.broadcasted_iota(I32, (tq, nk), 0)) // CHUNK
    adm_new = lax.broadcasted_iota(I32, (tq, nk), 1) // CHUNK <= qchunk
    segs = [(_mx(ki_ref[...]), k_ref, v_ref, adm_new)]
    if has_past:
        segs.insert(0, (_mx(pki_ref[0]), pk_ref.at[0], pv_ref.at[0], None))

    keys = []
    for ki, _, _, adm in segs:
        score = jnp.zeros((tq, ki.shape[0]), F32)
        for h in range(IDX_HEADS):
            s = _dot_nt(qi_ref[:, h * IDX_DIM:(h + 1) * IDX_DIM], ki)
            score = score + jnp.maximum(s, 0.0) * wi[:, h:h + 1]
        key = _sortable(score)
        keys.append(key if adm is None else jnp.where(adm, key, INT_MIN))

    def count_ge(cand):
        return sum(jnp.sum(jnp.where(key >= cand, 1.0, 0.0), axis=-1, keepdims=True) for key in keys)

    kf = float(topk)
    t = jnp.where(count_ge(jnp.zeros((tq, 1), I32)) >= kf, 0, INT_MIN).astype(I32)

    def bit_step(it, t):
        cand = t | jnp.left_shift(jnp.int32(1), 30 - it)
        return jnp.where(count_ge(cand) >= kf, cand, t)

    t = lax.fori_loop(0, 31, bit_step, t)
    sels = [(key >= t) if adm is None else (adm & (key >= t)) for key, (_, _, _, adm) in zip(keys, segs)]

    rep = A_HEADS // A_KV_HEADS
    for g in range(A_KV_HEADS):
        kv = [(_mx(k[:, g * HEAD_DIM:(g + 1) * HEAD_DIM]), _mx(v[:, g * HEAD_DIM:(g + 1) * HEAD_DIM]))
              for _, k, v, _ in segs]
        for r in range(rep):
            h = g * rep + r
            q = _mx(q_ref[:, h * HEAD_DIM:(h + 1) * HEAD_DIM] * HEAD_DIM ** -0.5)
            ss = [jnp.where(sel, _dot_nt(q, k), NEG) for sel, (k, _) in zip(sels, kv)]
            m = functools.reduce(jnp.maximum, [jnp.max(s, axis=-1, keepdims=True) for s in ss])
            ps = [jnp.exp(s - m) for s in ss]
            l = sum(jnp.sum(p, axis=-1, keepdims=True) for p in ps)
            o = sum(_dot(p, v) for p, (_, v) in zip(ps, kv))
            o_ref[:, h * HEAD_DIM:(h + 1) * HEAD_DIM] = (o / l).astype(o_ref.dtype)


def dsa_attention(grp, proj, past_ki, past_k, past_v):
    B = grp.batch
    tq = grp.tq_dsa
    n_tiles = grp.lq // tq
    wq, wkv = A_HEADS * HEAD_DIM, A_KV_HEADS * HEAD_DIM
    wqi = IDX_HEADS * IDX_DIM
    c = EVEN_COLS

    extents = sorted({grp.lq // f for f in (1, 2, 4) if grp.lq % f == 0 and (grp.lq // f) % tq == 0
                      and grp.row0 % (grp.lq // f) == 0})
    classes = []
    for i in range(n_tiles):
        need = min(grp.lq, _chunk_last_key(i, tq) + 1)
        nk = min(e for e in extents if e >= need)
        if classes and classes[-1][2] == nk:
            classes[-1][1] += 1
        else:
            classes.append([i, 1, nk])

    outs = []
    for i0, n_i, nk in classes:
        qrow = lambda b, i, i0=i0: grp.row0 // tq + b * n_tiles + i0 + i
        krow = lambda b, nk=nk: (grp.row0 + b * grp.lq) // nk
        in_specs = [pl.BlockSpec((tq, wqi), lambda b, i, f=qrow: (f(b, i), c["qi"] // wqi)),
                    pl.BlockSpec((tq, LANES), lambda b, i, f=qrow: (f(b, i), c["misc"] // LANES)),
                    pl.BlockSpec((tq, wq), lambda b, i, f=qrow: (f(b, i), c["qa"] // wq)),
                    pl.BlockSpec((nk, IDX_DIM), lambda b, i, f=krow: (f(b), c["ki"] // IDX_DIM)),
                    pl.BlockSpec((nk, wkv), lambda b, i, f=krow: (f(b), c["ka"] // wkv)),
                    pl.BlockSpec((nk, wkv), lambda b, i, f=krow: (f(b), c["va"] // wkv))]
        args = [proj] * 6
        if grp.lp:
            in_specs += [pl.BlockSpec((1, grp.lp, IDX_DIM), lambda b, i: (b, 0, 0)),
                         pl.BlockSpec((1, grp.lp, wkv), lambda b, i: (b, 0, 0)),
                         pl.BlockSpec((1, grp.lp, wkv), lambda b, i: (b, 0, 0))]
            args += [past_ki, past_k, past_v]
        o = pl.pallas_call(
            functools.partial(_dsa_body, tq=tq, i0=i0, nk=nk, topk=grp.topk, has_past=bool(grp.lp)),
            grid=(B, n_i),
            in_specs=in_specs,
            out_specs=pl.BlockSpec((tq, wq), lambda b, i, n_i=n_i: (b * n_i + i, 0)),
            out_shape=jax.ShapeDtypeStruct((B * n_i * tq, wq), MXU_DTYPE),
            compiler_params=_params(2),
            name="dsa_attention",
        )(*args)
        outs.append(o.reshape(B, n_i * tq, wq))
    return jnp.concatenate(outs, axis=1).reshape(B * grp.lq, wq)


def _first_index_of_max(x, iota, n):
    mx = jnp.max(x, axis=0, keepdims=True)
    return mx, jnp.min(jnp.where(x == mx, iota, n), axis=0, keepdims=True)


def _route_body(x_ref, sc_ref, sh_ref, wr_ref, bias_ref, hb_ref, gate_ref, rank_ref, cnt_ref, h32_ref, carry_ref,
                *, group):
    step = pl.program_id(0)
    tm = x_ref.shape[0]
    E = wr_ref.shape[0]
    per = E // N_GROUPS
    for g, part in enumerate(_modulated(x_ref, sc_ref, sh_ref, group)):
        h32_ref[g * group:(g + 1) * group, :] = part
    h = h32_ref[...]
    hb_ref[...] = h.astype(hb_ref.dtype)

    scores = jax.nn.sigmoid(_dot_nt(wr_ref[...], h, precision=lax.Precision.HIGHEST))
    biased = scores + bias_ref[...]
    member = lax.broadcasted_iota(I32, (per, tm), 0).astype(F32)
    gscore = []
    for g in range(N_GROUPS):
        blk = biased[g * per:(g + 1) * per, :]
        m1, i1 = _first_index_of_max(blk, member, per)
        m2 = jnp.max(jnp.where(member == i1, -jnp.inf, blk), axis=0, keepdims=True)
        gscore.append(m1 + m2)
    gscore = jnp.concatenate(gscore, axis=0)
    giota = lax.broadcasted_iota(I32, (N_GROUPS, tm), 0).astype(F32)
    gsel = jnp.zeros((N_GROUPS, tm), F32)
    for _ in range(TOPK_GROUPS):
        _, gi = _first_index_of_max(gscore, giota, N_GROUPS)
        hit = giota == gi
        gsel = jnp.where(hit, 1.0, gsel)
        gscore = jnp.where(hit, -jnp.inf, gscore)
    emask = jnp.concatenate([jnp.broadcast_to(gsel[g:g + 1, :], (per, tm)) for g in range(N_GROUPS)], axis=0)
    masked = jnp.where(emask > 0.0, biased, -jnp.inf)
    eiota = lax.broadcasted_iota(I32, (E, tm), 0).astype(F32)
    sel = jnp.zeros((E, tm), jnp.bool_)
    for _ in range(TOP_K):
        _, ei = _first_index_of_max(masked, eiota, E)
        hit = eiota == ei
        sel = sel | hit
        masked = jnp.where(hit, -jnp.inf, masked)
    w = jnp.where(sel, scores, 0.0)
    gate_ref[...] = w / jnp.sum(w, axis=0, keepdims=True) * ROUTED_SCALE

    @pl.when(step == 0)
    def _():
        carry_ref[...] = jnp.zeros(carry_ref.shape, F32)
    upper = (lax.broadcasted_iota(I32, (tm, tm), 0) <= lax.broadcasted_iota(I32, (tm, tm), 1))
    self = jnp.where(sel, 1.0, 0.0)
    incl = _dot(self, jnp.where(upper, 1.0, 0.0)) + carry_ref[...]
    rank_ref[...] = jnp.where(sel, incl - 1.0, -1.0).astype(I32)
    carry_ref[...] = incl[:, tm - 1:tm]
    cnt_ref[...] = incl[:, tm - 1:tm].astype(I32)


def route(x, scale_g, shift_g, w_router_t, r_bias, group, tm):
    T, D = x.shape
    E = w_router_t.shape[0]
    row = lambda i: (i, 0)
    col = lambda i: (0, i)
    fixed = lambda i: (0, 0)
    return pl.pallas_call(
        functools.partial(_route_body, group=group),
        grid=(T // tm,),
        in_specs=[pl.BlockSpec((tm, D), row),
                  pl.BlockSpec((1, tm // group, D), lambda i: (i, 0, 0)),
                  pl.BlockSpec((1, tm // group, D), lambda i: (i, 0, 0)),
                  pl.BlockSpec((E, D), fixed),
                  pl.BlockSpec((E, 1), fixed)],
        out_specs=[pl.BlockSpec((tm, D), row), pl.BlockSpec((E, tm), col), pl.BlockSpec((E, tm), col),
                   pl.BlockSpec((E, 1), fixed)],
        out_shape=[jax.ShapeDtypeStruct((T, D), MXU_DTYPE), jax.ShapeDtypeStruct((E, T), F32),
                   jax.ShapeDtypeStruct((E, T), I32), jax.ShapeDtypeStruct((E, 1), I32)],
        scratch_shapes=[pltpu.VMEM((tm, D), F32), pltpu.VMEM((E, 1), F32)],
        compiler_params=_params(1),
        name="route",
    )(x, _per_tile(scale_g, tm, group), _per_tile(shift_g, tm, group), w_router_t, r_bias)


def _compact_body(gate_ref, rank_ref, off_ref, dest_ref, g8_ref):
    E, tm = gate_ref.shape
    rank = rank_ref[...]
    sel = rank >= 0
    dest = (rank + off_ref[...]).astype(F32)
    gate = gate_ref[...]
    eiota = lax.broadcasted_iota(I32, (E, tm), 0).astype(F32)
    dests, gates = [], []
    for _ in range(TOP_K):
        ei = jnp.min(jnp.where(sel, eiota, E), axis=0, keepdims=True)
        hit = eiota == ei
        dests.append(jnp.sum(jnp.where(hit, dest, 0.0), axis=0, keepdims=True))
        gates.append(jnp.sum(jnp.where(hit, gate, 0.0), axis=0, keepdims=True))
        sel = sel & jnp.logical_not(hit)
    dest_ref[...] = jnp.concatenate(dests, axis=0).astype(I32)
    g8_ref[...] = jnp.concatenate(gates, axis=0)


def compact_routes(gate, rank, seg_off, tm):
    E, T = gate.shape
    col = lambda i: (0, i)
    return pl.pallas_call(
        _compact_body,
        grid=(T // tm,),
        in_specs=[pl.BlockSpec((E, tm), col), pl.BlockSpec((E, tm), col), pl.BlockSpec((E, 1), lambda i: (0, 0))],
        out_specs=[pl.BlockSpec((TOP_K, tm), col), pl.BlockSpec((TOP_K, tm), col)],
        out_shape=[jax.ShapeDtypeStruct((TOP_K, T), I32), jax.ShapeDtypeStruct((TOP_K, T), F32)],
        compiler_params=_params(1),
        name="compact_routes",
    )(gate, rank, seg_off)


def _ffn(x, w1, w3, w2):
    a = _dot(x, w1)
    return _dot(a * jax.nn.sigmoid(a) * _dot(x, w3), w2)


def _gmm_body(te_ref, nt_ref, x_ref, w1_ref, w3_ref, w2_ref, o_ref, w1b, w3b, w2b):
    i = pl.program_id(0)
    changed = jnp.logical_or(i == 0, te_ref[i] != te_ref[jnp.maximum(i - 1, 0)])

    @pl.when(changed)
    def _():
        w1b[...] = _mx(w1_ref[...])
        w3b[...] = _mx(w3_ref[...])
        w2b[...] = _mx(w2_ref[...])

    @pl.when(i < nt_ref[0])
    def _():
        o_ref[...] = _ffn(x_ref[...], w1b[...], w3b[...], w2b[...]).astype(o_ref.dtype)

    @pl.when(i >= nt_ref[0])
    def _():
        o_ref[...] = jnp.zeros(o_ref.shape, o_ref.dtype)


def grouped_ffn(tile_expert, n_tiles, x, w1, w3, w2, layer, tm):
    P, D = x.shape
    F = w1.shape[-1]
    row = lambda i, te, nt: (jnp.maximum(jnp.minimum(i, nt[0] - 1), 0), 0)
    wsel = lambda i, te, nt: (layer, te[i], 0, 0)
    return pl.pallas_call(
        _gmm_body,
        grid_spec=pltpu.PrefetchScalarGridSpec(
            num_scalar_prefetch=2,
            grid=(P // tm,),
            in_specs=[pl.BlockSpec((tm, D), row),
                      pl.BlockSpec((None, None, D, F), wsel),
                      pl.BlockSpec((None, None, D, F), wsel),
                      pl.BlockSpec((None, None, F, D), wsel)],
            out_specs=pl.BlockSpec((tm, D), lambda i, te, nt: (i, 0)),
            scratch_shapes=[pltpu.VMEM((D, F), MXU_DTYPE), pltpu.VMEM((D, F), MXU_DTYPE),
                            pltpu.VMEM((F, D), MXU_DTYPE)]),
        out_shape=jax.ShapeDtypeStruct((P, D), MXU_DTYPE),
        compiler_params=_params(1),
        name="grouped_ffn",
    )(tile_expert, n_tiles, x, w1, w3, w2)


def _close_moe_body(hb_ref, yk_ref, g8_ref, ws1_ref, ws3_ref, ws2_ref, x_ref, gate_ref, g_ref, b_ref, o_ref,
                    *, group, alpha):
    y = _ffn(hb_ref[...], ws1_ref[...], ws3_ref[...], ws2_ref[...])
    g8 = g8_ref[...]
    for k in range(yk_ref.shape[0]):
        y = y + g8[:, k:k + 1] * yk_ref[k]
    r = _gated_residual(x_ref, y, gate_ref, group, alpha)
    o_ref[...] = _layer_norm_rows(r, g_ref[...], b_ref[...])


def close_moe(hb, y_k, g8, ws1, ws3, ws2, x, gate_g, ln_g, ln_b, group, alpha, tm):
    T, D = x.shape
    K = y_k.shape[0]
    row = lambda i: (i, 0)
    fixed = lambda i: (0, 0)
    return pl.pallas_call(
        functools.partial(_close_moe_body, group=group, alpha=alpha),
        grid=(T // tm,),
        in_specs=[pl.BlockSpec((tm, D), row),
                  pl.BlockSpec((K, tm, D), lambda i: (0, i, 0)),
                  pl.BlockSpec((tm, K), row),
                  pl.BlockSpec(ws1.shape, fixed),
                  pl.BlockSpec(ws3.shape, fixed),
                  pl.BlockSpec(ws2.shape, fixed),
                  pl.BlockSpec((tm, D), row),
                  pl.BlockSpec((1, tm // group, D), lambda i: (i, 0, 0)),
                  pl.BlockSpec((1, D), fixed),
                  pl.BlockSpec((1, D), fixed)],
        out_specs=pl.BlockSpec((tm, D), row),
        out_shape=jax.ShapeDtypeStruct((T, D), F32),
        compiler_params=_params(1),
        name="close_moe",
    )(hb, y_k, g8, ws1, ws3, ws2, x, _per_tile(gate_g, tm, group), ln_g, ln_b)


def _rope_tables(pos, half):
    inv_freq = jnp.power(ROPE_THETA, -jnp.arange(half, dtype=F32) / half)
    ang = pos.astype(F32)[:, None] * inv_freq[None, :]
    cos, sin = jnp.cos(ang), jnp.sin(ang)
    reps = LANES // (2 * half)
    cos = jnp.tile(jnp.concatenate([cos, cos], axis=1), (1, reps))
    sin = jnp.tile(jnp.concatenate([-sin, sin], axis=1), (1, reps))
    return jnp.stack([jnp.ones_like(cos), cos]), jnp.stack([jnp.zeros_like(sin), sin])


def _place_cols(w, sizes, names, cols, width):
    out = jnp.zeros((w.shape[0], width), w.dtype)
    o = 0
    for size, name in zip(sizes, names):
        if name is not None:
            dst, sub = name
            out = lax.dynamic_update_slice(out, w[:, o:o + size], (0, cols[dst] + sub))
        o += size
    return out


def _moe(x, scale_g, shift_g, gate_g, ln_g, ln_b, w_router, r_bias, w1, w3, w2, ws1, ws3, ws2, layer, group, alpha):
    T, D = x.shape
    E = w1.shape[1]
    tm = 256
    t_route = _pick(T, 768, math.lcm(LANES, group))
    hb, gate, rank, counts = route(x, scale_g, shift_g, w_router.T, r_bias.reshape(E, 1), group, t_route)

    counts = counts[:, 0]
    tiles_per = (counts + tm - 1) // tm
    tile_end = jnp.cumsum(tiles_per)
    seg_off = ((tile_end - tiles_per) * tm).astype(I32)
    n_tiles = (T * TOP_K) // tm + E
    used = tile_end[-1].astype(I32)
    tile_ids = jnp.minimum(jnp.arange(n_tiles, dtype=I32), used - 1)
    tile_expert = jnp.sum((tile_end[None, :] <= tile_ids[:, None]).astype(I32), axis=1)

    dest, g8 = compact_routes(gate, rank, seg_off.reshape(E, 1), t_route)
    tok = jnp.broadcast_to(jnp.arange(T, dtype=I32)[None, :], dest.shape)
    row_token = jnp.zeros((n_tiles * tm,), I32).at[dest.reshape(-1)].set(
        tok.reshape(-1), unique_indices=True, mode="promise_in_bounds")
    y_rows = grouped_ffn(tile_expert, used.reshape(1), hb[row_token], w1, w3, w2, layer, tm)
    return close_moe(hb, y_rows[dest], g8.T, _mx(ws1), _mx(ws3), _mx(ws2), x, gate_g, ln_g, ln_b, group, alpha,
                     _pick(T, 256, math.lcm(2 * SUBLANES, group)))


def _fox_gates(grp, proj, b_f, past_logf):
    B, H = grp.batch, B_HEADS
    c0 = EVEN_COLS["misc"] + MISC_FL
    fl = proj[grp.row0:grp.row0 + B * grp.lq, c0:c0 + H].reshape(B, grp.lq, H)
    logf = jax.nn.log_sigmoid(fl + b_f)
    logf_all = logf if past_logf is None else jnp.concatenate([past_logf, logf], axis=1)
    cum = jnp.moveaxis(jnp.cumsum(logf_all, axis=1), 1, 2).reshape(B * H, grp.lk)
    fq = cum[:, grp.lp:, None]
    new = cum[:, grp.lp:].reshape(B * H, grp.lq // grp.tk, 1, grp.tk)
    new = jnp.pad(new, ((0, 0), (0, 0), (0, 0), (0, KEY_BLOCK - grp.tk)))
    if grp.lp:
        past = cum[:, :grp.lp].reshape(B * H, grp.n_past_blocks, 1, grp.tkp)
        past = jnp.pad(past, ((0, 0), (0, 0), (0, 0), (0, KEY_BLOCK - grp.tkp)))
        new = jnp.concatenate([past, new], axis=1)
    return logf, fq, new


def kernel(x_prompt, x_sample, c_prompt, c_sample, cache_a_k, cache_a_v, cache_a_kidx, cache_b_k, cache_b_v, cache_b_logf, cache_c_k, cache_c_v, cache_d_latent, cache_d_krope, w_in_even, b_forget, w_out_even, w_in_odd, c_lambda, c_subln, d_q_norm, d_w_uq, d_kv_norm, d_w_uk, d_w_uv, w_out_odd, ada_mix_w, ada_mix_b, ln_mix_g, ln_mix_b, ada_ffn_w, ada_ffn_b, ln_ffn_g, ln_ffn_b, router_w, router_bias, moe_w1, moe_w3, moe_w2, shared_w1, shared_w3, shared_w2):
    B, L, D = x_prompt.shape
    Bs, Ls, _ = x_sample.shape
    depth = ada_mix_w.shape[0]
    alpha = (2 * depth) ** 0.25
    past_len = cache_a_k.shape[2]
    Tp = B * L
    T = Tp + Bs * Ls
    grp_p = _Group(B, L, 0, 0)
    grp_s = _Group(Bs, Ls, past_len, Tp)
    group = math.gcd(L, Ls)
    assert group % (2 * SUBLANES) == 0
    tm_tok = _pick(T, 768, math.lcm(LANES, group))
    tm_close = _pick(T, 512, math.lcm(2 * SUBLANES, group))
    pos = jnp.concatenate([jnp.tile(jnp.arange(L), B), jnp.tile(past_len + jnp.arange(Ls), Bs)])
    c_act = jax.nn.silu(jnp.concatenate([c_prompt, c_sample], axis=0))

    def conditioning(ada_w, ada_b, layer):
        mod = matmul(c_act, ada_w, layer=layer) + ada_b[layer]
        per_group = jnp.concatenate([jnp.repeat(mod[:B], L // group, axis=0),
                                     jnp.repeat(mod[B:], Ls // group, axis=0)], axis=0)
        return jnp.split(per_group, 3, axis=-1)

    def flat_past(c):
        return c.reshape(c.shape[0], c.shape[1], -1)

    x = jnp.concatenate([x_prompt.reshape(Tp, D), x_sample.reshape(Bs * Ls, D)], axis=0)
    ev_p, ev_s, od_p, od_s = [], [], [], []
    for i in range(depth):
        j = i // 2
        shift_g, scale_g, gate_g = conditioning(ada_mix_w, ada_mix_b, i)
        if i % 2 == 0:
            sizes = (A_HEADS * HEAD_DIM, A_KV_HEADS * HEAD_DIM, A_KV_HEADS * HEAD_DIM, IDX_HEADS * IDX_DIM, IDX_DIM,
                     IDX_HEADS, B_HEADS * HEAD_DIM, B_HEADS * HEAD_DIM, B_HEADS * HEAD_DIM, B_HEADS)
            names = (("qa", 0), ("ka", 0), ("va", 0), ("qi", 0), ("ki", 0), ("misc", MISC_WI),
                     ("qf", 0), ("kf", 0), ("vf", 0), ("misc", MISC_FL))
            w_in = _mx(_place_cols(w_in_even[j], sizes, names, EVEN_COLS, EVEN_WIDTH))
            cos, sin = _rope_tables(pos, HEAD_DIM // 2)
            proj = project(x, scale_g, shift_g, w_in, cos, sin, EVEN_ROPE_RANGES, HEAD_DIM // 2, group, tm_tok)
            outs_a, outs_b = [], []
            for grp, store in ((grp_p, ev_p), (grp_s, ev_s)):
                if grp.lp:
                    logf, fq, fk = _fox_gates(grp, proj, b_forget[j], cache_b_logf[j])
                    outs_a.append(dsa_attention(grp, proj, cache_a_kidx[j], flat_past(cache_a_k[j]),
                                                flat_past(cache_a_v[j])))
                    outs_b.append(fox_attention_past(grp, proj, fq, fk, cache_b_k, cache_b_v, j))
                else:
                    logf, fq, fk = _fox_gates(grp, proj, b_forget[j], None)
                    outs_a.append(dsa_attention(grp, proj, None, None, None))
                    outs_b.append(fox_attention(grp, proj, fq, fk))
                r0, r1 = grp.row0, grp.row0 + grp.batch * grp.lq
                cut = lambda name, heads, d: proj[r0:r1, EVEN_COLS[name]:EVEN_COLS[name] + heads * d].reshape(
                    (grp.batch, grp.lq) + ((heads, d) if heads > 1 else (d,)))
                store.append((cut("ka", A_KV_HEADS, HEAD_DIM), cut("va", A_KV_HEADS, HEAD_DIM), cut("ki", 1, IDX_DIM),
                              cut("kf", B_HEADS, HEAD_DIM), cut("vf", B_HEADS, HEAD_DIM), logf))
            a1, a2 = jnp.concatenate(outs_a, axis=0), jnp.concatenate(outs_b, axis=0)
            n1 = A_HEADS * HEAD_DIM
            w1o, w2o = _mx(w_out_even[j][:n1]), _mx(w_out_even[j][n1:])
        else:
            lam_init = 0.8 - 0.6 * math.exp(-0.3 * i)
            lam_f = c_lambda[j]
            lam = (jnp.exp(jnp.sum(lam_f[0] * lam_f[1])) - jnp.exp(jnp.sum(lam_f[2] * lam_f[3])) + lam_init).reshape(1)
            q_lora, R = d_q_norm.shape[1], d_kv_norm.shape[1]
            sizes = (C_HEADS * 2 * C_QK_DIM, C_HEADS * 2 * C_QK_DIM, C_HEADS * C_V_DIM, q_lora, R, D_ROPE)
            names = (("qc", 0), ("kc", 0), ("vc", 0), ("qd", 0), ("ckv", 0), ("kr", 0))
            w_in = _mx(_place_cols(w_in_odd[j], sizes, names, ODD_COLS, ODD_WIDTH))
            cos, sin = _rope_tables(pos, C_QK_DIM // 2)
            proj = project(x, scale_g, shift_g, w_in, cos, sin, ODD_ROPE_RANGES, C_QK_DIM // 2, group, tm_tok)

            w_uq = d_w_uq[j].reshape(q_lora, D_HEADS, D_NOPE + D_ROPE)
            w_abs = bmm_precise(jnp.moveaxis(w_uq[:, :, :D_NOPE], 1, 0), jnp.transpose(d_w_uk[j], (1, 2, 0)))
            w_abs = jnp.moveaxis(w_abs, 0, 1).reshape(q_lora, D_HEADS * R)
            w_rope = jnp.pad(w_uq[:, :, D_NOPE:], ((0, 0), (0, 0), (0, LANES - D_ROPE))).reshape(q_lora, D_HEADS * LANES)
            q_abs, q_rope, lat = mla_prepare(proj, d_q_norm[j][None], d_kv_norm[j][None],
                                             _mx(jnp.concatenate([w_abs, w_rope], axis=1)), cos, sin, tm_close)
            n1 = C_HEADS * C_V_DIM
            w_od = w_out_odd[j][n1:].reshape(D_HEADS, D_V, D)
            w_lat_out = bmm_precise(jnp.transpose(d_w_uv[j], (1, 0, 2)), w_od).reshape(D_HEADS * R, D)

            outs_a, outs_b = [], []
            for grp, store in ((grp_p, od_p), (grp_s, od_s)):
                if grp.lp:
                    outs_a.append(diff_attention_past(grp, proj, lam, c_subln[j][None], 1.0 - lam_init,
                                                      flat_past(cache_c_k[j]), cache_c_v, j))
                    outs_b.append(mla_attention_past(grp, q_abs, q_rope, lat, proj, cache_d_latent, cache_d_krope, j))
                else:
                    outs_a.append(diff_attention(grp, proj, lam, c_subln[j][None], 1.0 - lam_init))
                    outs_b.append(mla_attention(grp, q_abs, q_rope, lat, proj))
                r0, r1 = grp.row0, grp.row0 + grp.batch * grp.lq
                shp = (grp.batch, grp.lq)
                oc = ODD_COLS
                store.append((proj[r0:r1, oc["kc"]:oc["kc"] + C_HEADS * 2 * C_QK_DIM].reshape(shp + (C_HEADS, 2, C_QK_DIM)),
                              proj[r0:r1, oc["vc"]:oc["vc"] + n1].reshape(shp + (C_HEADS, C_V_DIM)),
                              lat[r0:r1].reshape(shp + (R,)),
                              proj[r0:r1, oc["kr"]:oc["kr"] + D_ROPE].reshape(shp + (D_ROPE,))))
            a1, a2 = jnp.concatenate(outs_a, axis=0), jnp.concatenate(outs_b, axis=0)
            w1o, w2o = _mx(w_out_odd[j][:n1]), _mx(w_lat_out)
        x = close_mixer(a1, a2, w1o, w2o, x, gate_g, ln_mix_g[i][None], ln_mix_b[i][None], group, alpha, tm_close)

        shift_g, scale_g, gate_g = conditioning(ada_ffn_w, ada_ffn_b, i)
        x = _moe(x, scale_g, shift_g, gate_g, ln_ffn_g[i][None], ln_ffn_b[i][None], router_w[i], router_bias[i],
                 moe_w1, moe_w3, moe_w2, shared_w1[i], shared_w3[i], shared_w2[i], i, group, alpha)

    stack = lambda rows, idx: jnp.stack([r[idx] for r in rows])
    outs = [x[:Tp].reshape(B, L, D), x[Tp:].reshape(Bs, Ls, D)]
    for idx in range(6):
        outs += [stack(ev_p, idx), stack(ev_s, idx)]
    for idx in range(4):
        outs += [stack(od_p, idx), stack(od_s, idx)]
    return tuple(outs)
```

```python
import functools
import math

import jax
import jax.numpy as jnp
from jax import lax
from jax.experimental import pallas as pl
from jax.experimental.pallas import tpu as pltpu

CHUNK = 64
ROPE_THETA = 10000.0
HEAD_DIM = 128
A_HEADS = 8
A_KV_HEADS = 2
IDX_HEADS = 16
IDX_DIM = 128
A_TOPK_MAX = 256
B_HEADS = 8
C_HEADS = 8
C_QK_DIM = 64
C_V_DIM = 128
D_HEADS = 8
D_NOPE = 128
D_ROPE = 64
D_V = 128
N_GROUPS = 8
TOPK_GROUPS = 4
TOP_K = 8
ROUTED_SCALE = 2.5
NORM_EPS = 1e-6

LANES = 128
SUBLANES = 8
MXU_WIDTH = 256
VMEM_LIMIT_BYTES = 56 * 2**20
MXU_DTYPE = jnp.bfloat16

F32 = jnp.float32
I32 = jnp.int32
NEG = -1e30
INT_MIN = -2**31
KEY_BLOCK = 512
QUERY_BLOCK = 512
HEADS_PER_STEP = 2

EVEN_COLS = dict(qi=0, qa=2048, qf=3072, kf=4096, vf=5120, ka=6144, va=6400, ki=6656, misc=6912)
EVEN_WIDTH = 7168
EVEN_ROPE_RANGES = ((0, 3072), (6144, 6400), (6656, 6912))
MISC_WI, MISC_FL = 0, IDX_HEADS
ODD_COLS = dict(qc=0, kc=1024, vc=2048, qd=3072, ckv=3584, kr=3840)
ODD_WIDTH = 4096
ODD_ROPE_RANGES = ((0, 2048), (3840, 4096))


def _pick(n, target, mult):
    best = None
    for d in range(mult, min(n, target) + 1, mult):
        if n % d == 0:
            best = d
    return n if best is None else best


def _params(n_axes):
    return pltpu.CompilerParams(dimension_semantics=("arbitrary",) * n_axes,
                                vmem_limit_bytes=VMEM_LIMIT_BYTES)


def _mx(a):
    return a.astype(MXU_DTYPE)


def _dot(a, b):
    return jnp.dot(_mx(a), _mx(b), preferred_element_type=F32)


def _dot_nt(a, b, precision=None):
    if precision is None:
        a, b = _mx(a), _mx(b)
    return lax.dot_general(a, b, (((1,), (1,)), ((), ())), preferred_element_type=F32, precision=precision)


def _mm_body(a_ref, b_ref, o_ref, *, precise):
    if precise:
        o = jnp.dot(a_ref[...], b_ref[...], preferred_element_type=F32, precision=lax.Precision.HIGHEST)
    else:
        o = _dot(a_ref[...], b_ref[...])
    o_ref[...] = o.astype(o_ref.dtype)


def matmul(a, b, out_dtype=F32, precise=False, tm_target=1024, tn_target=512, layer=None):
    M, K = a.shape
    N = b.shape[-1]
    tm = _pick(M, tm_target, SUBLANES)
    tn = _pick(N, tn_target, LANES)
    if layer is None:
        b_spec = pl.BlockSpec((K, tn), lambda i, j: (0, j))
    else:
        b_spec = pl.BlockSpec((None, K, tn), lambda i, j: (layer, 0, j))
    return pl.pallas_call(
        functools.partial(_mm_body, precise=precise),
        grid=(M // tm, N // tn),
        in_specs=[pl.BlockSpec((tm, K), lambda i, j: (i, 0)), b_spec],
        out_specs=pl.BlockSpec((tm, tn), lambda i, j: (i, j)),
        out_shape=jax.ShapeDtypeStruct((M, N), out_dtype),
        compiler_params=_params(2),
        name="matmul",
    )(a, b)


def _bmm_body(a_ref, b_ref, o_ref):
    o_ref[0] = jnp.dot(a_ref[0], b_ref[0], preferred_element_type=F32,
                       precision=lax.Precision.HIGHEST).astype(o_ref.dtype)


def bmm_precise(a, b):
    H, M, K = a.shape
    N = b.shape[2]
    return pl.pallas_call(
        _bmm_body,
        grid=(H,),
        in_specs=[pl.BlockSpec((1, M, K), lambda h: (h, 0, 0)),
                  pl.BlockSpec((1, K, N), lambda h: (h, 0, 0))],
        out_specs=pl.BlockSpec((1, M, N), lambda h: (h, 0, 0)),
        out_shape=jax.ShapeDtypeStruct((H, M, N), F32),
        compiler_params=_params(1),
        name="bmm_precise",
    )(a, b)


def _per_tile(table, tm, group):
    return table.reshape(table.shape[0] * group // tm, tm // group, table.shape[1])


def _modulated(x_ref, sc_ref, sh_ref, group):
    parts = []
    for g in range(x_ref.shape[0] // group):
        rows = slice(g * group, (g + 1) * group)
        parts.append(x_ref[rows, :] * (1.0 + sc_ref[0, g:g + 1, :]) + sh_ref[0, g:g + 1, :])
    return parts


def _rope_lanes(x, cos, sin, half):
    if 2 * half == LANES:
        partner = pltpu.roll(x, half, axis=1)
    else:
        lane = lax.broadcasted_iota(I32, x.shape, 1)
        lower = (lane % (2 * half)) < half
        partner = jnp.where(lower, pltpu.roll(x, LANES - half, axis=1), pltpu.roll(x, half, axis=1))
    return x * cos + partner * sin


def _layer_norm_rows(r, g, b):
    rc = r - jnp.mean(r, axis=-1, keepdims=True)
    var = jnp.mean(rc * rc, axis=-1, keepdims=True)
    return rc * lax.rsqrt(var + NORM_EPS) * g + b


def _gated_residual(x_ref, y, gate_ref, group, alpha):
    parts = []
    for g in range(x_ref.shape[0] // group):
        rows = slice(g * group, (g + 1) * group)
        parts.append(alpha * x_ref[rows, :] + (1.0 + gate_ref[0, g:g + 1, :]) * y[rows, :])
    return jnp.concatenate(parts, axis=0)


def _proj_body(kind_ref, x_ref, sc_ref, sh_ref, w_ref, *rest, group, half, n_sub):
    tables, (o_ref, h_ref) = rest[:2 * n_sub], rest[2 * n_sub:]
    j = pl.program_id(1)

    @pl.when(j == 0)
    def _():
        for g, part in enumerate(_modulated(x_ref, sc_ref, sh_ref, group)):
            h_ref[g * group:(g + 1) * group, :] = part.astype(h_ref.dtype)

    for c in range(n_sub):
        cos, sin = tables[2 * c][...], tables[2 * c + 1][...]
        acc = jnp.dot(h_ref[...], w_ref[:, c * MXU_WIDTH:(c + 1) * MXU_WIDTH], preferred_element_type=F32)
        for p in range(MXU_WIDTH // LANES):
            o_ref[:, c * MXU_WIDTH + p * LANES:c * MXU_WIDTH + (p + 1) * LANES] = _rope_lanes(
                acc[:, p * LANES:(p + 1) * LANES], cos, sin, half)


def project(x, scale_g, shift_g, w, cos, sin, rope_ranges, half, group, tm):
    T, D = x.shape
    N = w.shape[1]
    n_sub = 2
    tn = n_sub * MXU_WIDTH
    start = jnp.arange(N // MXU_WIDTH) * MXU_WIDTH
    kinds = sum(((start >= lo) & (start < hi)).astype(I32) for lo, hi in rope_ranges)
    row = lambda i, j, k: (i, 0)
    tile = lambda i, j, k: (i, 0, 0)
    table_specs, table_args = [], []
    for c in range(n_sub):
        pick = lambda i, j, k, c=c: (k[j * n_sub + c], i, 0)
        table_specs += [pl.BlockSpec((None, tm, LANES), pick)] * 2
        table_args += [cos, sin]
    return pl.pallas_call(
        functools.partial(_proj_body, group=group, half=half, n_sub=n_sub),
        grid_spec=pltpu.PrefetchScalarGridSpec(
            num_scalar_prefetch=1,
            grid=(T // tm, N // tn),
            in_specs=[pl.BlockSpec((tm, D), row),
                      pl.BlockSpec((1, tm // group, D), tile),
                      pl.BlockSpec((1, tm // group, D), tile),
                      pl.BlockSpec((D, tn), lambda i, j, k: (0, j))] + table_specs,
            out_specs=pl.BlockSpec((tm, tn), lambda i, j, k: (i, j)),
            scratch_shapes=[pltpu.VMEM((tm, D), MXU_DTYPE)]),
        out_shape=jax.ShapeDtypeStruct((T, N), F32),
        compiler_params=_params(2),
        name="project",
    )(kinds, x, _per_tile(scale_g, tm, group), _per_tile(shift_g, tm, group), w, *table_args)


def _close_mix_body(a1_ref, a2_ref, w1_ref, w2_ref, x_ref, gate_ref, g_ref, b_ref, o_ref, *, group, alpha):
    y = _dot(a1_ref[...], w1_ref[...]) + _dot(a2_ref[...], w2_ref[...])
    r = _gated_residual(x_ref, y, gate_ref, group, alpha)
    o_ref[...] = _layer_norm_rows(r, g_ref[...], b_ref[...])


def close_mixer(a1, a2, w1, w2, x, gate_g, ln_g, ln_b, group, alpha, tm):
    T, D = x.shape
    row = lambda i: (i, 0)
    fixed = lambda i: (0, 0)
    return pl.pallas_call(
        functools.partial(_close_mix_body, group=group, alpha=alpha),
        grid=(T // tm,),
        in_specs=[pl.BlockSpec((tm, a1.shape[1]), row),
                  pl.BlockSpec((tm, a2.shape[1]), row),
                  pl.BlockSpec(w1.shape, fixed),
                  pl.BlockSpec(w2.shape, fixed),
                  pl.BlockSpec((tm, D), row),
                  pl.BlockSpec((1, tm // group, D), lambda i: (i, 0, 0)),
                  pl.BlockSpec((1, D), fixed),
                  pl.BlockSpec((1, D), fixed)],
        out_specs=pl.BlockSpec((tm, D), row),
        out_shape=jax.ShapeDtypeStruct((T, D), F32),
        compiler_params=_params(1),
        name="close_mixer",
    )(a1, a2, w1, w2, x, _per_tile(gate_g, tm, group), ln_g, ln_b)


def _online_step(s, v, carry):
    m, l, acc = carry
    m_new = jnp.maximum(m, jnp.max(s, axis=-1, keepdims=True))
    p = jnp.exp(s - m_new)
    alpha = jnp.exp(m - m_new)
    l = alpha * l + jnp.sum(p, axis=-1, keepdims=True)
    acc = alpha * acc + _dot(p, v)
    return m_new, l, acc


def _init_carry(tq, dv):
    return (jnp.full((tq, 1), NEG, F32), jnp.zeros((tq, 1), F32), jnp.zeros((tq, dv), F32))


def _chunk_last_key(i, tq):
    return ((i + 1) * tq - 1) // CHUNK * CHUNK + CHUNK - 1


class _Group:
    def __init__(self, batch, lq, lp, row0):
        self.batch, self.lq, self.lp, self.row0 = batch, lq, lp, row0
        self.lk = lp + lq
        self.tq = _pick(lq, QUERY_BLOCK, SUBLANES)
        self.tq_dsa = _pick(lq, 128, SUBLANES)
        self.tk = _pick(lq, KEY_BLOCK, SUBLANES)
        self.tkp = _pick(lp, KEY_BLOCK, LANES) if lp else 0
        self.n_past_blocks = lp // self.tkp if lp else 0
        self.topk = min(A_TOPK_MAX, self.lk // 4)
        assert lp % CHUNK == 0 and row0 % lq == 0 and lq % self.tk == 0

    def qrow(self, tq):
        base, per = self.row0 // tq, self.lq // tq
        return lambda b, i: base + b * per + i

    def krow(self):
        base = self.row0 // self.lq
        return lambda b: base + b


def _new_key_blocks(i, tq, grp_lq, tk, causal_last):
    return jnp.minimum(causal_last, grp_lq - 1) // tk + 1


def _fox_body(q_ref, k_ref, v_ref, fq_ref, fk_ref, o_ref, *, tq, tk, lq, hp):
    i = pl.program_id(2)
    d = HEAD_DIM
    head = lambda x, u: x[:, u * d:(u + 1) * d]
    q = _mx(q_ref[...] * d ** -0.5)
    carry = tuple(_init_carry(tq, d) for _ in range(hp))
    qpos = i * tq + lax.broadcasted_iota(I32, (tq, tk), 0)

    def new_step(j, c):
        off = pl.multiple_of(j * tk, tk)
        k, v = k_ref[pl.ds(off, tk), :], v_ref[pl.ds(off, tk), :]
        causal = off + lax.broadcasted_iota(I32, (tq, tk), 1) <= qpos
        out = []
        for u in range(hp):
            s = _dot_nt(head(q, u), head(k, u)) + fq_ref[u] - fk_ref[u, j][:, :tk]
            out.append(_online_step(jnp.where(causal, s, NEG), head(v, u), c[u]))
        return tuple(out)

    nb = _new_key_blocks(i, tq, lq, tk, (i + 1) * tq - 1)
    carry = lax.fori_loop(0, nb, new_step, carry)
    for u, (m, l, acc) in enumerate(carry):
        o_ref[:, u * d:(u + 1) * d] = (acc / l).astype(o_ref.dtype)


def fox_attention(grp, proj, fq, fk):
    assert grp.lp == 0
    B, H, hp = grp.batch, B_HEADS, HEADS_PER_STEP
    w = hp * HEAD_DIM
    tq = grp.tq
    qrow, krow = grp.qrow(tq), grp.krow()
    cq, ck, cv = (EVEN_COLS[n] // w for n in ("qf", "kf", "vf"))
    in_specs = [pl.BlockSpec((tq, w), lambda b, h, i: (qrow(b, i), cq + h)),
                pl.BlockSpec((grp.lq, w), lambda b, h, i: (krow(b), ck + h)),
                pl.BlockSpec((grp.lq, w), lambda b, h, i: (krow(b), cv + h))]
    in_specs += [pl.BlockSpec((hp, tq, 1), lambda b, h, i: (b * (H // hp) + h, i, 0)),
                 pl.BlockSpec((hp,) + fk.shape[1:], lambda b, h, i: (b * (H // hp) + h, 0, 0, 0))]
    return pl.pallas_call(
        functools.partial(_fox_body, tq=tq, tk=grp.tk, lq=grp.lq, hp=hp),
        grid=(B, H // hp, grp.lq // tq),
        in_specs=in_specs,
        out_specs=pl.BlockSpec((tq, w), lambda b, h, i: (b * (grp.lq // tq) + i, h)),
        out_shape=jax.ShapeDtypeStruct((B * grp.lq, H * HEAD_DIM), MXU_DTYPE),
        compiler_params=_params(3),
        name="fox_attention",
    )(proj, proj, proj, fq, fk)


def _diff_body(lam_ref, q_ref, k_ref, v_ref, g_ref, o_ref, *, tq, tk, lq, hp, out_scale):
    i = pl.program_id(2)
    d = C_V_DIM
    head = lambda x, u: x[:, u * d:(u + 1) * d]
    q = q_ref[...] * C_QK_DIM ** -0.5
    lane = lax.broadcasted_iota(I32, q.shape, 1) % (2 * C_QK_DIM)
    qs = (_mx(jnp.where(lane < C_QK_DIM, q, 0.0)), _mx(jnp.where(lane >= C_QK_DIM, q, 0.0)))
    carry = tuple(_init_carry(tq, d) for _ in range(2 * hp))

    def step(k, v, ok, c):
        out = []
        for u in range(hp):
            ku, vu = _mx(head(k, u)), _mx(head(v, u))
            for t in range(2):
                s = _dot_nt(head(qs[t], u), ku)
                out.append(_online_step(jnp.where(ok, s, NEG), vu, c[2 * u + t]))
        return tuple(out)

    qchunk = (i * tq + lax.broadcasted_iota(I32, (tq, tk), 0)) // CHUNK

    def new_step(j, c):
        off = pl.multiple_of(j * tk, tk)
        ok = (off + lax.broadcasted_iota(I32, (tq, tk), 1)) // CHUNK <= qchunk
        return step(k_ref[pl.ds(off, tk), :], v_ref[pl.ds(off, tk), :], ok, c)

    nb = _new_key_blocks(i, tq, lq, tk, _chunk_last_key(i, tq))
    carry = lax.fori_loop(0, nb, new_step, carry)
    for u in range(hp):
        (_, l0, a0), (_, l1, a1) = carry[2 * u], carry[2 * u + 1]
        o = a0 / l0 - lam_ref[0] * (a1 / l1)
        o = o * lax.rsqrt(jnp.mean(o * o, axis=-1, keepdims=True) + NORM_EPS)
        o_ref[:, u * d:(u + 1) * d] = (o * g_ref[...] * out_scale).astype(o_ref.dtype)


def diff_attention(grp, proj, lam, subln, out_scale):
    assert grp.lp == 0
    B, H, hp = grp.batch, C_HEADS, HEADS_PER_STEP
    w = hp * C_V_DIM
    tq = grp.tq
    qrow, krow = grp.qrow(tq), grp.krow()
    cq, ck, cv = (ODD_COLS[n] // w for n in ("qc", "kc", "vc"))
    in_specs = [pl.BlockSpec(memory_space=pltpu.SMEM),
                pl.BlockSpec((tq, w), lambda b, h, i: (qrow(b, i), cq + h)),
                pl.BlockSpec((grp.lq, w), lambda b, h, i: (krow(b), ck + h)),
                pl.BlockSpec((grp.lq, w), lambda b, h, i: (krow(b), cv + h))]
    in_specs += [pl.BlockSpec((1, C_V_DIM), lambda b, h, i: (0, 0))]
    return pl.pallas_call(
        functools.partial(_diff_body, tq=tq, tk=grp.tk, lq=grp.lq, hp=hp, out_scale=out_scale),
        grid=(B, H // hp, grp.lq // tq),
        in_specs=in_specs,
        out_specs=pl.BlockSpec((tq, w), lambda b, h, i: (b * (grp.lq // tq) + i, h)),
        out_shape=jax.ShapeDtypeStruct((B * grp.lq, H * C_V_DIM), MXU_DTYPE),
        compiler_params=_params(3),
        name="diff_attention",
    )(lam, proj, proj, proj, subln)


def _mla_prep_body(qd_ref, ckv_ref, qg_ref, kg_ref, w_ref, cos_ref, sin_ref, qa_ref, qr_ref, lat_ref, *, n_abs):
    qd = qd_ref[...]
    qn = qd * lax.rsqrt(jnp.mean(qd * qd, axis=-1, keepdims=True) + NORM_EPS) * qg_ref[...]
    q = _dot(qn, w_ref[...]) * (D_NOPE + D_ROPE) ** -0.5
    qa_ref[...] = q[:, :n_abs].astype(qa_ref.dtype)
    for h in range((q.shape[1] - n_abs) // LANES):
        lanes = slice(n_abs + h * LANES, n_abs + (h + 1) * LANES)
        qr_ref[:, h * LANES:(h + 1) * LANES] = _rope_lanes(
            q[:, lanes], cos_ref[...], sin_ref[...], D_ROPE // 2).astype(qr_ref.dtype)
    ckv = ckv_ref[...]
    lat_ref[...] = ckv * lax.rsqrt(jnp.mean(ckv * ckv, axis=-1, keepdims=True) + NORM_EPS) * kg_ref[...]


def mla_prepare(proj, q_norm, kv_norm, w_q, cos, sin, tm):
    T = proj.shape[0]
    q_lora, R = q_norm.shape[1], kv_norm.shape[1]
    n_abs = D_HEADS * R
    n_rope = D_HEADS * LANES
    row = lambda i: (i, 0)
    fixed = lambda i: (0, 0)
    return pl.pallas_call(
        functools.partial(_mla_prep_body, n_abs=n_abs),
        grid=(T // tm,),
        in_specs=[pl.BlockSpec((tm, q_lora), lambda i: (i, ODD_COLS["qd"] // q_lora)),
                  pl.BlockSpec((tm, R), lambda i: (i, ODD_COLS["ckv"] // R)),
                  pl.BlockSpec((1, q_lora), fixed),
                  pl.BlockSpec((1, R), fixed),
                  pl.BlockSpec(w_q.shape, fixed),
                  pl.BlockSpec((None, tm, LANES), lambda i: (1, i, 0)),
                  pl.BlockSpec((None, tm, LANES), lambda i: (1, i, 0))],
        out_specs=[pl.BlockSpec((tm, n_abs), row), pl.BlockSpec((tm, n_rope), row), pl.BlockSpec((tm, R), row)],
        out_shape=[jax.ShapeDtypeStruct((T, n_abs), MXU_DTYPE), jax.ShapeDtypeStruct((T, n_rope), MXU_DTYPE),
                   jax.ShapeDtypeStruct((T, R), F32)],
        compiler_params=_params(1),
        name="mla_prepare",
    )(proj, proj, q_norm, kv_norm, w_q, cos, sin)


def _mla_body(qa_ref, qr_ref, lat_ref, kr_ref, o_ref, *, tq, tk, lq, hp):
    i = pl.program_id(2)
    R = lat_ref.shape[-1]
    qa = [qa_ref[:, u * R:(u + 1) * R] for u in range(hp)]
    qr = [qr_ref[:, u * LANES:(u + 1) * LANES] for u in range(hp)]
    carry = tuple(_init_carry(tq, R) for _ in range(hp))

    qchunk = (i * tq + lax.broadcasted_iota(I32, (tq, tk), 0)) // CHUNK

    def new_step(j, c):
        off = pl.multiple_of(j * tk, tk)
        lat = _mx(lat_ref[pl.ds(off, tk), :])
        kr = _mx(kr_ref[pl.ds(off, tk), :])
        ok = (off + lax.broadcasted_iota(I32, (tq, tk), 1)) // CHUNK <= qchunk
        return tuple(_online_step(jnp.where(ok, _dot_nt(qa[u], lat) + _dot_nt(qr[u], kr), NEG), lat, c[u])
                     for u in range(hp))

    nb = _new_key_blocks(i, tq, lq, tk, _chunk_last_key(i, tq))
    carry = lax.fori_loop(0, nb, new_step, carry)
    for u, (m, l, acc) in enumerate(carry):
        o_ref[:, u * R:(u + 1) * R] = (acc / l).astype(o_ref.dtype)


def mla_attention(grp, q_abs, q_rope, lat, proj):
    assert grp.lp == 0
    B, H, hp = grp.batch, D_HEADS, HEADS_PER_STEP
    R = lat.shape[1]
    tq = grp.tq
    qrow, krow = grp.qrow(tq), grp.krow()
    in_specs = [pl.BlockSpec((tq, hp * R), lambda b, h, i: (qrow(b, i), h)),
                pl.BlockSpec((tq, hp * LANES), lambda b, h, i: (qrow(b, i), h)),
                pl.BlockSpec((grp.lq, R), lambda b, h, i: (krow(b), 0)),
                pl.BlockSpec((grp.lq, LANES), lambda b, h, i: (krow(b), ODD_COLS["kr"] // LANES))]
    return pl.pallas_call(
        functools.partial(_mla_body, tq=tq, tk=grp.tk, lq=grp.lq, hp=hp),
        grid=(B, H // hp, grp.lq // tq),
        in_specs=in_specs,
        out_specs=pl.BlockSpec((tq, hp * R), lambda b, h, i: (b * (grp.lq // tq) + i, h)),
        out_shape=jax.ShapeDtypeStruct((B * grp.lq, H * R), MXU_DTYPE),
        compiler_params=_params(3),
        name="mla_attention",
    )(q_abs, q_rope, lat, proj)


def _head_cols(ref, h, d):
    return ref[:, h, :] if len(ref.shape) == 3 else ref[:, h * d:(h + 1) * d]


def _chain_reset(j, m_ref, l_ref, acc_ref):
    @pl.when(j == 0)
    def _():
        m_ref[...] = jnp.full(m_ref.shape, NEG, F32)
        l_ref[...] = jnp.zeros(l_ref.shape, F32)
        acc_ref[...] = jnp.zeros(acc_ref.shape, F32)


def _chain_advance(c, s, v, m_ref, l_ref, acc_ref):
    m, l, acc = _online_step(s, v, (m_ref[c], l_ref[c], acc_ref[c]))
    m_ref[c], l_ref[c], acc_ref[c] = m, l, acc
    return l, acc


def _past_key_spec(cache, layer, tkp, n_past):
    blk = lambda j: jnp.minimum(j, n_past - 1)
    tail = cache.shape[3:] if layer is not None else cache.shape[2:]
    zeros = (0,) * len(tail)
    if layer is None:
        return pl.BlockSpec((None, tkp) + tail, lambda b, j: (b, blk(j)) + zeros)
    return pl.BlockSpec((None, None, tkp) + tail, lambda b, j: (layer, b, blk(j)) + zeros)


def _past_call(body, grp, heads, dv, chains, in_specs, args, name):
    lq = grp.lq
    assert grp.lq == grp.tk
    return pl.pallas_call(
        body,
        grid=(grp.batch, grp.n_past_blocks + 1),
        in_specs=in_specs,
        out_specs=pl.BlockSpec((lq, heads * dv), lambda b, j: (b, 0)),
        out_shape=jax.ShapeDtypeStruct((grp.batch * lq, heads * dv), MXU_DTYPE),
        scratch_shapes=[pltpu.VMEM((chains, lq, 1), F32), pltpu.VMEM((chains, lq, 1), F32),
                        pltpu.VMEM((chains, lq, dv), F32)],
        compiler_params=_params(2),
        name=name,
    )(*args)


def _fox_past_body(q_ref, k_ref, v_ref, pk_ref, pv_ref, fq_ref, fk_ref, o_ref, m_ref, l_ref, acc_ref, *, n_past):
    j = pl.program_id(1)
    d, lq = HEAD_DIM, q_ref.shape[0]
    _chain_reset(j, m_ref, l_ref, acc_ref)
    q = lambda h: _mx(q_ref[:, h * d:(h + 1) * d] * d ** -0.5)

    @pl.when(j < n_past)
    def _():
        for h in range(B_HEADS):
            s = _dot_nt(q(h), _head_cols(pk_ref, h, d)) + fq_ref[h] - fk_ref[h, 0]
            _chain_advance(h, s, _head_cols(pv_ref, h, d), m_ref, l_ref, acc_ref)

    @pl.when(j == n_past)
    def _():
        causal = lax.broadcasted_iota(I32, (lq, lq), 1) <= lax.broadcasted_iota(I32, (lq, lq), 0)
        for h in range(B_HEADS):
            s = _dot_nt(q(h), k_ref[:, h * d:(h + 1) * d]) + fq_ref[h] - fk_ref[h, 0][:, :lq]
            l, acc = _chain_advance(h, jnp.where(causal, s, NEG), v_ref[:, h * d:(h + 1) * d], m_ref, l_ref, acc_ref)
            o_ref[:, h * d:(h + 1) * d] = (acc / l).astype(o_ref.dtype)


def fox_attention_past(grp, proj, fq, fk, cache_k, cache_v, layer):
    H, d, lq = B_HEADS, HEAD_DIM, grp.lq
    row = grp.krow()
    w = H * d
    in_specs = [pl.BlockSpec((lq, w), lambda b, j, n=n: (row(b), EVEN_COLS[n] // w)) for n in ("qf", "kf", "vf")]
    in_specs += [_past_key_spec(cache_k, layer, grp.tkp, grp.n_past_blocks),
                 _past_key_spec(cache_v, layer, grp.tkp, grp.n_past_blocks),
                 pl.BlockSpec((H, lq, 1), lambda b, j: (b, 0, 0)),
                 pl.BlockSpec((H, 1, 1, fk.shape[-1]), lambda b, j: (b, j, 0, 0))]
    return _past_call(functools.partial(_fox_past_body, n_past=grp.n_past_blocks), grp, H, d, H, in_specs,
                      [proj, proj, proj, cache_k, cache_v, fq, fk], "fox_attention_past")


def _diff_past_body(lam_ref, q_ref, k_ref, v_ref, pk_ref, pv_ref, g_ref, o_ref, m_ref, l_ref, acc_ref,
                    *, n_past, out_scale):
    j = pl.program_id(1)
    d = C_V_DIM
    _chain_reset(j, m_ref, l_ref, acc_ref)
    lane = lax.broadcasted_iota(I32, (q_ref.shape[0], d), 1)

    def maps(h):
        q = q_ref[:, h * d:(h + 1) * d] * C_QK_DIM ** -0.5
        return _mx(jnp.where(lane < C_QK_DIM, q, 0.0)), _mx(jnp.where(lane >= C_QK_DIM, q, 0.0))

    @pl.when(j < n_past)
    def _():
        for h in range(C_HEADS):
            k, v = _mx(_head_cols(pk_ref, h, d)), _mx(_head_cols(pv_ref, h, d))
            for t, qt in enumerate(maps(h)):
                _chain_advance(2 * h + t, _dot_nt(qt, k), v, m_ref, l_ref, acc_ref)

    @pl.when(j == n_past)
    def _():
        for h in range(C_HEADS):
            k, v = _mx(k_ref[:, h * d:(h + 1) * d]), _mx(v_ref[:, h * d:(h + 1) * d])
            (l0, a0), (l1, a1) = [_chain_advance(2 * h + t, _dot_nt(qt, k), v, m_ref, l_ref, acc_ref)
                                  for t, qt in enumerate(maps(h))]
            o = a0 / l0 - lam_ref[0] * (a1 / l1)
            o = o * lax.rsqrt(jnp.mean(o * o, axis=-1, keepdims=True) + NORM_EPS)
            o_ref[:, h * d:(h + 1) * d] = (o * g_ref[...] * out_scale).astype(o_ref.dtype)


def diff_attention_past(grp, proj, lam, subln, out_scale, cache_k, cache_v, layer):
    H, d, lq = C_HEADS, C_V_DIM, grp.lq
    assert lq <= CHUNK and grp.lp % CHUNK == 0
    row = grp.krow()
    w = H * d
    in_specs = [pl.BlockSpec(memory_space=pltpu.SMEM)]
    in_specs += [pl.BlockSpec((lq, w), lambda b, j, n=n: (row(b), ODD_COLS[n] // w)) for n in ("qc", "kc", "vc")]
    in_specs += [_past_key_spec(cache_k, None, grp.tkp, grp.n_past_blocks),
                 _past_key_spec(cache_v, layer, grp.tkp, grp.n_past_blocks),
                 pl.BlockSpec((1, d), lambda b, j: (0, 0))]
    return _past_call(functools.partial(_diff_past_body, n_past=grp.n_past_blocks, out_scale=out_scale), grp, H, d,
                      2 * H, in_specs, [lam, proj, proj, proj, cache_k, cache_v, subln], "diff_attention_past")


def _mla_past_body(qa_ref, qr_ref, lat_ref, kr_ref, plat_ref, pkr_ref, o_ref, m_ref, l_ref, acc_ref, *, n_past):
    j = pl.program_id(1)
    R = lat_ref.shape[-1]
    _chain_reset(j, m_ref, l_ref, acc_ref)

    @pl.when(j < n_past)
    def _():
        lat, kr = _mx(plat_ref[...]), _mx(pkr_ref[...])
        for h in range(D_HEADS):
            s = _dot_nt(qa_ref[:, h * R:(h + 1) * R], lat) + _dot_nt(qr_ref[:, h * LANES:h * LANES + D_ROPE], kr)
            _chain_advance(h, s, lat, m_ref, l_ref, acc_ref)

    @pl.when(j == n_past)
    def _():
        lat, kr = _mx(lat_ref[...]), _mx(kr_ref[...])
        for h in range(D_HEADS):
            s = _dot_nt(qa_ref[:, h * R:(h + 1) * R], lat) + _dot_nt(qr_ref[:, h * LANES:(h + 1) * LANES], kr)
            l, acc = _chain_advance(h, s, lat, m_ref, l_ref, acc_ref)
            o_ref[:, h * R:(h + 1) * R] = (acc / l).astype(o_ref.dtype)


def mla_attention_past(grp, q_abs, q_rope, lat, proj, cache_lat, cache_kr, layer):
    H, lq = D_HEADS, grp.lq
    assert lq <= CHUNK and grp.lp % CHUNK == 0
    R = lat.shape[1]
    row = grp.krow()
    in_specs = [pl.BlockSpec((lq, H * R), lambda b, j: (row(b), 0)),
                pl.BlockSpec((lq, H * LANES), lambda b, j: (row(b), 0)),
                pl.BlockSpec((lq, R), lambda b, j: (row(b), 0)),
                pl.BlockSpec((lq, LANES), lambda b, j: (row(b), ODD_COLS["kr"] // LANES)),
                _past_key_spec(cache_lat, layer, grp.tkp, grp.n_past_blocks),
                _past_key_spec(cache_kr, layer, grp.tkp, grp.n_past_blocks)]
    return _past_call(functools.partial(_mla_past_body, n_past=grp.n_past_blocks), grp, H, R, H, in_specs,
                      [q_abs, q_rope, lat, proj, cache_lat, cache_kr], "mla_attention_past")


def _sortable(score):
    bits = pltpu.bitcast(score, I32)
    return jnp.where(bits < 0, bits ^ jnp.int32(0x7FFFFFFF), bits)


def _dsa_body(*refs, tq, i0, nk, topk, has_past):
    if has_past:
        qi_ref, misc_ref, q_ref, ki_ref, k_ref, v_ref, pki_ref, pk_ref, pv_ref, o_ref = refs
    else:
        qi_ref, misc_ref, q_ref, ki_ref, k_ref, v_ref, o_ref = refs
    i = i0 + pl.program_id(1)
    wi = misc_ref[:, MISC_WI:MISC_WI + IDX_HEADS] * (IDX_HEADS * IDX_DIM) ** -0.5

    qchunk = (i * tq + lax.broadcasted_iota(I32, (tq, nk), 0)) // CHUNK
    adm_new = lax.broadcasted_iota(I32, (tq, nk), 1) // CHUNK <= qchunk
    segs = [(_mx(ki_ref[...]), k_ref, v_ref, adm_new)]
    if has_past:
        segs.insert(0, (_mx(pki_ref[0]), pk_ref.at[0], pv_ref.at[0], None))

    keys = []
    for ki, _, _, adm in segs:
        score = jnp.zeros((tq, ki.shape[0]), F32)
        for h in range(IDX_HEADS):
            s = _dot_nt(qi_ref[:, h * IDX_DIM:(h + 1) * IDX_DIM], ki)
            score = score + jnp.maximum(s, 0.0) * wi[:, h:h + 1]
        key = _sortable(score)
        keys.append(key if adm is None else jnp.where(adm, key, INT_MIN))

    def count_ge(cand):
        return sum(jnp.sum(jnp.where(key >= cand, 1.0, 0.0), axis=-1, keepdims=True) for key in keys)

    kf = float(topk)
    t = jnp.where(count_ge(jnp.zeros((tq, 1), I32)) >= kf, 0, INT_MIN).astype(I32)

    def bit_step(it, t):
        cand = t | jnp.left_shift(jnp.int32(1), 30 - it)
        return jnp.where(count_ge(cand) >= kf, cand, t)

    t = lax.fori_loop(0, 31, bit_step, t)
    sels = [(key >= t) if adm is None else (adm & (key >= t)) for key, (_, _, _, adm) in zip(keys, segs)]

    rep = A_HEADS // A_KV_HEADS
    for g in range(A_KV_HEADS):
        kv = [(_mx(k[:, g * HEAD_DIM:(g + 1) * HEAD_DIM]), _mx(v[:, g * HEAD_DIM:(g + 1) * HEAD_DIM]))
              for _, k, v, _ in segs]
        for r in range(rep):
            h = g * rep + r
            q = _mx(q_ref[:, h * HEAD_DIM:(h + 1) * HEAD_DIM] * HEAD_DIM ** -0.5)
            ss = [jnp.where(sel, _dot_nt(q, k), NEG) for sel, (k, _) in zip(sels, kv)]
            m = functools.reduce(jnp.maximum, [jnp.max(s, axis=-1, keepdims=True) for s in ss])
            ps = [jnp.exp(s - m) for s in ss]
            l = sum(jnp.sum(p, axis=-1, keepdims=True) for p in ps)
            o = sum(_dot(p, v) for p, (_, v) in zip(ps, kv))
            o_ref[:, h * HEAD_DIM:(h + 1) * HEAD_DIM] = (o / l).astype(o_ref.dtype)


def dsa_attention(grp, proj, past_ki, past_k, past_v):
    B = grp.batch
    tq = grp.tq_dsa
    n_tiles = grp.lq // tq
    wq, wkv = A_HEADS * HEAD_DIM, A_KV_HEADS * HEAD_DIM
    wqi = IDX_HEADS * IDX_DIM
    c = EVEN_COLS

    extents = sorted({grp.lq // f for f in (1, 2, 4) if grp.lq % f == 0 and (grp.lq // f) % tq == 0
                      and grp.row0 % (grp.lq // f) == 0})
    classes = []
    for i in range(n_tiles):
        need = min(grp.lq, _chunk_last_key(i, tq) + 1)
        nk = min(e for e in extents if e >= need)
        if classes and classes[-1][2] == nk:
            classes[-1][1] += 1
        else:
            classes.append([i, 1, nk])

    outs = []
    for i0, n_i, nk in classes:
        qrow = lambda b, i, i0=i0: grp.row0 // tq + b * n_tiles + i0 + i
        krow = lambda b, nk=nk: (grp.row0 + b * grp.lq) // nk
        in_specs = [pl.BlockSpec((tq, wqi), lambda b, i, f=qrow: (f(b, i), c["qi"] // wqi)),
                    pl.BlockSpec((tq, LANES), lambda b, i, f=qrow: (f(b, i), c["misc"] // LANES)),
                    pl.BlockSpec((tq, wq), lambda b, i, f=qrow: (f(b, i), c["qa"] // wq)),
                    pl.BlockSpec((nk, IDX_DIM), lambda b, i, f=krow: (f(b), c["ki"] // IDX_DIM)),
                    pl.BlockSpec((nk, wkv), lambda b, i, f=krow: (f(b), c["ka"] // wkv)),
                    pl.BlockSpec((nk, wkv), lambda b, i, f=krow: (f(b), c["va"] // wkv))]
        args = [proj] * 6
        if grp.lp:
            in_specs += [pl.BlockSpec((1, grp.lp, IDX_DIM), lambda b, i: (b, 0, 0)),
                         pl.BlockSpec((1, grp.lp, wkv), lambda b, i: (b, 0, 0)),
                         pl.BlockSpec((1, grp.lp, wkv), lambda b, i: (b, 0, 0))]
            args += [past_ki, past_k, past_v]
        o = pl.pallas_call(
            functools.partial(_dsa_body, tq=tq, i0=i0, nk=nk, topk=grp.topk, has_past=bool(grp.lp)),
            grid=(B, n_i),
            in_specs=in_specs,
            out_specs=pl.BlockSpec((tq, wq), lambda b, i, n_i=n_i: (b * n_i + i, 0)),
            out_shape=jax.ShapeDtypeStruct((B * n_i * tq, wq), MXU_DTYPE),
            compiler_params=_params(2),
            name="dsa_attention",
        )(*args)
        outs.append(o.reshape(B, n_i * tq, wq))
    return jnp.concatenate(outs, axis=1).reshape(B * grp.lq, wq)


def _first_index_of_max(x, iota, n):
    mx = jnp.max(x, axis=0, keepdims=True)
    return mx, jnp.min(jnp.where(x == mx, iota, n), axis=0, keepdims=True)


def _route_body(x_ref, sc_ref, sh_ref, wr_ref, bias_ref, h_ref, gate_ref, rank_ref, cnt_ref, carry_ref, *, group):
    step = pl.program_id(0)
    tm = x_ref.shape[0]
    E = wr_ref.shape[0]
    per = E // N_GROUPS
    for g, part in enumerate(_modulated(x_ref, sc_ref, sh_ref, group)):
        h_ref[g * group:(g + 1) * group, :] = part
    h = h_ref[...]

    scores = jax.nn.sigmoid(_dot_nt(wr_ref[...], h, precision=lax.Precision.HIGHEST))
    biased = scores + bias_ref[...]
    member = lax.broadcasted_iota(I32, (per, tm), 0).astype(F32)
    gscore = []
    for g in range(N_GROUPS):
        blk = biased[g * per:(g + 1) * per, :]
        m1, i1 = _first_index_of_max(blk, member, per)
        m2 = jnp.max(jnp.where(member == i1, -jnp.inf, blk), axis=0, keepdims=True)
        gscore.append(m1 + m2)
    gscore = jnp.concatenate(gscore, axis=0)
    giota = lax.broadcasted_iota(I32, (N_GROUPS, tm), 0).astype(F32)
    gsel = jnp.zeros((N_GROUPS, tm), F32)
    for _ in range(TOPK_GROUPS):
        _, gi = _first_index_of_max(gscore, giota, N_GROUPS)
        hit = giota == gi
        gsel = jnp.where(hit, 1.0, gsel)
        gscore = jnp.where(hit, -jnp.inf, gscore)
    emask = jnp.concatenate([jnp.broadcast_to(gsel[g:g + 1, :], (per, tm)) for g in range(N_GROUPS)], axis=0)
    masked = jnp.where(emask > 0.0, biased, -jnp.inf)
    eiota = lax.broadcasted_iota(I32, (E, tm), 0).astype(F32)
    sel = jnp.zeros((E, tm), jnp.bool_)
    for _ in range(TOP_K):
        _, ei = _first_index_of_max(masked, eiota, E)
        hit = eiota == ei
        sel = sel | hit
        masked = jnp.where(hit, -jnp.inf, masked)
    w = jnp.where(sel, scores, 0.0)
    gate_ref[...] = w / jnp.sum(w, axis=0, keepdims=True) * ROUTED_SCALE

    @pl.when(step == 0)
    def _():
        carry_ref[...] = jnp.zeros(carry_ref.shape, F32)
    upper = (lax.broadcasted_iota(I32, (tm, tm), 0) <= lax.broadcasted_iota(I32, (tm, tm), 1))
    self = jnp.where(sel, 1.0, 0.0)
    incl = _dot(self, jnp.where(upper, 1.0, 0.0)) + carry_ref[...]
    rank_ref[...] = jnp.where(sel, incl - 1.0, -1.0).astype(I32)
    carry_ref[...] = incl[:, tm - 1:tm]
    cnt_ref[...] = incl[:, tm - 1:tm].astype(I32)


def route(x, scale_g, shift_g, w_router_t, r_bias, group, tm):
    T, D = x.shape
    E = w_router_t.shape[0]
    row = lambda i: (i, 0)
    col = lambda i: (0, i)
    fixed = lambda i: (0, 0)
    return pl.pallas_call(
        functools.partial(_route_body, group=group),
        grid=(T // tm,),
        in_specs=[pl.BlockSpec((tm, D), row),
                  pl.BlockSpec((1, tm // group, D), lambda i: (i, 0, 0)),
                  pl.BlockSpec((1, tm // group, D), lambda i: (i, 0, 0)),
                  pl.BlockSpec((E, D), fixed),
                  pl.BlockSpec((E, 1), fixed)],
        out_specs=[pl.BlockSpec((tm, D), row), pl.BlockSpec((E, tm), col), pl.BlockSpec((E, tm), col),
                   pl.BlockSpec((E, 1), fixed)],
        out_shape=[jax.ShapeDtypeStruct((T, D), F32), jax.ShapeDtypeStruct((E, T), F32),
                   jax.ShapeDtypeStruct((E, T), I32), jax.ShapeDtypeStruct((E, 1), I32)],
        scratch_shapes=[pltpu.VMEM((E, 1), F32)],
        compiler_params=_params(1),
        name="route",
    )(x, _per_tile(scale_g, tm, group), _per_tile(shift_g, tm, group), w_router_t, r_bias)


def _compact_body(gate_ref, rank_ref, off_ref, dest_ref, g8_ref):
    E, tm = gate_ref.shape
    rank = rank_ref[...]
    sel = rank >= 0
    dest = (rank + off_ref[...]).astype(F32)
    gate = gate_ref[...]
    eiota = lax.broadcasted_iota(I32, (E, tm), 0).astype(F32)
    dests, gates = [], []
    for _ in range(TOP_K):
        ei = jnp.min(jnp.where(sel, eiota, E), axis=0, keepdims=True)
        hit = eiota == ei
        dests.append(jnp.sum(jnp.where(hit, dest, 0.0), axis=0, keepdims=True))
        gates.append(jnp.sum(jnp.where(hit, gate, 0.0), axis=0, keepdims=True))
        sel = sel & jnp.logical_not(hit)
    dest_ref[...] = jnp.concatenate(dests, axis=0).astype(I32)
    g8_ref[...] = jnp.concatenate(gates, axis=0)


def compact_routes(gate, rank, seg_off, tm):
    E, T = gate.shape
    col = lambda i: (0, i)
    return pl.pallas_call(
        _compact_body,
        grid=(T // tm,),
        in_specs=[pl.BlockSpec((E, tm), col), pl.BlockSpec((E, tm), col), pl.BlockSpec((E, 1), lambda i: (0, 0))],
        out_specs=[pl.BlockSpec((TOP_K, tm), col), pl.BlockSpec((TOP_K, tm), col)],
        out_shape=[jax.ShapeDtypeStruct((TOP_K, T), I32), jax.ShapeDtypeStruct((TOP_K, T), F32)],
        compiler_params=_params(1),
        name="compact_routes",
    )(gate, rank, seg_off)


def _ffn(x, w1, w3, w2):
    a = _dot(x, w1)
    return _dot(a * jax.nn.sigmoid(a) * _dot(x, w3), w2)


def _gmm_body(te_ref, nt_ref, tok_ref, tok_next_ref, slot_ref, x_hbm, w1_ref, w3_ref, w2_ref, y_hbm,
              w1b, w3b, w2b, xbuf, ybuf, in_sem, out_sem, *, tm, parts):
    i = pl.program_id(0)
    n_used = nt_ref[0]
    cur, nxt = i % 2, (i + 1) % 2

    def row_in(tok, r, buf):
        return pltpu.make_async_copy(x_hbm.at[pl.ds(tok[0, 0, r], 1)], xbuf.at[buf, pl.ds(r, 1)], in_sem.at[buf])

    def row_out(r, buf):
        return pltpu.make_async_copy(ybuf.at[buf, pl.ds(r, 1)], y_hbm.at[pl.ds(slot_ref[0, 0, r], 1)], out_sem.at[buf])

    def wait_in(buf):
        pltpu.make_async_copy(x_hbm.at[pl.ds(0, tm)], xbuf.at[buf], in_sem.at[buf]).wait()

    def wait_out(buf):
        pltpu.make_async_copy(ybuf.at[buf], y_hbm.at[pl.ds(0, tm)], out_sem.at[buf]).wait()

    @pl.when(i == 0)
    def _():
        for r in range(tm):
            row_in(tok_ref, r, 0).start()
        ybuf[1] = jnp.zeros((tm, ybuf.shape[2]), F32)
        spare = pltpu.make_async_copy(ybuf.at[1], y_hbm.at[pl.ds(y_hbm.shape[0] - tm, tm)], out_sem.at[1])
        spare.start()
        spare.wait()

    @pl.when(jnp.logical_or(i == 0, te_ref[i] != te_ref[jnp.maximum(i - 1, 0)]))
    def _():
        w1b[...] = _mx(w1_ref[...])
        w3b[...] = _mx(w3_ref[...])
        w2b[...] = _mx(w2_ref[...])

    @pl.when(i < n_used)
    def _():
        wait_in(cur)
        for r in range(tm):
            row_in(tok_next_ref, r, nxt).start()
        rows = tm // parts
        for p in range(parts):
            x = xbuf[cur, p * rows:(p + 1) * rows, :]
            ybuf[cur, p * rows:(p + 1) * rows, :] = _ffn(x, w1b[...], w3b[...], w2b[...])
            for r in range(p * rows, (p + 1) * rows):
                row_out(r, cur).start()

    @pl.when(jnp.logical_and(i >= 1, i < n_used))
    def _():
        wait_out(nxt)

    @pl.when(i == n_used - 1)
    def _():
        wait_in(nxt)
        wait_out(cur)


def grouped_ffn(tile_expert, n_tiles, row_token, row_slot, x, w1, w3, w2, layer, n_out, tm):
    n_rows, D = row_token.shape[0] * tm, x.shape[1]
    F = w1.shape[-1]
    clamp = lambda i, nt: jnp.maximum(jnp.minimum(i, nt[0] - 1), 0)
    wsel = lambda i, te, nt: (layer, te[i], 0, 0)
    smem_rows = lambda f: pl.BlockSpec((1, 1, tm), f, memory_space=pltpu.SMEM)
    return pl.pallas_call(
        functools.partial(_gmm_body, tm=tm, parts=2),
        grid_spec=pltpu.PrefetchScalarGridSpec(
            num_scalar_prefetch=2,
            grid=(n_rows // tm,),
            in_specs=[smem_rows(lambda i, te, nt: (clamp(i, nt), 0, 0)),
                      smem_rows(lambda i, te, nt: (clamp(i + 1, nt), 0, 0)),
                      smem_rows(lambda i, te, nt: (clamp(i, nt), 0, 0)),
                      pl.BlockSpec(memory_space=pl.ANY),
                      pl.BlockSpec((None, None, D, F), wsel),
                      pl.BlockSpec((None, None, D, F), wsel),
                      pl.BlockSpec((None, None, F, D), wsel)],
            out_specs=pl.BlockSpec(memory_space=pl.ANY),
            scratch_shapes=[pltpu.VMEM((D, F), MXU_DTYPE), pltpu.VMEM((D, F), MXU_DTYPE),
                            pltpu.VMEM((F, D), MXU_DTYPE),
                            pltpu.VMEM((2, tm, D), F32), pltpu.VMEM((2, tm, D), F32),
                            pltpu.SemaphoreType.DMA((2,)), pltpu.SemaphoreType.DMA((2,))]),
        out_shape=jax.ShapeDtypeStruct((n_out, D), F32),
        compiler_params=pltpu.CompilerParams(dimension_semantics=("arbitrary",), vmem_limit_bytes=VMEM_LIMIT_BYTES,
                                             disable_bounds_checks=True),
        name="grouped_ffn",
    )(tile_expert, n_tiles, row_token, row_token, row_slot, x, w1, w3, w2)


def _close_moe_body(h_ref, *rest, group, alpha, n_k):
    yk_refs, (g8_ref, ws1_ref, ws3_ref, ws2_ref, x_ref, gate_ref, g_ref, b_ref, o_ref) = rest[:n_k], rest[n_k:]
    y = _ffn(h_ref[...], ws1_ref[...], ws3_ref[...], ws2_ref[...])
    g8 = g8_ref[...]
    for k, yk_ref in enumerate(yk_refs):
        y = y + g8[:, k:k + 1] * yk_ref[...]
    r = _gated_residual(x_ref, y, gate_ref, group, alpha)
    o_ref[...] = _layer_norm_rows(r, g_ref[...], b_ref[...])


def close_moe(h, y_rows, g8, ws1, ws3, ws2, x, gate_g, ln_g, ln_b, group, alpha, tm):
    T, D = x.shape
    K = g8.shape[1]
    row = lambda i: (i, 0)
    fixed = lambda i: (0, 0)
    return pl.pallas_call(
        functools.partial(_close_moe_body, group=group, alpha=alpha, n_k=K),
        grid=(T // tm,),
        in_specs=[pl.BlockSpec((tm, D), row)] +
                 [pl.BlockSpec((tm, D), lambda i, k=k: (k * (T // tm) + i, 0)) for k in range(K)] +
                 [pl.BlockSpec((tm, K), row),
                  pl.BlockSpec(ws1.shape, fixed),
                  pl.BlockSpec(ws3.shape, fixed),
                  pl.BlockSpec(ws2.shape, fixed),
                  pl.BlockSpec((tm, D), row),
                  pl.BlockSpec((1, tm // group, D), lambda i: (i, 0, 0)),
                  pl.BlockSpec((1, D), fixed),
                  pl.BlockSpec((1, D), fixed)],
        out_specs=pl.BlockSpec((tm, D), row),
        out_shape=jax.ShapeDtypeStruct((T, D), F32),
        compiler_params=_params(1),
        name="close_moe",
    )(h, *([y_rows] * K), g8, ws1, ws3, ws2, x, _per_tile(gate_g, tm, group), ln_g, ln_b)


def _rope_tables(pos, half):
    inv_freq = jnp.power(ROPE_THETA, -jnp.arange(half, dtype=F32) / half)
    ang = pos.astype(F32)[:, None] * inv_freq[None, :]
    cos, sin = jnp.cos(ang), jnp.sin(ang)
    reps = LANES // (2 * half)
    cos = jnp.tile(jnp.concatenate([cos, cos], axis=1), (1, reps))
    sin = jnp.tile(jnp.concatenate([-sin, sin], axis=1), (1, reps))
    return jnp.stack([jnp.ones_like(cos), cos]), jnp.stack([jnp.zeros_like(sin), sin])


def _place_cols(w, sizes, names, cols, width):
    out = jnp.zeros((w.shape[0], width), w.dtype)
    o = 0
    for size, name in zip(sizes, names):
        if name is not None:
            dst, sub = name
            out = lax.dynamic_update_slice(out, w[:, o:o + size], (0, cols[dst] + sub))
        o += size
    return out


def _moe(x, scale_g, shift_g, gate_g, ln_g, ln_b, w_router, r_bias, w1, w3, w2, ws1, ws3, ws2, layer, group, alpha):
    T, D = x.shape
    E = w1.shape[1]
    tm = 256
    t_route = _pick(T, 768, math.lcm(LANES, group))
    h, gate, rank, counts = route(x, scale_g, shift_g, w_router.T, r_bias.reshape(E, 1), group, t_route)

    counts = counts[:, 0]
    tiles_per = (counts + tm - 1) // tm
    tile_end = jnp.cumsum(tiles_per)
    seg_off = ((tile_end - tiles_per) * tm).astype(I32)
    n_tiles = (T * TOP_K) // tm + E
    used = tile_end[-1].astype(I32)
    tile_ids = jnp.minimum(jnp.arange(n_tiles, dtype=I32), used - 1)
    tile_expert = jnp.sum((tile_end[None, :] <= tile_ids[:, None]).astype(I32), axis=1)

    dest, g8 = compact_routes(gate, rank, seg_off.reshape(E, 1), t_route)
    n_pairs = TOP_K * T
    pair_slot = jnp.arange(n_pairs, dtype=I32).reshape(TOP_K, T)
    spare = n_pairs + jnp.arange(n_tiles * tm, dtype=I32) % tm
    row_slot = spare.at[dest.reshape(-1)].set(pair_slot.reshape(-1), unique_indices=True, mode="promise_in_bounds")
    row_token = jnp.where(row_slot < n_pairs, row_slot % T, 0)
    y_rows = grouped_ffn(tile_expert, used.reshape(1), row_token.reshape(n_tiles, 1, tm),
                         row_slot.reshape(n_tiles, 1, tm), h, w1, w3, w2, layer, n_pairs + tm, tm)
    return close_moe(h, y_rows, g8.T, _mx(ws1), _mx(ws3), _mx(ws2), x, gate_g, ln_g, ln_b, group, alpha,
                     _pick(T, 256, math.lcm(SUBLANES, group)))


def _fox_gates(grp, proj, b_f, past_logf):
    B, H = grp.batch, B_HEADS
    c0 = EVEN_COLS["misc"] + MISC_FL
    fl = proj[grp.row0:grp.row0 + B * grp.lq, c0:c0 + H].reshape(B, grp.lq, H)
    logf = jax.nn.log_sigmoid(fl + b_f)
    logf_all = logf if past_logf is None else jnp.concatenate([past_logf, logf], axis=1)
    cum = jnp.moveaxis(jnp.cumsum(logf_all, axis=1), 1, 2).reshape(B * H, grp.lk)
    fq = cum[:, grp.lp:, None]
    new = cum[:, grp.lp:].reshape(B * H, grp.lq // grp.tk, 1, grp.tk)
    new = jnp.pad(new, ((0, 0), (0, 0), (0, 0), (0, KEY_BLOCK - grp.tk)))
    if grp.lp:
        past = cum[:, :grp.lp].reshape(B * H, grp.n_past_blocks, 1, grp.tkp)
        past = jnp.pad(past, ((0, 0), (0, 0), (0, 0), (0, KEY_BLOCK - grp.tkp)))
        new = jnp.concatenate([past, new], axis=1)
    return logf, fq, new


def kernel(x_prompt, x_sample, c_prompt, c_sample, cache_a_k, cache_a_v, cache_a_kidx, cache_b_k, cache_b_v, cache_b_logf, cache_c_k, cache_c_v, cache_d_latent, cache_d_krope, w_in_even, b_forget, w_out_even, w_in_odd, c_lambda, c_subln, d_q_norm, d_w_uq, d_kv_norm, d_w_uk, d_w_uv, w_out_odd, ada_mix_w, ada_mix_b, ln_mix_g, ln_mix_b, ada_ffn_w, ada_ffn_b, ln_ffn_g, ln_ffn_b, router_w, router_bias, moe_w1, moe_w3, moe_w2, shared_w1, shared_w3, shared_w2):
    B, L, D = x_prompt.shape
    Bs, Ls, _ = x_sample.shape
    depth = ada_mix_w.shape[0]
    alpha = (2 * depth) ** 0.25
    past_len = cache_a_k.shape[2]
    Tp = B * L
    T = Tp + Bs * Ls
    grp_p = _Group(B, L, 0, 0)
    grp_s = _Group(Bs, Ls, past_len, Tp)
    group = math.gcd(L, Ls)
    assert group % (2 * SUBLANES) == 0
    tm_tok = _pick(T, 768, math.lcm(LANES, group))
    tm_close = _pick(T, 512, math.lcm(2 * SUBLANES, group))
    pos = jnp.concatenate([jnp.tile(jnp.arange(L), B), jnp.tile(past_len + jnp.arange(Ls), Bs)])
    c_act = jax.nn.silu(jnp.concatenate([c_prompt, c_sample], axis=0))

    def conditioning(ada_w, ada_b, layer):
        mod = matmul(c_act, ada_w, layer=layer) + ada_b[layer]
        per_group = jnp.concatenate([jnp.repeat(mod[:B], L // group, axis=0),
                                     jnp.repeat(mod[B:], Ls // group, axis=0)], axis=0)
        return jnp.split(per_group, 3, axis=-1)

    def flat_past(c):
        return c.reshape(c.shape[0], c.shape[1], -1)

    x = jnp.concatenate([x_prompt.reshape(Tp, D), x_sample.reshape(Bs * Ls, D)], axis=0)
    ev_p, ev_s, od_p, od_s = [], [], [], []
    for i in range(depth):
        j = i // 2
        shift_g, scale_g, gate_g = conditioning(ada_mix_w, ada_mix_b, i)
        if i % 2 == 0:
            sizes = (A_HEADS * HEAD_DIM, A_KV_HEADS * HEAD_DIM, A_KV_HEADS * HEAD_DIM, IDX_HEADS * IDX_DIM, IDX_DIM,
                     IDX_HEADS, B_HEADS * HEAD_DIM, B_HEADS * HEAD_DIM, B_HEADS * HEAD_DIM, B_HEADS)
            names = (("qa", 0), ("ka", 0), ("va", 0), ("qi", 0), ("ki", 0), ("misc", MISC_WI),
                     ("qf", 0), ("kf", 0), ("vf", 0), ("misc", MISC_FL))
            w_in = _mx(_place_cols(w_in_even[j], sizes, names, EVEN_COLS, EVEN_WIDTH))
            cos, sin = _rope_tables(pos, HEAD_DIM // 2)
            proj = project(x, scale_g, shift_g, w_in, cos, sin, EVEN_ROPE_RANGES, HEAD_DIM // 2, group, tm_tok)
            outs_a, outs_b = [], []
            for grp, store in ((grp_p, ev_p), (grp_s, ev_s)):
                if grp.lp:
                    logf, fq, fk = _fox_gates(grp, proj, b_forget[j], cache_b_logf[j])
                    outs_a.append(dsa_attention(grp, proj, cache_a_kidx[j], flat_past(cache_a_k[j]),
                                                flat_past(cache_a_v[j])))
                    outs_b.append(fox_attention_past(grp, proj, fq, fk, cache_b_k, cache_b_v, j))
                else:
                    logf, fq, fk = _fox_gates(grp, proj, b_forget[j], None)
                    outs_a.append(dsa_attention(grp, proj, None, None, None))
                    outs_b.append(fox_attention(grp, proj, fq, fk))
                r0, r1 = grp.row0, grp.row0 + grp.batch * grp.lq
                cut = lambda name, heads, d: proj[r0:r1, EVEN_COLS[name]:EVEN_COLS[name] + heads * d].reshape(
                    (grp.batch, grp.lq) + ((heads, d) if heads > 1 else (d,)))
                store.append((cut("ka", A_KV_HEADS, HEAD_DIM), cut("va", A_KV_HEADS, HEAD_DIM), cut("ki", 1, IDX_DIM),
                              cut("kf", B_HEADS, HEAD_DIM), cut("vf", B_HEADS, HEAD_DIM), logf))
            a1, a2 = jnp.concatenate(outs_a, axis=0), jnp.concatenate(outs_b, axis=0)
            n1 = A_HEADS * HEAD_DIM
            w1o, w2o = _mx(w_out_even[j][:n1]), _mx(w_out_even[j][n1:])
        else:
            lam_init = 0.8 - 0.6 * math.exp(-0.3 * i)
            lam_f = c_lambda[j]
            lam = (jnp.exp(jnp.sum(lam_f[0] * lam_f[1])) - jnp.exp(jnp.sum(lam_f[2] * lam_f[3])) + lam_init).reshape(1)
            q_lora, R = d_q_norm.shape[1], d_kv_norm.shape[1]
            sizes = (C_HEADS * 2 * C_QK_DIM, C_HEADS * 2 * C_QK_DIM, C_HEADS * C_V_DIM, q_lora, R, D_ROPE)
            names = (("qc", 0), ("kc", 0), ("vc", 0), ("qd", 0), ("ckv", 0), ("kr", 0))
            w_in = _mx(_place_cols(w_in_odd[j], sizes, names, ODD_COLS, ODD_WIDTH))
            cos, sin = _rope_tables(pos, C_QK_DIM // 2)
            proj = project(x, scale_g, shift_g, w_in, cos, sin, ODD_ROPE_RANGES, C_QK_DIM // 2, group, tm_tok)

            w_uq = d_w_uq[j].reshape(q_lora, D_HEADS, D_NOPE + D_ROPE)
            w_abs = bmm_precise(jnp.moveaxis(w_uq[:, :, :D_NOPE], 1, 0), jnp.transpose(d_w_uk[j], (1, 2, 0)))
            w_abs = jnp.moveaxis(w_abs, 0, 1).reshape(q_lora, D_HEADS * R)
            w_rope = jnp.pad(w_uq[:, :, D_NOPE:], ((0, 0), (0, 0), (0, LANES - D_ROPE))).reshape(q_lora, D_HEADS * LANES)
            q_abs, q_rope, lat = mla_prepare(proj, d_q_norm[j][None], d_kv_norm[j][None],
                                             _mx(jnp.concatenate([w_abs, w_rope], axis=1)), cos, sin, tm_close)
            n1 = C_HEADS * C_V_DIM
            w_od = w_out_odd[j][n1:].reshape(D_HEADS, D_V, D)
            w_lat_out = bmm_precise(jnp.transpose(d_w_uv[j], (1, 0, 2)), w_od).reshape(D_HEADS * R, D)

            outs_a, outs_b = [], []
            for grp, store in ((grp_p, od_p), (grp_s, od_s)):
                if grp.lp:
                    outs_a.append(diff_attention_past(grp, proj, lam, c_subln[j][None], 1.0 - lam_init,
                                                      flat_past(cache_c_k[j]), cache_c_v, j))
                    outs_b.append(mla_attention_past(grp, q_abs, q_rope, lat, proj, cache_d_latent, cache_d_krope, j))
                else:
                    outs_a.append(diff_attention(grp, proj, lam, c_subln[j][None], 1.0 - lam_init))
                    outs_b.append(mla_attention(grp, q_abs, q_rope, lat, proj))
                r0, r1 = grp.row0, grp.row0 + grp.batch * grp.lq
                shp = (grp.batch, grp.lq)
                oc = ODD_COLS
                store.append((proj[r0:r1, oc["kc"]:oc["kc"] + C_HEADS * 2 * C_QK_DIM].reshape(shp + (C_HEADS, 2, C_QK_DIM)),
                              proj[r0:r1, oc["vc"]:oc["vc"] + n1].reshape(shp + (C_HEADS, C_V_DIM)),
                              lat[r0:r1].reshape(shp + (R,)),
                              proj[r0:r1, oc["kr"]:oc["kr"] + D_ROPE].reshape(shp + (D_ROPE,))))
            a1, a2 = jnp.concatenate(outs_a, axis=0), jnp.concatenate(outs_b, axis=0)
            w1o, w2o = _mx(w_out_odd[j][:n1]), _mx(w_lat_out)
        x = close_mixer(a1, a2, w1o, w2o, x, gate_g, ln_mix_g[i][None], ln_mix_b[i][None], group, alpha, tm_close)

        shift_g, scale_g, gate_g = conditioning(ada_ffn_w, ada_ffn_b, i)
        x = _moe(x, scale_g, shift_g, gate_g, ln_ffn_g[i][None], ln_ffn_b[i][None], router_w[i], router_bias[i],
                 moe_w1, moe_w3, moe_w2, shared_w1[i], shared_w3[i], shared_w2[i], i, group, alpha)

    stack = lambda rows, idx: jnp.stack([r[idx] for r in rows])
    outs = [x[:Tp].reshape(B, L, D), x[Tp:].reshape(Bs, Ls, D)]
    for idx in range(6):
        outs += [stack(ev_p, idx), stack(ev_s, idx)]
    for idx in range(4):
        outs += [stack(od_p, idx), stack(od_s, idx)]
    return tuple(outs)
```

```python
import functools
import math

import jax
import jax.numpy as jnp
from jax import lax
from jax.experimental import pallas as pl
from jax.experimental.pallas import tpu as pltpu

CHUNK = 64
ROPE_THETA = 10000.0
HEAD_DIM = 128
A_HEADS = 8
A_KV_HEADS = 2
IDX_HEADS = 16
IDX_DIM = 128
A_TOPK_MAX = 256
B_HEADS = 8
C_HEADS = 8
C_QK_DIM = 64
C_V_DIM = 128
D_HEADS = 8
D_NOPE = 128
D_ROPE = 64
D_V = 128
N_GROUPS = 8
TOPK_GROUPS = 4
TOP_K = 8
ROUTED_SCALE = 2.5
NORM_EPS = 1e-6

LANES = 128
SUBLANES = 8
MXU_WIDTH = 256
VMEM_LIMIT_BYTES = 56 * 2**20
MXU_DTYPE = jnp.bfloat16

F32 = jnp.float32
I32 = jnp.int32
NEG = -1e30
INT_MIN = -2**31
KEY_BLOCK = 512
QUERY_BLOCK = 512
HEADS_PER_STEP = 2

EVEN_COLS = dict(qi=0, qa=2048, qf=3072, kf=4096, vf=5120, ka=6144, va=6400, ki=6656, misc=6912)
EVEN_WIDTH = 7168
EVEN_ROPE_RANGES = ((0, 3072), (6144, 6400), (6656, 6912))
MISC_WI, MISC_FL = 0, IDX_HEADS
ODD_COLS = dict(qc=0, kc=1024, vc=2048, qd=3072, ckv=3584, kr=3840)
ODD_WIDTH = 4096
ODD_ROPE_RANGES = ((0, 2048), (3840, 4096))


def _pick(n, target, mult):
    best = None
    for d in range(mult, min(n, target) + 1, mult):
        if n % d == 0:
            best = d
    return n if best is None else best


def _params(n_axes):
    return pltpu.CompilerParams(dimension_semantics=("arbitrary",) * n_axes,
                                vmem_limit_bytes=VMEM_LIMIT_BYTES)


def _mx(a):
    return a.astype(MXU_DTYPE)


def _dot(a, b):
    return jnp.dot(_mx(a), _mx(b), preferred_element_type=F32)


def _dot_nt(a, b, precision=None):
    if precision is None:
        a, b = _mx(a), _mx(b)
    return lax.dot_general(a, b, (((1,), (1,)), ((), ())), preferred_element_type=F32, precision=precision)


def _mm_body(a_ref, b_ref, o_ref, *, precise):
    if precise:
        o = jnp.dot(a_ref[...], b_ref[...], preferred_element_type=F32, precision=lax.Precision.HIGHEST)
    else:
        o = _dot(a_ref[...], b_ref[...])
    o_ref[...] = o.astype(o_ref.dtype)


def matmul(a, b, out_dtype=F32, precise=False, tm_target=1024, tn_target=512, layer=None):
    M, K = a.shape
    N = b.shape[-1]
    tm = _pick(M, tm_target, SUBLANES)
    tn = _pick(N, tn_target, LANES)
    if layer is None:
        b_spec = pl.BlockSpec((K, tn), lambda i, j: (0, j))
    else:
        b_spec = pl.BlockSpec((None, K, tn), lambda i, j: (layer, 0, j))
    return pl.pallas_call(
        functools.partial(_mm_body, precise=precise),
        grid=(M // tm, N // tn),
        in_specs=[pl.BlockSpec((tm, K), lambda i, j: (i, 0)), b_spec],
        out_specs=pl.BlockSpec((tm, tn), lambda i, j: (i, j)),
        out_shape=jax.ShapeDtypeStruct((M, N), out_dtype),
        compiler_params=_params(2),
        name="matmul",
    )(a, b)


def _bmm_body(a_ref, b_ref, o_ref):
    o_ref[0] = jnp.dot(a_ref[0], b_ref[0], preferred_element_type=F32,
                       precision=lax.Precision.HIGHEST).astype(o_ref.dtype)


def bmm_precise(a, b):
    H, M, K = a.shape
    N = b.shape[2]
    return pl.pallas_call(
        _bmm_body,
        grid=(H,),
        in_specs=[pl.BlockSpec((1, M, K), lambda h: (h, 0, 0)),
                  pl.BlockSpec((1, K, N), lambda h: (h, 0, 0))],
        out_specs=pl.BlockSpec((1, M, N), lambda h: (h, 0, 0)),
        out_shape=jax.ShapeDtypeStruct((H, M, N), F32),
        compiler_params=_params(1),
        name="bmm_precise",
    )(a, b)


def _per_tile(table, tm, group):
    return table.reshape(table.shape[0] * group // tm, tm // group, table.shape[1])


def _modulated(x_ref, sc_ref, sh_ref, group):
    parts = []
    for g in range(x_ref.shape[0] // group):
        rows = slice(g * group, (g + 1) * group)
        parts.append(x_ref[rows, :] * (1.0 + sc_ref[0, g:g + 1, :]) + sh_ref[0, g:g + 1, :])
    return parts


def _pack_rows(x):
    bits = pltpu.bitcast(x, I32)
    rounded = bits + 0x7FFF + ((bits >> 16) & 1)
    half = x.shape[1] // 2
    return ((rounded[:, :half] >> 16) & 0xFFFF) | (rounded[:, half:] & jnp.int32(-0x10000))


def _unpack_rows(w):
    lo = pltpu.bitcast(jnp.left_shift(w, 16), F32)
    hi = pltpu.bitcast(w & jnp.int32(-0x10000), F32)
    return jnp.concatenate([lo, hi], axis=1)


def _rope_lanes(x, cos, sin, half):
    if 2 * half == LANES:
        partner = pltpu.roll(x, half, axis=1)
    else:
        lane = lax.broadcasted_iota(I32, x.shape, 1)
        lower = (lane % (2 * half)) < half
        partner = jnp.where(lower, pltpu.roll(x, LANES - half, axis=1), pltpu.roll(x, half, axis=1))
    return x * cos + partner * sin


def _layer_norm_rows(r, g, b):
    rc = r - jnp.mean(r, axis=-1, keepdims=True)
    var = jnp.mean(rc * rc, axis=-1, keepdims=True)
    return rc * lax.rsqrt(var + NORM_EPS) * g + b


def _gated_residual(x_ref, y, gate_ref, group, alpha):
    parts = []
    for g in range(x_ref.shape[0] // group):
        rows = slice(g * group, (g + 1) * group)
        parts.append(alpha * x_ref[rows, :] + (1.0 + gate_ref[0, g:g + 1, :]) * y[rows, :])
    return jnp.concatenate(parts, axis=0)


def _proj_body(kind_ref, x_ref, sc_ref, sh_ref, w_ref, *rest, group, half, n_sub):
    tables, (o_ref, h_ref) = rest[:2 * n_sub], rest[2 * n_sub:]
    j = pl.program_id(1)

    @pl.when(j == 0)
    def _():
        for g, part in enumerate(_modulated(x_ref, sc_ref, sh_ref, group)):
            h_ref[g * group:(g + 1) * group, :] = part.astype(h_ref.dtype)

    for c in range(n_sub):
        cos, sin = tables[2 * c][...], tables[2 * c + 1][...]
        acc = jnp.dot(h_ref[...], w_ref[:, c * MXU_WIDTH:(c + 1) * MXU_WIDTH], preferred_element_type=F32)
        for p in range(MXU_WIDTH // LANES):
            o_ref[:, c * MXU_WIDTH + p * LANES:c * MXU_WIDTH + (p + 1) * LANES] = _rope_lanes(
                acc[:, p * LANES:(p + 1) * LANES], cos, sin, half)


def project(x, scale_g, shift_g, w, cos, sin, rope_ranges, half, group, tm):
    T, D = x.shape
    N = w.shape[1]
    n_sub = 2
    tn = n_sub * MXU_WIDTH
    start = jnp.arange(N // MXU_WIDTH) * MXU_WIDTH
    kinds = sum(((start >= lo) & (start < hi)).astype(I32) for lo, hi in rope_ranges)
    row = lambda i, j, k: (i, 0)
    tile = lambda i, j, k: (i, 0, 0)
    table_specs, table_args = [], []
    for c in range(n_sub):
        pick = lambda i, j, k, c=c: (k[j * n_sub + c], i, 0)
        table_specs += [pl.BlockSpec((None, tm, LANES), pick)] * 2
        table_args += [cos, sin]
    return pl.pallas_call(
        functools.partial(_proj_body, group=group, half=half, n_sub=n_sub),
        grid_spec=pltpu.PrefetchScalarGridSpec(
            num_scalar_prefetch=1,
            grid=(T // tm, N // tn),
            in_specs=[pl.BlockSpec((tm, D), row),
                      pl.BlockSpec((1, tm // group, D), tile),
                      pl.BlockSpec((1, tm // group, D), tile),
                      pl.BlockSpec((D, tn), lambda i, j, k: (0, j))] + table_specs,
            out_specs=pl.BlockSpec((tm, tn), lambda i, j, k: (i, j)),
            scratch_shapes=[pltpu.VMEM((tm, D), MXU_DTYPE)]),
        out_shape=jax.ShapeDtypeStruct((T, N), F32),
        compiler_params=_params(2),
        name="project",
    )(kinds, x, _per_tile(scale_g, tm, group), _per_tile(shift_g, tm, group), w, *table_args)


def _close_mix_body(a1_ref, a2_ref, w1_ref, w2_ref, x_ref, gate_ref, g_ref, b_ref, o_ref, *, group, alpha):
    y = _dot(a1_ref[...], w1_ref[...]) + _dot(a2_ref[...], w2_ref[...])
    r = _gated_residual(x_ref, y, gate_ref, group, alpha)
    o_ref[...] = _layer_norm_rows(r, g_ref[...], b_ref[...])


def close_mixer(a1, a2, w1, w2, x, gate_g, ln_g, ln_b, group, alpha, tm):
    T, D = x.shape
    row = lambda i: (i, 0)
    fixed = lambda i: (0, 0)
    return pl.pallas_call(
        functools.partial(_close_mix_body, group=group, alpha=alpha),
        grid=(T // tm,),
        in_specs=[pl.BlockSpec((tm, a1.shape[1]), row),
                  pl.BlockSpec((tm, a2.shape[1]), row),
                  pl.BlockSpec(w1.shape, fixed),
                  pl.BlockSpec(w2.shape, fixed),
                  pl.BlockSpec((tm, D), row),
                  pl.BlockSpec((1, tm // group, D), lambda i: (i, 0, 0)),
                  pl.BlockSpec((1, D), fixed),
                  pl.BlockSpec((1, D), fixed)],
        out_specs=pl.BlockSpec((tm, D), row),
        out_shape=jax.ShapeDtypeStruct((T, D), F32),
        compiler_params=_params(1),
        name="close_mixer",
    )(a1, a2, w1, w2, x, _per_tile(gate_g, tm, group), ln_g, ln_b)


def _online_step(s, v, carry):
    m, l, acc = carry
    m_new = jnp.maximum(m, jnp.max(s, axis=-1, keepdims=True))
    p = jnp.exp(s - m_new)
    alpha = jnp.exp(m - m_new)
    l = alpha * l + jnp.sum(p, axis=-1, keepdims=True)
    acc = alpha * acc + _dot(p, v)
    return m_new, l, acc


def _init_carry(tq, dv):
    return (jnp.full((tq, 1), NEG, F32), jnp.zeros((tq, 1), F32), jnp.zeros((tq, dv), F32))


def _chunk_last_key(i, tq):
    return ((i + 1) * tq - 1) // CHUNK * CHUNK + CHUNK - 1


class _Group:
    def __init__(self, batch, lq, lp, row0):
        self.batch, self.lq, self.lp, self.row0 = batch, lq, lp, row0
        self.lk = lp + lq
        self.tq = _pick(lq, QUERY_BLOCK, SUBLANES)
        self.tq_dsa = _pick(lq, 128, SUBLANES)
        self.tk = _pick(lq, KEY_BLOCK, SUBLANES)
        self.tkp = _pick(lp, KEY_BLOCK, LANES) if lp else 0
        self.n_past_blocks = lp // self.tkp if lp else 0
        self.topk = min(A_TOPK_MAX, self.lk // 4)
        assert lp % CHUNK == 0 and row0 % lq == 0 and lq % self.tk == 0

    def qrow(self, tq):
        base, per = self.row0 // tq, self.lq // tq
        return lambda b, i: base + b * per + i

    def krow(self):
        base = self.row0 // self.lq
        return lambda b: base + b


def _new_key_blocks(i, tq, grp_lq, tk, causal_last):
    return jnp.minimum(causal_last, grp_lq - 1) // tk + 1


def _fox_body(q_ref, k_ref, v_ref, fq_ref, fk_ref, o_ref, *, tq, tk, lq, hp):
    i = pl.program_id(2)
    d = HEAD_DIM
    head = lambda x, u: x[:, u * d:(u + 1) * d]
    q = _mx(q_ref[...] * d ** -0.5)
    carry = tuple(_init_carry(tq, d) for _ in range(hp))
    qpos = i * tq + lax.broadcasted_iota(I32, (tq, tk), 0)

    def new_step(j, c):
        off = pl.multiple_of(j * tk, tk)
        k, v = k_ref[pl.ds(off, tk), :], v_ref[pl.ds(off, tk), :]
        causal = off + lax.broadcasted_iota(I32, (tq, tk), 1) <= qpos
        out = []
        for u in range(hp):
            s = _dot_nt(head(q, u), head(k, u)) + fq_ref[u] - fk_ref[u, j][:, :tk]
            out.append(_online_step(jnp.where(causal, s, NEG), head(v, u), c[u]))
        return tuple(out)

    nb = _new_key_blocks(i, tq, lq, tk, (i + 1) * tq - 1)
    carry = lax.fori_loop(0, nb, new_step, carry)
    for u, (m, l, acc) in enumerate(carry):
        o_ref[:, u * d:(u + 1) * d] = (acc / l).astype(o_ref.dtype)


def fox_attention(grp, proj, fq, fk):
    assert grp.lp == 0
    B, H, hp = grp.batch, B_HEADS, HEADS_PER_STEP
    w = hp * HEAD_DIM
    tq = grp.tq
    qrow, krow = grp.qrow(tq), grp.krow()
    cq, ck, cv = (EVEN_COLS[n] // w for n in ("qf", "kf", "vf"))
    in_specs = [pl.BlockSpec((tq, w), lambda b, h, i: (qrow(b, i), cq + h)),
                pl.BlockSpec((grp.lq, w), lambda b, h, i: (krow(b), ck + h)),
                pl.BlockSpec((grp.lq, w), lambda b, h, i: (krow(b), cv + h))]
    in_specs += [pl.BlockSpec((hp, tq, 1), lambda b, h, i: (b * (H // hp) + h, i, 0)),
                 pl.BlockSpec((hp,) + fk.shape[1:], lambda b, h, i: (b * (H // hp) + h, 0, 0, 0))]
    return pl.pallas_call(
        functools.partial(_fox_body, tq=tq, tk=grp.tk, lq=grp.lq, hp=hp),
        grid=(B, H // hp, grp.lq // tq),
        in_specs=in_specs,
        out_specs=pl.BlockSpec((tq, w), lambda b, h, i: (b * (grp.lq // tq) + i, h)),
        out_shape=jax.ShapeDtypeStruct((B * grp.lq, H * HEAD_DIM), MXU_DTYPE),
        compiler_params=_params(3),
        name="fox_attention",
    )(proj, proj, proj, fq, fk)


def _diff_body(lam_ref, q_ref, k_ref, v_ref, g_ref, o_ref, *, tq, tk, lq, hp, out_scale):
    i = pl.program_id(2)
    d = C_V_DIM
    head = lambda x, u: x[:, u * d:(u + 1) * d]
    q = q_ref[...] * C_QK_DIM ** -0.5
    lane = lax.broadcasted_iota(I32, q.shape, 1) % (2 * C_QK_DIM)
    qs = (_mx(jnp.where(lane < C_QK_DIM, q, 0.0)), _mx(jnp.where(lane >= C_QK_DIM, q, 0.0)))
    carry = tuple(_init_carry(tq, d) for _ in range(2 * hp))

    def step(k, v, ok, c):
        out = []
        for u in range(hp):
            ku, vu = _mx(head(k, u)), _mx(head(v, u))
            for t in range(2):
                s = _dot_nt(head(qs[t], u), ku)
                out.append(_online_step(jnp.where(ok, s, NEG), vu, c[2 * u + t]))
        return tuple(out)

    qchunk = (i * tq + lax.broadcasted_iota(I32, (tq, tk), 0)) // CHUNK

    def new_step(j, c):
        off = pl.multiple_of(j * tk, tk)
        ok = (off + lax.broadcasted_iota(I32, (tq, tk), 1)) // CHUNK <= qchunk
        return step(k_ref[pl.ds(off, tk), :], v_ref[pl.ds(off, tk), :], ok, c)

    nb = _new_key_blocks(i, tq, lq, tk, _chunk_last_key(i, tq))
    carry = lax.fori_loop(0, nb, new_step, carry)
    for u in range(hp):
        (_, l0, a0), (_, l1, a1) = carry[2 * u], carry[2 * u + 1]
        o = a0 / l0 - lam_ref[0] * (a1 / l1)
        o = o * lax.rsqrt(jnp.mean(o * o, axis=-1, keepdims=True) + NORM_EPS)
        o_ref[:, u * d:(u + 1) * d] = (o * g_ref[...] * out_scale).astype(o_ref.dtype)


def diff_attention(grp, proj, lam, subln, out_scale):
    assert grp.lp == 0
    B, H, hp = grp.batch, C_HEADS, HEADS_PER_STEP
    w = hp * C_V_DIM
    tq = grp.tq
    qrow, krow = grp.qrow(tq), grp.krow()
    cq, ck, cv = (ODD_COLS[n] // w for n in ("qc", "kc", "vc"))
    in_specs = [pl.BlockSpec(memory_space=pltpu.SMEM),
                pl.BlockSpec((tq, w), lambda b, h, i: (qrow(b, i), cq + h)),
                pl.BlockSpec((grp.lq, w), lambda b, h, i: (krow(b), ck + h)),
                pl.BlockSpec((grp.lq, w), lambda b, h, i: (krow(b), cv + h))]
    in_specs += [pl.BlockSpec((1, C_V_DIM), lambda b, h, i: (0, 0))]
    return pl.pallas_call(
        functools.partial(_diff_body, tq=tq, tk=grp.tk, lq=grp.lq, hp=hp, out_scale=out_scale),
        grid=(B, H // hp, grp.lq // tq),
        in_specs=in_specs,
        out_specs=pl.BlockSpec((tq, w), lambda b, h, i: (b * (grp.lq // tq) + i, h)),
        out_shape=jax.ShapeDtypeStruct((B * grp.lq, H * C_V_DIM), MXU_DTYPE),
        compiler_params=_params(3),
        name="diff_attention",
    )(lam, proj, proj, proj, subln)


def _mla_prep_body(qd_ref, ckv_ref, qg_ref, kg_ref, w_ref, cos_ref, sin_ref, qa_ref, qr_ref, lat_ref, *, n_abs):
    qd = qd_ref[...]
    qn = qd * lax.rsqrt(jnp.mean(qd * qd, axis=-1, keepdims=True) + NORM_EPS) * qg_ref[...]
    q = _dot(qn, w_ref[...]) * (D_NOPE + D_ROPE) ** -0.5
    qa_ref[...] = q[:, :n_abs].astype(qa_ref.dtype)
    for h in range((q.shape[1] - n_abs) // LANES):
        lanes = slice(n_abs + h * LANES, n_abs + (h + 1) * LANES)
        qr_ref[:, h * LANES:(h + 1) * LANES] = _rope_lanes(
            q[:, lanes], cos_ref[...], sin_ref[...], D_ROPE // 2).astype(qr_ref.dtype)
    ckv = ckv_ref[...]
    lat_ref[...] = ckv * lax.rsqrt(jnp.mean(ckv * ckv, axis=-1, keepdims=True) + NORM_EPS) * kg_ref[...]


def mla_prepare(proj, q_norm, kv_norm, w_q, cos, sin, tm):
    T = proj.shape[0]
    q_lora, R = q_norm.shape[1], kv_norm.shape[1]
    n_abs = D_HEADS * R
    n_rope = D_HEADS * LANES
    row = lambda i: (i, 0)
    fixed = lambda i: (0, 0)
    return pl.pallas_call(
        functools.partial(_mla_prep_body, n_abs=n_abs),
        grid=(T // tm,),
        in_specs=[pl.BlockSpec((tm, q_lora), lambda i: (i, ODD_COLS["qd"] // q_lora)),
                  pl.BlockSpec((tm, R), lambda i: (i, ODD_COLS["ckv"] // R)),
                  pl.BlockSpec((1, q_lora), fixed),
                  pl.BlockSpec((1, R), fixed),
                  pl.BlockSpec(w_q.shape, fixed),
                  pl.BlockSpec((None, tm, LANES), lambda i: (1, i, 0)),
                  pl.BlockSpec((None, tm, LANES), lambda i: (1, i, 0))],
        out_specs=[pl.BlockSpec((tm, n_abs), row), pl.BlockSpec((tm, n_rope), row), pl.BlockSpec((tm, R), row)],
        out_shape=[jax.ShapeDtypeStruct((T, n_abs), MXU_DTYPE), jax.ShapeDtypeStruct((T, n_rope), MXU_DTYPE),
                   jax.ShapeDtypeStruct((T, R), F32)],
        compiler_params=_params(1),
        name="mla_prepare",
    )(proj, proj, q_norm, kv_norm, w_q, cos, sin)


def _mla_body(qa_ref, qr_ref, lat_ref, kr_ref, o_ref, *, tq, tk, lq, hp):
    i = pl.program_id(2)
    R = lat_ref.shape[-1]
    qa = [qa_ref[:, u * R:(u + 1) * R] for u in range(hp)]
    qr = [qr_ref[:, u * LANES:(u + 1) * LANES] for u in range(hp)]
    carry = tuple(_init_carry(tq, R) for _ in range(hp))

    qchunk = (i * tq + lax.broadcasted_iota(I32, (tq, tk), 0)) // CHUNK

    def new_step(j, c):
        off = pl.multiple_of(j * tk, tk)
        lat = _mx(lat_ref[pl.ds(off, tk), :])
        kr = _mx(kr_ref[pl.ds(off, tk), :])
        ok = (off + lax.broadcasted_iota(I32, (tq, tk), 1)) // CHUNK <= qchunk
        return tuple(_online_step(jnp.where(ok, _dot_nt(qa[u], lat) + _dot_nt(qr[u], kr), NEG), lat, c[u])
                     for u in range(hp))

    nb = _new_key_blocks(i, tq, lq, tk, _chunk_last_key(i, tq))
    carry = lax.fori_loop(0, nb, new_step, carry)
    for u, (m, l, acc) in enumerate(carry):
        o_ref[:, u * R:(u + 1) * R] = (acc / l).astype(o_ref.dtype)


def mla_attention(grp, q_abs, q_rope, lat, proj):
    assert grp.lp == 0
    B, H, hp = grp.batch, D_HEADS, HEADS_PER_STEP
    R = lat.shape[1]
    tq = grp.tq
    qrow, krow = grp.qrow(tq), grp.krow()
    in_specs = [pl.BlockSpec((tq, hp * R), lambda b, h, i: (qrow(b, i), h)),
                pl.BlockSpec((tq, hp * LANES), lambda b, h, i: (qrow(b, i), h)),
                pl.BlockSpec((grp.lq, R), lambda b, h, i: (krow(b), 0)),
                pl.BlockSpec((grp.lq, LANES), lambda b, h, i: (krow(b), ODD_COLS["kr"] // LANES))]
    return pl.pallas_call(
        functools.partial(_mla_body, tq=tq, tk=grp.tk, lq=grp.lq, hp=hp),
        grid=(B, H // hp, grp.lq // tq),
        in_specs=in_specs,
        out_specs=pl.BlockSpec((tq, hp * R), lambda b, h, i: (b * (grp.lq // tq) + i, h)),
        out_shape=jax.ShapeDtypeStruct((B * grp.lq, H * R), MXU_DTYPE),
        compiler_params=_params(3),
        name="mla_attention",
    )(q_abs, q_rope, lat, proj)


def _head_cols(ref, h, d, heads):
    if ref.shape[-1] == d:
        return ref[pl.ds(h, ref.shape[0] // heads, stride=heads), :]
    return ref[:, h * d:(h + 1) * d]


def _chain_reset(j, m_ref, l_ref, acc_ref):
    @pl.when(j == 0)
    def _():
        m_ref[...] = jnp.full(m_ref.shape, NEG, F32)
        l_ref[...] = jnp.zeros(l_ref.shape, F32)
        acc_ref[...] = jnp.zeros(acc_ref.shape, F32)


def _chains_advance(scores_values, m_ref, l_ref, acc_ref):
    states = [(m_ref[c], l_ref[c], acc_ref[c]) for c in range(len(scores_values))]
    new = [_online_step(s, v, st) for (s, v), st in zip(scores_values, states)]
    for c, (m, l, acc) in enumerate(new):
        m_ref[c], l_ref[c], acc_ref[c] = m, l, acc
    return [(l, acc) for _, l, acc in new]


def _past_key_spec(cache, layer, tkp, n_past, rows_per_key=1):
    blk = lambda j: jnp.minimum(j, n_past - 1)
    rows, w = tkp * rows_per_key, cache.shape[-1]
    if layer is None:
        return pl.BlockSpec((None, rows, w), lambda b, j: (b, blk(j), 0))
    return pl.BlockSpec((None, None, rows, w), lambda b, j: (layer, b, blk(j), 0))


def _keys_by_head(cache):
    n, b, p, h, d = cache.shape
    return cache.reshape(n, b, p * h, d)


def _past_call(body, grp, heads, dv, chains, in_specs, args, name):
    lq = grp.lq
    assert grp.lq == grp.tk
    return pl.pallas_call(
        body,
        grid=(grp.batch, grp.n_past_blocks + 1),
        in_specs=in_specs,
        out_specs=pl.BlockSpec((lq, heads * dv), lambda b, j: (b, 0)),
        out_shape=jax.ShapeDtypeStruct((grp.batch * lq, heads * dv), MXU_DTYPE),
        scratch_shapes=[pltpu.VMEM((chains, lq, 1), F32), pltpu.VMEM((chains, lq, 1), F32),
                        pltpu.VMEM((chains, lq, dv), F32)],
        compiler_params=_params(2),
        name=name,
    )(*args)


def _fox_past_body(q_ref, k_ref, v_ref, pk_ref, pv_ref, fq_ref, fk_ref, o_ref, m_ref, l_ref, acc_ref, *, n_past):
    j = pl.program_id(1)
    d, lq = HEAD_DIM, q_ref.shape[0]
    _chain_reset(j, m_ref, l_ref, acc_ref)
    q = lambda h: _mx(q_ref[:, h * d:(h + 1) * d] * d ** -0.5)

    @pl.when(j < n_past)
    def _():
        _chains_advance([(_dot_nt(q(h), _head_cols(pk_ref, h, d, B_HEADS)) + fq_ref[h] - fk_ref[h, 0],
                          _head_cols(pv_ref, h, d, B_HEADS)) for h in range(B_HEADS)], m_ref, l_ref, acc_ref)

    @pl.when(j == n_past)
    def _():
        causal = lax.broadcasted_iota(I32, (lq, lq), 1) <= lax.broadcasted_iota(I32, (lq, lq), 0)
        logits = lambda h: _dot_nt(q(h), k_ref[:, h * d:(h + 1) * d]) + fq_ref[h] - fk_ref[h, 0][:, :lq]
        done = _chains_advance([(jnp.where(causal, logits(h), NEG), v_ref[:, h * d:(h + 1) * d])
                                for h in range(B_HEADS)], m_ref, l_ref, acc_ref)
        for h, (l, acc) in enumerate(done):
            o_ref[:, h * d:(h + 1) * d] = (acc / l).astype(o_ref.dtype)


def fox_attention_past(grp, proj, fq, fk, cache_k, cache_v, layer):
    H, d, lq = B_HEADS, HEAD_DIM, grp.lq
    row = grp.krow()
    w = H * d
    in_specs = [pl.BlockSpec((lq, w), lambda b, j, n=n: (row(b), EVEN_COLS[n] // w)) for n in ("qf", "kf", "vf")]
    cache_k, cache_v = _keys_by_head(cache_k), _keys_by_head(cache_v)
    in_specs += [_past_key_spec(cache_k, layer, grp.tkp, grp.n_past_blocks, H),
                 _past_key_spec(cache_v, layer, grp.tkp, grp.n_past_blocks, H),
                 pl.BlockSpec((H, lq, 1), lambda b, j: (b, 0, 0)),
                 pl.BlockSpec((H, 1, 1, fk.shape[-1]), lambda b, j: (b, j, 0, 0))]
    return _past_call(functools.partial(_fox_past_body, n_past=grp.n_past_blocks), grp, H, d, H, in_specs,
                      [proj, proj, proj, cache_k, cache_v, fq, fk], "fox_attention_past")


def _diff_past_body(lam_ref, q_ref, k_ref, v_ref, pk_ref, pv_ref, g_ref, o_ref, m_ref, l_ref, acc_ref,
                    *, n_past, out_scale):
    j = pl.program_id(1)
    d = C_V_DIM
    _chain_reset(j, m_ref, l_ref, acc_ref)
    lane = lax.broadcasted_iota(I32, (q_ref.shape[0], d), 1)

    def maps(h):
        q = q_ref[:, h * d:(h + 1) * d] * C_QK_DIM ** -0.5
        return _mx(jnp.where(lane < C_QK_DIM, q, 0.0)), _mx(jnp.where(lane >= C_QK_DIM, q, 0.0))

    def both_maps(key_of, value_of):
        items = []
        for h in range(C_HEADS):
            k, v = _mx(key_of(h)), _mx(value_of(h))
            items += [(_dot_nt(qt, k), v) for qt in maps(h)]
        return _chains_advance(items, m_ref, l_ref, acc_ref)

    @pl.when(j < n_past)
    def _():
        both_maps(lambda h: _head_cols(pk_ref, h, d, C_HEADS), lambda h: _head_cols(pv_ref, h, d, C_HEADS))

    @pl.when(j == n_past)
    def _():
        done = both_maps(lambda h: k_ref[:, h * d:(h + 1) * d], lambda h: v_ref[:, h * d:(h + 1) * d])
        for h in range(C_HEADS):
            (l0, a0), (l1, a1) = done[2 * h], done[2 * h + 1]
            o = a0 / l0 - lam_ref[0] * (a1 / l1)
            o = o * lax.rsqrt(jnp.mean(o * o, axis=-1, keepdims=True) + NORM_EPS)
            o_ref[:, h * d:(h + 1) * d] = (o * g_ref[...] * out_scale).astype(o_ref.dtype)


def diff_attention_past(grp, proj, lam, subln, out_scale, cache_k, cache_v, layer):
    H, d, lq = C_HEADS, C_V_DIM, grp.lq
    assert lq <= CHUNK and grp.lp % CHUNK == 0
    row = grp.krow()
    w = H * d
    in_specs = [pl.BlockSpec(memory_space=pltpu.SMEM)]
    in_specs += [pl.BlockSpec((lq, w), lambda b, j, n=n: (row(b), ODD_COLS[n] // w)) for n in ("qc", "kc", "vc")]
    cache_v = _keys_by_head(cache_v)
    in_specs += [_past_key_spec(cache_k, None, grp.tkp, grp.n_past_blocks),
                 _past_key_spec(cache_v, layer, grp.tkp, grp.n_past_blocks, H),
                 pl.BlockSpec((1, d), lambda b, j: (0, 0))]
    return _past_call(functools.partial(_diff_past_body, n_past=grp.n_past_blocks, out_scale=out_scale), grp, H, d,
                      2 * H, in_specs, [lam, proj, proj, proj, cache_k, cache_v, subln], "diff_attention_past")


def _mla_past_body(qa_ref, qr_ref, lat_ref, kr_ref, plat_ref, pkr_ref, o_ref, m_ref, l_ref, acc_ref, *, n_past):
    j = pl.program_id(1)
    R = lat_ref.shape[-1]
    _chain_reset(j, m_ref, l_ref, acc_ref)

    @pl.when(j < n_past)
    def _():
        lat, kr = _mx(plat_ref[...]), _mx(pkr_ref[...])
        _chains_advance([(_dot_nt(qa_ref[:, h * R:(h + 1) * R], lat)
                          + _dot_nt(qr_ref[:, h * LANES:h * LANES + D_ROPE], kr), lat) for h in range(D_HEADS)],
                        m_ref, l_ref, acc_ref)

    @pl.when(j == n_past)
    def _():
        lat, kr = _mx(lat_ref[...]), _mx(kr_ref[...])
        done = _chains_advance([(_dot_nt(qa_ref[:, h * R:(h + 1) * R], lat)
                                 + _dot_nt(qr_ref[:, h * LANES:(h + 1) * LANES], kr), lat) for h in range(D_HEADS)],
                               m_ref, l_ref, acc_ref)
        for h, (l, acc) in enumerate(done):
            o_ref[:, h * R:(h + 1) * R] = (acc / l).astype(o_ref.dtype)


def mla_attention_past(grp, q_abs, q_rope, lat, proj, cache_lat, cache_kr, layer):
    H, lq = D_HEADS, grp.lq
    assert lq <= CHUNK and grp.lp % CHUNK == 0
    R = lat.shape[1]
    row = grp.krow()
    in_specs = [pl.BlockSpec((lq, H * R), lambda b, j: (row(b), 0)),
                pl.BlockSpec((lq, H * LANES), lambda b, j: (row(b), 0)),
                pl.BlockSpec((lq, R), lambda b, j: (row(b), 0)),
                pl.BlockSpec((lq, LANES), lambda b, j: (row(b), ODD_COLS["kr"] // LANES)),
                _past_key_spec(cache_lat, layer, grp.tkp, grp.n_past_blocks),
                _past_key_spec(cache_kr, layer, grp.tkp, grp.n_past_blocks)]
    return _past_call(functools.partial(_mla_past_body, n_past=grp.n_past_blocks), grp, H, R, H, in_specs,
                      [q_abs, q_rope, lat, proj, cache_lat, cache_kr], "mla_attention_past")


def _sortable(score):
    bits = pltpu.bitcast(score, I32)
    return jnp.where(bits < 0, bits ^ jnp.int32(0x7FFFFFFF), bits)


def _dsa_body(*refs, tq, i0, nk, topk, has_past):
    if has_past:
        qi_ref, misc_ref, q_ref, ki_ref, k_ref, v_ref, pki_ref, pk_ref, pv_ref, o_ref = refs
    else:
        qi_ref, misc_ref, q_ref, ki_ref, k_ref, v_ref, o_ref = refs
    i = i0 + pl.program_id(1)
    wi = misc_ref[:, MISC_WI:MISC_WI + IDX_HEADS] * (IDX_HEADS * IDX_DIM) ** -0.5

    qchunk = (i * tq + lax.broadcasted_iota(I32, (tq, nk), 0)) // CHUNK
    adm_new = lax.broadcasted_iota(I32, (tq, nk), 1) // CHUNK <= qchunk
    segs = [(_mx(ki_ref[...]), k_ref, v_ref, adm_new)]
    if has_past:
        segs.insert(0, (_mx(pki_ref[0]), pk_ref.at[0], pv_ref.at[0], None))

    keys = []
    for ki, _, _, adm in segs:
        score = jnp.zeros((tq, ki.shape[0]), F32)
        for h in range(IDX_HEADS):
            s = _dot_nt(qi_ref[:, h * IDX_DIM:(h + 1) * IDX_DIM], ki)
            score = score + jnp.maximum(s, 0.0) * wi[:, h:h + 1]
        key = _sortable(score)
        keys.append(key if adm is None else jnp.where(adm, key, INT_MIN))

    def count_ge(cand):
        return sum(jnp.sum(jnp.where(key >= cand, 1.0, 0.0), axis=-1, keepdims=True) for key in keys)

    kf = float(topk)
    t = jnp.where(count_ge(jnp.zeros((tq, 1), I32)) >= kf, 0, INT_MIN).astype(I32)

    def bit_step(it, t):
        cand = t | jnp.left_shift(jnp.int32(1), 30 - it)
        return jnp.where(count_ge(cand) >= kf, cand, t)

    t = lax.fori_loop(0, 31, bit_step, t)
    sels = [(key >= t) if adm is None else (adm & (key >= t)) for key, (_, _, _, adm) in zip(keys, segs)]

    rep = A_HEADS // A_KV_HEADS
    for g in range(A_KV_HEADS):
        kv = [(_mx(k[:, g * HEAD_DIM:(g + 1) * HEAD_DIM]), _mx(v[:, g * HEAD_DIM:(g + 1) * HEAD_DIM]))
              for _, k, v, _ in segs]
        for r in range(rep):
            h = g * rep + r
            q = _mx(q_ref[:, h * HEAD_DIM:(h + 1) * HEAD_DIM] * HEAD_DIM ** -0.5)
            ss = [jnp.where(sel, _dot_nt(q, k), NEG) for sel, (k, _) in zip(sels, kv)]
            m = functools.reduce(jnp.maximum, [jnp.max(s, axis=-1, keepdims=True) for s in ss])
            ps = [jnp.exp(s - m) for s in ss]
            l = sum(jnp.sum(p, axis=-1, keepdims=True) for p in ps)
            o = sum(_dot(p, v) for p, (_, v) in zip(ps, kv))
            o_ref[:, h * HEAD_DIM:(h + 1) * HEAD_DIM] = (o / l).astype(o_ref.dtype)


def dsa_attention(grp, proj, past_ki, past_k, past_v):
    B = grp.batch
    tq = grp.tq_dsa
    n_tiles = grp.lq // tq
    wq, wkv = A_HEADS * HEAD_DIM, A_KV_HEADS * HEAD_DIM
    wqi = IDX_HEADS * IDX_DIM
    c = EVEN_COLS

    extents = sorted({grp.lq // f for f in (1, 2, 4) if grp.lq % f == 0 and (grp.lq // f) % tq == 0
                      and grp.row0 % (grp.lq // f) == 0})
    classes = []
    for i in range(n_tiles):
        need = min(grp.lq, _chunk_last_key(i, tq) + 1)
        nk = min(e for e in extents if e >= need)
        if classes and classes[-1][2] == nk:
            classes[-1][1] += 1
        else:
            classes.append([i, 1, nk])

    outs = []
    for i0, n_i, nk in classes:
        qrow = lambda b, i, i0=i0: grp.row0 // tq + b * n_tiles + i0 + i
        krow = lambda b, nk=nk: (grp.row0 + b * grp.lq) // nk
        in_specs = [pl.BlockSpec((tq, wqi), lambda b, i, f=qrow: (f(b, i), c["qi"] // wqi)),
                    pl.BlockSpec((tq, LANES), lambda b, i, f=qrow: (f(b, i), c["misc"] // LANES)),
                    pl.BlockSpec((tq, wq), lambda b, i, f=qrow: (f(b, i), c["qa"] // wq)),
                    pl.BlockSpec((nk, IDX_DIM), lambda b, i, f=krow: (f(b), c["ki"] // IDX_DIM)),
                    pl.BlockSpec((nk, wkv), lambda b, i, f=krow: (f(b), c["ka"] // wkv)),
                    pl.BlockSpec((nk, wkv), lambda b, i, f=krow: (f(b), c["va"] // wkv))]
        args = [proj] * 6
        if grp.lp:
            in_specs += [pl.BlockSpec((1, grp.lp, IDX_DIM), lambda b, i: (b, 0, 0)),
                         pl.BlockSpec((1, grp.lp, wkv), lambda b, i: (b, 0, 0)),
                         pl.BlockSpec((1, grp.lp, wkv), lambda b, i: (b, 0, 0))]
            args += [past_ki, past_k, past_v]
        o = pl.pallas_call(
            functools.partial(_dsa_body, tq=tq, i0=i0, nk=nk, topk=grp.topk, has_past=bool(grp.lp)),
            grid=(B, n_i),
            in_specs=in_specs,
            out_specs=pl.BlockSpec((tq, wq), lambda b, i, n_i=n_i: (b * n_i + i, 0)),
            out_shape=jax.ShapeDtypeStruct((B * n_i * tq, wq), MXU_DTYPE),
            compiler_params=_params(2),
            name="dsa_attention",
        )(*args)
        outs.append(o.reshape(B, n_i * tq, wq))
    return jnp.concatenate(outs, axis=1).reshape(B * grp.lq, wq)


def _first_index_of_max(x, iota, n):
    mx = jnp.max(x, axis=0, keepdims=True)
    return mx, jnp.min(jnp.where(x == mx, iota, n), axis=0, keepdims=True)


def _route_body(x_ref, sc_ref, sh_ref, wr_ref, bias_ref, hp_ref, gate_ref, rank_ref, cnt_ref, h_ref, carry_ref,
                *, group):
    step = pl.program_id(0)
    tm = x_ref.shape[0]
    E = wr_ref.shape[0]
    per = E // N_GROUPS
    for g, part in enumerate(_modulated(x_ref, sc_ref, sh_ref, group)):
        h_ref[g * group:(g + 1) * group, :] = part
    h = h_ref[...]
    hp_ref[...] = _pack_rows(h)

    scores = jax.nn.sigmoid(_dot_nt(wr_ref[...], h, precision=lax.Precision.HIGHEST))
    biased = scores + bias_ref[...]
    member = lax.broadcasted_iota(I32, (per, tm), 0).astype(F32)
    gscore = []
    for g in range(N_GROUPS):
        blk = biased[g * per:(g + 1) * per, :]
        m1, i1 = _first_index_of_max(blk, member, per)
        m2 = jnp.max(jnp.where(member == i1, -jnp.inf, blk), axis=0, keepdims=True)
        gscore.append(m1 + m2)
    gscore = jnp.concatenate(gscore, axis=0)
    giota = lax.broadcasted_iota(I32, (N_GROUPS, tm), 0).astype(F32)
    gsel = jnp.zeros((N_GROUPS, tm), F32)
    for _ in range(TOPK_GROUPS):
        _, gi = _first_index_of_max(gscore, giota, N_GROUPS)
        hit = giota == gi
        gsel = jnp.where(hit, 1.0, gsel)
        gscore = jnp.where(hit, -jnp.inf, gscore)
    emask = jnp.concatenate([jnp.broadcast_to(gsel[g:g + 1, :], (per, tm)) for g in range(N_GROUPS)], axis=0)
    masked = jnp.where(emask > 0.0, biased, -jnp.inf)
    eiota = lax.broadcasted_iota(I32, (E, tm), 0).astype(F32)
    sel = jnp.zeros((E, tm), jnp.bool_)
    for _ in range(TOP_K):
        _, ei = _first_index_of_max(masked, eiota, E)
        hit = eiota == ei
        sel = sel | hit
        masked = jnp.where(hit, -jnp.inf, masked)
    w = jnp.where(sel, scores, 0.0)
    gate_ref[...] = w / jnp.sum(w, axis=0, keepdims=True) * ROUTED_SCALE

    @pl.when(step == 0)
    def _():
        carry_ref[...] = jnp.zeros(carry_ref.shape, F32)
    upper = (lax.broadcasted_iota(I32, (tm, tm), 0) <= lax.broadcasted_iota(I32, (tm, tm), 1))
    self = jnp.where(sel, 1.0, 0.0)
    incl = _dot(self, jnp.where(upper, 1.0, 0.0)) + carry_ref[...]
    rank_ref[...] = jnp.where(sel, incl - 1.0, -1.0).astype(I32)
    carry_ref[...] = incl[:, tm - 1:tm]
    cnt_ref[...] = incl[:, tm - 1:tm].astype(I32)


def route(x, scale_g, shift_g, w_router_t, r_bias, group, tm):
    T, D = x.shape
    E = w_router_t.shape[0]
    row = lambda i: (i, 0)
    col = lambda i: (0, i)
    fixed = lambda i: (0, 0)
    return pl.pallas_call(
        functools.partial(_route_body, group=group),
        grid=(T // tm,),
        in_specs=[pl.BlockSpec((tm, D), row),
                  pl.BlockSpec((1, tm // group, D), lambda i: (i, 0, 0)),
                  pl.BlockSpec((1, tm // group, D), lambda i: (i, 0, 0)),
                  pl.BlockSpec((E, D), fixed),
                  pl.BlockSpec((E, 1), fixed)],
        out_specs=[pl.BlockSpec((tm, D // 2), row), pl.BlockSpec((E, tm), col), pl.BlockSpec((E, tm), col),
                   pl.BlockSpec((E, 1), fixed)],
        out_shape=[jax.ShapeDtypeStruct((T, D // 2), I32), jax.ShapeDtypeStruct((E, T), F32),
                   jax.ShapeDtypeStruct((E, T), I32), jax.ShapeDtypeStruct((E, 1), I32)],
        scratch_shapes=[pltpu.VMEM((tm, D), F32), pltpu.VMEM((E, 1), F32)],
        compiler_params=_params(1),
        name="route",
    )(x, _per_tile(scale_g, tm, group), _per_tile(shift_g, tm, group), w_router_t, r_bias)


def _compact_body(gate_ref, rank_ref, off_ref, dest_ref, g8_ref):
    E, tm = gate_ref.shape
    rank = rank_ref[...]
    sel = rank >= 0
    dest = (rank + off_ref[...]).astype(F32)
    gate = gate_ref[...]
    eiota = lax.broadcasted_iota(I32, (E, tm), 0).astype(F32)
    dests, gates = [], []
    for _ in range(TOP_K):
        ei = jnp.min(jnp.where(sel, eiota, E), axis=0, keepdims=True)
        hit = eiota == ei
        dests.append(jnp.sum(jnp.where(hit, dest, 0.0), axis=0, keepdims=True))
        gates.append(jnp.sum(jnp.where(hit, gate, 0.0), axis=0, keepdims=True))
        sel = sel & jnp.logical_not(hit)
    dest_ref[...] = jnp.concatenate(dests, axis=0).astype(I32)
    g8_ref[...] = jnp.concatenate(gates, axis=0)


def compact_routes(gate, rank, seg_off, tm):
    E, T = gate.shape
    col = lambda i: (0, i)
    return pl.pallas_call(
        _compact_body,
        grid=(T // tm,),
        in_specs=[pl.BlockSpec((E, tm), col), pl.BlockSpec((E, tm), col), pl.BlockSpec((E, 1), lambda i: (0, 0))],
        out_specs=[pl.BlockSpec((TOP_K, tm), col), pl.BlockSpec((TOP_K, tm), col)],
        out_shape=[jax.ShapeDtypeStruct((TOP_K, T), I32), jax.ShapeDtypeStruct((TOP_K, T), F32)],
        compiler_params=_params(1),
        name="compact_routes",
    )(gate, rank, seg_off)


def _ffn(x, w1, w3, w2):
    a = _dot(x, w1)
    return _dot(a * jax.nn.sigmoid(a) * _dot(x, w3), w2)


def _gmm_body(te_ref, nt_ref, tok_ref, tok_next_ref, slot_ref, x_hbm, w1_ref, w3_ref, w2_ref, y_hbm,
              w1b, w3b, w2b, xbuf, ybuf, in_sem, out_sem, *, tm, parts):
    i = pl.program_id(0)
    n_used = nt_ref[0]
    cur, nxt = i % 2, (i + 1) % 2

    def row_in(tok, r, buf):
        return pltpu.make_async_copy(x_hbm.at[pl.ds(tok[0, 0, r], 1)], xbuf.at[buf, pl.ds(r, 1)], in_sem.at[buf])

    def row_out(r, buf):
        return pltpu.make_async_copy(ybuf.at[buf, pl.ds(r, 1)], y_hbm.at[pl.ds(slot_ref[0, 0, r], 1)], out_sem.at[buf])

    def wait_in(buf):
        pltpu.make_async_copy(x_hbm.at[pl.ds(0, tm)], xbuf.at[buf], in_sem.at[buf]).wait()

    def wait_out(buf):
        pltpu.make_async_copy(ybuf.at[buf], y_hbm.at[pl.ds(0, tm)], out_sem.at[buf]).wait()

    @pl.when(i == 0)
    def _():
        for r in range(tm):
            row_in(tok_ref, r, 0).start()
        ybuf[1] = jnp.zeros((tm, ybuf.shape[2]), I32)
        spare = pltpu.make_async_copy(ybuf.at[1], y_hbm.at[pl.ds(y_hbm.shape[0] - tm, tm)], out_sem.at[1])
        spare.start()
        spare.wait()

    @pl.when(jnp.logical_or(i == 0, te_ref[i] != te_ref[jnp.maximum(i - 1, 0)]))
    def _():
        w1b[...] = _mx(w1_ref[...])
        w3b[...] = _mx(w3_ref[...])
        w2b[...] = _mx(w2_ref[...])

    @pl.when(i < n_used)
    def _():
        wait_in(cur)
        for r in range(tm):
            row_in(tok_next_ref, r, nxt).start()
        rows = tm // parts
        for p in range(parts):
            x = _unpack_rows(xbuf[cur, p * rows:(p + 1) * rows, :])
            ybuf[cur, p * rows:(p + 1) * rows, :] = _pack_rows(_ffn(x, w1b[...], w3b[...], w2b[...]))
            for r in range(p * rows, (p + 1) * rows):
                row_out(r, cur).start()

    @pl.when(jnp.logical_and(i >= 1, i < n_used))
    def _():
        wait_out(nxt)

    @pl.when(i == n_used - 1)
    def _():
        wait_in(nxt)
        wait_out(cur)


def grouped_ffn(tile_expert, n_tiles, row_token, row_slot, x, w1, w3, w2, layer, n_out, tm):
    n_rows, W = row_token.shape[0] * tm, x.shape[1]
    D, F = w1.shape[-2], w1.shape[-1]
    clamp = lambda i, nt: jnp.maximum(jnp.minimum(i, nt[0] - 1), 0)
    wsel = lambda i, te, nt: (layer, te[i], 0, 0)
    smem_rows = lambda f: pl.BlockSpec((1, 1, tm), f, memory_space=pltpu.SMEM)
    return pl.pallas_call(
        functools.partial(_gmm_body, tm=tm, parts=2),
        grid_spec=pltpu.PrefetchScalarGridSpec(
            num_scalar_prefetch=2,
            grid=(n_rows // tm,),
            in_specs=[smem_rows(lambda i, te, nt: (clamp(i, nt), 0, 0)),
                      smem_rows(lambda i, te, nt: (clamp(i + 1, nt), 0, 0)),
                      smem_rows(lambda i, te, nt: (clamp(i, nt), 0, 0)),
                      pl.BlockSpec(memory_space=pl.ANY),
                      pl.BlockSpec((None, None, D, F), wsel),
                      pl.BlockSpec((None, None, D, F), wsel),
                      pl.BlockSpec((None, None, F, D), wsel)],
            out_specs=pl.BlockSpec(memory_space=pl.ANY),
            scratch_shapes=[pltpu.VMEM((D, F), MXU_DTYPE), pltpu.VMEM((D, F), MXU_DTYPE),
                            pltpu.VMEM((F, D), MXU_DTYPE),
                            pltpu.VMEM((2, tm, W), I32), pltpu.VMEM((2, tm, W), I32),
                            pltpu.SemaphoreType.DMA((2,)), pltpu.SemaphoreType.DMA((2,))]),
        out_shape=jax.ShapeDtypeStruct((n_out, W), I32),
        compiler_params=pltpu.CompilerParams(dimension_semantics=("arbitrary",), vmem_limit_bytes=VMEM_LIMIT_BYTES,
                                             disable_bounds_checks=True),
        name="grouped_ffn",
    )(tile_expert, n_tiles, row_token, row_token, row_slot, x, w1, w3, w2)


def _close_moe_body(h_ref, *rest, group, alpha, n_k):
    yk_refs, (g8_ref, ws1_ref, ws3_ref, ws2_ref, x_ref, gate_ref, g_ref, b_ref, o_ref) = rest[:n_k], rest[n_k:]
    y = _ffn(_unpack_rows(h_ref[...]), ws1_ref[...], ws3_ref[...], ws2_ref[...])
    g8 = g8_ref[...]
    for k, yk_ref in enumerate(yk_refs):
        y = y + g8[:, k:k + 1] * _unpack_rows(yk_ref[...])
    r = _gated_residual(x_ref, y, gate_ref, group, alpha)
    o_ref[...] = _layer_norm_rows(r, g_ref[...], b_ref[...])


def close_moe(h, y_rows, g8, ws1, ws3, ws2, x, gate_g, ln_g, ln_b, group, alpha, tm):
    T, D = x.shape
    K = g8.shape[1]
    W = h.shape[1]
    row = lambda i: (i, 0)
    fixed = lambda i: (0, 0)
    return pl.pallas_call(
        functools.partial(_close_moe_body, group=group, alpha=alpha, n_k=K),
        grid=(T // tm,),
        in_specs=[pl.BlockSpec((tm, W), row)] +
                 [pl.BlockSpec((tm, W), lambda i, k=k: (k * (T // tm) + i, 0)) for k in range(K)] +
                 [pl.BlockSpec((tm, K), row),
                  pl.BlockSpec(ws1.shape, fixed),
                  pl.BlockSpec(ws3.shape, fixed),
                  pl.BlockSpec(ws2.shape, fixed),
                  pl.BlockSpec((tm, D), row),
                  pl.BlockSpec((1, tm // group, D), lambda i: (i, 0, 0)),
                  pl.BlockSpec((1, D), fixed),
                  pl.BlockSpec((1, D), fixed)],
        out_specs=pl.BlockSpec((tm, D), row),
        out_shape=jax.ShapeDtypeStruct((T, D), F32),
        compiler_params=_params(1),
        name="close_moe",
    )(h, *([y_rows] * K), g8, ws1, ws3, ws2, x, _per_tile(gate_g, tm, group), ln_g, ln_b)


def _rope_tables(pos, half):
    inv_freq = jnp.power(ROPE_THETA, -jnp.arange(half, dtype=F32) / half)
    ang = pos.astype(F32)[:, None] * inv_freq[None, :]
    cos, sin = jnp.cos(ang), jnp.sin(ang)
    reps = LANES // (2 * half)
    cos = jnp.tile(jnp.concatenate([cos, cos], axis=1), (1, reps))
    sin = jnp.tile(jnp.concatenate([-sin, sin], axis=1), (1, reps))
    return jnp.stack([jnp.ones_like(cos), cos]), jnp.stack([jnp.zeros_like(sin), sin])


def _place_cols(w, sizes, names, cols, width):
    out = jnp.zeros((w.shape[0], width), w.dtype)
    o = 0
    for size, name in zip(sizes, names):
        if name is not None:
            dst, sub = name
            out = lax.dynamic_update_slice(out, w[:, o:o + size], (0, cols[dst] + sub))
        o += size
    return out


def _moe(x, scale_g, shift_g, gate_g, ln_g, ln_b, w_router, r_bias, w1, w3, w2, ws1, ws3, ws2, layer, group, alpha):
    T, D = x.shape
    E = w1.shape[1]
    tm = 256
    t_route = _pick(T, 768, math.lcm(LANES, group))
    h, gate, rank, counts = route(x, scale_g, shift_g, w_router.T, r_bias.reshape(E, 1), group, t_route)

    counts = counts[:, 0]
    tiles_per = (counts + tm - 1) // tm
    tile_end = jnp.cumsum(tiles_per)
    seg_off = ((tile_end - tiles_per) * tm).astype(I32)
    n_tiles = (T * TOP_K) // tm + E
    used = tile_end[-1].astype(I32)
    tile_ids = jnp.minimum(jnp.arange(n_tiles, dtype=I32), used - 1)
    tile_expert = jnp.sum((tile_end[None, :] <= tile_ids[:, None]).astype(I32), axis=1)

    dest, g8 = compact_routes(gate, rank, seg_off.reshape(E, 1), t_route)
    n_pairs = TOP_K * T
    pair_slot = jnp.arange(n_pairs, dtype=I32).reshape(TOP_K, T)
    spare = n_pairs + jnp.arange(n_tiles * tm, dtype=I32) % tm
    row_slot = spare.at[dest.reshape(-1)].set(pair_slot.reshape(-1), unique_indices=True, mode="promise_in_bounds")
    row_token = jnp.where(row_slot < n_pairs, row_slot % T, 0)
    y_rows = grouped_ffn(tile_expert, used.reshape(1), row_token.reshape(n_tiles, 1, tm),
                         row_slot.reshape(n_tiles, 1, tm), h, w1, w3, w2, layer, n_pairs + tm, tm)
    return close_moe(h, y_rows, g8.T, _mx(ws1), _mx(ws3), _mx(ws2), x, gate_g, ln_g, ln_b, group, alpha,
                     _pick(T, 256, math.lcm(SUBLANES, group)))


def _fox_gates(grp, proj, b_f, past_logf):
    B, H = grp.batch, B_HEADS
    c0 = EVEN_COLS["misc"] + MISC_FL
    fl = proj[grp.row0:grp.row0 + B * grp.lq, c0:c0 + H].reshape(B, grp.lq, H)
    logf = jax.nn.log_sigmoid(fl + b_f)
    logf_all = logf if past_logf is None else jnp.concatenate([past_logf, logf], axis=1)
    cum = jnp.moveaxis(jnp.cumsum(logf_all, axis=1), 1, 2).reshape(B * H, grp.lk)
    fq = cum[:, grp.lp:, None]
    new = cum[:, grp.lp:].reshape(B * H, grp.lq // grp.tk, 1, grp.tk)
    new = jnp.pad(new, ((0, 0), (0, 0), (0, 0), (0, KEY_BLOCK - grp.tk)))
    if grp.lp:
        past = cum[:, :grp.lp].reshape(B * H, grp.n_past_blocks, 1, grp.tkp)
        past = jnp.pad(past, ((0, 0), (0, 0), (0, 0), (0, KEY_BLOCK - grp.tkp)))
        new = jnp.concatenate([past, new], axis=1)
    return logf, fq, new


def kernel(x_prompt, x_sample, c_prompt, c_sample, cache_a_k, cache_a_v, cache_a_kidx, cache_b_k, cache_b_v, cache_b_logf, cache_c_k, cache_c_v, cache_d_latent, cache_d_krope, w_in_even, b_forget, w_out_even, w_in_odd, c_lambda, c_subln, d_q_norm, d_w_uq, d_kv_norm, d_w_uk, d_w_uv, w_out_odd, ada_mix_w, ada_mix_b, ln_mix_g, ln_mix_b, ada_ffn_w, ada_ffn_b, ln_ffn_g, ln_ffn_b, router_w, router_bias, moe_w1, moe_w3, moe_w2, shared_w1, shared_w3, shared_w2):
    B, L, D = x_prompt.shape
    Bs, Ls, _ = x_sample.shape
    depth = ada_mix_w.shape[0]
    alpha = (2 * depth) ** 0.25
    past_len = cache_a_k.shape[2]
    Tp = B * L
    T = Tp + Bs * Ls
    grp_p = _Group(B, L, 0, 0)
    grp_s = _Group(Bs, Ls, past_len, Tp)
    group = math.gcd(L, Ls)
    assert group % (2 * SUBLANES) == 0
    tm_tok = _pick(T, 768, math.lcm(LANES, group))
    tm_close = _pick(T, 512, math.lcm(2 * SUBLANES, group))
    pos = jnp.concatenate([jnp.tile(jnp.arange(L), B), jnp.tile(past_len + jnp.arange(Ls), Bs)])
    c_act = jax.nn.silu(jnp.concatenate([c_prompt, c_sample], axis=0))

    def conditioning(ada_w, ada_b, layer):
        mod = matmul(c_act, ada_w, layer=layer) + ada_b[layer]
        per_group = jnp.concatenate([jnp.repeat(mod[:B], L // group, axis=0),
                                     jnp.repeat(mod[B:], Ls // group, axis=0)], axis=0)
        return jnp.split(per_group, 3, axis=-1)

    def flat_past(c):
        return c.reshape(c.shape[0], c.shape[1], -1)

    x = jnp.concatenate([x_prompt.reshape(Tp, D), x_sample.reshape(Bs * Ls, D)], axis=0)
    ev_p, ev_s, od_p, od_s = [], [], [], []
    for i in range(depth):
        j = i // 2
        shift_g, scale_g, gate_g = conditioning(ada_mix_w, ada_mix_b, i)
        if i % 2 == 0:
            sizes = (A_HEADS * HEAD_DIM, A_KV_HEADS * HEAD_DIM, A_KV_HEADS * HEAD_DIM, IDX_HEADS * IDX_DIM, IDX_DIM,
                     IDX_HEADS, B_HEADS * HEAD_DIM, B_HEADS * HEAD_DIM, B_HEADS * HEAD_DIM, B_HEADS)
            names = (("qa", 0), ("ka", 0), ("va", 0), ("qi", 0), ("ki", 0), ("misc", MISC_WI),
                     ("qf", 0), ("kf", 0), ("vf", 0), ("misc", MISC_FL))
            w_in = _mx(_place_cols(w_in_even[j], sizes, names, EVEN_COLS, EVEN_WIDTH))
            cos, sin = _rope_tables(pos, HEAD_DIM // 2)
            proj = project(x, scale_g, shift_g, w_in, cos, sin, EVEN_ROPE_RANGES, HEAD_DIM // 2, group, tm_tok)
            outs_a, outs_b = [], []
            for grp, store in ((grp_p, ev_p), (grp_s, ev_s)):
                if grp.lp:
                    logf, fq, fk = _fox_gates(grp, proj, b_forget[j], cache_b_logf[j])
                    outs_a.append(dsa_attention(grp, proj, cache_a_kidx[j], flat_past(cache_a_k[j]),
                                                flat_past(cache_a_v[j])))
                    outs_b.append(fox_attention_past(grp, proj, fq, fk, cache_b_k, cache_b_v, j))
                else:
                    logf, fq, fk = _fox_gates(grp, proj, b_forget[j], None)
                    outs_a.append(dsa_attention(grp, proj, None, None, None))
                    outs_b.append(fox_attention(grp, proj, fq, fk))
                r0, r1 = grp.row0, grp.row0 + grp.batch * grp.lq
                cut = lambda name, heads, d: proj[r0:r1, EVEN_COLS[name]:EVEN_COLS[name] + heads * d].reshape(
                    (grp.batch, grp.lq) + ((heads, d) if heads > 1 else (d,)))
                store.append((cut("ka", A_KV_HEADS, HEAD_DIM), cut("va", A_KV_HEADS, HEAD_DIM), cut("ki", 1, IDX_DIM),
                              cut("kf", B_HEADS, HEAD_DIM), cut("vf", B_HEADS, HEAD_DIM), logf))
            a1, a2 = jnp.concatenate(outs_a, axis=0), jnp.concatenate(outs_b, axis=0)
            n1 = A_HEADS * HEAD_DIM
            w1o, w2o = _mx(w_out_even[j][:n1]), _mx(w_out_even[j][n1:])
        else:
            lam_init = 0.8 - 0.6 * math.exp(-0.3 * i)
            lam_f = c_lambda[j]
            lam = (jnp.exp(jnp.sum(lam_f[0] * lam_f[1])) - jnp.exp(jnp.sum(lam_f[2] * lam_f[3])) + lam_init).reshape(1)
            q_lora, R = d_q_norm.shape[1], d_kv_norm.shape[1]
            sizes = (C_HEADS * 2 * C_QK_DIM, C_HEADS * 2 * C_QK_DIM, C_HEADS * C_V_DIM, q_lora, R, D_ROPE)
            names = (("qc", 0), ("kc", 0), ("vc", 0), ("qd", 0), ("ckv", 0), ("kr", 0))
            w_in = _mx(_place_cols(w_in_odd[j], sizes, names, ODD_COLS, ODD_WIDTH))
            cos, sin = _rope_tables(pos, C_QK_DIM // 2)
            proj = project(x, scale_g, shift_g, w_in, cos, sin, ODD_ROPE_RANGES, C_QK_DIM // 2, group, tm_tok)

            w_uq = d_w_uq[j].reshape(q_lora, D_HEADS, D_NOPE + D_ROPE)
            w_abs = bmm_precise(jnp.moveaxis(w_uq[:, :, :D_NOPE], 1, 0), jnp.transpose(d_w_uk[j], (1, 2, 0)))
            w_abs = jnp.moveaxis(w_abs, 0, 1).reshape(q_lora, D_HEADS * R)
            w_rope = jnp.pad(w_uq[:, :, D_NOPE:], ((0, 0), (0, 0), (0, LANES - D_ROPE))).reshape(q_lora, D_HEADS * LANES)
            q_abs, q_rope, lat = mla_prepare(proj, d_q_norm[j][None], d_kv_norm[j][None],
                                             _mx(jnp.concatenate([w_abs, w_rope], axis=1)), cos, sin, tm_close)
            n1 = C_HEADS * C_V_DIM
            w_od = w_out_odd[j][n1:].reshape(D_HEADS, D_V, D)
            w_lat_out = bmm_precise(jnp.transpose(d_w_uv[j], (1, 0, 2)), w_od).reshape(D_HEADS * R, D)

            outs_a, outs_b = [], []
            for grp, store in ((grp_p, od_p), (grp_s, od_s)):
                if grp.lp:
                    outs_a.append(diff_attention_past(grp, proj, lam, c_subln[j][None], 1.0 - lam_init,
                                                      flat_past(cache_c_k[j]), cache_c_v, j))
                    outs_b.append(mla_attention_past(grp, q_abs, q_rope, lat, proj, cache_d_latent, cache_d_krope, j))
                else:
                    outs_a.append(diff_attention(grp, proj, lam, c_subln[j][None], 1.0 - lam_init))
                    outs_b.append(mla_attention(grp, q_abs, q_rope, lat, proj))
                r0, r1 = grp.row0, grp.row0 + grp.batch * grp.lq
                shp = (grp.batch, grp.lq)
                oc = ODD_COLS
                store.append((proj[r0:r1, oc["kc"]:oc["kc"] + C_HEADS * 2 * C_QK_DIM].reshape(shp + (C_HEADS, 2, C_QK_DIM)),
                              proj[r0:r1, oc["vc"]:oc["vc"] + n1].reshape(shp + (C_HEADS, C_V_DIM)),
                              lat[r0:r1].reshape(shp + (R,)),
                              proj[r0:r1, oc["kr"]:oc["kr"] + D_ROPE].reshape(shp + (D_ROPE,))))
            a1, a2 = jnp.concatenate(outs_a, axis=0), jnp.concatenate(outs_b, axis=0)
            w1o, w2o = _mx(w_out_odd[j][:n1]), _mx(w_lat_out)
        x = close_mixer(a1, a2, w1o, w2o, x, gate_g, ln_mix_g[i][None], ln_mix_b[i][None], group, alpha, tm_close)

        shift_g, scale_g, gate_g = conditioning(ada_ffn_w, ada_ffn_b, i)
        x = _moe(x, scale_g, shift_g, gate_g, ln_ffn_g[i][None], ln_ffn_b[i][None], router_w[i], router_bias[i],
                 moe_w1, moe_w3, moe_w2, shared_w1[i], shared_w3[i], shared_w2[i], i, group, alpha)

    stack = lambda rows, idx: jnp.stack([r[idx] for r in rows])
    outs = [x[:Tp].reshape(B, L, D), x[Tp:].reshape(Bs, Ls, D)]
    for idx in range(6):
        outs += [stack(ev_p, idx), stack(ev_s, idx)]
    for idx in range(4):
        outs += [stack(od_p, idx), stack(od_s, idx)]
    return tuple(outs)
```

```python
import functools
import math

import jax
import jax.numpy as jnp
from jax import lax
from jax.experimental import pallas as pl
from jax.experimental.pallas import tpu as pltpu

CHUNK = 64
ROPE_THETA = 10000.0
HEAD_DIM = 128
A_HEADS = 8
A_KV_HEADS = 2
IDX_HEADS = 16
IDX_DIM = 128
A_TOPK_MAX = 256
B_HEADS = 8
C_HEADS = 8
C_QK_DIM = 64
C_V_DIM = 128
D_HEADS = 8
D_NOPE = 128
D_ROPE = 64
D_V = 128
N_GROUPS = 8
TOPK_GROUPS = 4
TOP_K = 8
ROUTED_SCALE = 2.5
NORM_EPS = 1e-6

LANES = 128
SUBLANES = 8
MXU_WIDTH = 256
VMEM_LIMIT_BYTES = 56 * 2**20
MXU_DTYPE = jnp.bfloat16

F32 = jnp.float32
I32 = jnp.int32
NEG = -1e30
INT_MIN = -2**31
KEY_BLOCK = 512
QUERY_BLOCK = 512
HEADS_PER_STEP = 2
GMM_SPLIT = 4

EVEN_COLS = dict(qi=0, qa=2048, qf=3072, kf=4096, vf=5120, ka=6144, va=6400, ki=6656, misc=6912)
EVEN_WIDTH = 7168
EVEN_ROPE_RANGES = ((0, 3072), (6144, 6400), (6656, 6912))
MISC_WI, MISC_FL = 0, IDX_HEADS
ODD_COLS = dict(qc=0, kc=1024, vc=2048, qd=3072, ckv=3584, kr=3840)
ODD_WIDTH = 4096
ODD_ROPE_RANGES = ((0, 2048), (3840, 4096))


def _pick(n, target, mult):
    best = None
    for d in range(mult, min(n, target) + 1, mult):
        if n % d == 0:
            best = d
    return n if best is None else best


def _params(n_axes):
    return pltpu.CompilerParams(dimension_semantics=("arbitrary",) * n_axes,
                                vmem_limit_bytes=VMEM_LIMIT_BYTES)


def _mx(a):
    return a.astype(MXU_DTYPE)


def _dot(a, b):
    return jnp.dot(_mx(a), _mx(b), preferred_element_type=F32)


def _dot_nt(a, b, precision=None):
    if precision is None:
        a, b = _mx(a), _mx(b)
    return lax.dot_general(a, b, (((1,), (1,)), ((), ())), preferred_element_type=F32, precision=precision)


def _mm_body(a_ref, b_ref, o_ref, *, precise):
    if precise:
        o = jnp.dot(a_ref[...], b_ref[...], preferred_element_type=F32, precision=lax.Precision.HIGHEST)
    else:
        o = _dot(a_ref[...], b_ref[...])
    o_ref[...] = o.astype(o_ref.dtype)


def matmul(a, b, out_dtype=F32, precise=False, tm_target=1024, tn_target=512, layer=None):
    M, K = a.shape
    N = b.shape[-1]
    tm = _pick(M, tm_target, SUBLANES)
    tn = _pick(N, tn_target, LANES)
    if layer is None:
        b_spec = pl.BlockSpec((K, tn), lambda i, j: (0, j))
    else:
        b_spec = pl.BlockSpec((None, K, tn), lambda i, j: (layer, 0, j))
    return pl.pallas_call(
        functools.partial(_mm_body, precise=precise),
        grid=(M // tm, N // tn),
        in_specs=[pl.BlockSpec((tm, K), lambda i, j: (i, 0)), b_spec],
        out_specs=pl.BlockSpec((tm, tn), lambda i, j: (i, j)),
        out_shape=jax.ShapeDtypeStruct((M, N), out_dtype),
        compiler_params=_params(2),
        name="matmul",
    )(a, b)


def _bmm_body(a_ref, b_ref, o_ref):
    o_ref[0] = jnp.dot(a_ref[0], b_ref[0], preferred_element_type=F32,
                       precision=lax.Precision.HIGHEST).astype(o_ref.dtype)


def bmm_precise(a, b):
    H, M, K = a.shape
    N = b.shape[2]
    return pl.pallas_call(
        _bmm_body,
        grid=(H,),
        in_specs=[pl.BlockSpec((1, M, K), lambda h: (h, 0, 0)),
                  pl.BlockSpec((1, K, N), lambda h: (h, 0, 0))],
        out_specs=pl.BlockSpec((1, M, N), lambda h: (h, 0, 0)),
        out_shape=jax.ShapeDtypeStruct((H, M, N), F32),
        compiler_params=_params(1),
        name="bmm_precise",
    )(a, b)


def _per_tile(table, tm, group):
    return table.reshape(table.shape[0] * group // tm, tm // group, table.shape[1])


def _modulated(x_ref, sc_ref, sh_ref, group):
    parts = []
    for g in range(x_ref.shape[0] // group):
        rows = slice(g * group, (g + 1) * group)
        parts.append(x_ref[rows, :] * (1.0 + sc_ref[0, g:g + 1, :]) + sh_ref[0, g:g + 1, :])
    return parts


def _pack_rows(x):
    bits = pltpu.bitcast(x, I32)
    rounded = bits + 0x7FFF + ((bits >> 16) & 1)
    half = x.shape[1] // 2
    return ((rounded[:, :half] >> 16) & 0xFFFF) | (rounded[:, half:] & jnp.int32(-0x10000))


def _unpack_rows(w):
    lo = pltpu.bitcast(jnp.left_shift(w, 16), F32)
    hi = pltpu.bitcast(w & jnp.int32(-0x10000), F32)
    return jnp.concatenate([lo, hi], axis=1)


def _rope_lanes(x, cos, sin, half):
    if 2 * half == LANES:
        partner = pltpu.roll(x, half, axis=1)
    else:
        lane = lax.broadcasted_iota(I32, x.shape, 1)
        lower = (lane % (2 * half)) < half
        partner = jnp.where(lower, pltpu.roll(x, LANES - half, axis=1), pltpu.roll(x, half, axis=1))
    return x * cos + partner * sin


def _layer_norm_rows(r, g, b):
    rc = r - jnp.mean(r, axis=-1, keepdims=True)
    var = jnp.mean(rc * rc, axis=-1, keepdims=True)
    return rc * lax.rsqrt(var + NORM_EPS) * g + b


def _gated_residual(x_ref, y, gate_ref, group, alpha):
    parts = []
    for g in range(x_ref.shape[0] // group):
        rows = slice(g * group, (g + 1) * group)
        parts.append(alpha * x_ref[rows, :] + (1.0 + gate_ref[0, g:g + 1, :]) * y[rows, :])
    return jnp.concatenate(parts, axis=0)


def _proj_body(kind_ref, x_ref, sc_ref, sh_ref, w_ref, *rest, group, half, n_sub):
    tables, (o_ref, h_ref) = rest[:2 * n_sub], rest[2 * n_sub:]
    j = pl.program_id(1)

    @pl.when(j == 0)
    def _():
        for g, part in enumerate(_modulated(x_ref, sc_ref, sh_ref, group)):
            h_ref[g * group:(g + 1) * group, :] = part.astype(h_ref.dtype)

    for c in range(n_sub):
        cos, sin = tables[2 * c][...], tables[2 * c + 1][...]
        acc = jnp.dot(h_ref[...], w_ref[:, c * MXU_WIDTH:(c + 1) * MXU_WIDTH], preferred_element_type=F32)
        for p in range(MXU_WIDTH // LANES):
            o_ref[:, c * MXU_WIDTH + p * LANES:c * MXU_WIDTH + (p + 1) * LANES] = _rope_lanes(
                acc[:, p * LANES:(p + 1) * LANES], cos, sin, half)


def project(x, scale_g, shift_g, w, cos, sin, rope_ranges, half, group, tm):
    T, D = x.shape
    N = w.shape[1]
    n_sub = 2
    tn = n_sub * MXU_WIDTH
    start = jnp.arange(N // MXU_WIDTH) * MXU_WIDTH
    kinds = sum(((start >= lo) & (start < hi)).astype(I32) for lo, hi in rope_ranges)
    row = lambda i, j, k: (i, 0)
    tile = lambda i, j, k: (i, 0, 0)
    table_specs, table_args = [], []
    for c in range(n_sub):
        pick = lambda i, j, k, c=c: (k[j * n_sub + c], i, 0)
        table_specs += [pl.BlockSpec((None, tm, LANES), pick)] * 2
        table_args += [cos, sin]
    return pl.pallas_call(
        functools.partial(_proj_body, group=group, half=half, n_sub=n_sub),
        grid_spec=pltpu.PrefetchScalarGridSpec(
            num_scalar_prefetch=1,
            grid=(T // tm, N // tn),
            in_specs=[pl.BlockSpec((tm, D), row),
                      pl.BlockSpec((1, tm // group, D), tile),
                      pl.BlockSpec((1, tm // group, D), tile),
                      pl.BlockSpec((D, tn), lambda i, j, k: (0, j))] + table_specs,
            out_specs=pl.BlockSpec((tm, tn), lambda i, j, k: (i, j)),
            scratch_shapes=[pltpu.VMEM((tm, D), MXU_DTYPE)]),
        out_shape=jax.ShapeDtypeStruct((T, N), F32),
        compiler_params=_params(2),
        name="project",
    )(kinds, x, _per_tile(scale_g, tm, group), _per_tile(shift_g, tm, group), w, *table_args)


def _close_mix_body(a1_ref, a2_ref, w1_ref, w2_ref, x_ref, gate_ref, g_ref, b_ref, o_ref, *, group, alpha):
    y = _dot(a1_ref[...], w1_ref[...]) + _dot(a2_ref[...], w2_ref[...])
    r = _gated_residual(x_ref, y, gate_ref, group, alpha)
    o_ref[...] = _layer_norm_rows(r, g_ref[...], b_ref[...])


def close_mixer(a1, a2, w1, w2, x, gate_g, ln_g, ln_b, group, alpha, tm):
    T, D = x.shape
    row = lambda i: (i, 0)
    fixed = lambda i: (0, 0)
    return pl.pallas_call(
        functools.partial(_close_mix_body, group=group, alpha=alpha),
        grid=(T // tm,),
        in_specs=[pl.BlockSpec((tm, a1.shape[1]), row),
                  pl.BlockSpec((tm, a2.shape[1]), row),
                  pl.BlockSpec(w1.shape, fixed),
                  pl.BlockSpec(w2.shape, fixed),
                  pl.BlockSpec((tm, D), row),
                  pl.BlockSpec((1, tm // group, D), lambda i: (i, 0, 0)),
                  pl.BlockSpec((1, D), fixed),
                  pl.BlockSpec((1, D), fixed)],
        out_specs=pl.BlockSpec((tm, D), row),
        out_shape=jax.ShapeDtypeStruct((T, D), F32),
        compiler_params=_params(1),
        name="close_mixer",
    )(a1, a2, w1, w2, x, _per_tile(gate_g, tm, group), ln_g, ln_b)


def _online_step(s, v, carry):
    m, l, acc = carry
    m_new = jnp.maximum(m, jnp.max(s, axis=-1, keepdims=True))
    p = jnp.exp(s - m_new)
    alpha = jnp.exp(m - m_new)
    l = alpha * l + jnp.sum(p, axis=-1, keepdims=True)
    acc = alpha * acc + _dot(p, v)
    return m_new, l, acc


def _init_carry(tq, dv):
    return (jnp.full((tq, 1), NEG, F32), jnp.zeros((tq, 1), F32), jnp.zeros((tq, dv), F32))


def _chunk_last_key(i, tq):
    return ((i + 1) * tq - 1) // CHUNK * CHUNK + CHUNK - 1


class _Group:
    def __init__(self, batch, lq, lp, row0):
        self.batch, self.lq, self.lp, self.row0 = batch, lq, lp, row0
        self.lk = lp + lq
        self.tq = _pick(lq, QUERY_BLOCK, SUBLANES)
        self.tq_dsa = _pick(lq, 128, SUBLANES)
        self.tk = _pick(lq, KEY_BLOCK, SUBLANES)
        self.tkp = _pick(lp, KEY_BLOCK, LANES) if lp else 0
        self.n_past_blocks = lp // self.tkp if lp else 0
        self.topk = min(A_TOPK_MAX, self.lk // 4)
        assert lp % CHUNK == 0 and row0 % lq == 0 and lq % self.tk == 0

    def qrow(self, tq):
        base, per = self.row0 // tq, self.lq // tq
        return lambda b, i: base + b * per + i

    def krow(self):
        base = self.row0 // self.lq
        return lambda b: base + b


def _new_key_blocks(i, tq, grp_lq, tk, causal_last):
    return jnp.minimum(causal_last, grp_lq - 1) // tk + 1


def _fox_body(q_ref, k_ref, v_ref, fq_ref, fk_ref, o_ref, *, tq, tk, lq, hp):
    i = pl.program_id(2)
    d = HEAD_DIM
    head = lambda x, u: x[:, u * d:(u + 1) * d]
    q = _mx(q_ref[...] * d ** -0.5)
    carry = tuple(_init_carry(tq, d) for _ in range(hp))
    qpos = i * tq + lax.broadcasted_iota(I32, (tq, tk), 0)

    def new_step(j, c):
        off = pl.multiple_of(j * tk, tk)
        k, v = k_ref[pl.ds(off, tk), :], v_ref[pl.ds(off, tk), :]
        causal = off + lax.broadcasted_iota(I32, (tq, tk), 1) <= qpos
        out = []
        for u in range(hp):
            s = _dot_nt(head(q, u), head(k, u)) + fq_ref[u] - fk_ref[u, j][:, :tk]
            out.append(_online_step(jnp.where(causal, s, NEG), head(v, u), c[u]))
        return tuple(out)

    nb = _new_key_blocks(i, tq, lq, tk, (i + 1) * tq - 1)
    carry = lax.fori_loop(0, nb, new_step, carry)
    for u, (m, l, acc) in enumerate(carry):
        o_ref[:, u * d:(u + 1) * d] = (acc / l).astype(o_ref.dtype)


def fox_attention(grp, proj, fq, fk):
    assert grp.lp == 0
    B, H, hp = grp.batch, B_HEADS, HEADS_PER_STEP
    w = hp * HEAD_DIM
    tq = grp.tq
    qrow, krow = grp.qrow(tq), grp.krow()
    cq, ck, cv = (EVEN_COLS[n] // w for n in ("qf", "kf", "vf"))
    in_specs = [pl.BlockSpec((tq, w), lambda b, h, i: (qrow(b, i), cq + h)),
                pl.BlockSpec((grp.lq, w), lambda b, h, i: (krow(b), ck + h)),
                pl.BlockSpec((grp.lq, w), lambda b, h, i: (krow(b), cv + h))]
    in_specs += [pl.BlockSpec((hp, tq, 1), lambda b, h, i: (b * (H // hp) + h, i, 0)),
                 pl.BlockSpec((hp,) + fk.shape[1:], lambda b, h, i: (b * (H // hp) + h, 0, 0, 0))]
    return pl.pallas_call(
        functools.partial(_fox_body, tq=tq, tk=grp.tk, lq=grp.lq, hp=hp),
        grid=(B, H // hp, grp.lq // tq),
        in_specs=in_specs,
        out_specs=pl.BlockSpec((tq, w), lambda b, h, i: (b * (grp.lq // tq) + i, h)),
        out_shape=jax.ShapeDtypeStruct((B * grp.lq, H * HEAD_DIM), MXU_DTYPE),
        compiler_params=_params(3),
        name="fox_attention",
    )(proj, proj, proj, fq, fk)


def _diff_body(lam_ref, q_ref, k_ref, v_ref, g_ref, o_ref, *, tq, tk, lq, hp, out_scale):
    i = pl.program_id(2)
    d = C_V_DIM
    head = lambda x, u: x[:, u * d:(u + 1) * d]
    q = q_ref[...] * C_QK_DIM ** -0.5
    lane = lax.broadcasted_iota(I32, q.shape, 1) % (2 * C_QK_DIM)
    qs = (_mx(jnp.where(lane < C_QK_DIM, q, 0.0)), _mx(jnp.where(lane >= C_QK_DIM, q, 0.0)))
    carry = tuple(_init_carry(tq, d) for _ in range(2 * hp))

    def step(k, v, ok, c):
        out = []
        for u in range(hp):
            ku, vu = _mx(head(k, u)), _mx(head(v, u))
            for t in range(2):
                s = _dot_nt(head(qs[t], u), ku)
                out.append(_online_step(jnp.where(ok, s, NEG), vu, c[2 * u + t]))
        return tuple(out)

    qchunk = (i * tq + lax.broadcasted_iota(I32, (tq, tk), 0)) // CHUNK

    def new_step(j, c):
        off = pl.multiple_of(j * tk, tk)
        ok = (off + lax.broadcasted_iota(I32, (tq, tk), 1)) // CHUNK <= qchunk
        return step(k_ref[pl.ds(off, tk), :], v_ref[pl.ds(off, tk), :], ok, c)

    nb = _new_key_blocks(i, tq, lq, tk, _chunk_last_key(i, tq))
    carry = lax.fori_loop(0, nb, new_step, carry)
    for u in range(hp):
        (_, l0, a0), (_, l1, a1) = carry[2 * u], carry[2 * u + 1]
        o = a0 / l0 - lam_ref[0] * (a1 / l1)
        o = o * lax.rsqrt(jnp.mean(o * o, axis=-1, keepdims=True) + NORM_EPS)
        o_ref[:, u * d:(u + 1) * d] = (o * g_ref[...] * out_scale).astype(o_ref.dtype)


def diff_attention(grp, proj, lam, subln, out_scale):
    assert grp.lp == 0
    B, H, hp = grp.batch, C_HEADS, HEADS_PER_STEP
    w = hp * C_V_DIM
    tq = grp.tq
    qrow, krow = grp.qrow(tq), grp.krow()
    cq, ck, cv = (ODD_COLS[n] // w for n in ("qc", "kc", "vc"))
    in_specs = [pl.BlockSpec(memory_space=pltpu.SMEM),
                pl.BlockSpec((tq, w), lambda b, h, i: (qrow(b, i), cq + h)),
                pl.BlockSpec((grp.lq, w), lambda b, h, i: (krow(b), ck + h)),
                pl.BlockSpec((grp.lq, w), lambda b, h, i: (krow(b), cv + h))]
    in_specs += [pl.BlockSpec((1, C_V_DIM), lambda b, h, i: (0, 0))]
    return pl.pallas_call(
        functools.partial(_diff_body, tq=tq, tk=grp.tk, lq=grp.lq, hp=hp, out_scale=out_scale),
        grid=(B, H // hp, grp.lq // tq),
        in_specs=in_specs,
        out_specs=pl.BlockSpec((tq, w), lambda b, h, i: (b * (grp.lq // tq) + i, h)),
        out_shape=jax.ShapeDtypeStruct((B * grp.lq, H * C_V_DIM), MXU_DTYPE),
        compiler_params=_params(3),
        name="diff_attention",
    )(lam, proj, proj, proj, subln)


def _mla_prep_body(qd_ref, ckv_ref, qg_ref, kg_ref, w_ref, cos_ref, sin_ref, qa_ref, qr_ref, lat_ref, *, n_abs):
    qd = qd_ref[...]
    qn = qd * lax.rsqrt(jnp.mean(qd * qd, axis=-1, keepdims=True) + NORM_EPS) * qg_ref[...]
    q = _dot(qn, w_ref[...]) * (D_NOPE + D_ROPE) ** -0.5
    qa_ref[...] = q[:, :n_abs].astype(qa_ref.dtype)
    for h in range((q.shape[1] - n_abs) // LANES):
        lanes = slice(n_abs + h * LANES, n_abs + (h + 1) * LANES)
        qr_ref[:, h * LANES:(h + 1) * LANES] = _rope_lanes(
            q[:, lanes], cos_ref[...], sin_ref[...], D_ROPE // 2).astype(qr_ref.dtype)
    ckv = ckv_ref[...]
    lat_ref[...] = ckv * lax.rsqrt(jnp.mean(ckv * ckv, axis=-1, keepdims=True) + NORM_EPS) * kg_ref[...]


def mla_prepare(proj, q_norm, kv_norm, w_q, cos, sin, tm):
    T = proj.shape[0]
    q_lora, R = q_norm.shape[1], kv_norm.shape[1]
    n_abs = D_HEADS * R
    n_rope = D_HEADS * LANES
    row = lambda i: (i, 0)
    fixed = lambda i: (0, 0)
    return pl.pallas_call(
        functools.partial(_mla_prep_body, n_abs=n_abs),
        grid=(T // tm,),
        in_specs=[pl.BlockSpec((tm, q_lora), lambda i: (i, ODD_COLS["qd"] // q_lora)),
                  pl.BlockSpec((tm, R), lambda i: (i, ODD_COLS["ckv"] // R)),
                  pl.BlockSpec((1, q_lora), fixed),
                  pl.BlockSpec((1, R), fixed),
                  pl.BlockSpec(w_q.shape, fixed),
                  pl.BlockSpec((None, tm, LANES), lambda i: (1, i, 0)),
                  pl.BlockSpec((None, tm, LANES), lambda i: (1, i, 0))],
        out_specs=[pl.BlockSpec((tm, n_abs), row), pl.BlockSpec((tm, n_rope), row), pl.BlockSpec((tm, R), row)],
        out_shape=[jax.ShapeDtypeStruct((T, n_abs), MXU_DTYPE), jax.ShapeDtypeStruct((T, n_rope), MXU_DTYPE),
                   jax.ShapeDtypeStruct((T, R), F32)],
        compiler_params=_params(1),
        name="mla_prepare",
    )(proj, proj, q_norm, kv_norm, w_q, cos, sin)


def _mla_body(qa_ref, qr_ref, lat_ref, kr_ref, o_ref, *, tq, tk, lq, hp):
    i = pl.program_id(2)
    R = lat_ref.shape[-1]
    qa = [qa_ref[:, u * R:(u + 1) * R] for u in range(hp)]
    qr = [qr_ref[:, u * LANES:(u + 1) * LANES] for u in range(hp)]
    carry = tuple(_init_carry(tq, R) for _ in range(hp))

    qchunk = (i * tq + lax.broadcasted_iota(I32, (tq, tk), 0)) // CHUNK

    def new_step(j, c):
        off = pl.multiple_of(j * tk, tk)
        lat = _mx(lat_ref[pl.ds(off, tk), :])
        kr = _mx(kr_ref[pl.ds(off, tk), :])
        ok = (off + lax.broadcasted_iota(I32, (tq, tk), 1)) // CHUNK <= qchunk
        return tuple(_online_step(jnp.where(ok, _dot_nt(qa[u], lat) + _dot_nt(qr[u], kr), NEG), lat, c[u])
                     for u in range(hp))

    nb = _new_key_blocks(i, tq, lq, tk, _chunk_last_key(i, tq))
    carry = lax.fori_loop(0, nb, new_step, carry)
    for u, (m, l, acc) in enumerate(carry):
        o_ref[:, u * R:(u + 1) * R] = (acc / l).astype(o_ref.dtype)


def mla_attention(grp, q_abs, q_rope, lat, proj):
    assert grp.lp == 0
    B, H, hp = grp.batch, D_HEADS, HEADS_PER_STEP
    R = lat.shape[1]
    tq = grp.tq
    qrow, krow = grp.qrow(tq), grp.krow()
    in_specs = [pl.BlockSpec((tq, hp * R), lambda b, h, i: (qrow(b, i), h)),
                pl.BlockSpec((tq, hp * LANES), lambda b, h, i: (qrow(b, i), h)),
                pl.BlockSpec((grp.lq, R), lambda b, h, i: (krow(b), 0)),
                pl.BlockSpec((grp.lq, LANES), lambda b, h, i: (krow(b), ODD_COLS["kr"] // LANES))]
    return pl.pallas_call(
        functools.partial(_mla_body, tq=tq, tk=grp.tk, lq=grp.lq, hp=hp),
        grid=(B, H // hp, grp.lq // tq),
        in_specs=in_specs,
        out_specs=pl.BlockSpec((tq, hp * R), lambda b, h, i: (b * (grp.lq // tq) + i, h)),
        out_shape=jax.ShapeDtypeStruct((B * grp.lq, H * R), MXU_DTYPE),
        compiler_params=_params(3),
        name="mla_attention",
    )(q_abs, q_rope, lat, proj)


def _head_cols(ref, h, d, heads):
    if ref.shape[-1] == d:
        return ref[pl.ds(h, ref.shape[0] // heads, stride=heads), :]
    return ref[:, h * d:(h + 1) * d]


def _chain_reset(j, m_ref, l_ref, acc_ref):
    @pl.when(j == 0)
    def _():
        m_ref[...] = jnp.full(m_ref.shape, NEG, F32)
        l_ref[...] = jnp.zeros(l_ref.shape, F32)
        acc_ref[...] = jnp.zeros(acc_ref.shape, F32)


def _chains_advance(scores_values, m_ref, l_ref, acc_ref):
    states = [(m_ref[c], l_ref[c], acc_ref[c]) for c in range(len(scores_values))]
    new = [_online_step(s, v, st) for (s, v), st in zip(scores_values, states)]
    for c, (m, l, acc) in enumerate(new):
        m_ref[c], l_ref[c], acc_ref[c] = m, l, acc
    return [(l, acc) for _, l, acc in new]


def _past_key_spec(cache, layer, tkp, n_past, rows_per_key=1):
    blk = lambda j: jnp.minimum(j, n_past - 1)
    rows, w = tkp * rows_per_key, cache.shape[-1]
    if layer is None:
        return pl.BlockSpec((None, rows, w), lambda b, j: (b, blk(j), 0))
    return pl.BlockSpec((None, None, rows, w), lambda b, j: (layer, b, blk(j), 0))


def _keys_by_head(cache):
    n, b, p, h, d = cache.shape
    return cache.reshape(n, b, p * h, d)


def _past_call(body, grp, heads, dv, chains, in_specs, args, name):
    lq = grp.lq
    assert grp.lq == grp.tk
    return pl.pallas_call(
        body,
        grid=(grp.batch, grp.n_past_blocks + 1),
        in_specs=in_specs,
        out_specs=pl.BlockSpec((lq, heads * dv), lambda b, j: (b, 0)),
        out_shape=jax.ShapeDtypeStruct((grp.batch * lq, heads * dv), MXU_DTYPE),
        scratch_shapes=[pltpu.VMEM((chains, lq, 1), F32), pltpu.VMEM((chains, lq, 1), F32),
                        pltpu.VMEM((chains, lq, dv), F32)],
        compiler_params=_params(2),
        name=name,
    )(*args)


def _fox_past_body(q_ref, k_ref, v_ref, pk_ref, pv_ref, fq_ref, fk_ref, o_ref, m_ref, l_ref, acc_ref, *, n_past):
    j = pl.program_id(1)
    d, lq = HEAD_DIM, q_ref.shape[0]
    _chain_reset(j, m_ref, l_ref, acc_ref)
    q = lambda h: _mx(q_ref[:, h * d:(h + 1) * d] * d ** -0.5)

    @pl.when(j < n_past)
    def _():
        _chains_advance([(_dot_nt(q(h), _head_cols(pk_ref, h, d, B_HEADS)) + fq_ref[h] - fk_ref[h, 0],
                          _head_cols(pv_ref, h, d, B_HEADS)) for h in range(B_HEADS)], m_ref, l_ref, acc_ref)

    @pl.when(j == n_past)
    def _():
        causal = lax.broadcasted_iota(I32, (lq, lq), 1) <= lax.broadcasted_iota(I32, (lq, lq), 0)
        logits = lambda h: _dot_nt(q(h), k_ref[:, h * d:(h + 1) * d]) + fq_ref[h] - fk_ref[h, 0][:, :lq]
        done = _chains_advance([(jnp.where(causal, logits(h), NEG), v_ref[:, h * d:(h + 1) * d])
                                for h in range(B_HEADS)], m_ref, l_ref, acc_ref)
        for h, (l, acc) in enumerate(done):
            o_ref[:, h * d:(h + 1) * d] = (acc / l).astype(o_ref.dtype)


def fox_attention_past(grp, proj, fq, fk, cache_k, cache_v, layer):
    H, d, lq = B_HEADS, HEAD_DIM, grp.lq
    row = grp.krow()
    w = H * d
    in_specs = [pl.BlockSpec((lq, w), lambda b, j, n=n: (row(b), EVEN_COLS[n] // w)) for n in ("qf", "kf", "vf")]
    cache_k, cache_v = _keys_by_head(cache_k), _keys_by_head(cache_v)
    in_specs += [_past_key_spec(cache_k, layer, grp.tkp, grp.n_past_blocks, H),
                 _past_key_spec(cache_v, layer, grp.tkp, grp.n_past_blocks, H),
                 pl.BlockSpec((H, lq, 1), lambda b, j: (b, 0, 0)),
                 pl.BlockSpec((H, 1, 1, fk.shape[-1]), lambda b, j: (b, j, 0, 0))]
    return _past_call(functools.partial(_fox_past_body, n_past=grp.n_past_blocks), grp, H, d, H, in_specs,
                      [proj, proj, proj, cache_k, cache_v, fq, fk], "fox_attention_past")


def _diff_past_body(lam_ref, q_ref, k_ref, v_ref, pk_ref, pv_ref, g_ref, o_ref, m_ref, l_ref, acc_ref,
                    *, n_past, out_scale):
    j = pl.program_id(1)
    d = C_V_DIM
    _chain_reset(j, m_ref, l_ref, acc_ref)
    lane = lax.broadcasted_iota(I32, (q_ref.shape[0], d), 1)

    def maps(h):
        q = q_ref[:, h * d:(h + 1) * d] * C_QK_DIM ** -0.5
        return _mx(jnp.where(lane < C_QK_DIM, q, 0.0)), _mx(jnp.where(lane >= C_QK_DIM, q, 0.0))

    def both_maps(key_of, value_of):
        items = []
        for h in range(C_HEADS):
            k, v = _mx(key_of(h)), _mx(value_of(h))
            items += [(_dot_nt(qt, k), v) for qt in maps(h)]
        return _chains_advance(items, m_ref, l_ref, acc_ref)

    @pl.when(j < n_past)
    def _():
        both_maps(lambda h: _head_cols(pk_ref, h, d, C_HEADS), lambda h: _head_cols(pv_ref, h, d, C_HEADS))

    @pl.when(j == n_past)
    def _():
        done = both_maps(lambda h: k_ref[:, h * d:(h + 1) * d], lambda h: v_ref[:, h * d:(h + 1) * d])
        for h in range(C_HEADS):
            (l0, a0), (l1, a1) = done[2 * h], done[2 * h + 1]
            o = a0 / l0 - lam_ref[0] * (a1 / l1)
            o = o * lax.rsqrt(jnp.mean(o * o, axis=-1, keepdims=True) + NORM_EPS)
            o_ref[:, h * d:(h + 1) * d] = (o * g_ref[...] * out_scale).astype(o_ref.dtype)


def diff_attention_past(grp, proj, lam, subln, out_scale, cache_k, cache_v, layer):
    H, d, lq = C_HEADS, C_V_DIM, grp.lq
    assert lq <= CHUNK and grp.lp % CHUNK == 0
    row = grp.krow()
    w = H * d
    in_specs = [pl.BlockSpec(memory_space=pltpu.SMEM)]
    in_specs += [pl.BlockSpec((lq, w), lambda b, j, n=n: (row(b), ODD_COLS[n] // w)) for n in ("qc", "kc", "vc")]
    cache_v = _keys_by_head(cache_v)
    in_specs += [_past_key_spec(cache_k, None, grp.tkp, grp.n_past_blocks),
                 _past_key_spec(cache_v, layer, grp.tkp, grp.n_past_blocks, H),
                 pl.BlockSpec((1, d), lambda b, j: (0, 0))]
    return _past_call(functools.partial(_diff_past_body, n_past=grp.n_past_blocks, out_scale=out_scale), grp, H, d,
                      2 * H, in_specs, [lam, proj, proj, proj, cache_k, cache_v, subln], "diff_attention_past")


def _mla_past_body(qa_ref, qr_ref, lat_ref, kr_ref, plat_ref, pkr_ref, o_ref, m_ref, l_ref, acc_ref, *, n_past):
    j = pl.program_id(1)
    R = lat_ref.shape[-1]
    _chain_reset(j, m_ref, l_ref, acc_ref)

    @pl.when(j < n_past)
    def _():
        lat, kr = _mx(plat_ref[...]), _mx(pkr_ref[...])
        _chains_advance([(_dot_nt(qa_ref[:, h * R:(h + 1) * R], lat)
                          + _dot_nt(qr_ref[:, h * LANES:h * LANES + D_ROPE], kr), lat) for h in range(D_HEADS)],
                        m_ref, l_ref, acc_ref)

    @pl.when(j == n_past)
    def _():
        lat, kr = _mx(lat_ref[...]), _mx(kr_ref[...])
        done = _chains_advance([(_dot_nt(qa_ref[:, h * R:(h + 1) * R], lat)
                                 + _dot_nt(qr_ref[:, h * LANES:(h + 1) * LANES], kr), lat) for h in range(D_HEADS)],
                               m_ref, l_ref, acc_ref)
        for h, (l, acc) in enumerate(done):
            o_ref[:, h * R:(h + 1) * R] = (acc / l).astype(o_ref.dtype)


def mla_attention_past(grp, q_abs, q_rope, lat, proj, cache_lat, cache_kr, layer):
    H, lq = D_HEADS, grp.lq
    assert lq <= CHUNK and grp.lp % CHUNK == 0
    R = lat.shape[1]
    row = grp.krow()
    in_specs = [pl.BlockSpec((lq, H * R), lambda b, j: (row(b), 0)),
                pl.BlockSpec((lq, H * LANES), lambda b, j: (row(b), 0)),
                pl.BlockSpec((lq, R), lambda b, j: (row(b), 0)),
                pl.BlockSpec((lq, LANES), lambda b, j: (row(b), ODD_COLS["kr"] // LANES)),
                _past_key_spec(cache_lat, layer, grp.tkp, grp.n_past_blocks),
                _past_key_spec(cache_kr, layer, grp.tkp, grp.n_past_blocks)]
    return _past_call(functools.partial(_mla_past_body, n_past=grp.n_past_blocks), grp, H, R, H, in_specs,
                      [q_abs, q_rope, lat, proj, cache_lat, cache_kr], "mla_attention_past")


def _sortable(score):
    bits = pltpu.bitcast(score, I32)
    return jnp.where(bits < 0, bits ^ jnp.int32(0x7FFFFFFF), bits)


def _dsa_body(*refs, tq, i0, nk, topk, has_past):
    if has_past:
        qi_ref, misc_ref, q_ref, ki_ref, k_ref, v_ref, pki_ref, pk_ref, pv_ref, o_ref = refs
    else:
        qi_ref, misc_ref, q_ref, ki_ref, k_ref, v_ref, o_ref = refs
    i = i0 + pl.program_id(1)
    wi = misc_ref[:, MISC_WI:MISC_WI + IDX_HEADS] * (IDX_HEADS * IDX_DIM) ** -0.5

    qchunk = (i * tq + lax.broadcasted_iota(I32, (tq, nk), 0)) // CHUNK
    adm_new = lax.broadcasted_iota(I32, (tq, nk), 1) // CHUNK <= qchunk
    segs = [(_mx(ki_ref[...]), k_ref, v_ref, adm_new)]
    if has_past:
        segs.insert(0, (_mx(pki_ref[0]), pk_ref.at[0], pv_ref.at[0], None))

    keys = []
    for ki, _, _, adm in segs:
        score = jnp.zeros((tq, ki.shape[0]), F32)
        for h in range(IDX_HEADS):
            s = _dot_nt(qi_ref[:, h * IDX_DIM:(h + 1) * IDX_DIM], ki)
            score = score + jnp.maximum(s, 0.0) * wi[:, h:h + 1]
        key = _sortable(score)
        keys.append(key if adm is None else jnp.where(adm, key, INT_MIN))

    def count_ge(cand):
        return sum(jnp.sum(jnp.where(key >= cand, 1.0, 0.0), axis=-1, keepdims=True) for key in keys)

    kf = float(topk)
    t = jnp.where(count_ge(jnp.zeros((tq, 1), I32)) >= kf, 0, INT_MIN).astype(I32)

    def bit_step(it, t):
        cand = t | jnp.left_shift(jnp.int32(1), 30 - it)
        return jnp.where(count_ge(cand) >= kf, cand, t)

    t = lax.fori_loop(0, 31, bit_step, t)
    sels = [(key >= t) if adm is None else (adm & (key >= t)) for key, (_, _, _, adm) in zip(keys, segs)]

    rep = A_HEADS // A_KV_HEADS
    for g in range(A_KV_HEADS):
        kv = [(_mx(k[:, g * HEAD_DIM:(g + 1) * HEAD_DIM]), _mx(v[:, g * HEAD_DIM:(g + 1) * HEAD_DIM]))
              for _, k, v, _ in segs]
        for r in range(rep):
            h = g * rep + r
            q = _mx(q_ref[:, h * HEAD_DIM:(h + 1) * HEAD_DIM] * HEAD_DIM ** -0.5)
            ss = [jnp.where(sel, _dot_nt(q, k), NEG) for sel, (k, _) in zip(sels, kv)]
            m = functools.reduce(jnp.maximum, [jnp.max(s, axis=-1, keepdims=True) for s in ss])
            ps = [jnp.exp(s - m) for s in ss]
            l = sum(jnp.sum(p, axis=-1, keepdims=True) for p in ps)
            o = sum(_dot(p, v) for p, (_, v) in zip(ps, kv))
            o_ref[:, h * HEAD_DIM:(h + 1) * HEAD_DIM] = (o / l).astype(o_ref.dtype)


def dsa_attention(grp, proj, past_ki, past_k, past_v):
    B = grp.batch
    tq = grp.tq_dsa
    n_tiles = grp.lq // tq
    wq, wkv = A_HEADS * HEAD_DIM, A_KV_HEADS * HEAD_DIM
    wqi = IDX_HEADS * IDX_DIM
    c = EVEN_COLS

    extents = sorted({grp.lq // f for f in (1, 2, 4) if grp.lq % f == 0 and (grp.lq // f) % tq == 0
                      and grp.row0 % (grp.lq // f) == 0})
    classes = []
    for i in range(n_tiles):
        need = min(grp.lq, _chunk_last_key(i, tq) + 1)
        nk = min(e for e in extents if e >= need)
        if classes and classes[-1][2] == nk:
            classes[-1][1] += 1
        else:
            classes.append([i, 1, nk])

    outs = []
    for i0, n_i, nk in classes:
        qrow = lambda b, i, i0=i0: grp.row0 // tq + b * n_tiles + i0 + i
        krow = lambda b, nk=nk: (grp.row0 + b * grp.lq) // nk
        in_specs = [pl.BlockSpec((tq, wqi), lambda b, i, f=qrow: (f(b, i), c["qi"] // wqi)),
                    pl.BlockSpec((tq, LANES), lambda b, i, f=qrow: (f(b, i), c["misc"] // LANES)),
                    pl.BlockSpec((tq, wq), lambda b, i, f=qrow: (f(b, i), c["qa"] // wq)),
                    pl.BlockSpec((nk, IDX_DIM), lambda b, i, f=krow: (f(b), c["ki"] // IDX_DIM)),
                    pl.BlockSpec((nk, wkv), lambda b, i, f=krow: (f(b), c["ka"] // wkv)),
                    pl.BlockSpec((nk, wkv), lambda b, i, f=krow: (f(b), c["va"] // wkv))]
        args = [proj] * 6
        if grp.lp:
            in_specs += [pl.BlockSpec((1, grp.lp, IDX_DIM), lambda b, i: (b, 0, 0)),
                         pl.BlockSpec((1, grp.lp, wkv), lambda b, i: (b, 0, 0)),
                         pl.BlockSpec((1, grp.lp, wkv), lambda b, i: (b, 0, 0))]
            args += [past_ki, past_k, past_v]
        o = pl.pallas_call(
            functools.partial(_dsa_body, tq=tq, i0=i0, nk=nk, topk=grp.topk, has_past=bool(grp.lp)),
            grid=(B, n_i),
            in_specs=in_specs,
            out_specs=pl.BlockSpec((tq, wq), lambda b, i, n_i=n_i: (b * n_i + i, 0)),
            out_shape=jax.ShapeDtypeStruct((B * n_i * tq, wq), MXU_DTYPE),
            compiler_params=_params(2),
            name="dsa_attention",
        )(*args)
        outs.append(o.reshape(B, n_i * tq, wq))
    return jnp.concatenate(outs, axis=1).reshape(B * grp.lq, wq)


def _first_index_of_max(x, iota, n):
    mx = jnp.max(x, axis=0, keepdims=True)
    return mx, jnp.min(jnp.where(x == mx, iota, n), axis=0, keepdims=True)


def _route_body(x_ref, sc_ref, sh_ref, wr_ref, bias_ref, hp_ref, gate_ref, rank_ref, cnt_ref, h_ref, carry_ref,
                *, group):
    step = pl.program_id(0)
    tm = x_ref.shape[0]
    E = wr_ref.shape[0]
    per = E // N_GROUPS
    for g, part in enumerate(_modulated(x_ref, sc_ref, sh_ref, group)):
        h_ref[g * group:(g + 1) * group, :] = part
    h = h_ref[...]
    hp_ref[...] = _pack_rows(h)

    scores = jax.nn.sigmoid(_dot_nt(wr_ref[...], h, precision=lax.Precision.HIGHEST))
    biased = scores + bias_ref[...]
    member = lax.broadcasted_iota(I32, (per, tm), 0).astype(F32)
    gscore = []
    for g in range(N_GROUPS):
        blk = biased[g * per:(g + 1) * per, :]
        m1, i1 = _first_index_of_max(blk, member, per)
        m2 = jnp.max(jnp.where(member == i1, -jnp.inf, blk), axis=0, keepdims=True)
        gscore.append(m1 + m2)
    gscore = jnp.concatenate(gscore, axis=0)
    giota = lax.broadcasted_iota(I32, (N_GROUPS, tm), 0).astype(F32)
    gsel = jnp.zeros((N_GROUPS, tm), F32)
    for _ in range(TOPK_GROUPS):
        _, gi = _first_index_of_max(gscore, giota, N_GROUPS)
        hit = giota == gi
        gsel = jnp.where(hit, 1.0, gsel)
        gscore = jnp.where(hit, -jnp.inf, gscore)
    emask = jnp.concatenate([jnp.broadcast_to(gsel[g:g + 1, :], (per, tm)) for g in range(N_GROUPS)], axis=0)
    masked = jnp.where(emask > 0.0, biased, -jnp.inf)
    eiota = lax.broadcasted_iota(I32, (E, tm), 0).astype(F32)
    sel = jnp.zeros((E, tm), jnp.bool_)
    for _ in range(TOP_K):
        _, ei = _first_index_of_max(masked, eiota, E)
        hit = eiota == ei
        sel = sel | hit
        masked = jnp.where(hit, -jnp.inf, masked)
    w = jnp.where(sel, scores, 0.0)
    gate_ref[...] = w / jnp.sum(w, axis=0, keepdims=True) * ROUTED_SCALE

    @pl.when(step == 0)
    def _():
        carry_ref[...] = jnp.zeros(carry_ref.shape, F32)
    upper = (lax.broadcasted_iota(I32, (tm, tm), 0) <= lax.broadcasted_iota(I32, (tm, tm), 1))
    self = jnp.where(sel, 1.0, 0.0)
    incl = _dot(self, jnp.where(upper, 1.0, 0.0)) + carry_ref[...]
    rank_ref[...] = jnp.where(sel, incl - 1.0, -1.0).astype(I32)
    carry_ref[...] = incl[:, tm - 1:tm]
    cnt_ref[...] = incl[:, tm - 1:tm].astype(I32)


def route(x, scale_g, shift_g, w_router_t, r_bias, group, tm):
    T, D = x.shape
    E = w_router_t.shape[0]
    row = lambda i: (i, 0)
    col = lambda i: (0, i)
    fixed = lambda i: (0, 0)
    return pl.pallas_call(
        functools.partial(_route_body, group=group),
        grid=(T // tm,),
        in_specs=[pl.BlockSpec((tm, D), row),
                  pl.BlockSpec((1, tm // group, D), lambda i: (i, 0, 0)),
                  pl.BlockSpec((1, tm // group, D), lambda i: (i, 0, 0)),
                  pl.BlockSpec((E, D), fixed),
                  pl.BlockSpec((E, 1), fixed)],
        out_specs=[pl.BlockSpec((tm, D // 2), row), pl.BlockSpec((E, tm), col), pl.BlockSpec((E, tm), col),
                   pl.BlockSpec((E, 1), fixed)],
        out_shape=[jax.ShapeDtypeStruct((T, D // 2), I32), jax.ShapeDtypeStruct((E, T), F32),
                   jax.ShapeDtypeStruct((E, T), I32), jax.ShapeDtypeStruct((E, 1), I32)],
        scratch_shapes=[pltpu.VMEM((tm, D), F32), pltpu.VMEM((E, 1), F32)],
        compiler_params=_params(1),
        name="route",
    )(x, _per_tile(scale_g, tm, group), _per_tile(shift_g, tm, group), w_router_t, r_bias)


def _compact_body(gate_ref, rank_ref, off_ref, dest_ref, g8_ref):
    E, tm = gate_ref.shape
    rank = rank_ref[...]
    sel = rank >= 0
    dest = (rank + off_ref[...]).astype(F32)
    gate = gate_ref[...]
    eiota = lax.broadcasted_iota(I32, (E, tm), 0).astype(F32)
    dests, gates = [], []
    for _ in range(TOP_K):
        ei = jnp.min(jnp.where(sel, eiota, E), axis=0, keepdims=True)
        hit = eiota == ei
        dests.append(jnp.sum(jnp.where(hit, dest, 0.0), axis=0, keepdims=True))
        gates.append(jnp.sum(jnp.where(hit, gate, 0.0), axis=0, keepdims=True))
        sel = sel & jnp.logical_not(hit)
    dest_ref[...] = jnp.concatenate(dests, axis=0).astype(I32)
    g8_ref[...] = jnp.concatenate(gates, axis=0)


def compact_routes(gate, rank, seg_off, tm):
    E, T = gate.shape
    col = lambda i: (0, i)
    return pl.pallas_call(
        _compact_body,
        grid=(T // tm,),
        in_specs=[pl.BlockSpec((E, tm), col), pl.BlockSpec((E, tm), col), pl.BlockSpec((E, 1), lambda i: (0, 0))],
        out_specs=[pl.BlockSpec((TOP_K, tm), col), pl.BlockSpec((TOP_K, tm), col)],
        out_shape=[jax.ShapeDtypeStruct((TOP_K, T), I32), jax.ShapeDtypeStruct((TOP_K, T), F32)],
        compiler_params=_params(1),
        name="compact_routes",
    )(gate, rank, seg_off)


def _ffn(x, w1, w3, w2):
    a = _dot(x, w1)
    return _dot(a * jax.nn.sigmoid(a) * _dot(x, w3), w2)


def _gmm_body(te_ref, nt_ref, tok_ref, tok_next_ref, slot_prev_ref, slot_ref, x_hbm, w1_ref, w3_ref, w2_ref, y_hbm,
              w1b, w3b, w2b, xbuf, ybuf, in_sem, out_sem, *, tm, parts):
    i = pl.program_id(0)
    n_used = nt_ref[0]
    cur, nxt = i % 2, (i + 1) % 2

    def row_in(tok, r, buf):
        return pltpu.make_async_copy(x_hbm.at[pl.ds(tok[0, 0, r], 1)], xbuf.at[buf, pl.ds(r, 1)], in_sem.at[buf])

    def row_out(slot, r, buf):
        return pltpu.make_async_copy(ybuf.at[buf, pl.ds(r, 1)], y_hbm.at[pl.ds(slot[0, 0, r], 1)], out_sem.at[buf])

    def wait_in(buf):
        pltpu.make_async_copy(x_hbm.at[pl.ds(0, tm)], xbuf.at[buf], in_sem.at[buf]).wait()

    def wait_out(buf):
        pltpu.make_async_copy(ybuf.at[buf], y_hbm.at[pl.ds(0, tm)], out_sem.at[buf]).wait()

    @pl.when(i == 0)
    def _():
        for r in range(tm):
            row_in(tok_ref, r, 0).start()
        ybuf[1] = jnp.zeros((tm, ybuf.shape[2]), I32)
        spare = pltpu.make_async_copy(ybuf.at[1], y_hbm.at[pl.ds(y_hbm.shape[0] - tm, tm)], out_sem.at[1])
        spare.start()
        spare.wait()

    @pl.when(jnp.logical_or(i == 0, te_ref[i] != te_ref[jnp.maximum(i - 1, 0)]))
    def _():
        w1b[...] = _mx(w1_ref[...])
        w3b[...] = _mx(w3_ref[...])
        w2b[...] = _mx(w2_ref[...])

    @pl.when(jnp.logical_and(i >= 1, i < n_used))
    def _():
        wait_out(cur)

    @pl.when(i < n_used)
    def _():
        wait_in(cur)
        pending = []
        for r in range(tm):
            pending += [row_in(tok_next_ref, r, nxt), row_out(slot_prev_ref, r, nxt)]
        rows, D, F = tm // parts, w1b.shape[0], w1b.shape[1]
        kc = D // GMM_SPLIT
        nc = D // GMM_SPLIT
        per_gap = -(-len(pending) // (parts * 3 * GMM_SPLIT))

        def issue():
            for cp in pending[:per_gap]:
                cp.start()
            del pending[:per_gap]

        for p in range(parts):
            x = _mx(_unpack_rows(xbuf[cur, p * rows:(p + 1) * rows, :]))
            a = b = jnp.zeros((rows, F), F32)
            for c in range(GMM_SPLIT):
                a = a + jnp.dot(x[:, c * kc:(c + 1) * kc], w1b[c * kc:(c + 1) * kc, :], preferred_element_type=F32)
                issue()
            for c in range(GMM_SPLIT):
                b = b + jnp.dot(x[:, c * kc:(c + 1) * kc], w3b[c * kc:(c + 1) * kc, :], preferred_element_type=F32)
                issue()
            act = _mx(a * jax.nn.sigmoid(a) * b)
            y = []
            for c in range(GMM_SPLIT):
                y.append(jnp.dot(act, w2b[:, c * nc:(c + 1) * nc], preferred_element_type=F32))
                issue()
            ybuf[cur, p * rows:(p + 1) * rows, :] = _pack_rows(jnp.concatenate(y, axis=1))
        assert not pending

    @pl.when(i == n_used - 1)
    def _():
        for r in range(tm):
            row_out(slot_ref, r, cur).start()
        wait_in(nxt)
        wait_out(nxt)
        wait_out(cur)


def grouped_ffn(tile_expert, n_tiles, row_token, row_slot, x, w1, w3, w2, layer, n_out, tm):
    n_rows, W = row_token.shape[0] * tm, x.shape[1]
    D, F = w1.shape[-2], w1.shape[-1]
    spare_rows = (n_out - tm + jnp.arange(tm, dtype=I32)).reshape(1, 1, tm)
    slot_shifted = jnp.concatenate([spare_rows, row_slot], axis=0)
    clamp = lambda i, nt: jnp.maximum(jnp.minimum(i, nt[0] - 1), 0)
    wsel = lambda i, te, nt: (layer, te[i], 0, 0)
    smem_rows = lambda f: pl.BlockSpec((1, 1, tm), f, memory_space=pltpu.SMEM)
    return pl.pallas_call(
        functools.partial(_gmm_body, tm=tm, parts=2),
        grid_spec=pltpu.PrefetchScalarGridSpec(
            num_scalar_prefetch=2,
            grid=(n_rows // tm,),
            in_specs=[smem_rows(lambda i, te, nt: (clamp(i, nt), 0, 0)),
                      smem_rows(lambda i, te, nt: (clamp(i + 1, nt), 0, 0)),
                      smem_rows(lambda i, te, nt: (clamp(i, nt), 0, 0)),
                      smem_rows(lambda i, te, nt: (clamp(i, nt) + 1, 0, 0)),
                      pl.BlockSpec(memory_space=pl.ANY),
                      pl.BlockSpec((None, None, D, F), wsel),
                      pl.BlockSpec((None, None, D, F), wsel),
                      pl.BlockSpec((None, None, F, D), wsel)],
            out_specs=pl.BlockSpec(memory_space=pl.ANY),
            scratch_shapes=[pltpu.VMEM((D, F), MXU_DTYPE), pltpu.VMEM((D, F), MXU_DTYPE),
                            pltpu.VMEM((F, D), MXU_DTYPE),
                            pltpu.VMEM((2, tm, W), I32), pltpu.VMEM((2, tm, W), I32),
                            pltpu.SemaphoreType.DMA((2,)), pltpu.SemaphoreType.DMA((2,))]),
        out_shape=jax.ShapeDtypeStruct((n_out, W), I32),
        compiler_params=pltpu.CompilerParams(dimension_semantics=("arbitrary",), vmem_limit_bytes=VMEM_LIMIT_BYTES,
                                             disable_bounds_checks=True),
        name="grouped_ffn",
    )(tile_expert, n_tiles, row_token, row_token, slot_shifted, slot_shifted, x, w1, w3, w2)


def _close_moe_body(h_ref, *rest, group, alpha, n_k):
    yk_refs, (g8_ref, ws1_ref, ws3_ref, ws2_ref, x_ref, gate_ref, g_ref, b_ref, o_ref) = rest[:n_k], rest[n_k:]
    y = _ffn(_unpack_rows(h_ref[...]), ws1_ref[...], ws3_ref[...], ws2_ref[...])
    g8 = g8_ref[...]
    for k, yk_ref in enumerate(yk_refs):
        y = y + g8[:, k:k + 1] * _unpack_rows(yk_ref[...])
    r = _gated_residual(x_ref, y, gate_ref, group, alpha)
    o_ref[...] = _layer_norm_rows(r, g_ref[...], b_ref[...])


def close_moe(h, y_rows, g8, ws1, ws3, ws2, x, gate_g, ln_g, ln_b, group, alpha, tm):
    T, D = x.shape
    K = g8.shape[1]
    W = h.shape[1]
    row = lambda i: (i, 0)
    fixed = lambda i: (0, 0)
    return pl.pallas_call(
        functools.partial(_close_moe_body, group=group, alpha=alpha, n_k=K),
        grid=(T // tm,),
        in_specs=[pl.BlockSpec((tm, W), row)] +
                 [pl.BlockSpec((tm, W), lambda i, k=k: (k * (T // tm) + i, 0)) for k in range(K)] +
                 [pl.BlockSpec((tm, K), row),
                  pl.BlockSpec(ws1.shape, fixed),
                  pl.BlockSpec(ws3.shape, fixed),
                  pl.BlockSpec(ws2.shape, fixed),
                  pl.BlockSpec((tm, D), row),
                  pl.BlockSpec((1, tm // group, D), lambda i: (i, 0, 0)),
                  pl.BlockSpec((1, D), fixed),
                  pl.BlockSpec((1, D), fixed)],
        out_specs=pl.BlockSpec((tm, D), row),
        out_shape=jax.ShapeDtypeStruct((T, D), F32),
        compiler_params=_params(1),
        name="close_moe",
    )(h, *([y_rows] * K), g8, ws1, ws3, ws2, x, _per_tile(gate_g, tm, group), ln_g, ln_b)


def _rope_tables(pos, half):
    inv_freq = jnp.power(ROPE_THETA, -jnp.arange(half, dtype=F32) / half)
    ang = pos.astype(F32)[:, None] * inv_freq[None, :]
    cos, sin = jnp.cos(ang), jnp.sin(ang)
    reps = LANES // (2 * half)
    cos = jnp.tile(jnp.concatenate([cos, cos], axis=1), (1, reps))
    sin = jnp.tile(jnp.concatenate([-sin, sin], axis=1), (1, reps))
    return jnp.stack([jnp.ones_like(cos), cos]), jnp.stack([jnp.zeros_like(sin), sin])


def _place_cols(w, sizes, names, cols, width):
    out = jnp.zeros((w.shape[0], width), w.dtype)
    o = 0
    for size, name in zip(sizes, names):
        if name is not None:
            dst, sub = name
            out = lax.dynamic_update_slice(out, w[:, o:o + size], (0, cols[dst] + sub))
        o += size
    return out


def _moe(x, scale_g, shift_g, gate_g, ln_g, ln_b, w_router, r_bias, w1, w3, w2, ws1, ws3, ws2, layer, group, alpha):
    T, D = x.shape
    E = w1.shape[1]
    tm = 256
    t_route = _pick(T, 768, math.lcm(LANES, group))
    h, gate, rank, counts = route(x, scale_g, shift_g, w_router.T, r_bias.reshape(E, 1), group, t_route)

    counts = counts[:, 0]
    tiles_per = (counts + tm - 1) // tm
    tile_end = jnp.cumsum(tiles_per)
    seg_off = ((tile_end - tiles_per) * tm).astype(I32)
    n_tiles = (T * TOP_K) // tm + E
    used = tile_end[-1].astype(I32)
    tile_ids = jnp.minimum(jnp.arange(n_tiles, dtype=I32), used - 1)
    tile_expert = jnp.sum((tile_end[None, :] <= tile_ids[:, None]).astype(I32), axis=1)

    dest, g8 = compact_routes(gate, rank, seg_off.reshape(E, 1), t_route)
    n_pairs = TOP_K * T
    pair_slot = jnp.arange(n_pairs, dtype=I32).reshape(TOP_K, T)
    spare = n_pairs + jnp.arange(n_tiles * tm, dtype=I32) % tm
    row_slot = spare.at[dest.reshape(-1)].set(pair_slot.reshape(-1), unique_indices=True, mode="promise_in_bounds")
    row_token = jnp.where(row_slot < n_pairs, row_slot % T, 0)
    y_rows = grouped_ffn(tile_expert, used.reshape(1), row_token.reshape(n_tiles, 1, tm),
                         row_slot.reshape(n_tiles, 1, tm), h, w1, w3, w2, layer, n_pairs + tm, tm)
    return close_moe(h, y_rows, g8.T, _mx(ws1), _mx(ws3), _mx(ws2), x, gate_g, ln_g, ln_b, group, alpha,
                     _pick(T, 256, math.lcm(SUBLANES, group)))


def _fox_gates(grp, proj, b_f, past_logf):
    B, H = grp.batch, B_HEADS
    c0 = EVEN_COLS["misc"] + MISC_FL
    fl = proj[grp.row0:grp.row0 + B * grp.lq, c0:c0 + H].reshape(B, grp.lq, H)
    logf = jax.nn.log_sigmoid(fl + b_f)
    logf_all = logf if past_logf is None else jnp.concatenate([past_logf, logf], axis=1)
    cum = jnp.moveaxis(jnp.cumsum(logf_all, axis=1), 1, 2).reshape(B * H, grp.lk)
    fq = cum[:, grp.lp:, None]
    new = cum[:, grp.lp:].reshape(B * H, grp.lq // grp.tk, 1, grp.tk)
    new = jnp.pad(new, ((0, 0), (0, 0), (0, 0), (0, KEY_BLOCK - grp.tk)))
    if grp.lp:
        past = cum[:, :grp.lp].reshape(B * H, grp.n_past_blocks, 1, grp.tkp)
        past = jnp.pad(past, ((0, 0), (0, 0), (0, 0), (0, KEY_BLOCK - grp.tkp)))
        new = jnp.concatenate([past, new], axis=1)
    return logf, fq, new


def kernel(x_prompt, x_sample, c_prompt, c_sample, cache_a_k, cache_a_v, cache_a_kidx, cache_b_k, cache_b_v, cache_b_logf, cache_c_k, cache_c_v, cache_d_latent, cache_d_krope, w_in_even, b_forget, w_out_even, w_in_odd, c_lambda, c_subln, d_q_norm, d_w_uq, d_kv_norm, d_w_uk, d_w_uv, w_out_odd, ada_mix_w, ada_mix_b, ln_mix_g, ln_mix_b, ada_ffn_w, ada_ffn_b, ln_ffn_g, ln_ffn_b, router_w, router_bias, moe_w1, moe_w3, moe_w2, shared_w1, shared_w3, shared_w2):
    B, L, D = x_prompt.shape
    Bs, Ls, _ = x_sample.shape
    depth = ada_mix_w.shape[0]
    alpha = (2 * depth) ** 0.25
    past_len = cache_a_k.shape[2]
    Tp = B * L
    T = Tp + Bs * Ls
    grp_p = _Group(B, L, 0, 0)
    grp_s = _Group(Bs, Ls, past_len, Tp)
    group = math.gcd(L, Ls)
    assert group % (2 * SUBLANES) == 0
    tm_tok = _pick(T, 768, math.lcm(LANES, group))
    tm_close = _pick(T, 512, math.lcm(2 * SUBLANES, group))
    pos = jnp.concatenate([jnp.tile(jnp.arange(L), B), jnp.tile(past_len + jnp.arange(Ls), Bs)])
    c_act = jax.nn.silu(jnp.concatenate([c_prompt, c_sample], axis=0))

    def conditioning(ada_w, ada_b, layer):
        mod = matmul(c_act, ada_w, layer=layer) + ada_b[layer]
        per_group = jnp.concatenate([jnp.repeat(mod[:B], L // group, axis=0),
                                     jnp.repeat(mod[B:], Ls // group, axis=0)], axis=0)
        return jnp.split(per_group, 3, axis=-1)

    def flat_past(c):
        return c.reshape(c.shape[0], c.shape[1], -1)

    x = jnp.concatenate([x_prompt.reshape(Tp, D), x_sample.reshape(Bs * Ls, D)], axis=0)
    ev_p, ev_s, od_p, od_s = [], [], [], []
    for i in range(depth):
        j = i // 2
        shift_g, scale_g, gate_g = conditioning(ada_mix_w, ada_mix_b, i)
        if i % 2 == 0:
            sizes = (A_HEADS * HEAD_DIM, A_KV_HEADS * HEAD_DIM, A_KV_HEADS * HEAD_DIM, IDX_HEADS * IDX_DIM, IDX_DIM,
                     IDX_HEADS, B_HEADS * HEAD_DIM, B_HEADS * HEAD_DIM, B_HEADS * HEAD_DIM, B_HEADS)
            names = (("qa", 0), ("ka", 0), ("va", 0), ("qi", 0), ("ki", 0), ("misc", MISC_WI),
                     ("qf", 0), ("kf", 0), ("vf", 0), ("misc", MISC_FL))
            w_in = _mx(_place_cols(w_in_even[j], sizes, names, EVEN_COLS, EVEN_WIDTH))
            cos, sin = _rope_tables(pos, HEAD_DIM // 2)
            proj = project(x, scale_g, shift_g, w_in, cos, sin, EVEN_ROPE_RANGES, HEAD_DIM // 2, group, tm_tok)
            outs_a, outs_b = [], []
            for grp, store in ((grp_p, ev_p), (grp_s, ev_s)):
                if grp.lp:
                    logf, fq, fk = _fox_gates(grp, proj, b_forget[j], cache_b_logf[j])
                    outs_a.append(dsa_attention(grp, proj, cache_a_kidx[j], flat_past(cache_a_k[j]),
                                                flat_past(cache_a_v[j])))
                    outs_b.append(fox_attention_past(grp, proj, fq, fk, cache_b_k, cache_b_v, j))
                else:
                    logf, fq, fk = _fox_gates(grp, proj, b_forget[j], None)
                    outs_a.append(dsa_attention(grp, proj, None, None, None))
                    outs_b.append(fox_attention(grp, proj, fq, fk))
                r0, r1 = grp.row0, grp.row0 + grp.batch * grp.lq
                cut = lambda name, heads, d: proj[r0:r1, EVEN_COLS[name]:EVEN_COLS[name] + heads * d].reshape(
                    (grp.batch, grp.lq) + ((heads, d) if heads > 1 else (d,)))
                store.append((cut("ka", A_KV_HEADS, HEAD_DIM), cut("va", A_KV_HEADS, HEAD_DIM), cut("ki", 1, IDX_DIM),
                              cut("kf", B_HEADS, HEAD_DIM), cut("vf", B_HEADS, HEAD_DIM), logf))
            a1, a2 = jnp.concatenate(outs_a, axis=0), jnp.concatenate(outs_b, axis=0)
            n1 = A_HEADS * HEAD_DIM
            w1o, w2o = _mx(w_out_even[j][:n1]), _mx(w_out_even[j][n1:])
        else:
            lam_init = 0.8 - 0.6 * math.exp(-0.3 * i)
            lam_f = c_lambda[j]
            lam = (jnp.exp(jnp.sum(lam_f[0] * lam_f[1])) - jnp.exp(jnp.sum(lam_f[2] * lam_f[3])) + lam_init).reshape(1)
            q_lora, R = d_q_norm.shape[1], d_kv_norm.shape[1]
            sizes = (C_HEADS * 2 * C_QK_DIM, C_HEADS * 2 * C_QK_DIM, C_HEADS * C_V_DIM, q_lora, R, D_ROPE)
            names = (("qc", 0), ("kc", 0), ("vc", 0), ("qd", 0), ("ckv", 0), ("kr", 0))
            w_in = _mx(_place_cols(w_in_odd[j], sizes, names, ODD_COLS, ODD_WIDTH))
            cos, sin = _rope_tables(pos, C_QK_DIM // 2)
            proj = project(x, scale_g, shift_g, w_in, cos, sin, ODD_ROPE_RANGES, C_QK_DIM // 2, group, tm_tok)

            w_uq = d_w_uq[j].reshape(q_lora, D_HEADS, D_NOPE + D_ROPE)
            w_abs = bmm_precise(jnp.moveaxis(w_uq[:, :, :D_NOPE], 1, 0), jnp.transpose(d_w_uk[j], (1, 2, 0)))
            w_abs = jnp.moveaxis(w_abs, 0, 1).reshape(q_lora, D_HEADS * R)
            w_rope = jnp.pad(w_uq[:, :, D_NOPE:], ((0, 0), (0, 0), (0, LANES - D_ROPE))).reshape(q_lora, D_HEADS * LANES)
            q_abs, q_rope, lat = mla_prepare(proj, d_q_norm[j][None], d_kv_norm[j][None],
                                             _mx(jnp.concatenate([w_abs, w_rope], axis=1)), cos, sin, tm_close)
            n1 = C_HEADS * C_V_DIM
            w_od = w_out_odd[j][n1:].reshape(D_HEADS, D_V, D)
            w_lat_out = bmm_precise(jnp.transpose(d_w_uv[j], (1, 0, 2)), w_od).reshape(D_HEADS * R, D)

            outs_a, outs_b = [], []
            for grp, store in ((grp_p, od_p), (grp_s, od_s)):
                if grp.lp:
                    outs_a.append(diff_attention_past(grp, proj, lam, c_subln[j][None], 1.0 - lam_init,
                                                      flat_past(cache_c_k[j]), cache_c_v, j))
                    outs_b.append(mla_attention_past(grp, q_abs, q_rope, lat, proj, cache_d_latent, cache_d_krope, j))
                else:
                    outs_a.append(diff_attention(grp, proj, lam, c_subln[j][None], 1.0 - lam_init))
                    outs_b.append(mla_attention(grp, q_abs, q_rope, lat, proj))
                r0, r1 = grp.row0, grp.row0 + grp.batch * grp.lq
                shp = (grp.batch, grp.lq)
                oc = ODD_COLS
                store.append((proj[r0:r1, oc["kc"]:oc["kc"] + C_HEADS * 2 * C_QK_DIM].reshape(shp + (C_HEADS, 2, C_QK_DIM)),
                              proj[r0:r1, oc["vc"]:oc["vc"] + n1].reshape(shp + (C_HEADS, C_V_DIM)),
                              lat[r0:r1].reshape(shp + (R,)),
                              proj[r0:r1, oc["kr"]:oc["kr"] + D_ROPE].reshape(shp + (D_ROPE,))))
            a1, a2 = jnp.concatenate(outs_a, axis=0), jnp.concatenate(outs_b, axis=0)
            w1o, w2o = _mx(w_out_odd[j][:n1]), _mx(w_lat_out)
        x = close_mixer(a1, a2, w1o, w2o, x, gate_g, ln_mix_g[i][None], ln_mix_b[i][None], group, alpha, tm_close)

        shift_g, scale_g, gate_g = conditioning(ada_ffn_w, ada_ffn_b, i)
        x = _moe(x, scale_g, shift_g, gate_g, ln_ffn_g[i][None], ln_ffn_b[i][None], router_w[i], router_bias[i],
                 moe_w1, moe_w3, moe_w2, shared_w1[i], shared_w3[i], shared_w2[i], i, group, alpha)

    stack = lambda rows, idx: jnp.stack([r[idx] for r in rows])
    outs = [x[:Tp].reshape(B, L, D), x[Tp:].reshape(Bs, Ls, D)]
    for idx in range(6):
        outs += [stack(ev_p, idx), stack(ev_s, idx)]
    for idx in range(4):
        outs += [stack(od_p, idx), stack(od_s, idx)]
    return tuple(outs)
```

```python
import functools
import math

import jax
import jax.numpy as jnp
from jax import lax
from jax.experimental import pallas as pl
from jax.experimental.pallas import tpu as pltpu

CHUNK = 64
ROPE_THETA = 10000.0
HEAD_DIM = 128
A_HEADS = 8
A_KV_HEADS = 2
IDX_HEADS = 16
IDX_DIM = 128
A_TOPK_MAX = 256
B_HEADS = 8
C_HEADS = 8
C_QK_DIM = 64
C_V_DIM = 128
D_HEADS = 8
D_NOPE = 128
D_ROPE = 64
D_V = 128
N_GROUPS = 8
TOPK_GROUPS = 4
TOP_K = 8
ROUTED_SCALE = 2.5
NORM_EPS = 1e-6

LANES = 128
SUBLANES = 8
MXU_WIDTH = 256
VMEM_LIMIT_BYTES = 56 * 2**20
MXU_DTYPE = jnp.bfloat16

F32 = jnp.float32
I32 = jnp.int32
NEG = -1e30
INT_MIN = -2**31
KEY_BLOCK = 512
QUERY_BLOCK = 512
HEADS_PER_STEP = 2

EVEN_COLS = dict(qi=0, qa=2048, qf=3072, kf=4096, vf=5120, ka=6144, va=6400, ki=6656, misc=6912)
EVEN_WIDTH = 7168
EVEN_ROPE_RANGES = ((0, 3072), (6144, 6400), (6656, 6912))
MISC_WI, MISC_FL = 0, IDX_HEADS
ODD_COLS = dict(qc=0, kc=1024, vc=2048, qd=3072, ckv=3584, kr=3840)
ODD_WIDTH = 4096
ODD_ROPE_RANGES = ((0, 2048), (3840, 4096))


def _pick(n, target, mult):
    best = None
    for d in range(mult, min(n, target) + 1, mult):
        if n % d == 0:
            best = d
    return n if best is None else best


def _params(n_axes):
    return pltpu.CompilerParams(dimension_semantics=("arbitrary",) * n_axes,
                                vmem_limit_bytes=VMEM_LIMIT_BYTES)


def _mx(a):
    return a.astype(MXU_DTYPE)


def _dot(a, b):
    return jnp.dot(_mx(a), _mx(b), preferred_element_type=F32)


def _dot_nt(a, b, precision=None):
    if precision is None:
        a, b = _mx(a), _mx(b)
    return lax.dot_general(a, b, (((1,), (1,)), ((), ())), preferred_element_type=F32, precision=precision)


def _mm_body(a_ref, b_ref, o_ref, *, precise):
    if precise:
        o = jnp.dot(a_ref[...], b_ref[...], preferred_element_type=F32, precision=lax.Precision.HIGHEST)
    else:
        o = _dot(a_ref[...], b_ref[...])
    o_ref[...] = o.astype(o_ref.dtype)


def matmul(a, b, out_dtype=F32, precise=False, tm_target=1024, tn_target=512, layer=None):
    M, K = a.shape
    N = b.shape[-1]
    tm = _pick(M, tm_target, SUBLANES)
    tn = _pick(N, tn_target, LANES)
    if layer is None:
        b_spec = pl.BlockSpec((K, tn), lambda i, j: (0, j))
    else:
        b_spec = pl.BlockSpec((None, K, tn), lambda i, j: (layer, 0, j))
    return pl.pallas_call(
        functools.partial(_mm_body, precise=precise),
        grid=(M // tm, N // tn),
        in_specs=[pl.BlockSpec((tm, K), lambda i, j: (i, 0)), b_spec],
        out_specs=pl.BlockSpec((tm, tn), lambda i, j: (i, j)),
        out_shape=jax.ShapeDtypeStruct((M, N), out_dtype),
        compiler_params=_params(2),
        name="matmul",
    )(a, b)


def _bmm_body(a_ref, b_ref, o_ref):
    o_ref[0] = jnp.dot(a_ref[0], b_ref[0], preferred_element_type=F32,
                       precision=lax.Precision.HIGHEST).astype(o_ref.dtype)


def bmm_precise(a, b):
    H, M, K = a.shape
    N = b.shape[2]
    return pl.pallas_call(
        _bmm_body,
        grid=(H,),
        in_specs=[pl.BlockSpec((1, M, K), lambda h: (h, 0, 0)),
                  pl.BlockSpec((1, K, N), lambda h: (h, 0, 0))],
        out_specs=pl.BlockSpec((1, M, N), lambda h: (h, 0, 0)),
        out_shape=jax.ShapeDtypeStruct((H, M, N), F32),
        compiler_params=_params(1),
        name="bmm_precise",
    )(a, b)


def _per_tile(table, tm, group):
    return table.reshape(table.shape[0] * group // tm, tm // group, table.shape[1])


def _modulated(x_ref, sc_ref, sh_ref, group):
    parts = []
    for g in range(x_ref.shape[0] // group):
        rows = slice(g * group, (g + 1) * group)
        parts.append(x_ref[rows, :] * (1.0 + sc_ref[0, g:g + 1, :]) + sh_ref[0, g:g + 1, :])
    return parts


def _pack_rows(x):
    bits = pltpu.bitcast(x, I32)
    rounded = bits + 0x7FFF + ((bits >> 16) & 1)
    half = x.shape[1] // 2
    return ((rounded[:, :half] >> 16) & 0xFFFF) | (rounded[:, half:] & jnp.int32(-0x10000))


def _unpack_rows(w):
    lo = pltpu.bitcast(jnp.left_shift(w, 16), F32)
    hi = pltpu.bitcast(w & jnp.int32(-0x10000), F32)
    return jnp.concatenate([lo, hi], axis=1)


def _rope_lanes(x, cos, sin, half):
    if 2 * half == LANES:
        partner = pltpu.roll(x, half, axis=1)
    else:
        lane = lax.broadcasted_iota(I32, x.shape, 1)
        lower = (lane % (2 * half)) < half
        partner = jnp.where(lower, pltpu.roll(x, LANES - half, axis=1), pltpu.roll(x, half, axis=1))
    return x * cos + partner * sin


def _layer_norm_rows(r, g, b):
    rc = r - jnp.mean(r, axis=-1, keepdims=True)
    var = jnp.mean(rc * rc, axis=-1, keepdims=True)
    return rc * lax.rsqrt(var + NORM_EPS) * g + b


def _gated_residual(x_ref, y, gate_ref, group, alpha):
    parts = []
    for g in range(x_ref.shape[0] // group):
        rows = slice(g * group, (g + 1) * group)
        parts.append(alpha * x_ref[rows, :] + (1.0 + gate_ref[0, g:g + 1, :]) * y[rows, :])
    return jnp.concatenate(parts, axis=0)


def _proj_body(kind_ref, x_ref, sc_ref, sh_ref, w_ref, *rest, group, half, n_sub):
    tables, (o_ref, h_ref) = rest[:2 * n_sub], rest[2 * n_sub:]
    j = pl.program_id(1)

    @pl.when(j == 0)
    def _():
        for g, part in enumerate(_modulated(x_ref, sc_ref, sh_ref, group)):
            h_ref[g * group:(g + 1) * group, :] = part.astype(h_ref.dtype)

    for c in range(n_sub):
        cos, sin = tables[2 * c][...], tables[2 * c + 1][...]
        acc = jnp.dot(h_ref[...], w_ref[:, c * MXU_WIDTH:(c + 1) * MXU_WIDTH], preferred_element_type=F32)
        for p in range(MXU_WIDTH // LANES):
            o_ref[:, c * MXU_WIDTH + p * LANES:c * MXU_WIDTH + (p + 1) * LANES] = _rope_lanes(
                acc[:, p * LANES:(p + 1) * LANES], cos, sin, half)


def project(x, scale_g, shift_g, w, cos, sin, rope_ranges, half, group, tm):
    T, D = x.shape
    N = w.shape[1]
    n_sub = 2
    tn = n_sub * MXU_WIDTH
    start = jnp.arange(N // MXU_WIDTH) * MXU_WIDTH
    kinds = sum(((start >= lo) & (start < hi)).astype(I32) for lo, hi in rope_ranges)
    row = lambda i, j, k: (i, 0)
    tile = lambda i, j, k: (i, 0, 0)
    table_specs, table_args = [], []
    for c in range(n_sub):
        pick = lambda i, j, k, c=c: (k[j * n_sub + c], i, 0)
        table_specs += [pl.BlockSpec((None, tm, LANES), pick)] * 2
        table_args += [cos, sin]
    return pl.pallas_call(
        functools.partial(_proj_body, group=group, half=half, n_sub=n_sub),
        grid_spec=pltpu.PrefetchScalarGridSpec(
            num_scalar_prefetch=1,
            grid=(T // tm, N // tn),
            in_specs=[pl.BlockSpec((tm, D), row),
                      pl.BlockSpec((1, tm // group, D), tile),
                      pl.BlockSpec((1, tm // group, D), tile),
                      pl.BlockSpec((D, tn), lambda i, j, k: (0, j))] + table_specs,
            out_specs=pl.BlockSpec((tm, tn), lambda i, j, k: (i, j)),
            scratch_shapes=[pltpu.VMEM((tm, D), MXU_DTYPE)]),
        out_shape=jax.ShapeDtypeStruct((T, N), F32),
        compiler_params=_params(2),
        name="project",
    )(kinds, x, _per_tile(scale_g, tm, group), _per_tile(shift_g, tm, group), w, *table_args)


def _close_mix_body(a1_ref, a2_ref, w1_ref, w2_ref, x_ref, gate_ref, g_ref, b_ref, o_ref, *, group, alpha):
    y = _dot(a1_ref[...], w1_ref[...]) + _dot(a2_ref[...], w2_ref[...])
    r = _gated_residual(x_ref, y, gate_ref, group, alpha)
    o_ref[...] = _layer_norm_rows(r, g_ref[...], b_ref[...])


def close_mixer(a1, a2, w1, w2, x, gate_g, ln_g, ln_b, group, alpha, tm):
    T, D = x.shape
    row = lambda i: (i, 0)
    fixed = lambda i: (0, 0)
    return pl.pallas_call(
        functools.partial(_close_mix_body, group=group, alpha=alpha),
        grid=(T // tm,),
        in_specs=[pl.BlockSpec((tm, a1.shape[1]), row),
                  pl.BlockSpec((tm, a2.shape[1]), row),
                  pl.BlockSpec(w1.shape, fixed),
                  pl.BlockSpec(w2.shape, fixed),
                  pl.BlockSpec((tm, D), row),
                  pl.BlockSpec((1, tm // group, D), lambda i: (i, 0, 0)),
                  pl.BlockSpec((1, D), fixed),
                  pl.BlockSpec((1, D), fixed)],
        out_specs=pl.BlockSpec((tm, D), row),
        out_shape=jax.ShapeDtypeStruct((T, D), F32),
        compiler_params=_params(1),
        name="close_mixer",
    )(a1, a2, w1, w2, x, _per_tile(gate_g, tm, group), ln_g, ln_b)


def _online_step(s, v, carry):
    m, l, acc = carry
    m_new = jnp.maximum(m, jnp.max(s, axis=-1, keepdims=True))
    p = jnp.exp(s - m_new)
    alpha = jnp.exp(m - m_new)
    l = alpha * l + jnp.sum(p, axis=-1, keepdims=True)
    acc = alpha * acc + _dot(p, v)
    return m_new, l, acc


def _init_carry(tq, dv):
    return (jnp.full((tq, 1), NEG, F32), jnp.zeros((tq, 1), F32), jnp.zeros((tq, dv), F32))


def _chunk_last_key(i, tq):
    return ((i + 1) * tq - 1) // CHUNK * CHUNK + CHUNK - 1


class _Group:
    def __init__(self, batch, lq, lp, row0):
        self.batch, self.lq, self.lp, self.row0 = batch, lq, lp, row0
        self.lk = lp + lq
        self.tq = _pick(lq, QUERY_BLOCK, SUBLANES)
        self.tq_dsa = _pick(lq, 128, SUBLANES)
        self.tk = _pick(lq, KEY_BLOCK, SUBLANES)
        self.tkp = _pick(lp, KEY_BLOCK, LANES) if lp else 0
        self.n_past_blocks = lp // self.tkp if lp else 0
        self.topk = min(A_TOPK_MAX, self.lk // 4)
        assert lp % CHUNK == 0 and row0 % lq == 0 and lq % self.tk == 0

    def qrow(self, tq):
        base, per = self.row0 // tq, self.lq // tq
        return lambda b, i: base + b * per + i

    def krow(self):
        base = self.row0 // self.lq
        return lambda b: base + b


def _new_key_blocks(i, tq, grp_lq, tk, causal_last):
    return jnp.minimum(causal_last, grp_lq - 1) // tk + 1


def _fox_body(q_ref, k_ref, v_ref, fq_ref, fk_ref, o_ref, *, tq, tk, lq, hp):
    i = pl.program_id(2)
    d = HEAD_DIM
    head = lambda x, u: x[:, u * d:(u + 1) * d]
    q = _mx(q_ref[...] * d ** -0.5)
    carry = tuple(_init_carry(tq, d) for _ in range(hp))
    qpos = i * tq + lax.broadcasted_iota(I32, (tq, tk), 0)

    def new_step(j, c):
        off = pl.multiple_of(j * tk, tk)
        k, v = k_ref[pl.ds(off, tk), :], v_ref[pl.ds(off, tk), :]
        causal = off + lax.broadcasted_iota(I32, (tq, tk), 1) <= qpos
        out = []
        for u in range(hp):
            s = _dot_nt(head(q, u), head(k, u)) + fq_ref[u] - fk_ref[u, j][:, :tk]
            out.append(_online_step(jnp.where(causal, s, NEG), head(v, u), c[u]))
        return tuple(out)

    nb = _new_key_blocks(i, tq, lq, tk, (i + 1) * tq - 1)
    carry = lax.fori_loop(0, nb, new_step, carry)
    for u, (m, l, acc) in enumerate(carry):
        o_ref[:, u * d:(u + 1) * d] = (acc / l).astype(o_ref.dtype)


def fox_attention(grp, proj, fq, fk):
    assert grp.lp == 0
    B, H, hp = grp.batch, B_HEADS, HEADS_PER_STEP
    w = hp * HEAD_DIM
    tq = grp.tq
    qrow, krow = grp.qrow(tq), grp.krow()
    cq, ck, cv = (EVEN_COLS[n] // w for n in ("qf", "kf", "vf"))
    in_specs = [pl.BlockSpec((tq, w), lambda b, h, i: (qrow(b, i), cq + h)),
                pl.BlockSpec((grp.lq, w), lambda b, h, i: (krow(b), ck + h)),
                pl.BlockSpec((grp.lq, w), lambda b, h, i: (krow(b), cv + h))]
    in_specs += [pl.BlockSpec((hp, tq, 1), lambda b, h, i: (b * (H // hp) + h, i, 0)),
                 pl.BlockSpec((hp,) + fk.shape[1:], lambda b, h, i: (b * (H // hp) + h, 0, 0, 0))]
    return pl.pallas_call(
        functools.partial(_fox_body, tq=tq, tk=grp.tk, lq=grp.lq, hp=hp),
        grid=(B, H // hp, grp.lq // tq),
        in_specs=in_specs,
        out_specs=pl.BlockSpec((tq, w), lambda b, h, i: (b * (grp.lq // tq) + i, h)),
        out_shape=jax.ShapeDtypeStruct((B * grp.lq, H * HEAD_DIM), MXU_DTYPE),
        compiler_params=_params(3),
        name="fox_attention",
    )(proj, proj, proj, fq, fk)


def _diff_body(lam_ref, q_ref, k_ref, v_ref, g_ref, o_ref, *, tq, tk, lq, hp, out_scale):
    i = pl.program_id(2)
    d = C_V_DIM
    head = lambda x, u: x[:, u * d:(u + 1) * d]
    q = q_ref[...] * C_QK_DIM ** -0.5
    lane = lax.broadcasted_iota(I32, q.shape, 1) % (2 * C_QK_DIM)
    qs = (_mx(jnp.where(lane < C_QK_DIM, q, 0.0)), _mx(jnp.where(lane >= C_QK_DIM, q, 0.0)))
    carry = tuple(_init_carry(tq, d) for _ in range(2 * hp))

    def step(k, v, ok, c):
        out = []
        for u in range(hp):
            ku, vu = _mx(head(k, u)), _mx(head(v, u))
            for t in range(2):
                s = _dot_nt(head(qs[t], u), ku)
                out.append(_online_step(jnp.where(ok, s, NEG), vu, c[2 * u + t]))
        return tuple(out)

    qchunk = (i * tq + lax.broadcasted_iota(I32, (tq, tk), 0)) // CHUNK

    def new_step(j, c):
        off = pl.multiple_of(j * tk, tk)
        ok = (off + lax.broadcasted_iota(I32, (tq, tk), 1)) // CHUNK <= qchunk
        return step(k_ref[pl.ds(off, tk), :], v_ref[pl.ds(off, tk), :], ok, c)

    nb = _new_key_blocks(i, tq, lq, tk, _chunk_last_key(i, tq))
    carry = lax.fori_loop(0, nb, new_step, carry)
    for u in range(hp):
        (_, l0, a0), (_, l1, a1) = carry[2 * u], carry[2 * u + 1]
        o = a0 / l0 - lam_ref[0] * (a1 / l1)
        o = o * lax.rsqrt(jnp.mean(o * o, axis=-1, keepdims=True) + NORM_EPS)
        o_ref[:, u * d:(u + 1) * d] = (o * g_ref[...] * out_scale).astype(o_ref.dtype)


def diff_attention(grp, proj, lam, subln, out_scale):
    assert grp.lp == 0
    B, H, hp = grp.batch, C_HEADS, HEADS_PER_STEP
    w = hp * C_V_DIM
    tq = grp.tq
    qrow, krow = grp.qrow(tq), grp.krow()
    cq, ck, cv = (ODD_COLS[n] // w for n in ("qc", "kc", "vc"))
    in_specs = [pl.BlockSpec(memory_space=pltpu.SMEM),
                pl.BlockSpec((tq, w), lambda b, h, i: (qrow(b, i), cq + h)),
                pl.BlockSpec((grp.lq, w), lambda b, h, i: (krow(b), ck + h)),
                pl.BlockSpec((grp.lq, w), lambda b, h, i: (krow(b), cv + h))]
    in_specs += [pl.BlockSpec((1, C_V_DIM), lambda b, h, i: (0, 0))]
    return pl.pallas_call(
        functools.partial(_diff_body, tq=tq, tk=grp.tk, lq=grp.lq, hp=hp, out_scale=out_scale),
        grid=(B, H // hp, grp.lq // tq),
        in_specs=in_specs,
        out_specs=pl.BlockSpec((tq, w), lambda b, h, i: (b * (grp.lq // tq) + i, h)),
        out_shape=jax.ShapeDtypeStruct((B * grp.lq, H * C_V_DIM), MXU_DTYPE),
        compiler_params=_params(3),
        name="diff_attention",
    )(lam, proj, proj, proj, subln)


def _mla_prep_body(qd_ref, ckv_ref, qg_ref, kg_ref, w_ref, cos_ref, sin_ref, qa_ref, qr_ref, lat_ref, *, n_abs):
    qd = qd_ref[...]
    qn = qd * lax.rsqrt(jnp.mean(qd * qd, axis=-1, keepdims=True) + NORM_EPS) * qg_ref[...]
    q = _dot(qn, w_ref[...]) * (D_NOPE + D_ROPE) ** -0.5
    qa_ref[...] = q[:, :n_abs].astype(qa_ref.dtype)
    for h in range((q.shape[1] - n_abs) // LANES):
        lanes = slice(n_abs + h * LANES, n_abs + (h + 1) * LANES)
        qr_ref[:, h * LANES:(h + 1) * LANES] = _rope_lanes(
            q[:, lanes], cos_ref[...], sin_ref[...], D_ROPE // 2).astype(qr_ref.dtype)
    ckv = ckv_ref[...]
    lat_ref[...] = ckv * lax.rsqrt(jnp.mean(ckv * ckv, axis=-1, keepdims=True) + NORM_EPS) * kg_ref[...]


def mla_prepare(proj, q_norm, kv_norm, w_q, cos, sin, tm):
    T = proj.shape[0]
    q_lora, R = q_norm.shape[1], kv_norm.shape[1]
    n_abs = D_HEADS * R
    n_rope = D_HEADS * LANES
    row = lambda i: (i, 0)
    fixed = lambda i: (0, 0)
    return pl.pallas_call(
        functools.partial(_mla_prep_body, n_abs=n_abs),
        grid=(T // tm,),
        in_specs=[pl.BlockSpec((tm, q_lora), lambda i: (i, ODD_COLS["qd"] // q_lora)),
                  pl.BlockSpec((tm, R), lambda i: (i, ODD_COLS["ckv"] // R)),
                  pl.BlockSpec((1, q_lora), fixed),
                  pl.BlockSpec((1, R), fixed),
                  pl.BlockSpec(w_q.shape, fixed),
                  pl.BlockSpec((None, tm, LANES), lambda i: (1, i, 0)),
                  pl.BlockSpec((None, tm, LANES), lambda i: (1, i, 0))],
        out_specs=[pl.BlockSpec((tm, n_abs), row), pl.BlockSpec((tm, n_rope), row), pl.BlockSpec((tm, R), row)],
        out_shape=[jax.ShapeDtypeStruct((T, n_abs), MXU_DTYPE), jax.ShapeDtypeStruct((T, n_rope), MXU_DTYPE),
                   jax.ShapeDtypeStruct((T, R), F32)],
        compiler_params=_params(1),
        name="mla_prepare",
    )(proj, proj, q_norm, kv_norm, w_q, cos, sin)


def _mla_body(qa_ref, qr_ref, lat_ref, kr_ref, o_ref, *, tq, tk, lq, hp):
    i = pl.program_id(2)
    R = lat_ref.shape[-1]
    qa = [qa_ref[:, u * R:(u + 1) * R] for u in range(hp)]
    qr = [qr_ref[:, u * LANES:(u + 1) * LANES] for u in range(hp)]
    carry = tuple(_init_carry(tq, R) for _ in range(hp))

    qchunk = (i * tq + lax.broadcasted_iota(I32, (tq, tk), 0)) // CHUNK

    def new_step(j, c):
        off = pl.multiple_of(j * tk, tk)
        lat = _mx(lat_ref[pl.ds(off, tk), :])
        kr = _mx(kr_ref[pl.ds(off, tk), :])
        ok = (off + lax.broadcasted_iota(I32, (tq, tk), 1)) // CHUNK <= qchunk
        return tuple(_online_step(jnp.where(ok, _dot_nt(qa[u], lat) + _dot_nt(qr[u], kr), NEG), lat, c[u])
                     for u in range(hp))

    nb = _new_key_blocks(i, tq, lq, tk, _chunk_last_key(i, tq))
    carry = lax.fori_loop(0, nb, new_step, carry)
    for u, (m, l, acc) in enumerate(carry):
        o_ref[:, u * R:(u + 1) * R] = (acc / l).astype(o_ref.dtype)


def mla_attention(grp, q_abs, q_rope, lat, proj):
    assert grp.lp == 0
    B, H, hp = grp.batch, D_HEADS, HEADS_PER_STEP
    R = lat.shape[1]
    tq = grp.tq
    qrow, krow = grp.qrow(tq), grp.krow()
    in_specs = [pl.BlockSpec((tq, hp * R), lambda b, h, i: (qrow(b, i), h)),
                pl.BlockSpec((tq, hp * LANES), lambda b, h, i: (qrow(b, i), h)),
                pl.BlockSpec((grp.lq, R), lambda b, h, i: (krow(b), 0)),
                pl.BlockSpec((grp.lq, LANES), lambda b, h, i: (krow(b), ODD_COLS["kr"] // LANES))]
    return pl.pallas_call(
        functools.partial(_mla_body, tq=tq, tk=grp.tk, lq=grp.lq, hp=hp),
        grid=(B, H // hp, grp.lq // tq),
        in_specs=in_specs,
        out_specs=pl.BlockSpec((tq, hp * R), lambda b, h, i: (b * (grp.lq // tq) + i, h)),
        out_shape=jax.ShapeDtypeStruct((B * grp.lq, H * R), MXU_DTYPE),
        compiler_params=_params(3),
        name="mla_attention",
    )(q_abs, q_rope, lat, proj)


def _head_cols(ref, h, d, heads):
    if ref.shape[-1] == d:
        return ref[pl.ds(h, ref.shape[0] // heads, stride=heads), :]
    return ref[:, h * d:(h + 1) * d]


def _chain_reset(j, m_ref, l_ref, acc_ref):
    @pl.when(j == 0)
    def _():
        m_ref[...] = jnp.full(m_ref.shape, NEG, F32)
        l_ref[...] = jnp.zeros(l_ref.shape, F32)
        acc_ref[...] = jnp.zeros(acc_ref.shape, F32)


def _chains_advance(scores_values, m_ref, l_ref, acc_ref):
    states = [(m_ref[c], l_ref[c], acc_ref[c]) for c in range(len(scores_values))]
    new = [_online_step(s, v, st) for (s, v), st in zip(scores_values, states)]
    for c, (m, l, acc) in enumerate(new):
        m_ref[c], l_ref[c], acc_ref[c] = m, l, acc
    return [(l, acc) for _, l, acc in new]


def _past_key_spec(cache, layer, tkp, n_past, rows_per_key=1):
    blk = lambda j: jnp.minimum(j, n_past - 1)
    rows, w = tkp * rows_per_key, cache.shape[-1]
    if layer is None:
        return pl.BlockSpec((None, rows, w), lambda b, j: (b, blk(j), 0))
    return pl.BlockSpec((None, None, rows, w), lambda b, j: (layer, b, blk(j), 0))


def _keys_by_head(cache):
    n, b, p, h, d = cache.shape
    return cache.reshape(n, b, p * h, d)


def _past_call(body, grp, heads, dv, chains, in_specs, args, name):
    lq = grp.lq
    assert grp.lq == grp.tk
    return pl.pallas_call(
        body,
        grid=(grp.batch, grp.n_past_blocks + 1),
        in_specs=in_specs,
        out_specs=pl.BlockSpec((lq, heads * dv), lambda b, j: (b, 0)),
        out_shape=jax.ShapeDtypeStruct((grp.batch * lq, heads * dv), MXU_DTYPE),
        scratch_shapes=[pltpu.VMEM((chains, lq, 1), F32), pltpu.VMEM((chains, lq, 1), F32),
                        pltpu.VMEM((chains, lq, dv), F32)],
        compiler_params=_params(2),
        name=name,
    )(*args)


def _fox_past_body(q_ref, k_ref, v_ref, pk_ref, pv_ref, fq_ref, fk_ref, o_ref, m_ref, l_ref, acc_ref, *, n_past):
    j = pl.program_id(1)
    d, lq = HEAD_DIM, q_ref.shape[0]
    _chain_reset(j, m_ref, l_ref, acc_ref)
    q = lambda h: _mx(q_ref[:, h * d:(h + 1) * d] * d ** -0.5)

    @pl.when(j < n_past)
    def _():
        _chains_advance([(_dot_nt(q(h), _head_cols(pk_ref, h, d, B_HEADS)) + fq_ref[h] - fk_ref[h, 0],
                          _head_cols(pv_ref, h, d, B_HEADS)) for h in range(B_HEADS)], m_ref, l_ref, acc_ref)

    @pl.when(j == n_past)
    def _():
        causal = lax.broadcasted_iota(I32, (lq, lq), 1) <= lax.broadcasted_iota(I32, (lq, lq), 0)
        logits = lambda h: _dot_nt(q(h), k_ref[:, h * d:(h + 1) * d]) + fq_ref[h] - fk_ref[h, 0][:, :lq]
        done = _chains_advance([(jnp.where(causal, logits(h), NEG), v_ref[:, h * d:(h + 1) * d])
                                for h in range(B_HEADS)], m_ref, l_ref, acc_ref)
        for h, (l, acc) in enumerate(done):
            o_ref[:, h * d:(h + 1) * d] = (acc / l).astype(o_ref.dtype)


def fox_attention_past(grp, proj, fq, fk, cache_k, cache_v, layer):
    H, d, lq = B_HEADS, HEAD_DIM, grp.lq
    row = grp.krow()
    w = H * d
    in_specs = [pl.BlockSpec((lq, w), lambda b, j, n=n: (row(b), EVEN_COLS[n] // w)) for n in ("qf", "kf", "vf")]
    cache_k, cache_v = _keys_by_head(cache_k), _keys_by_head(cache_v)
    in_specs += [_past_key_spec(cache_k, layer, grp.tkp, grp.n_past_blocks, H),
                 _past_key_spec(cache_v, layer, grp.tkp, grp.n_past_blocks, H),
                 pl.BlockSpec((H, lq, 1), lambda b, j: (b, 0, 0)),
                 pl.BlockSpec((H, 1, 1, fk.shape[-1]), lambda b, j: (b, j, 0, 0))]
    return _past_call(functools.partial(_fox_past_body, n_past=grp.n_past_blocks), grp, H, d, H, in_specs,
                      [proj, proj, proj, cache_k, cache_v, fq, fk], "fox_attention_past")


def _diff_past_body(lam_ref, q_ref, k_ref, v_ref, pk_ref, pv_ref, g_ref, o_ref, m_ref, l_ref, acc_ref,
                    *, n_past, out_scale):
    j = pl.program_id(1)
    d = C_V_DIM
    _chain_reset(j, m_ref, l_ref, acc_ref)
    lane = lax.broadcasted_iota(I32, (q_ref.shape[0], d), 1)

    def maps(h):
        q = q_ref[:, h * d:(h + 1) * d] * C_QK_DIM ** -0.5
        return _mx(jnp.where(lane < C_QK_DIM, q, 0.0)), _mx(jnp.where(lane >= C_QK_DIM, q, 0.0))

    def both_maps(key_of, value_of):
        items = []
        for h in range(C_HEADS):
            k, v = _mx(key_of(h)), _mx(value_of(h))
            items += [(_dot_nt(qt, k), v) for qt in maps(h)]
        return _chains_advance(items, m_ref, l_ref, acc_ref)

    @pl.when(j < n_past)
    def _():
        both_maps(lambda h: _head_cols(pk_ref, h, d, C_HEADS), lambda h: _head_cols(pv_ref, h, d, C_HEADS))

    @pl.when(j == n_past)
    def _():
        done = both_maps(lambda h: k_ref[:, h * d:(h + 1) * d], lambda h: v_ref[:, h * d:(h + 1) * d])
        for h in range(C_HEADS):
            (l0, a0), (l1, a1) = done[2 * h], done[2 * h + 1]
            o = a0 / l0 - lam_ref[0] * (a1 / l1)
            o = o * lax.rsqrt(jnp.mean(o * o, axis=-1, keepdims=True) + NORM_EPS)
            o_ref[:, h * d:(h + 1) * d] = (o * g_ref[...] * out_scale).astype(o_ref.dtype)


def diff_attention_past(grp, proj, lam, subln, out_scale, cache_k, cache_v, layer):
    H, d, lq = C_HEADS, C_V_DIM, grp.lq
    assert lq <= CHUNK and grp.lp % CHUNK == 0
    row = grp.krow()
    w = H * d
    in_specs = [pl.BlockSpec(memory_space=pltpu.SMEM)]
    in_specs += [pl.BlockSpec((lq, w), lambda b, j, n=n: (row(b), ODD_COLS[n] // w)) for n in ("qc", "kc", "vc")]
    cache_v = _keys_by_head(cache_v)
    in_specs += [_past_key_spec(cache_k, None, grp.tkp, grp.n_past_blocks),
                 _past_key_spec(cache_v, layer, grp.tkp, grp.n_past_blocks, H),
                 pl.BlockSpec((1, d), lambda b, j: (0, 0))]
    return _past_call(functools.partial(_diff_past_body, n_past=grp.n_past_blocks, out_scale=out_scale), grp, H, d,
                      2 * H, in_specs, [lam, proj, proj, proj, cache_k, cache_v, subln], "diff_attention_past")


def _mla_past_body(qa_ref, qr_ref, lat_ref, kr_ref, plat_ref, pkr_ref, o_ref, m_ref, l_ref, acc_ref, *, n_past):
    j = pl.program_id(1)
    R = lat_ref.shape[-1]
    _chain_reset(j, m_ref, l_ref, acc_ref)

    @pl.when(j < n_past)
    def _():
        lat, kr = _mx(plat_ref[...]), _mx(pkr_ref[...])
        _chains_advance([(_dot_nt(qa_ref[:, h * R:(h + 1) * R], lat)
                          + _dot_nt(qr_ref[:, h * LANES:h * LANES + D_ROPE], kr), lat) for h in range(D_HEADS)],
                        m_ref, l_ref, acc_ref)

    @pl.when(j == n_past)
    def _():
        lat, kr = _mx(lat_ref[...]), _mx(kr_ref[...])
        done = _chains_advance([(_dot_nt(qa_ref[:, h * R:(h + 1) * R], lat)
                                 + _dot_nt(qr_ref[:, h * LANES:(h + 1) * LANES], kr), lat) for h in range(D_HEADS)],
                               m_ref, l_ref, acc_ref)
        for h, (l, acc) in enumerate(done):
            o_ref[:, h * R:(h + 1) * R] = (acc / l).astype(o_ref.dtype)


def mla_attention_past(grp, q_abs, q_rope, lat, proj, cache_lat, cache_kr, layer):
    H, lq = D_HEADS, grp.lq
    assert lq <= CHUNK and grp.lp % CHUNK == 0
    R = lat.shape[1]
    row = grp.krow()
    in_specs = [pl.BlockSpec((lq, H * R), lambda b, j: (row(b), 0)),
                pl.BlockSpec((lq, H * LANES), lambda b, j: (row(b), 0)),
                pl.BlockSpec((lq, R), lambda b, j: (row(b), 0)),
                pl.BlockSpec((lq, LANES), lambda b, j: (row(b), ODD_COLS["kr"] // LANES)),
                _past_key_spec(cache_lat, layer, grp.tkp, grp.n_past_blocks),
                _past_key_spec(cache_kr, layer, grp.tkp, grp.n_past_blocks)]
    return _past_call(functools.partial(_mla_past_body, n_past=grp.n_past_blocks), grp, H, R, H, in_specs,
                      [q_abs, q_rope, lat, proj, cache_lat, cache_kr], "mla_attention_past")


def _sortable(score):
    bits = pltpu.bitcast(score, I32)
    return jnp.where(bits < 0, bits ^ jnp.int32(0x7FFFFFFF), bits)


def _dsa_body(*refs, tq, i0, nk, topk, has_past):
    if has_past:
        qi_ref, misc_ref, q_ref, ki_ref, k_ref, v_ref, pki_ref, pk_ref, pv_ref, o_ref, *sel_refs = refs
    else:
        qi_ref, misc_ref, q_ref, ki_ref, k_ref, v_ref, o_ref, *sel_refs = refs
    i = i0 + pl.program_id(1)
    wi = misc_ref[:, MISC_WI:MISC_WI + IDX_HEADS] * (IDX_HEADS * IDX_DIM) ** -0.5

    qchunk = (i * tq + lax.broadcasted_iota(I32, (tq, nk), 0)) // CHUNK
    adm_new = lax.broadcasted_iota(I32, (tq, nk), 1) // CHUNK <= qchunk
    segs = [(_mx(ki_ref[...]), k_ref, v_ref, adm_new)]
    if has_past:
        segs.insert(0, (_mx(pki_ref[0]), pk_ref.at[0], pv_ref.at[0], None))

    keys = []
    for ki, _, _, adm in segs:
        score = jnp.zeros((tq, ki.shape[0]), F32)
        for h in range(IDX_HEADS):
            s = _dot_nt(qi_ref[:, h * IDX_DIM:(h + 1) * IDX_DIM], ki)
            score = score + jnp.maximum(s, 0.0) * wi[:, h:h + 1]
        key = _sortable(jnp.where(score == 0.0, 0.0, score))
        keys.append(key if adm is None else jnp.where(adm, key, INT_MIN))

    def count(pred):
        return sum(jnp.sum(jnp.where(pred(key), 1.0, 0.0), axis=-1, keepdims=True) for key in keys)

    kf = float(topk)
    n_ge = count(lambda k: k >= 0)
    t = jnp.where(n_ge >= kf, 0, INT_MIN).astype(I32)

    def bit_step(it, carry):
        t, n_ge = carry
        cand = t | jnp.left_shift(jnp.int32(1), 30 - it)
        n_cand = count(lambda k: k >= cand)
        fits = n_cand >= kf
        return jnp.where(fits, cand, t), jnp.where(fits, n_cand, n_ge)

    t, n_ge = lax.fori_loop(0, 31, bit_step, (t, n_ge))
    for sel_ref, key, (_, _, _, adm) in zip(sel_refs, keys, segs):
        sel = (key >= t) if adm is None else (adm & (key >= t))
        sel_ref[...] = jnp.where(sel, 1, 0)

    bounded = t > INT_MIN
    crowded = jnp.logical_and(bounded, n_ge > kf)

    @pl.when(jnp.max(jnp.where(crowded, 1.0, 0.0)) > 0.0)
    def _():
        places = jnp.where(bounded, kf - count(lambda k: k > t), float(2 ** 30))
        seen = jnp.zeros((tq, 1), F32)
        for sel_ref, key, (_, _, _, adm) in zip(sel_refs, keys, segs):
            n = key.shape[1]
            blk = min(n, KEY_BLOCK)
            upper = jnp.where(lax.broadcasted_iota(I32, (blk, blk), 0) <= lax.broadcasted_iota(I32, (blk, blk), 1),
                              1.0, 0.0)
            for c in range(n // blk):
                kb = key[:, c * blk:(c + 1) * blk]
                tie = kb == t
                tie_f = jnp.where(tie, 1.0, 0.0)
                ahead = _dot(tie_f, upper) + seen
                sel = (kb > t) | (tie & (ahead <= places))
                if adm is not None:
                    sel = sel & adm[:, c * blk:(c + 1) * blk]
                sel_ref[:, c * blk:(c + 1) * blk] = jnp.where(sel, 1, 0)
                seen = seen + jnp.sum(tie_f, axis=-1, keepdims=True)

    sels = [sel_ref[...] > 0 for sel_ref in sel_refs]

    rep = A_HEADS // A_KV_HEADS
    for g in range(A_KV_HEADS):
        kv = [(_mx(k[:, g * HEAD_DIM:(g + 1) * HEAD_DIM]), _mx(v[:, g * HEAD_DIM:(g + 1) * HEAD_DIM]))
              for _, k, v, _ in segs]
        for r in range(rep):
            h = g * rep + r
            q = _mx(q_ref[:, h * HEAD_DIM:(h + 1) * HEAD_DIM] * HEAD_DIM ** -0.5)
            ss = [jnp.where(sel, _dot_nt(q, k), NEG) for sel, (k, _) in zip(sels, kv)]
            m = functools.reduce(jnp.maximum, [jnp.max(s, axis=-1, keepdims=True) for s in ss])
            ps = [jnp.exp(s - m) for s in ss]
            l = sum(jnp.sum(p, axis=-1, keepdims=True) for p in ps)
            o = sum(_dot(p, v) for p, (_, v) in zip(ps, kv))
            o_ref[:, h * HEAD_DIM:(h + 1) * HEAD_DIM] = (o / l).astype(o_ref.dtype)


def dsa_attention(grp, proj, past_ki, past_k, past_v):
    B = grp.batch
    tq = grp.tq_dsa
    n_tiles = grp.lq // tq
    wq, wkv = A_HEADS * HEAD_DIM, A_KV_HEADS * HEAD_DIM
    wqi = IDX_HEADS * IDX_DIM
    c = EVEN_COLS

    extents = sorted({grp.lq // f for f in (1, 2, 4) if grp.lq % f == 0 and (grp.lq // f) % tq == 0
                      and grp.row0 % (grp.lq // f) == 0})
    classes = []
    for i in range(n_tiles):
        need = min(grp.lq, _chunk_last_key(i, tq) + 1)
        nk = min(e for e in extents if e >= need)
        if classes and classes[-1][2] == nk:
            classes[-1][1] += 1
        else:
            classes.append([i, 1, nk])

    outs = []
    for i0, n_i, nk in classes:
        qrow = lambda b, i, i0=i0: grp.row0 // tq + b * n_tiles + i0 + i
        krow = lambda b, nk=nk: (grp.row0 + b * grp.lq) // nk
        in_specs = [pl.BlockSpec((tq, wqi), lambda b, i, f=qrow: (f(b, i), c["qi"] // wqi)),
                    pl.BlockSpec((tq, LANES), lambda b, i, f=qrow: (f(b, i), c["misc"] // LANES)),
                    pl.BlockSpec((tq, wq), lambda b, i, f=qrow: (f(b, i), c["qa"] // wq)),
                    pl.BlockSpec((nk, IDX_DIM), lambda b, i, f=krow: (f(b), c["ki"] // IDX_DIM)),
                    pl.BlockSpec((nk, wkv), lambda b, i, f=krow: (f(b), c["ka"] // wkv)),
                    pl.BlockSpec((nk, wkv), lambda b, i, f=krow: (f(b), c["va"] // wkv))]
        args = [proj] * 6
        if grp.lp:
            in_specs += [pl.BlockSpec((1, grp.lp, IDX_DIM), lambda b, i: (b, 0, 0)),
                         pl.BlockSpec((1, grp.lp, wkv), lambda b, i: (b, 0, 0)),
                         pl.BlockSpec((1, grp.lp, wkv), lambda b, i: (b, 0, 0))]
            args += [past_ki, past_k, past_v]
        o = pl.pallas_call(
            functools.partial(_dsa_body, tq=tq, i0=i0, nk=nk, topk=grp.topk, has_past=bool(grp.lp)),
            grid=(B, n_i),
            in_specs=in_specs,
            out_specs=pl.BlockSpec((tq, wq), lambda b, i, n_i=n_i: (b * n_i + i, 0)),
            out_shape=jax.ShapeDtypeStruct((B * n_i * tq, wq), MXU_DTYPE),
            scratch_shapes=[pltpu.VMEM((tq, n), I32) for n in ([grp.lp] if grp.lp else []) + [nk]],
            compiler_params=_params(2),
            name="dsa_attention",
        )(*args)
        outs.append(o.reshape(B, n_i * tq, wq))
    return jnp.concatenate(outs, axis=1).reshape(B * grp.lq, wq)


def _first_index_of_max(x, iota, n):
    mx = jnp.max(x, axis=0, keepdims=True)
    return mx, jnp.min(jnp.where(x == mx, iota, n), axis=0, keepdims=True)


def _route_body(x_ref, sc_ref, sh_ref, wr_ref, bias_ref, hp_ref, gate_ref, rank_ref, cnt_ref, h_ref, carry_ref,
                *, group):
    step = pl.program_id(0)
    tm = x_ref.shape[0]
    E = wr_ref.shape[0]
    per = E // N_GROUPS
    for g, part in enumerate(_modulated(x_ref, sc_ref, sh_ref, group)):
        h_ref[g * group:(g + 1) * group, :] = part
    h = h_ref[...]
    hp_ref[...] = _pack_rows(h)

    scores = jax.nn.sigmoid(_dot_nt(wr_ref[...], h, precision=lax.Precision.HIGHEST))
    biased = scores + bias_ref[...]
    member = lax.broadcasted_iota(I32, (per, tm), 0).astype(F32)
    gscore = []
    for g in range(N_GROUPS):
        blk = biased[g * per:(g + 1) * per, :]
        m1, i1 = _first_index_of_max(blk, member, per)
        m2 = jnp.max(jnp.where(member == i1, -jnp.inf, blk), axis=0, keepdims=True)
        gscore.append(m1 + m2)
    gscore = jnp.concatenate(gscore, axis=0)
    giota = lax.broadcasted_iota(I32, (N_GROUPS, tm), 0).astype(F32)
    gsel = jnp.zeros((N_GROUPS, tm), F32)
    for _ in range(TOPK_GROUPS):
        _, gi = _first_index_of_max(gscore, giota, N_GROUPS)
        hit = giota == gi
        gsel = jnp.where(hit, 1.0, gsel)
        gscore = jnp.where(hit, -jnp.inf, gscore)
    emask = jnp.concatenate([jnp.broadcast_to(gsel[g:g + 1, :], (per, tm)) for g in range(N_GROUPS)], axis=0)
    masked = jnp.where(emask > 0.0, biased, -jnp.inf)
    eiota = lax.broadcasted_iota(I32, (E, tm), 0).astype(F32)
    sel = jnp.zeros((E, tm), jnp.bool_)
    for _ in range(TOP_K):
        _, ei = _first_index_of_max(masked, eiota, E)
        hit = eiota == ei
        sel = sel | hit
        masked = jnp.where(hit, -jnp.inf, masked)
    w = jnp.where(sel, scores, 0.0)
    gate_ref[...] = w / jnp.sum(w, axis=0, keepdims=True) * ROUTED_SCALE

    @pl.when(step == 0)
    def _():
        carry_ref[...] = jnp.zeros(carry_ref.shape, F32)
    upper = (lax.broadcasted_iota(I32, (tm, tm), 0) <= lax.broadcasted_iota(I32, (tm, tm), 1))
    self = jnp.where(sel, 1.0, 0.0)
    incl = _dot(self, jnp.where(upper, 1.0, 0.0)) + carry_ref[...]
    rank_ref[...] = jnp.where(sel, incl - 1.0, -1.0).astype(I32)
    carry_ref[...] = incl[:, tm - 1:tm]
    cnt_ref[...] = incl[:, tm - 1:tm].astype(I32)


def route(x, scale_g, shift_g, w_router_t, r_bias, group, tm):
    T, D = x.shape
    E = w_router_t.shape[0]
    row = lambda i: (i, 0)
    col = lambda i: (0, i)
    fixed = lambda i: (0, 0)
    return pl.pallas_call(
        functools.partial(_route_body, group=group),
        grid=(T // tm,),
        in_specs=[pl.BlockSpec((tm, D), row),
                  pl.BlockSpec((1, tm // group, D), lambda i: (i, 0, 0)),
                  pl.BlockSpec((1, tm // group, D), lambda i: (i, 0, 0)),
                  pl.BlockSpec((E, D), fixed),
                  pl.BlockSpec((E, 1), fixed)],
        out_specs=[pl.BlockSpec((tm, D // 2), row), pl.BlockSpec((E, tm), col), pl.BlockSpec((E, tm), col),
                   pl.BlockSpec((E, 1), fixed)],
        out_shape=[jax.ShapeDtypeStruct((T, D // 2), I32), jax.ShapeDtypeStruct((E, T), F32),
                   jax.ShapeDtypeStruct((E, T), I32), jax.ShapeDtypeStruct((E, 1), I32)],
        scratch_shapes=[pltpu.VMEM((tm, D), F32), pltpu.VMEM((E, 1), F32)],
        compiler_params=_params(1),
        name="route",
    )(x, _per_tile(scale_g, tm, group), _per_tile(shift_g, tm, group), w_router_t, r_bias)


def _compact_body(gate_ref, rank_ref, off_ref, dest_ref, g8_ref):
    E, tm = gate_ref.shape
    rank = rank_ref[...]
    sel = rank >= 0
    dest = (rank + off_ref[...]).astype(F32)
    gate = gate_ref[...]
    eiota = lax.broadcasted_iota(I32, (E, tm), 0).astype(F32)
    dests, gates = [], []
    for _ in range(TOP_K):
        ei = jnp.min(jnp.where(sel, eiota, E), axis=0, keepdims=True)
        hit = eiota == ei
        dests.append(jnp.sum(jnp.where(hit, dest, 0.0), axis=0, keepdims=True))
        gates.append(jnp.sum(jnp.where(hit, gate, 0.0), axis=0, keepdims=True))
        sel = sel & jnp.logical_not(hit)
    dest_ref[...] = jnp.concatenate(dests, axis=0).astype(I32)
    g8_ref[...] = jnp.concatenate(gates, axis=0)


def compact_routes(gate, rank, seg_off, tm):
    E, T = gate.shape
    col = lambda i: (0, i)
    return pl.pallas_call(
        _compact_body,
        grid=(T // tm,),
        in_specs=[pl.BlockSpec((E, tm), col), pl.BlockSpec((E, tm), col), pl.BlockSpec((E, 1), lambda i: (0, 0))],
        out_specs=[pl.BlockSpec((TOP_K, tm), col), pl.BlockSpec((TOP_K, tm), col)],
        out_shape=[jax.ShapeDtypeStruct((TOP_K, T), I32), jax.ShapeDtypeStruct((TOP_K, T), F32)],
        compiler_params=_params(1),
        name="compact_routes",
    )(gate, rank, seg_off)


def _ffn(x, w1, w3, w2):
    a = _dot(x, w1)
    return _dot(a * jax.nn.sigmoid(a) * _dot(x, w3), w2)


def _gmm_body(te_ref, nt_ref, tok_ref, tok_next_ref, slot_ref, x_hbm, w1_ref, w3_ref, w2_ref, y_hbm,
              w1b, w3b, w2b, xbuf, ybuf, in_sem, out_sem, *, tm, parts):
    i = pl.program_id(0)
    n_used = nt_ref[0]
    cur, nxt = i % 2, (i + 1) % 2

    def row_in(tok, r, buf):
        return pltpu.make_async_copy(x_hbm.at[pl.ds(tok[0, 0, r], 1)], xbuf.at[buf, pl.ds(r, 1)], in_sem.at[buf])

    def row_out(r, buf):
        return pltpu.make_async_copy(ybuf.at[buf, pl.ds(r, 1)], y_hbm.at[pl.ds(slot_ref[0, 0, r], 1)], out_sem.at[buf])

    def wait_in(buf):
        pltpu.make_async_copy(x_hbm.at[pl.ds(0, tm)], xbuf.at[buf], in_sem.at[buf]).wait()

    def wait_out(buf):
        pltpu.make_async_copy(ybuf.at[buf], y_hbm.at[pl.ds(0, tm)], out_sem.at[buf]).wait()

    @pl.when(i == 0)
    def _():
        for r in range(tm):
            row_in(tok_ref, r, 0).start()
        ybuf[1] = jnp.zeros((tm, ybuf.shape[2]), I32)
        spare = pltpu.make_async_copy(ybuf.at[1], y_hbm.at[pl.ds(y_hbm.shape[0] - tm, tm)], out_sem.at[1])
        spare.start()
        spare.wait()

    @pl.when(jnp.logical_or(i == 0, te_ref[i] != te_ref[jnp.maximum(i - 1, 0)]))
    def _():
        w1b[...] = _mx(w1_ref[...])
        w3b[...] = _mx(w3_ref[...])
        w2b[...] = _mx(w2_ref[...])

    @pl.when(i < n_used)
    def _():
        wait_in(cur)
        for r in range(tm):
            row_in(tok_next_ref, r, nxt).start()
        rows = tm // parts
        for p in range(parts):
            x = _unpack_rows(xbuf[cur, p * rows:(p + 1) * rows, :])
            ybuf[cur, p * rows:(p + 1) * rows, :] = _pack_rows(_ffn(x, w1b[...], w3b[...], w2b[...]))
            for r in range(p * rows, (p + 1) * rows):
                row_out(r, cur).start()

    @pl.when(jnp.logical_and(i >= 1, i < n_used))
    def _():
        wait_out(nxt)

    @pl.when(i == n_used - 1)
    def _():
        wait_in(nxt)
        wait_out(cur)


def grouped_ffn(tile_expert, n_tiles, row_token, row_slot, x, w1, w3, w2, layer, n_out, tm):
    n_rows, W = row_token.shape[0] * tm, x.shape[1]
    D, F = w1.shape[-2], w1.shape[-1]
    clamp = lambda i, nt: jnp.maximum(jnp.minimum(i, nt[0] - 1), 0)
    wsel = lambda i, te, nt: (layer, te[i], 0, 0)
    smem_rows = lambda f: pl.BlockSpec((1, 1, tm), f, memory_space=pltpu.SMEM)
    return pl.pallas_call(
        functools.partial(_gmm_body, tm=tm, parts=2),
        grid_spec=pltpu.PrefetchScalarGridSpec(
            num_scalar_prefetch=2,
            grid=(n_rows // tm,),
            in_specs=[smem_rows(lambda i, te, nt: (clamp(i, nt), 0, 0)),
                      smem_rows(lambda i, te, nt: (clamp(i + 1, nt), 0, 0)),
                      smem_rows(lambda i, te, nt: (clamp(i, nt), 0, 0)),
                      pl.BlockSpec(memory_space=pl.ANY),
                      pl.BlockSpec((None, None, D, F), wsel),
                      pl.BlockSpec((None, None, D, F), wsel),
                      pl.BlockSpec((None, None, F, D), wsel)],
            out_specs=pl.BlockSpec(memory_space=pl.ANY),
            scratch_shapes=[pltpu.VMEM((D, F), MXU_DTYPE), pltpu.VMEM((D, F), MXU_DTYPE),
                            pltpu.VMEM((F, D), MXU_DTYPE),
                            pltpu.VMEM((2, tm, W), I32), pltpu.VMEM((2, tm, W), I32),
                            pltpu.SemaphoreType.DMA((2,)), pltpu.SemaphoreType.DMA((2,))]),
        out_shape=jax.ShapeDtypeStruct((n_out, W), I32),
        compiler_params=pltpu.CompilerParams(dimension_semantics=("arbitrary",), vmem_limit_bytes=VMEM_LIMIT_BYTES,
                                             disable_bounds_checks=True),
        name="grouped_ffn",
    )(tile_expert, n_tiles, row_token, row_token, row_slot, x, w1, w3, w2)


def _close_moe_body(h_ref, *rest, group, alpha, n_k):
    yk_refs, (g8_ref, ws1_ref, ws3_ref, ws2_ref, x_ref, gate_ref, g_ref, b_ref, o_ref) = rest[:n_k], rest[n_k:]
    y = _ffn(_unpack_rows(h_ref[...]), ws1_ref[...], ws3_ref[...], ws2_ref[...])
    g8 = g8_ref[...]
    for k, yk_ref in enumerate(yk_refs):
        y = y + g8[:, k:k + 1] * _unpack_rows(yk_ref[...])
    r = _gated_residual(x_ref, y, gate_ref, group, alpha)
    o_ref[...] = _layer_norm_rows(r, g_ref[...], b_ref[...])


def close_moe(h, y_rows, g8, ws1, ws3, ws2, x, gate_g, ln_g, ln_b, group, alpha, tm):
    T, D = x.shape
    K = g8.shape[1]
    W = h.shape[1]
    row = lambda i: (i, 0)
    fixed = lambda i: (0, 0)
    return pl.pallas_call(
        functools.partial(_close_moe_body, group=group, alpha=alpha, n_k=K),
        grid=(T // tm,),
        in_specs=[pl.BlockSpec((tm, W), row)] +
                 [pl.BlockSpec((tm, W), lambda i, k=k: (k * (T // tm) + i, 0)) for k in range(K)] +
                 [pl.BlockSpec((tm, K), row),
                  pl.BlockSpec(ws1.shape, fixed),
                  pl.BlockSpec(ws3.shape, fixed),
                  pl.BlockSpec(ws2.shape, fixed),
                  pl.BlockSpec((tm, D), row),
                  pl.BlockSpec((1, tm // group, D), lambda i: (i, 0, 0)),
                  pl.BlockSpec((1, D), fixed),
                  pl.BlockSpec((1, D), fixed)],
        out_specs=pl.BlockSpec((tm, D), row),
        out_shape=jax.ShapeDtypeStruct((T, D), F32),
        compiler_params=_params(1),
        name="close_moe",
    )(h, *([y_rows] * K), g8, ws1, ws3, ws2, x, _per_tile(gate_g, tm, group), ln_g, ln_b)


def _rope_tables(pos, half):
    inv_freq = jnp.power(ROPE_THETA, -jnp.arange(half, dtype=F32) / half)
    ang = pos.astype(F32)[:, None] * inv_freq[None, :]
    cos, sin = jnp.cos(ang), jnp.sin(ang)
    reps = LANES // (2 * half)
    cos = jnp.tile(jnp.concatenate([cos, cos], axis=1), (1, reps))
    sin = jnp.tile(jnp.concatenate([-sin, sin], axis=1), (1, reps))
    return jnp.stack([jnp.ones_like(cos), cos]), jnp.stack([jnp.zeros_like(sin), sin])


def _place_cols(w, sizes, names, cols, width):
    out = jnp.zeros((w.shape[0], width), w.dtype)
    o = 0
    for size, name in zip(sizes, names):
        if name is not None:
            dst, sub = name
            out = lax.dynamic_update_slice(out, w[:, o:o + size], (0, cols[dst] + sub))
        o += size
    return out


def _moe(x, scale_g, shift_g, gate_g, ln_g, ln_b, w_router, r_bias, w1, w3, w2, ws1, ws3, ws2, layer, group, alpha):
    T, D = x.shape
    E = w1.shape[1]
    tm = 256
    t_route = _pick(T, 768, math.lcm(LANES, group))
    h, gate, rank, counts = route(x, scale_g, shift_g, w_router.T, r_bias.reshape(E, 1), group, t_route)

    counts = counts[:, 0]
    tiles_per = (counts + tm - 1) // tm
    tile_end = jnp.cumsum(tiles_per)
    seg_off = ((tile_end - tiles_per) * tm).astype(I32)
    n_tiles = (T * TOP_K) // tm + E
    used = tile_end[-1].astype(I32)
    tile_ids = jnp.minimum(jnp.arange(n_tiles, dtype=I32), used - 1)
    tile_expert = jnp.sum((tile_end[None, :] <= tile_ids[:, None]).astype(I32), axis=1)

    dest, g8 = compact_routes(gate, rank, seg_off.reshape(E, 1), t_route)
    n_pairs = TOP_K * T
    pair_slot = jnp.arange(n_pairs, dtype=I32).reshape(TOP_K, T)
    spare = n_pairs + jnp.arange(n_tiles * tm, dtype=I32) % tm
    row_slot = spare.at[dest.reshape(-1)].set(pair_slot.reshape(-1), unique_indices=True, mode="promise_in_bounds")
    row_token = jnp.where(row_slot < n_pairs, row_slot % T, 0)
    y_rows = grouped_ffn(tile_expert, used.reshape(1), row_token.reshape(n_tiles, 1, tm),
                         row_slot.reshape(n_tiles, 1, tm), h, w1, w3, w2, layer, n_pairs + tm, tm)
    return close_moe(h, y_rows, g8.T, _mx(ws1), _mx(ws3), _mx(ws2), x, gate_g, ln_g, ln_b, group, alpha,
                     _pick(T, 256, math.lcm(SUBLANES, group)))


def _fox_gates(grp, proj, b_f, past_logf):
    B, H = grp.batch, B_HEADS
    c0 = EVEN_COLS["misc"] + MISC_FL
    fl = proj[grp.row0:grp.row0 + B * grp.lq, c0:c0 + H].reshape(B, grp.lq, H)
    logf = jax.nn.log_sigmoid(fl + b_f)
    logf_all = logf if past_logf is None else jnp.concatenate([past_logf, logf], axis=1)
    cum = jnp.moveaxis(jnp.cumsum(logf_all, axis=1), 1, 2).reshape(B * H, grp.lk)
    fq = cum[:, grp.lp:, None]
    new = cum[:, grp.lp:].reshape(B * H, grp.lq // grp.tk, 1, grp.tk)
    new = jnp.pad(new, ((0, 0), (0, 0), (0, 0), (0, KEY_BLOCK - grp.tk)))
    if grp.lp:
        past = cum[:, :grp.lp].reshape(B * H, grp.n_past_blocks, 1, grp.tkp)
        past = jnp.pad(past, ((0, 0), (0, 0), (0, 0), (0, KEY_BLOCK - grp.tkp)))
        new = jnp.concatenate([past, new], axis=1)
    return logf, fq, new


def kernel(x_prompt, x_sample, c_prompt, c_sample, cache_a_k, cache_a_v, cache_a_kidx, cache_b_k, cache_b_v, cache_b_logf, cache_c_k, cache_c_v, cache_d_latent, cache_d_krope, w_in_even, b_forget, w_out_even, w_in_odd, c_lambda, c_subln, d_q_norm, d_w_uq, d_kv_norm, d_w_uk, d_w_uv, w_out_odd, ada_mix_w, ada_mix_b, ln_mix_g, ln_mix_b, ada_ffn_w, ada_ffn_b, ln_ffn_g, ln_ffn_b, router_w, router_bias, moe_w1, moe_w3, moe_w2, shared_w1, shared_w3, shared_w2):
    B, L, D = x_prompt.shape
    Bs, Ls, _ = x_sample.shape
    depth = ada_mix_w.shape[0]
    alpha = (2 * depth) ** 0.25
    past_len = cache_a_k.shape[2]
    Tp = B * L
    T = Tp + Bs * Ls
    grp_p = _Group(B, L, 0, 0)
    grp_s = _Group(Bs, Ls, past_len, Tp)
    group = math.gcd(L, Ls)
    assert group % (2 * SUBLANES) == 0
    tm_tok = _pick(T, 768, math.lcm(LANES, group))
    tm_close = _pick(T, 512, math.lcm(2 * SUBLANES, group))
    pos = jnp.concatenate([jnp.tile(jnp.arange(L), B), jnp.tile(past_len + jnp.arange(Ls), Bs)])
    c_act = jax.nn.silu(jnp.concatenate([c_prompt, c_sample], axis=0))

    def conditioning(ada_w, ada_b, layer):
        mod = matmul(c_act, ada_w, layer=layer) + ada_b[layer]
        per_group = jnp.concatenate([jnp.repeat(mod[:B], L // group, axis=0),
                                     jnp.repeat(mod[B:], Ls // group, axis=0)], axis=0)
        return jnp.split(per_group, 3, axis=-1)

    def flat_past(c):
        return c.reshape(c.shape[0], c.shape[1], -1)

    x = jnp.concatenate([x_prompt.reshape(Tp, D), x_sample.reshape(Bs * Ls, D)], axis=0)
    ev_p, ev_s, od_p, od_s = [], [], [], []
    for i in range(depth):
        j = i // 2
        shift_g, scale_g, gate_g = conditioning(ada_mix_w, ada_mix_b, i)
        if i % 2 == 0:
            sizes = (A_HEADS * HEAD_DIM, A_KV_HEADS * HEAD_DIM, A_KV_HEADS * HEAD_DIM, IDX_HEADS * IDX_DIM, IDX_DIM,
                     IDX_HEADS, B_HEADS * HEAD_DIM, B_HEADS * HEAD_DIM, B_HEADS * HEAD_DIM, B_HEADS)
            names = (("qa", 0), ("ka", 0), ("va", 0), ("qi", 0), ("ki", 0), ("misc", MISC_WI),
                     ("qf", 0), ("kf", 0), ("vf", 0), ("misc", MISC_FL))
            w_in = _mx(_place_cols(w_in_even[j], sizes, names, EVEN_COLS, EVEN_WIDTH))
            cos, sin = _rope_tables(pos, HEAD_DIM // 2)
            proj = project(x, scale_g, shift_g, w_in, cos, sin, EVEN_ROPE_RANGES, HEAD_DIM // 2, group, tm_tok)
            outs_a, outs_b = [], []
            for grp, store in ((grp_p, ev_p), (grp_s, ev_s)):
                if grp.lp:
                    logf, fq, fk = _fox_gates(grp, proj, b_forget[j], cache_b_logf[j])
                    outs_a.append(dsa_attention(grp, proj, cache_a_kidx[j], flat_past(cache_a_k[j]),
                                                flat_past(cache_a_v[j])))
                    outs_b.append(fox_attention_past(grp, proj, fq, fk, cache_b_k, cache_b_v, j))
                else:
                    logf, fq, fk = _fox_gates(grp, proj, b_forget[j], None)
                    outs_a.append(dsa_attention(grp, proj, None, None, None))
                    outs_b.append(fox_attention(grp, proj, fq, fk))
                r0, r1 = grp.row0, grp.row0 + grp.batch * grp.lq
                cut = lambda name, heads, d: proj[r0:r1, EVEN_COLS[name]:EVEN_COLS[name] + heads * d].reshape(
                    (grp.batch, grp.lq) + ((heads, d) if heads > 1 else (d,)))
                store.append((cut("ka", A_KV_HEADS, HEAD_DIM), cut("va", A_KV_HEADS, HEAD_DIM), cut("ki", 1, IDX_DIM),
                              cut("kf", B_HEADS, HEAD_DIM), cut("vf", B_HEADS, HEAD_DIM), logf))
            a1, a2 = jnp.concatenate(outs_a, axis=0), jnp.concatenate(outs_b, axis=0)
            n1 = A_HEADS * HEAD_DIM
            w1o, w2o = _mx(w_out_even[j][:n1]), _mx(w_out_even[j][n1:])
        else:
            lam_init = 0.8 - 0.6 * math.exp(-0.3 * i)
            lam_f = c_lambda[j]
            lam = (jnp.exp(jnp.sum(lam_f[0] * lam_f[1])) - jnp.exp(jnp.sum(lam_f[2] * lam_f[3])) + lam_init).reshape(1)
            q_lora, R = d_q_norm.shape[1], d_kv_norm.shape[1]
            sizes = (C_HEADS * 2 * C_QK_DIM, C_HEADS * 2 * C_QK_DIM, C_HEADS * C_V_DIM, q_lora, R, D_ROPE)
            names = (("qc", 0), ("kc", 0), ("vc", 0), ("qd", 0), ("ckv", 0), ("kr", 0))
            w_in = _mx(_place_cols(w_in_odd[j], sizes, names, ODD_COLS, ODD_WIDTH))
            cos, sin = _rope_tables(pos, C_QK_DIM // 2)
            proj = project(x, scale_g, shift_g, w_in, cos, sin, ODD_ROPE_RANGES, C_QK_DIM // 2, group, tm_tok)

            w_uq = d_w_uq[j].reshape(q_lora, D_HEADS, D_NOPE + D_ROPE)
            w_abs = bmm_precise(jnp.moveaxis(w_uq[:, :, :D_NOPE], 1, 0), jnp.transpose(d_w_uk[j], (1, 2, 0)))
            w_abs = jnp.moveaxis(w_abs, 0, 1).reshape(q_lora, D_HEADS * R)
            w_rope = jnp.pad(w_uq[:, :, D_NOPE:], ((0, 0), (0, 0), (0, LANES - D_ROPE))).reshape(q_lora, D_HEADS * LANES)
            q_abs, q_rope, lat = mla_prepare(proj, d_q_norm[j][None], d_kv_norm[j][None],
                                             _mx(jnp.concatenate([w_abs, w_rope], axis=1)), cos, sin, tm_close)
            n1 = C_HEADS * C_V_DIM
            w_od = w_out_odd[j][n1:].reshape(D_HEADS, D_V, D)
            w_lat_out = bmm_precise(jnp.transpose(d_w_uv[j], (1, 0, 2)), w_od).reshape(D_HEADS * R, D)

            outs_a, outs_b = [], []
            for grp, store in ((grp_p, od_p), (grp_s, od_s)):
                if grp.lp:
                    outs_a.append(diff_attention_past(grp, proj, lam, c_subln[j][None], 1.0 - lam_init,
                                                      flat_past(cache_c_k[j]), cache_c_v, j))
                    outs_b.append(mla_attention_past(grp, q_abs, q_rope, lat, proj, cache_d_latent, cache_d_krope, j))
                else:
                    outs_a.append(diff_attention(grp, proj, lam, c_subln[j][None], 1.0 - lam_init))
                    outs_b.append(mla_attention(grp, q_abs, q_rope, lat, proj))
                r0, r1 = grp.row0, grp.row0 + grp.batch * grp.lq
                shp = (grp.batch, grp.lq)
                oc = ODD_COLS
                store.append((proj[r0:r1, oc["kc"]:oc["kc"] + C_HEADS * 2 * C_QK_DIM].reshape(shp + (C_HEADS, 2, C_QK_DIM)),
                              proj[r0:r1, oc["vc"]:oc["vc"] + n1].reshape(shp + (C_HEADS, C_V_DIM)),
                              lat[r0:r1].reshape(shp + (R,)),
                              proj[r0:r1, oc["kr"]:oc["kr"] + D_ROPE].reshape(shp + (D_ROPE,))))
            a1, a2 = jnp.concatenate(outs_a, axis=0), jnp.concatenate(outs_b, axis=0)
            w1o, w2o = _mx(w_out_odd[j][:n1]), _mx(w_lat_out)
        x = close_mixer(a1, a2, w1o, w2o, x, gate_g, ln_mix_g[i][None], ln_mix_b[i][None], group, alpha, tm_close)

        shift_g, scale_g, gate_g = conditioning(ada_ffn_w, ada_ffn_b, i)
        x = _moe(x, scale_g, shift_g, gate_g, ln_ffn_g[i][None], ln_ffn_b[i][None], router_w[i], router_bias[i],
                 moe_w1, moe_w3, moe_w2, shared_w1[i], shared_w3[i], shared_w2[i], i, group, alpha)

    stack = lambda rows, idx: jnp.stack([r[idx] for r in rows])
    outs = [x[:Tp].reshape(B, L, D), x[Tp:].reshape(Bs, Ls, D)]
    for idx in range(6):
        outs += [stack(ev_p, idx), stack(ev_s, idx)]
    for idx in range(4):
        outs += [stack(od_p, idx), stack(od_s, idx)]
    return tuple(outs)
```

```python
import functools
import math

import jax
import jax.numpy as jnp
from jax import lax
from jax.experimental import pallas as pl
from jax.experimental.pallas import tpu as pltpu

CHUNK = 64
ROPE_THETA = 10000.0
HEAD_DIM = 128
A_HEADS = 8
A_KV_HEADS = 2
IDX_HEADS = 16
IDX_DIM = 128
A_TOPK_MAX = 256
B_HEADS = 8
C_HEADS = 8
C_QK_DIM = 64
C_V_DIM = 128
D_HEADS = 8
D_NOPE = 128
D_ROPE = 64
D_V = 128
N_GROUPS = 8
TOPK_GROUPS = 4
TOP_K = 8
ROUTED_SCALE = 2.5
NORM_EPS = 1e-6

LANES = 128
SUBLANES = 8
MXU_WIDTH = 256
VMEM_LIMIT_BYTES = 56 * 2**20
MXU_DTYPE = jnp.bfloat16

F32 = jnp.float32
I32 = jnp.int32
NEG = -1e30
INT_MIN = -2**31
KEY_BLOCK = 512
QUERY_BLOCK = 512
HEADS_PER_STEP = 2

EVEN_COLS = dict(qi=0, qa=2048, qf=3072, kf=4096, vf=5120, ka=6144, va=6400, ki=6656, misc=6912)
EVEN_WIDTH = 7168
EVEN_ROPE_RANGES = ((0, 3072), (6144, 6400), (6656, 6912))
MISC_WI, MISC_FL = 0, IDX_HEADS
ODD_COLS = dict(qc=0, kc=1024, vc=2048, qd=3072, ckv=3584, kr=3840)
ODD_WIDTH = 4096
ODD_ROPE_RANGES = ((0, 2048), (3840, 4096))


def _pick(n, target, mult):
    best = None
    for d in range(mult, min(n, target) + 1, mult):
        if n % d == 0:
            best = d
    return n if best is None else best


def _params(n_axes):
    return pltpu.CompilerParams(dimension_semantics=("arbitrary",) * n_axes,
                                vmem_limit_bytes=VMEM_LIMIT_BYTES)


def _mx(a):
    return a.astype(MXU_DTYPE)


def _dot(a, b):
    return jnp.dot(_mx(a), _mx(b), preferred_element_type=F32)


def _dot_nt(a, b, precision=None):
    if precision is None:
        a, b = _mx(a), _mx(b)
    return lax.dot_general(a, b, (((1,), (1,)), ((), ())), preferred_element_type=F32, precision=precision)


def _mm_body(a_ref, b_ref, o_ref, *, precise):
    if precise:
        o = jnp.dot(a_ref[...], b_ref[...], preferred_element_type=F32, precision=lax.Precision.HIGHEST)
    else:
        o = _dot(a_ref[...], b_ref[...])
    o_ref[...] = o.astype(o_ref.dtype)


def matmul(a, b, out_dtype=F32, precise=False, tm_target=1024, tn_target=512, layer=None):
    M, K = a.shape
    N = b.shape[-1]
    tm = _pick(M, tm_target, SUBLANES)
    tn = _pick(N, tn_target, LANES)
    if layer is None:
        b_spec = pl.BlockSpec((K, tn), lambda i, j: (0, j))
    else:
        b_spec = pl.BlockSpec((None, K, tn), lambda i, j: (layer, 0, j))
    return pl.pallas_call(
        functools.partial(_mm_body, precise=precise),
        grid=(M // tm, N // tn),
        in_specs=[pl.BlockSpec((tm, K), lambda i, j: (i, 0)), b_spec],
        out_specs=pl.BlockSpec((tm, tn), lambda i, j: (i, j)),
        out_shape=jax.ShapeDtypeStruct((M, N), out_dtype),
        compiler_params=_params(2),
        name="matmul",
    )(a, b)


def _bmm_body(a_ref, b_ref, o_ref):
    o_ref[0] = jnp.dot(a_ref[0], b_ref[0], preferred_element_type=F32,
                       precision=lax.Precision.HIGHEST).astype(o_ref.dtype)


def bmm_precise(a, b):
    H, M, K = a.shape
    N = b.shape[2]
    return pl.pallas_call(
        _bmm_body,
        grid=(H,),
        in_specs=[pl.BlockSpec((1, M, K), lambda h: (h, 0, 0)),
                  pl.BlockSpec((1, K, N), lambda h: (h, 0, 0))],
        out_specs=pl.BlockSpec((1, M, N), lambda h: (h, 0, 0)),
        out_shape=jax.ShapeDtypeStruct((H, M, N), F32),
        compiler_params=_params(1),
        name="bmm_precise",
    )(a, b)


def _per_tile(table, tm, group):
    return table.reshape(table.shape[0] * group // tm, tm // group, table.shape[1])


def _modulated(x_ref, sc_ref, sh_ref, group):
    parts = []
    for g in range(x_ref.shape[0] // group):
        rows = slice(g * group, (g + 1) * group)
        parts.append(x_ref[rows, :] * (1.0 + sc_ref[0, g:g + 1, :]) + sh_ref[0, g:g + 1, :])
    return parts


def _pack_rows(x):
    bits = pltpu.bitcast(x, I32)
    rounded = bits + 0x7FFF + ((bits >> 16) & 1)
    half = x.shape[1] // 2
    return ((rounded[:, :half] >> 16) & 0xFFFF) | (rounded[:, half:] & jnp.int32(-0x10000))


def _unpack_rows(w):
    lo = pltpu.bitcast(jnp.left_shift(w, 16), F32)
    hi = pltpu.bitcast(w & jnp.int32(-0x10000), F32)
    return jnp.concatenate([lo, hi], axis=1)


def _load_words(ref, first, n, per):
    return jnp.concatenate([ref[pl.ds(first * per + s, n, stride=per), :] for s in range(per)], axis=1)


def _store_words(ref, first, words, per):
    for s in range(per):
        ref[pl.ds(first * per + s, words.shape[0], stride=per), :] = words[:, s * LANES:(s + 1) * LANES]


def _rope_lanes(x, cos, sin, half):
    if 2 * half == LANES:
        partner = pltpu.roll(x, half, axis=1)
    else:
        lane = lax.broadcasted_iota(I32, x.shape, 1)
        lower = (lane % (2 * half)) < half
        partner = jnp.where(lower, pltpu.roll(x, LANES - half, axis=1), pltpu.roll(x, half, axis=1))
    return x * cos + partner * sin


def _layer_norm_rows(r, g, b):
    rc = r - jnp.mean(r, axis=-1, keepdims=True)
    var = jnp.mean(rc * rc, axis=-1, keepdims=True)
    return rc * lax.rsqrt(var + NORM_EPS) * g + b


def _gated_residual(x_ref, y, gate_ref, group, alpha):
    parts = []
    for g in range(x_ref.shape[0] // group):
        rows = slice(g * group, (g + 1) * group)
        parts.append(alpha * x_ref[rows, :] + (1.0 + gate_ref[0, g:g + 1, :]) * y[rows, :])
    return jnp.concatenate(parts, axis=0)


def _proj_body(kind_ref, x_ref, sc_ref, sh_ref, w_ref, *rest, group, half, n_sub):
    tables, (o_ref, h_ref) = rest[:2 * n_sub], rest[2 * n_sub:]
    j = pl.program_id(1)

    @pl.when(j == 0)
    def _():
        for g, part in enumerate(_modulated(x_ref, sc_ref, sh_ref, group)):
            h_ref[g * group:(g + 1) * group, :] = part.astype(h_ref.dtype)

    for c in range(n_sub):
        cos, sin = tables[2 * c][...], tables[2 * c + 1][...]
        acc = jnp.dot(h_ref[...], w_ref[:, c * MXU_WIDTH:(c + 1) * MXU_WIDTH], preferred_element_type=F32)
        for p in range(MXU_WIDTH // LANES):
            o_ref[:, c * MXU_WIDTH + p * LANES:c * MXU_WIDTH + (p + 1) * LANES] = _rope_lanes(
                acc[:, p * LANES:(p + 1) * LANES], cos, sin, half)


def project(x, scale_g, shift_g, w, cos, sin, rope_ranges, half, group, tm):
    T, D = x.shape
    N = w.shape[1]
    n_sub = 2
    tn = n_sub * MXU_WIDTH
    start = jnp.arange(N // MXU_WIDTH) * MXU_WIDTH
    kinds = sum(((start >= lo) & (start < hi)).astype(I32) for lo, hi in rope_ranges)
    row = lambda i, j, k: (i, 0)
    tile = lambda i, j, k: (i, 0, 0)
    table_specs, table_args = [], []
    for c in range(n_sub):
        pick = lambda i, j, k, c=c: (k[j * n_sub + c], i, 0)
        table_specs += [pl.BlockSpec((None, tm, LANES), pick)] * 2
        table_args += [cos, sin]
    return pl.pallas_call(
        functools.partial(_proj_body, group=group, half=half, n_sub=n_sub),
        grid_spec=pltpu.PrefetchScalarGridSpec(
            num_scalar_prefetch=1,
            grid=(T // tm, N // tn),
            in_specs=[pl.BlockSpec((tm, D), row),
                      pl.BlockSpec((1, tm // group, D), tile),
                      pl.BlockSpec((1, tm // group, D), tile),
                      pl.BlockSpec((D, tn), lambda i, j, k: (0, j))] + table_specs,
            out_specs=pl.BlockSpec((tm, tn), lambda i, j, k: (i, j)),
            scratch_shapes=[pltpu.VMEM((tm, D), MXU_DTYPE)]),
        out_shape=jax.ShapeDtypeStruct((T, N), F32),
        compiler_params=_params(2),
        name="project",
    )(kinds, x, _per_tile(scale_g, tm, group), _per_tile(shift_g, tm, group), w, *table_args)


def _close_mix_body(a1_ref, a2_ref, w1_ref, w2_ref, x_ref, gate_ref, g_ref, b_ref, o_ref, *, group, alpha):
    y = _dot(a1_ref[...], w1_ref[...]) + _dot(a2_ref[...], w2_ref[...])
    r = _gated_residual(x_ref, y, gate_ref, group, alpha)
    o_ref[...] = _layer_norm_rows(r, g_ref[...], b_ref[...])


def close_mixer(a1, a2, w1, w2, x, gate_g, ln_g, ln_b, group, alpha, tm):
    T, D = x.shape
    row = lambda i: (i, 0)
    fixed = lambda i: (0, 0)
    return pl.pallas_call(
        functools.partial(_close_mix_body, group=group, alpha=alpha),
        grid=(T // tm,),
        in_specs=[pl.BlockSpec((tm, a1.shape[1]), row),
                  pl.BlockSpec((tm, a2.shape[1]), row),
                  pl.BlockSpec(w1.shape, fixed),
                  pl.BlockSpec(w2.shape, fixed),
                  pl.BlockSpec((tm, D), row),
                  pl.BlockSpec((1, tm // group, D), lambda i: (i, 0, 0)),
                  pl.BlockSpec((1, D), fixed),
                  pl.BlockSpec((1, D), fixed)],
        out_specs=pl.BlockSpec((tm, D), row),
        out_shape=jax.ShapeDtypeStruct((T, D), F32),
        compiler_params=_params(1),
        name="close_mixer",
    )(a1, a2, w1, w2, x, _per_tile(gate_g, tm, group), ln_g, ln_b)


def _online_step(s, v, carry):
    m, l, acc = carry
    m_new = jnp.maximum(m, jnp.max(s, axis=-1, keepdims=True))
    p = jnp.exp(s - m_new)
    alpha = jnp.exp(m - m_new)
    l = alpha * l + jnp.sum(p, axis=-1, keepdims=True)
    acc = alpha * acc + _dot(p, v)
    return m_new, l, acc


def _init_carry(tq, dv):
    return (jnp.full((tq, 1), NEG, F32), jnp.zeros((tq, 1), F32), jnp.zeros((tq, dv), F32))


def _chunk_last_key(i, tq):
    return ((i + 1) * tq - 1) // CHUNK * CHUNK + CHUNK - 1


class _Group:
    def __init__(self, batch, lq, lp, row0):
        self.batch, self.lq, self.lp, self.row0 = batch, lq, lp, row0
        self.lk = lp + lq
        self.tq = _pick(lq, QUERY_BLOCK, SUBLANES)
        self.tq_dsa = _pick(lq, 128, SUBLANES)
        self.tk = _pick(lq, KEY_BLOCK, SUBLANES)
        self.tkp = _pick(lp, KEY_BLOCK, LANES) if lp else 0
        self.n_past_blocks = lp // self.tkp if lp else 0
        self.topk = min(A_TOPK_MAX, self.lk // 4)
        assert lp % CHUNK == 0 and row0 % lq == 0 and lq % self.tk == 0

    def qrow(self, tq):
        base, per = self.row0 // tq, self.lq // tq
        return lambda b, i: base + b * per + i

    def krow(self):
        base = self.row0 // self.lq
        return lambda b: base + b


def _new_key_blocks(i, tq, grp_lq, tk, causal_last):
    return jnp.minimum(causal_last, grp_lq - 1) // tk + 1


def _fox_body(q_ref, k_ref, v_ref, fq_ref, fk_ref, o_ref, *, tq, tk, lq, hp):
    i = pl.program_id(2)
    d = HEAD_DIM
    head = lambda x, u: x[:, u * d:(u + 1) * d]
    q = _mx(q_ref[...] * d ** -0.5)
    carry = tuple(_init_carry(tq, d) for _ in range(hp))
    qpos = i * tq + lax.broadcasted_iota(I32, (tq, tk), 0)

    def new_step(j, c):
        off = pl.multiple_of(j * tk, tk)
        k, v = k_ref[pl.ds(off, tk), :], v_ref[pl.ds(off, tk), :]
        causal = off + lax.broadcasted_iota(I32, (tq, tk), 1) <= qpos
        out = []
        for u in range(hp):
            s = _dot_nt(head(q, u), head(k, u)) + fq_ref[u] - fk_ref[u, j][:, :tk]
            out.append(_online_step(jnp.where(causal, s, NEG), head(v, u), c[u]))
        return tuple(out)

    nb = _new_key_blocks(i, tq, lq, tk, (i + 1) * tq - 1)
    carry = lax.fori_loop(0, nb, new_step, carry)
    for u, (m, l, acc) in enumerate(carry):
        o_ref[:, u * d:(u + 1) * d] = (acc / l).astype(o_ref.dtype)


def fox_attention(grp, proj, fq, fk):
    assert grp.lp == 0
    B, H, hp = grp.batch, B_HEADS, HEADS_PER_STEP
    w = hp * HEAD_DIM
    tq = grp.tq
    qrow, krow = grp.qrow(tq), grp.krow()
    cq, ck, cv = (EVEN_COLS[n] // w for n in ("qf", "kf", "vf"))
    in_specs = [pl.BlockSpec((tq, w), lambda b, h, i: (qrow(b, i), cq + h)),
                pl.BlockSpec((grp.lq, w), lambda b, h, i: (krow(b), ck + h)),
                pl.BlockSpec((grp.lq, w), lambda b, h, i: (krow(b), cv + h))]
    in_specs += [pl.BlockSpec((hp, tq, 1), lambda b, h, i: (b * (H // hp) + h, i, 0)),
                 pl.BlockSpec((hp,) + fk.shape[1:], lambda b, h, i: (b * (H // hp) + h, 0, 0, 0))]
    return pl.pallas_call(
        functools.partial(_fox_body, tq=tq, tk=grp.tk, lq=grp.lq, hp=hp),
        grid=(B, H // hp, grp.lq // tq),
        in_specs=in_specs,
        out_specs=pl.BlockSpec((tq, w), lambda b, h, i: (b * (grp.lq // tq) + i, h)),
        out_shape=jax.ShapeDtypeStruct((B * grp.lq, H * HEAD_DIM), MXU_DTYPE),
        compiler_params=_params(3),
        name="fox_attention",
    )(proj, proj, proj, fq, fk)


def _diff_body(lam_ref, q_ref, k_ref, v_ref, g_ref, o_ref, *, tq, tk, lq, hp, out_scale):
    i = pl.program_id(2)
    d = C_V_DIM
    head = lambda x, u: x[:, u * d:(u + 1) * d]
    q = q_ref[...] * C_QK_DIM ** -0.5
    lane = lax.broadcasted_iota(I32, q.shape, 1) % (2 * C_QK_DIM)
    qs = (_mx(jnp.where(lane < C_QK_DIM, q, 0.0)), _mx(jnp.where(lane >= C_QK_DIM, q, 0.0)))
    carry = tuple(_init_carry(tq, d) for _ in range(2 * hp))

    def step(k, v, ok, c):
        out = []
        for u in range(hp):
            ku, vu = _mx(head(k, u)), _mx(head(v, u))
            for t in range(2):
                s = _dot_nt(head(qs[t], u), ku)
                out.append(_online_step(jnp.where(ok, s, NEG), vu, c[2 * u + t]))
        return tuple(out)

    qchunk = (i * tq + lax.broadcasted_iota(I32, (tq, tk), 0)) // CHUNK

    def new_step(j, c):
        off = pl.multiple_of(j * tk, tk)
        ok = (off + lax.broadcasted_iota(I32, (tq, tk), 1)) // CHUNK <= qchunk
        return step(k_ref[pl.ds(off, tk), :], v_ref[pl.ds(off, tk), :], ok, c)

    nb = _new_key_blocks(i, tq, lq, tk, _chunk_last_key(i, tq))
    carry = lax.fori_loop(0, nb, new_step, carry)
    for u in range(hp):
        (_, l0, a0), (_, l1, a1) = carry[2 * u], carry[2 * u + 1]
        o = a0 / l0 - lam_ref[0] * (a1 / l1)
        o = o * lax.rsqrt(jnp.mean(o * o, axis=-1, keepdims=True) + NORM_EPS)
        o_ref[:, u * d:(u + 1) * d] = (o * g_ref[...] * out_scale).astype(o_ref.dtype)


def diff_attention(grp, proj, lam, subln, out_scale):
    assert grp.lp == 0
    B, H, hp = grp.batch, C_HEADS, HEADS_PER_STEP
    w = hp * C_V_DIM
    tq = grp.tq
    qrow, krow = grp.qrow(tq), grp.krow()
    cq, ck, cv = (ODD_COLS[n] // w for n in ("qc", "kc", "vc"))
    in_specs = [pl.BlockSpec(memory_space=pltpu.SMEM),
                pl.BlockSpec((tq, w), lambda b, h, i: (qrow(b, i), cq + h)),
                pl.BlockSpec((grp.lq, w), lambda b, h, i: (krow(b), ck + h)),
                pl.BlockSpec((grp.lq, w), lambda b, h, i: (krow(b), cv + h))]
    in_specs += [pl.BlockSpec((1, C_V_DIM), lambda b, h, i: (0, 0))]
    return pl.pallas_call(
        functools.partial(_diff_body, tq=tq, tk=grp.tk, lq=grp.lq, hp=hp, out_scale=out_scale),
        grid=(B, H // hp, grp.lq // tq),
        in_specs=in_specs,
        out_specs=pl.BlockSpec((tq, w), lambda b, h, i: (b * (grp.lq // tq) + i, h)),
        out_shape=jax.ShapeDtypeStruct((B * grp.lq, H * C_V_DIM), MXU_DTYPE),
        compiler_params=_params(3),
        name="diff_attention",
    )(lam, proj, proj, proj, subln)


def _mla_prep_body(qd_ref, ckv_ref, qg_ref, kg_ref, w_ref, cos_ref, sin_ref, qa_ref, qr_ref, lat_ref, *, n_abs):
    qd = qd_ref[...]
    qn = qd * lax.rsqrt(jnp.mean(qd * qd, axis=-1, keepdims=True) + NORM_EPS) * qg_ref[...]
    q = _dot(qn, w_ref[...]) * (D_NOPE + D_ROPE) ** -0.5
    qa_ref[...] = q[:, :n_abs].astype(qa_ref.dtype)
    for h in range((q.shape[1] - n_abs) // LANES):
        lanes = slice(n_abs + h * LANES, n_abs + (h + 1) * LANES)
        qr_ref[:, h * LANES:(h + 1) * LANES] = _rope_lanes(
            q[:, lanes], cos_ref[...], sin_ref[...], D_ROPE // 2).astype(qr_ref.dtype)
    ckv = ckv_ref[...]
    lat_ref[...] = ckv * lax.rsqrt(jnp.mean(ckv * ckv, axis=-1, keepdims=True) + NORM_EPS) * kg_ref[...]


def mla_prepare(proj, q_norm, kv_norm, w_q, cos, sin, tm):
    T = proj.shape[0]
    q_lora, R = q_norm.shape[1], kv_norm.shape[1]
    n_abs = D_HEADS * R
    n_rope = D_HEADS * LANES
    row = lambda i: (i, 0)
    fixed = lambda i: (0, 0)
    return pl.pallas_call(
        functools.partial(_mla_prep_body, n_abs=n_abs),
        grid=(T // tm,),
        in_specs=[pl.BlockSpec((tm, q_lora), lambda i: (i, ODD_COLS["qd"] // q_lora)),
                  pl.BlockSpec((tm, R), lambda i: (i, ODD_COLS["ckv"] // R)),
                  pl.BlockSpec((1, q_lora), fixed),
                  pl.BlockSpec((1, R), fixed),
                  pl.BlockSpec(w_q.shape, fixed),
                  pl.BlockSpec((None, tm, LANES), lambda i: (1, i, 0)),
                  pl.BlockSpec((None, tm, LANES), lambda i: (1, i, 0))],
        out_specs=[pl.BlockSpec((tm, n_abs), row), pl.BlockSpec((tm, n_rope), row), pl.BlockSpec((tm, R), row)],
        out_shape=[jax.ShapeDtypeStruct((T, n_abs), MXU_DTYPE), jax.ShapeDtypeStruct((T, n_rope), MXU_DTYPE),
                   jax.ShapeDtypeStruct((T, R), F32)],
        compiler_params=_params(1),
        name="mla_prepare",
    )(proj, proj, q_norm, kv_norm, w_q, cos, sin)


def _mla_body(qa_ref, qr_ref, lat_ref, kr_ref, o_ref, *, tq, tk, lq, hp):
    i = pl.program_id(2)
    R = lat_ref.shape[-1]
    qa = [qa_ref[:, u * R:(u + 1) * R] for u in range(hp)]
    qr = [qr_ref[:, u * LANES:(u + 1) * LANES] for u in range(hp)]
    carry = tuple(_init_carry(tq, R) for _ in range(hp))

    qchunk = (i * tq + lax.broadcasted_iota(I32, (tq, tk), 0)) // CHUNK

    def new_step(j, c):
        off = pl.multiple_of(j * tk, tk)
        lat = _mx(lat_ref[pl.ds(off, tk), :])
        kr = _mx(kr_ref[pl.ds(off, tk), :])
        ok = (off + lax.broadcasted_iota(I32, (tq, tk), 1)) // CHUNK <= qchunk
        return tuple(_online_step(jnp.where(ok, _dot_nt(qa[u], lat) + _dot_nt(qr[u], kr), NEG), lat, c[u])
                     for u in range(hp))

    nb = _new_key_blocks(i, tq, lq, tk, _chunk_last_key(i, tq))
    carry = lax.fori_loop(0, nb, new_step, carry)
    for u, (m, l, acc) in enumerate(carry):
        o_ref[:, u * R:(u + 1) * R] = (acc / l).astype(o_ref.dtype)


def mla_attention(grp, q_abs, q_rope, lat, proj):
    assert grp.lp == 0
    B, H, hp = grp.batch, D_HEADS, HEADS_PER_STEP
    R = lat.shape[1]
    tq = grp.tq
    qrow, krow = grp.qrow(tq), grp.krow()
    in_specs = [pl.BlockSpec((tq, hp * R), lambda b, h, i: (qrow(b, i), h)),
                pl.BlockSpec((tq, hp * LANES), lambda b, h, i: (qrow(b, i), h)),
                pl.BlockSpec((grp.lq, R), lambda b, h, i: (krow(b), 0)),
                pl.BlockSpec((grp.lq, LANES), lambda b, h, i: (krow(b), ODD_COLS["kr"] // LANES))]
    return pl.pallas_call(
        functools.partial(_mla_body, tq=tq, tk=grp.tk, lq=grp.lq, hp=hp),
        grid=(B, H // hp, grp.lq // tq),
        in_specs=in_specs,
        out_specs=pl.BlockSpec((tq, hp * R), lambda b, h, i: (b * (grp.lq // tq) + i, h)),
        out_shape=jax.ShapeDtypeStruct((B * grp.lq, H * R), MXU_DTYPE),
        compiler_params=_params(3),
        name="mla_attention",
    )(q_abs, q_rope, lat, proj)


def _head_cols(ref, h, d, heads):
    if ref.shape[-1] == d:
        return ref[pl.ds(h, ref.shape[0] // heads, stride=heads), :]
    return ref[:, h * d:(h + 1) * d]


def _chain_reset(j, m_ref, l_ref, acc_ref):
    @pl.when(j == 0)
    def _():
        m_ref[...] = jnp.full(m_ref.shape, NEG, F32)
        l_ref[...] = jnp.zeros(l_ref.shape, F32)
        acc_ref[...] = jnp.zeros(acc_ref.shape, F32)


def _chains_advance(scores_values, m_ref, l_ref, acc_ref):
    states = [(m_ref[c], l_ref[c], acc_ref[c]) for c in range(len(scores_values))]
    new = [_online_step(s, v, st) for (s, v), st in zip(scores_values, states)]
    for c, (m, l, acc) in enumerate(new):
        m_ref[c], l_ref[c], acc_ref[c] = m, l, acc
    return [(l, acc) for _, l, acc in new]


def _past_key_spec(cache, layer, tkp, n_past, rows_per_key=1):
    blk = lambda j: jnp.minimum(j, n_past - 1)
    rows, w = tkp * rows_per_key, cache.shape[-1]
    if layer is None:
        return pl.BlockSpec((None, rows, w), lambda b, j: (b, blk(j), 0))
    return pl.BlockSpec((None, None, rows, w), lambda b, j: (layer, b, blk(j), 0))


def _keys_by_head(cache):
    n, b, p, h, d = cache.shape
    return cache.reshape(n, b, p * h, d)


def _past_call(body, grp, heads, dv, chains, in_specs, args, name):
    lq = grp.lq
    assert grp.lq == grp.tk
    return pl.pallas_call(
        body,
        grid=(grp.batch, grp.n_past_blocks + 1),
        in_specs=in_specs,
        out_specs=pl.BlockSpec((lq, heads * dv), lambda b, j: (b, 0)),
        out_shape=jax.ShapeDtypeStruct((grp.batch * lq, heads * dv), MXU_DTYPE),
        scratch_shapes=[pltpu.VMEM((chains, lq, 1), F32), pltpu.VMEM((chains, lq, 1), F32),
                        pltpu.VMEM((chains, lq, dv), F32)],
        compiler_params=_params(2),
        name=name,
    )(*args)


def _fox_past_body(q_ref, k_ref, v_ref, pk_ref, pv_ref, fq_ref, fk_ref, o_ref, m_ref, l_ref, acc_ref, *, n_past):
    j = pl.program_id(1)
    d, lq = HEAD_DIM, q_ref.shape[0]
    _chain_reset(j, m_ref, l_ref, acc_ref)
    q = lambda h: _mx(q_ref[:, h * d:(h + 1) * d] * d ** -0.5)

    @pl.when(j < n_past)
    def _():
        _chains_advance([(_dot_nt(q(h), _head_cols(pk_ref, h, d, B_HEADS)) + fq_ref[h] - fk_ref[h, 0],
                          _head_cols(pv_ref, h, d, B_HEADS)) for h in range(B_HEADS)], m_ref, l_ref, acc_ref)

    @pl.when(j == n_past)
    def _():
        causal = lax.broadcasted_iota(I32, (lq, lq), 1) <= lax.broadcasted_iota(I32, (lq, lq), 0)
        logits = lambda h: _dot_nt(q(h), k_ref[:, h * d:(h + 1) * d]) + fq_ref[h] - fk_ref[h, 0][:, :lq]
        done = _chains_advance([(jnp.where(causal, logits(h), NEG), v_ref[:, h * d:(h + 1) * d])
                                for h in range(B_HEADS)], m_ref, l_ref, acc_ref)
        for h, (l, acc) in enumerate(done):
            o_ref[:, h * d:(h + 1) * d] = (acc / l).astype(o_ref.dtype)


def fox_attention_past(grp, proj, fq, fk, cache_k, cache_v, layer):
    H, d, lq = B_HEADS, HEAD_DIM, grp.lq
    row = grp.krow()
    w = H * d
    in_specs = [pl.BlockSpec((lq, w), lambda b, j, n=n: (row(b), EVEN_COLS[n] // w)) for n in ("qf", "kf", "vf")]
    cache_k, cache_v = _keys_by_head(cache_k), _keys_by_head(cache_v)
    in_specs += [_past_key_spec(cache_k, layer, grp.tkp, grp.n_past_blocks, H),
                 _past_key_spec(cache_v, layer, grp.tkp, grp.n_past_blocks, H),
                 pl.BlockSpec((H, lq, 1), lambda b, j: (b, 0, 0)),
                 pl.BlockSpec((H, 1, 1, fk.shape[-1]), lambda b, j: (b, j, 0, 0))]
    return _past_call(functools.partial(_fox_past_body, n_past=grp.n_past_blocks), grp, H, d, H, in_specs,
                      [proj, proj, proj, cache_k, cache_v, fq, fk], "fox_attention_past")


def _diff_past_body(lam_ref, q_ref, k_ref, v_ref, pk_ref, pv_ref, g_ref, o_ref, m_ref, l_ref, acc_ref,
                    *, n_past, out_scale):
    j = pl.program_id(1)
    d = C_V_DIM
    _chain_reset(j, m_ref, l_ref, acc_ref)
    lane = lax.broadcasted_iota(I32, (q_ref.shape[0], d), 1)

    def maps(h):
        q = q_ref[:, h * d:(h + 1) * d] * C_QK_DIM ** -0.5
        return _mx(jnp.where(lane < C_QK_DIM, q, 0.0)), _mx(jnp.where(lane >= C_QK_DIM, q, 0.0))

    def both_maps(key_of, value_of):
        items = []
        for h in range(C_HEADS):
            k, v = _mx(key_of(h)), _mx(value_of(h))
            items += [(_dot_nt(qt, k), v) for qt in maps(h)]
        return _chains_advance(items, m_ref, l_ref, acc_ref)

    @pl.when(j < n_past)
    def _():
        both_maps(lambda h: _head_cols(pk_ref, h, d, C_HEADS), lambda h: _head_cols(pv_ref, h, d, C_HEADS))

    @pl.when(j == n_past)
    def _():
        done = both_maps(lambda h: k_ref[:, h * d:(h + 1) * d], lambda h: v_ref[:, h * d:(h + 1) * d])
        for h in range(C_HEADS):
            (l0, a0), (l1, a1) = done[2 * h], done[2 * h + 1]
            o = a0 / l0 - lam_ref[0] * (a1 / l1)
            o = o * lax.rsqrt(jnp.mean(o * o, axis=-1, keepdims=True) + NORM_EPS)
            o_ref[:, h * d:(h + 1) * d] = (o * g_ref[...] * out_scale).astype(o_ref.dtype)


def diff_attention_past(grp, proj, lam, subln, out_scale, cache_k, cache_v, layer):
    H, d, lq = C_HEADS, C_V_DIM, grp.lq
    assert lq <= CHUNK and grp.lp % CHUNK == 0
    row = grp.krow()
    w = H * d
    in_specs = [pl.BlockSpec(memory_space=pltpu.SMEM)]
    in_specs += [pl.BlockSpec((lq, w), lambda b, j, n=n: (row(b), ODD_COLS[n] // w)) for n in ("qc", "kc", "vc")]
    cache_v = _keys_by_head(cache_v)
    in_specs += [_past_key_spec(cache_k, None, grp.tkp, grp.n_past_blocks),
                 _past_key_spec(cache_v, layer, grp.tkp, grp.n_past_blocks, H),
                 pl.BlockSpec((1, d), lambda b, j: (0, 0))]
    return _past_call(functools.partial(_diff_past_body, n_past=grp.n_past_blocks, out_scale=out_scale), grp, H, d,
                      2 * H, in_specs, [lam, proj, proj, proj, cache_k, cache_v, subln], "diff_attention_past")


def _mla_past_body(qa_ref, qr_ref, lat_ref, kr_ref, plat_ref, pkr_ref, o_ref, m_ref, l_ref, acc_ref, *, n_past):
    j = pl.program_id(1)
    R = lat_ref.shape[-1]
    _chain_reset(j, m_ref, l_ref, acc_ref)

    @pl.when(j < n_past)
    def _():
        lat, kr = _mx(plat_ref[...]), _mx(pkr_ref[...])
        _chains_advance([(_dot_nt(qa_ref[:, h * R:(h + 1) * R], lat)
                          + _dot_nt(qr_ref[:, h * LANES:h * LANES + D_ROPE], kr), lat) for h in range(D_HEADS)],
                        m_ref, l_ref, acc_ref)

    @pl.when(j == n_past)
    def _():
        lat, kr = _mx(lat_ref[...]), _mx(kr_ref[...])
        done = _chains_advance([(_dot_nt(qa_ref[:, h * R:(h + 1) * R], lat)
                                 + _dot_nt(qr_ref[:, h * LANES:(h + 1) * LANES], kr), lat) for h in range(D_HEADS)],
                               m_ref, l_ref, acc_ref)
        for h, (l, acc) in enumerate(done):
            o_ref[:, h * R:(h + 1) * R] = (acc / l).astype(o_ref.dtype)


def mla_attention_past(grp, q_abs, q_rope, lat, proj, cache_lat, cache_kr, layer):
    H, lq = D_HEADS, grp.lq
    assert lq <= CHUNK and grp.lp % CHUNK == 0
    R = lat.shape[1]
    row = grp.krow()
    in_specs = [pl.BlockSpec((lq, H * R), lambda b, j: (row(b), 0)),
                pl.BlockSpec((lq, H * LANES), lambda b, j: (row(b), 0)),
                pl.BlockSpec((lq, R), lambda b, j: (row(b), 0)),
                pl.BlockSpec((lq, LANES), lambda b, j: (row(b), ODD_COLS["kr"] // LANES)),
                _past_key_spec(cache_lat, layer, grp.tkp, grp.n_past_blocks),
                _past_key_spec(cache_kr, layer, grp.tkp, grp.n_past_blocks)]
    return _past_call(functools.partial(_mla_past_body, n_past=grp.n_past_blocks), grp, H, R, H, in_specs,
                      [q_abs, q_rope, lat, proj, cache_lat, cache_kr], "mla_attention_past")


def _sortable(score):
    bits = pltpu.bitcast(score, I32)
    return jnp.where(bits < 0, bits ^ jnp.int32(0x7FFFFFFF), bits)


def _dsa_body(*refs, tq, i0, nk, topk, has_past):
    if has_past:
        qi_ref, misc_ref, q_ref, ki_ref, k_ref, v_ref, pki_ref, pk_ref, pv_ref, o_ref, *sel_refs = refs
    else:
        qi_ref, misc_ref, q_ref, ki_ref, k_ref, v_ref, o_ref, *sel_refs = refs
    i = i0 + pl.program_id(1)
    wi = misc_ref[:, MISC_WI:MISC_WI + IDX_HEADS] * (IDX_HEADS * IDX_DIM) ** -0.5

    qchunk = (i * tq + lax.broadcasted_iota(I32, (tq, nk), 0)) // CHUNK
    adm_new = lax.broadcasted_iota(I32, (tq, nk), 1) // CHUNK <= qchunk
    segs = [(_mx(ki_ref[...]), k_ref, v_ref, adm_new)]
    if has_past:
        segs.insert(0, (_mx(pki_ref[0]), pk_ref.at[0], pv_ref.at[0], None))

    keys = []
    for ki, _, _, adm in segs:
        score = jnp.zeros((tq, ki.shape[0]), F32)
        for h in range(IDX_HEADS):
            s = _dot_nt(qi_ref[:, h * IDX_DIM:(h + 1) * IDX_DIM], ki)
            score = score + jnp.maximum(s, 0.0) * wi[:, h:h + 1]
        key = _sortable(jnp.where(score == 0.0, 0.0, score))
        keys.append(key if adm is None else jnp.where(adm, key, INT_MIN))

    def count(pred):
        return sum(jnp.sum(jnp.where(pred(key), 1.0, 0.0), axis=-1, keepdims=True) for key in keys)

    kf = float(topk)
    n_ge = count(lambda k: k >= 0)
    t = jnp.where(n_ge >= kf, 0, INT_MIN).astype(I32)

    def bit_step(it, carry):
        t, n_ge = carry
        cand = t | jnp.left_shift(jnp.int32(1), 30 - it)
        n_cand = count(lambda k: k >= cand)
        fits = n_cand >= kf
        return jnp.where(fits, cand, t), jnp.where(fits, n_cand, n_ge)

    t, n_ge = lax.fori_loop(0, 31, bit_step, (t, n_ge))
    for sel_ref, key, (_, _, _, adm) in zip(sel_refs, keys, segs):
        sel = (key >= t) if adm is None else (adm & (key >= t))
        sel_ref[...] = jnp.where(sel, 1, 0)

    bounded = t > INT_MIN
    crowded = jnp.logical_and(bounded, n_ge > kf)

    @pl.when(jnp.max(jnp.where(crowded, 1.0, 0.0)) > 0.0)
    def _():
        places = jnp.where(bounded, kf - count(lambda k: k > t), float(2 ** 30))
        seen = jnp.zeros((tq, 1), F32)
        for sel_ref, key, (_, _, _, adm) in zip(sel_refs, keys, segs):
            n = key.shape[1]
            blk = min(n, KEY_BLOCK)
            upper = jnp.where(lax.broadcasted_iota(I32, (blk, blk), 0) <= lax.broadcasted_iota(I32, (blk, blk), 1),
                              1.0, 0.0)
            for c in range(n // blk):
                kb = key[:, c * blk:(c + 1) * blk]
                tie = kb == t
                tie_f = jnp.where(tie, 1.0, 0.0)
                ahead = _dot(tie_f, upper) + seen
                sel = (kb > t) | (tie & (ahead <= places))
                if adm is not None:
                    sel = sel & adm[:, c * blk:(c + 1) * blk]
                sel_ref[:, c * blk:(c + 1) * blk] = jnp.where(sel, 1, 0)
                seen = seen + jnp.sum(tie_f, axis=-1, keepdims=True)

    sels = [sel_ref[...] > 0 for sel_ref in sel_refs]

    rep = A_HEADS // A_KV_HEADS
    for g in range(A_KV_HEADS):
        kv = [(_mx(k[:, g * HEAD_DIM:(g + 1) * HEAD_DIM]), _mx(v[:, g * HEAD_DIM:(g + 1) * HEAD_DIM]))
              for _, k, v, _ in segs]
        for r in range(rep):
            h = g * rep + r
            q = _mx(q_ref[:, h * HEAD_DIM:(h + 1) * HEAD_DIM] * HEAD_DIM ** -0.5)
            ss = [jnp.where(sel, _dot_nt(q, k), NEG) for sel, (k, _) in zip(sels, kv)]
            m = functools.reduce(jnp.maximum, [jnp.max(s, axis=-1, keepdims=True) for s in ss])
            ps = [jnp.exp(s - m) for s in ss]
            l = sum(jnp.sum(p, axis=-1, keepdims=True) for p in ps)
            o = sum(_dot(p, v) for p, (_, v) in zip(ps, kv))
            o_ref[:, h * HEAD_DIM:(h + 1) * HEAD_DIM] = (o / l).astype(o_ref.dtype)


def dsa_attention(grp, proj, past_ki, past_k, past_v):
    B = grp.batch
    tq = grp.tq_dsa
    n_tiles = grp.lq // tq
    wq, wkv = A_HEADS * HEAD_DIM, A_KV_HEADS * HEAD_DIM
    wqi = IDX_HEADS * IDX_DIM
    c = EVEN_COLS

    extents = sorted({grp.lq // f for f in (1, 2, 4) if grp.lq % f == 0 and (grp.lq // f) % tq == 0
                      and grp.row0 % (grp.lq // f) == 0})
    classes = []
    for i in range(n_tiles):
        need = min(grp.lq, _chunk_last_key(i, tq) + 1)
        nk = min(e for e in extents if e >= need)
        if classes and classes[-1][2] == nk:
            classes[-1][1] += 1
        else:
            classes.append([i, 1, nk])

    outs = []
    for i0, n_i, nk in classes:
        qrow = lambda b, i, i0=i0: grp.row0 // tq + b * n_tiles + i0 + i
        krow = lambda b, nk=nk: (grp.row0 + b * grp.lq) // nk
        in_specs = [pl.BlockSpec((tq, wqi), lambda b, i, f=qrow: (f(b, i), c["qi"] // wqi)),
                    pl.BlockSpec((tq, LANES), lambda b, i, f=qrow: (f(b, i), c["misc"] // LANES)),
                    pl.BlockSpec((tq, wq), lambda b, i, f=qrow: (f(b, i), c["qa"] // wq)),
                    pl.BlockSpec((nk, IDX_DIM), lambda b, i, f=krow: (f(b), c["ki"] // IDX_DIM)),
                    pl.BlockSpec((nk, wkv), lambda b, i, f=krow: (f(b), c["ka"] // wkv)),
                    pl.BlockSpec((nk, wkv), lambda b, i, f=krow: (f(b), c["va"] // wkv))]
        args = [proj] * 6
        if grp.lp:
            in_specs += [pl.BlockSpec((1, grp.lp, IDX_DIM), lambda b, i: (b, 0, 0)),
                         pl.BlockSpec((1, grp.lp, wkv), lambda b, i: (b, 0, 0)),
                         pl.BlockSpec((1, grp.lp, wkv), lambda b, i: (b, 0, 0))]
            args += [past_ki, past_k, past_v]
        o = pl.pallas_call(
            functools.partial(_dsa_body, tq=tq, i0=i0, nk=nk, topk=grp.topk, has_past=bool(grp.lp)),
            grid=(B, n_i),
            in_specs=in_specs,
            out_specs=pl.BlockSpec((tq, wq), lambda b, i, n_i=n_i: (b * n_i + i, 0)),
            out_shape=jax.ShapeDtypeStruct((B * n_i * tq, wq), MXU_DTYPE),
            scratch_shapes=[pltpu.VMEM((tq, n), I32) for n in ([grp.lp] if grp.lp else []) + [nk]],
            compiler_params=_params(2),
            name="dsa_attention",
        )(*args)
        outs.append(o.reshape(B, n_i * tq, wq))
    return jnp.concatenate(outs, axis=1).reshape(B * grp.lq, wq)


def _first_index_of_max(x, iota, n):
    mx = jnp.max(x, axis=0, keepdims=True)
    return mx, jnp.min(jnp.where(x == mx, iota, n), axis=0, keepdims=True)


def _route_body(x_ref, sc_ref, sh_ref, wr_ref, bias_ref, hp_ref, gate_ref, rank_ref, cnt_ref, h_ref, carry_ref,
                *, group):
    step = pl.program_id(0)
    tm = x_ref.shape[0]
    E = wr_ref.shape[0]
    per = E // N_GROUPS
    for g, part in enumerate(_modulated(x_ref, sc_ref, sh_ref, group)):
        h_ref[g * group:(g + 1) * group, :] = part
    h = h_ref[...]
    _store_words(hp_ref, 0, _pack_rows(h), hp_ref.shape[0] // tm)

    scores = jax.nn.sigmoid(_dot_nt(wr_ref[...], h, precision=lax.Precision.HIGHEST))
    biased = scores + bias_ref[...]
    member = lax.broadcasted_iota(I32, (per, tm), 0).astype(F32)
    gscore = []
    for g in range(N_GROUPS):
        blk = biased[g * per:(g + 1) * per, :]
        m1, i1 = _first_index_of_max(blk, member, per)
        m2 = jnp.max(jnp.where(member == i1, -jnp.inf, blk), axis=0, keepdims=True)
        gscore.append(m1 + m2)
    gscore = jnp.concatenate(gscore, axis=0)
    giota = lax.broadcasted_iota(I32, (N_GROUPS, tm), 0).astype(F32)
    gsel = jnp.zeros((N_GROUPS, tm), F32)
    for _ in range(TOPK_GROUPS):
        _, gi = _first_index_of_max(gscore, giota, N_GROUPS)
        hit = giota == gi
        gsel = jnp.where(hit, 1.0, gsel)
        gscore = jnp.where(hit, -jnp.inf, gscore)
    emask = jnp.concatenate([jnp.broadcast_to(gsel[g:g + 1, :], (per, tm)) for g in range(N_GROUPS)], axis=0)
    masked = jnp.where(emask > 0.0, biased, -jnp.inf)
    eiota = lax.broadcasted_iota(I32, (E, tm), 0).astype(F32)
    sel = jnp.zeros((E, tm), jnp.bool_)
    for _ in range(TOP_K):
        _, ei = _first_index_of_max(masked, eiota, E)
        hit = eiota == ei
        sel = sel | hit
        masked = jnp.where(hit, -jnp.inf, masked)
    w = jnp.where(sel, scores, 0.0)
    gate_ref[...] = w / jnp.sum(w, axis=0, keepdims=True) * ROUTED_SCALE

    @pl.when(step == 0)
    def _():
        carry_ref[...] = jnp.zeros(carry_ref.shape, F32)
    upper = (lax.broadcasted_iota(I32, (tm, tm), 0) <= lax.broadcasted_iota(I32, (tm, tm), 1))
    self = jnp.where(sel, 1.0, 0.0)
    incl = _dot(self, jnp.where(upper, 1.0, 0.0)) + carry_ref[...]
    rank_ref[...] = jnp.where(sel, incl - 1.0, -1.0).astype(I32)
    carry_ref[...] = incl[:, tm - 1:tm]
    cnt_ref[...] = incl[:, tm - 1:tm].astype(I32)


def route(x, scale_g, shift_g, w_router_t, r_bias, group, tm):
    T, D = x.shape
    E = w_router_t.shape[0]
    row = lambda i: (i, 0)
    col = lambda i: (0, i)
    fixed = lambda i: (0, 0)
    return pl.pallas_call(
        functools.partial(_route_body, group=group),
        grid=(T // tm,),
        in_specs=[pl.BlockSpec((tm, D), row),
                  pl.BlockSpec((1, tm // group, D), lambda i: (i, 0, 0)),
                  pl.BlockSpec((1, tm // group, D), lambda i: (i, 0, 0)),
                  pl.BlockSpec((E, D), fixed),
                  pl.BlockSpec((E, 1), fixed)],
        out_specs=[pl.BlockSpec((tm * (D // 2 // LANES), LANES), row), pl.BlockSpec((E, tm), col),
                   pl.BlockSpec((E, tm), col), pl.BlockSpec((E, 1), fixed)],
        out_shape=[jax.ShapeDtypeStruct((T * (D // 2 // LANES), LANES), I32), jax.ShapeDtypeStruct((E, T), F32),
                   jax.ShapeDtypeStruct((E, T), I32), jax.ShapeDtypeStruct((E, 1), I32)],
        scratch_shapes=[pltpu.VMEM((tm, D), F32), pltpu.VMEM((E, 1), F32)],
        compiler_params=_params(1),
        name="route",
    )(x, _per_tile(scale_g, tm, group), _per_tile(shift_g, tm, group), w_router_t, r_bias)


def _compact_body(gate_ref, rank_ref, off_ref, dest_ref, g8_ref):
    E, tm = gate_ref.shape
    rank = rank_ref[...]
    sel = rank >= 0
    dest = (rank + off_ref[...]).astype(F32)
    gate = gate_ref[...]
    eiota = lax.broadcasted_iota(I32, (E, tm), 0).astype(F32)
    dests, gates = [], []
    for _ in range(TOP_K):
        ei = jnp.min(jnp.where(sel, eiota, E), axis=0, keepdims=True)
        hit = eiota == ei
        dests.append(jnp.sum(jnp.where(hit, dest, 0.0), axis=0, keepdims=True))
        gates.append(jnp.sum(jnp.where(hit, gate, 0.0), axis=0, keepdims=True))
        sel = sel & jnp.logical_not(hit)
    dest_ref[...] = jnp.concatenate(dests, axis=0).astype(I32)
    g8_ref[...] = jnp.concatenate(gates, axis=0)


def compact_routes(gate, rank, seg_off, tm):
    E, T = gate.shape
    col = lambda i: (0, i)
    return pl.pallas_call(
        _compact_body,
        grid=(T // tm,),
        in_specs=[pl.BlockSpec((E, tm), col), pl.BlockSpec((E, tm), col), pl.BlockSpec((E, 1), lambda i: (0, 0))],
        out_specs=[pl.BlockSpec((TOP_K, tm), col), pl.BlockSpec((TOP_K, tm), col)],
        out_shape=[jax.ShapeDtypeStruct((TOP_K, T), I32), jax.ShapeDtypeStruct((TOP_K, T), F32)],
        compiler_params=_params(1),
        name="compact_routes",
    )(gate, rank, seg_off)


def _ffn(x, w1, w3, w2):
    a = _dot(x, w1)
    return _dot(a * jax.nn.sigmoid(a) * _dot(x, w3), w2)


def _gmm_body(te_ref, nt_ref, tok_ref, tok_next_ref, slot_ref, x_hbm, w1_ref, w3_ref, w2_ref, y_hbm,
              w1b, w3b, w2b, xbuf, ybuf, in_sem, out_sem, *, tm, parts, per):
    i = pl.program_id(0)
    n_used = nt_ref[0]
    cur, nxt = i % 2, (i + 1) % 2

    def token_rows(t):
        return pl.ds(pl.multiple_of(t * per, per), per)

    def row_in(tok, r, buf):
        return pltpu.make_async_copy(x_hbm.at[token_rows(tok[0, 0, r])], xbuf.at[buf, pl.ds(r * per, per)],
                                     in_sem.at[buf])

    def row_out(r, buf):
        return pltpu.make_async_copy(ybuf.at[buf, pl.ds(r * per, per)], y_hbm.at[token_rows(slot_ref[0, 0, r])],
                                     out_sem.at[buf])

    def wait_in(buf):
        pltpu.make_async_copy(x_hbm.at[pl.ds(0, tm * per)], xbuf.at[buf], in_sem.at[buf]).wait()

    def wait_out(buf):
        pltpu.make_async_copy(ybuf.at[buf], y_hbm.at[pl.ds(0, tm * per)], out_sem.at[buf]).wait()

    @pl.when(i == 0)
    def _():
        for r in range(tm):
            row_in(tok_ref, r, 0).start()
        ybuf[1] = jnp.zeros(ybuf.shape[1:], I32)
        spare = pltpu.make_async_copy(ybuf.at[1], y_hbm.at[pl.ds(y_hbm.shape[0] - tm * per, tm * per)], out_sem.at[1])
        spare.start()
        spare.wait()

    @pl.when(jnp.logical_or(i == 0, te_ref[i] != te_ref[jnp.maximum(i - 1, 0)]))
    def _():
        w1b[...] = _mx(w1_ref[...])
        w3b[...] = _mx(w3_ref[...])
        w2b[...] = _mx(w2_ref[...])

    @pl.when(i < n_used)
    def _():
        wait_in(cur)
        for r in range(tm):
            row_in(tok_next_ref, r, nxt).start()
        rows = tm // parts
        for p in range(parts):
            x = _unpack_rows(_load_words(xbuf.at[cur], p * rows, rows, per))
            _store_words(ybuf.at[cur], p * rows, _pack_rows(_ffn(x, w1b[...], w3b[...], w2b[...])), per)
            for r in range(p * rows, (p + 1) * rows):
                row_out(r, cur).start()

    @pl.when(jnp.logical_and(i >= 1, i < n_used))
    def _():
        wait_out(nxt)

    @pl.when(i == n_used - 1)
    def _():
        wait_in(nxt)
        wait_out(cur)


def grouped_ffn(tile_expert, n_tiles, row_token, row_slot, x, w1, w3, w2, layer, n_out, tm):
    n_rows = row_token.shape[0] * tm
    D, F = w1.shape[-2], w1.shape[-1]
    per = D // 2 // LANES
    clamp = lambda i, nt: jnp.maximum(jnp.minimum(i, nt[0] - 1), 0)
    wsel = lambda i, te, nt: (layer, te[i], 0, 0)
    smem_rows = lambda f: pl.BlockSpec((1, 1, tm), f, memory_space=pltpu.SMEM)
    return pl.pallas_call(
        functools.partial(_gmm_body, tm=tm, parts=2, per=per),
        grid_spec=pltpu.PrefetchScalarGridSpec(
            num_scalar_prefetch=2,
            grid=(n_rows // tm,),
            in_specs=[smem_rows(lambda i, te, nt: (clamp(i, nt), 0, 0)),
                      smem_rows(lambda i, te, nt: (clamp(i + 1, nt), 0, 0)),
                      smem_rows(lambda i, te, nt: (clamp(i, nt), 0, 0)),
                      pl.BlockSpec(memory_space=pl.ANY),
                      pl.BlockSpec((None, None, D, F), wsel),
                      pl.BlockSpec((None, None, D, F), wsel),
                      pl.BlockSpec((None, None, F, D), wsel)],
            out_specs=pl.BlockSpec(memory_space=pl.ANY),
            scratch_shapes=[pltpu.VMEM((D, F), MXU_DTYPE), pltpu.VMEM((D, F), MXU_DTYPE),
                            pltpu.VMEM((F, D), MXU_DTYPE),
                            pltpu.VMEM((2, tm * per, LANES), I32), pltpu.VMEM((2, tm * per, LANES), I32),
                            pltpu.SemaphoreType.DMA((2,)), pltpu.SemaphoreType.DMA((2,))]),
        out_shape=jax.ShapeDtypeStruct((n_out * per, LANES), I32),
        compiler_params=pltpu.CompilerParams(dimension_semantics=("arbitrary",), vmem_limit_bytes=VMEM_LIMIT_BYTES,
                                             disable_bounds_checks=True),
        name="grouped_ffn",
    )(tile_expert, n_tiles, row_token, row_token, row_slot, x, w1, w3, w2)


def _close_moe_body(h_ref, *rest, group, alpha, n_k):
    yk_refs, (g8_ref, ws1_ref, ws3_ref, ws2_ref, x_ref, gate_ref, g_ref, b_ref, o_ref) = rest[:n_k], rest[n_k:]
    tm = x_ref.shape[0]
    per = h_ref.shape[0] // tm
    words = lambda ref: _unpack_rows(_load_words(ref, 0, tm, per))
    y = _ffn(words(h_ref), ws1_ref[...], ws3_ref[...], ws2_ref[...])
    g8 = g8_ref[...]
    for k, yk_ref in enumerate(yk_refs):
        y = y + g8[:, k:k + 1] * words(yk_ref)
    r = _gated_residual(x_ref, y, gate_ref, group, alpha)
    o_ref[...] = _layer_norm_rows(r, g_ref[...], b_ref[...])


def close_moe(h, y_rows, g8, ws1, ws3, ws2, x, gate_g, ln_g, ln_b, group, alpha, tm):
    T, D = x.shape
    K = g8.shape[1]
    per = h.shape[0] // T
    row = lambda i: (i, 0)
    fixed = lambda i: (0, 0)
    return pl.pallas_call(
        functools.partial(_close_moe_body, group=group, alpha=alpha, n_k=K),
        grid=(T // tm,),
        in_specs=[pl.BlockSpec((tm * per, LANES), row)] +
                 [pl.BlockSpec((tm * per, LANES), lambda i, k=k: (k * (T // tm) + i, 0)) for k in range(K)] +
                 [pl.BlockSpec((tm, K), row),
                  pl.BlockSpec(ws1.shape, fixed),
                  pl.BlockSpec(ws3.shape, fixed),
                  pl.BlockSpec(ws2.shape, fixed),
                  pl.BlockSpec((tm, D), row),
                  pl.BlockSpec((1, tm // group, D), lambda i: (i, 0, 0)),
                  pl.BlockSpec((1, D), fixed),
                  pl.BlockSpec((1, D), fixed)],
        out_specs=pl.BlockSpec((tm, D), row),
        out_shape=jax.ShapeDtypeStruct((T, D), F32),
        compiler_params=_params(1),
        name="close_moe",
    )(h, *([y_rows] * K), g8, ws1, ws3, ws2, x, _per_tile(gate_g, tm, group), ln_g, ln_b)


def _rope_tables(pos, half):
    inv_freq = jnp.power(ROPE_THETA, -jnp.arange(half, dtype=F32) / half)
    ang = pos.astype(F32)[:, None] * inv_freq[None, :]
    cos, sin = jnp.cos(ang), jnp.sin(ang)
    reps = LANES // (2 * half)
    cos = jnp.tile(jnp.concatenate([cos, cos], axis=1), (1, reps))
    sin = jnp.tile(jnp.concatenate([-sin, sin], axis=1), (1, reps))
    return jnp.stack([jnp.ones_like(cos), cos]), jnp.stack([jnp.zeros_like(sin), sin])


def _place_cols(w, sizes, names, cols, width):
    out = jnp.zeros((w.shape[0], width), w.dtype)
    o = 0
    for size, name in zip(sizes, names):
        if name is not None:
            dst, sub = name
            out = lax.dynamic_update_slice(out, w[:, o:o + size], (0, cols[dst] + sub))
        o += size
    return out


def _moe(x, scale_g, shift_g, gate_g, ln_g, ln_b, w_router, r_bias, w1, w3, w2, ws1, ws3, ws2, layer, group, alpha):
    T, D = x.shape
    E = w1.shape[1]
    tm = 256
    t_route = _pick(T, 768, math.lcm(LANES, group))
    h, gate, rank, counts = route(x, scale_g, shift_g, w_router.T, r_bias.reshape(E, 1), group, t_route)

    counts = counts[:, 0]
    tiles_per = (counts + tm - 1) // tm
    tile_end = jnp.cumsum(tiles_per)
    seg_off = ((tile_end - tiles_per) * tm).astype(I32)
    n_tiles = (T * TOP_K) // tm + E
    used = tile_end[-1].astype(I32)
    tile_ids = jnp.minimum(jnp.arange(n_tiles, dtype=I32), used - 1)
    tile_expert = jnp.sum((tile_end[None, :] <= tile_ids[:, None]).astype(I32), axis=1)

    dest, g8 = compact_routes(gate, rank, seg_off.reshape(E, 1), t_route)
    n_pairs = TOP_K * T
    pair_slot = jnp.arange(n_pairs, dtype=I32).reshape(TOP_K, T)
    spare = n_pairs + jnp.arange(n_tiles * tm, dtype=I32) % tm
    row_slot = spare.at[dest.reshape(-1)].set(pair_slot.reshape(-1), unique_indices=True, mode="promise_in_bounds")
    row_token = jnp.where(row_slot < n_pairs, row_slot % T, 0)
    y_rows = grouped_ffn(tile_expert, used.reshape(1), row_token.reshape(n_tiles, 1, tm),
                         row_slot.reshape(n_tiles, 1, tm), h, w1, w3, w2, layer, n_pairs + tm, tm)
    return close_moe(h, y_rows, g8.T, _mx(ws1), _mx(ws3), _mx(ws2), x, gate_g, ln_g, ln_b, group, alpha,
                     _pick(T, 256, math.lcm(SUBLANES, group)))


def _fox_gates(grp, proj, b_f, past_logf):
    B, H = grp.batch, B_HEADS
    c0 = EVEN_COLS["misc"] + MISC_FL
    fl = proj[grp.row0:grp.row0 + B * grp.lq, c0:c0 + H].reshape(B, grp.lq, H)
    logf = jax.nn.log_sigmoid(fl + b_f)
    logf_all = logf if past_logf is None else jnp.concatenate([past_logf, logf], axis=1)
    cum = jnp.moveaxis(jnp.cumsum(logf_all, axis=1), 1, 2).reshape(B * H, grp.lk)
    fq = cum[:, grp.lp:, None]
    new = cum[:, grp.lp:].reshape(B * H, grp.lq // grp.tk, 1, grp.tk)
    new = jnp.pad(new, ((0, 0), (0, 0), (0, 0), (0, KEY_BLOCK - grp.tk)))
    if grp.lp:
        past = cum[:, :grp.lp].reshape(B * H, grp.n_past_blocks, 1, grp.tkp)
        past = jnp.pad(past, ((0, 0), (0, 0), (0, 0), (0, KEY_BLOCK - grp.tkp)))
        new = jnp.concatenate([past, new], axis=1)
    return logf, fq, new


def kernel(x_prompt, x_sample, c_prompt, c_sample, cache_a_k, cache_a_v, cache_a_kidx, cache_b_k, cache_b_v, cache_b_logf, cache_c_k, cache_c_v, cache_d_latent, cache_d_krope, w_in_even, b_forget, w_out_even, w_in_odd, c_lambda, c_subln, d_q_norm, d_w_uq, d_kv_norm, d_w_uk, d_w_uv, w_out_odd, ada_mix_w, ada_mix_b, ln_mix_g, ln_mix_b, ada_ffn_w, ada_ffn_b, ln_ffn_g, ln_ffn_b, router_w, router_bias, moe_w1, moe_w3, moe_w2, shared_w1, shared_w3, shared_w2):
    B, L, D = x_prompt.shape
    Bs, Ls, _ = x_sample.shape
    depth = ada_mix_w.shape[0]
    alpha = (2 * depth) ** 0.25
    past_len = cache_a_k.shape[2]
    Tp = B * L
    T = Tp + Bs * Ls
    grp_p = _Group(B, L, 0, 0)
    grp_s = _Group(Bs, Ls, past_len, Tp)
    group = math.gcd(L, Ls)
    assert group % (2 * SUBLANES) == 0
    tm_tok = _pick(T, 768, math.lcm(LANES, group))
    tm_close = _pick(T, 512, math.lcm(2 * SUBLANES, group))
    pos = jnp.concatenate([jnp.tile(jnp.arange(L), B), jnp.tile(past_len + jnp.arange(Ls), Bs)])
    c_act = jax.nn.silu(jnp.concatenate([c_prompt, c_sample], axis=0))

    def conditioning(ada_w, ada_b, layer):
        mod = matmul(c_act, ada_w, layer=layer) + ada_b[layer]
        per_group = jnp.concatenate([jnp.repeat(mod[:B], L // group, axis=0),
                                     jnp.repeat(mod[B:], Ls // group, axis=0)], axis=0)
        return jnp.split(per_group, 3, axis=-1)

    def flat_past(c):
        return c.reshape(c.shape[0], c.shape[1], -1)

    x = jnp.concatenate([x_prompt.reshape(Tp, D), x_sample.reshape(Bs * Ls, D)], axis=0)
    ev_p, ev_s, od_p, od_s = [], [], [], []
    for i in range(depth):
        j = i // 2
        shift_g, scale_g, gate_g = conditioning(ada_mix_w, ada_mix_b, i)
        if i % 2 == 0:
            sizes = (A_HEADS * HEAD_DIM, A_KV_HEADS * HEAD_DIM, A_KV_HEADS * HEAD_DIM, IDX_HEADS * IDX_DIM, IDX_DIM,
                     IDX_HEADS, B_HEADS * HEAD_DIM, B_HEADS * HEAD_DIM, B_HEADS * HEAD_DIM, B_HEADS)
            names = (("qa", 0), ("ka", 0), ("va", 0), ("qi", 0), ("ki", 0), ("misc", MISC_WI),
                     ("qf", 0), ("kf", 0), ("vf", 0), ("misc", MISC_FL))
            w_in = _mx(_place_cols(w_in_even[j], sizes, names, EVEN_COLS, EVEN_WIDTH))
            cos, sin = _rope_tables(pos, HEAD_DIM // 2)
            proj = project(x, scale_g, shift_g, w_in, cos, sin, EVEN_ROPE_RANGES, HEAD_DIM // 2, group, tm_tok)
            outs_a, outs_b = [], []
            for grp, store in ((grp_p, ev_p), (grp_s, ev_s)):
                if grp.lp:
                    logf, fq, fk = _fox_gates(grp, proj, b_forget[j], cache_b_logf[j])
                    outs_a.append(dsa_attention(grp, proj, cache_a_kidx[j], flat_past(cache_a_k[j]),
                                                flat_past(cache_a_v[j])))
                    outs_b.append(fox_attention_past(grp, proj, fq, fk, cache_b_k, cache_b_v, j))
                else:
                    logf, fq, fk = _fox_gates(grp, proj, b_forget[j], None)
                    outs_a.append(dsa_attention(grp, proj, None, None, None))
                    outs_b.append(fox_attention(grp, proj, fq, fk))
                r0, r1 = grp.row0, grp.row0 + grp.batch * grp.lq
                cut = lambda name, heads, d: proj[r0:r1, EVEN_COLS[name]:EVEN_COLS[name] + heads * d].reshape(
                    (grp.batch, grp.lq) + ((heads, d) if heads > 1 else (d,)))
                store.append((cut("ka", A_KV_HEADS, HEAD_DIM), cut("va", A_KV_HEADS, HEAD_DIM), cut("ki", 1, IDX_DIM),
                              cut("kf", B_HEADS, HEAD_DIM), cut("vf", B_HEADS, HEAD_DIM), logf))
            a1, a2 = jnp.concatenate(outs_a, axis=0), jnp.concatenate(outs_b, axis=0)
            n1 = A_HEADS * HEAD_DIM
            w1o, w2o = _mx(w_out_even[j][:n1]), _mx(w_out_even[j][n1:])
        else:
            lam_init = 0.8 - 0.6 * math.exp(-0.3 * i)
            lam_f = c_lambda[j]
            lam = (jnp.exp(jnp.sum(lam_f[0] * lam_f[1])) - jnp.exp(jnp.sum(lam_f[2] * lam_f[3])) + lam_init).reshape(1)
            q_lora, R = d_q_norm.shape[1], d_kv_norm.shape[1]
            sizes = (C_HEADS * 2 * C_QK_DIM, C_HEADS * 2 * C_QK_DIM, C_HEADS * C_V_DIM, q_lora, R, D_ROPE)
            names = (("qc", 0), ("kc", 0), ("vc", 0), ("qd", 0), ("ckv", 0), ("kr", 0))
            w_in = _mx(_place_cols(w_in_odd[j], sizes, names, ODD_COLS, ODD_WIDTH))
            cos, sin = _rope_tables(pos, C_QK_DIM // 2)
            proj = project(x, scale_g, shift_g, w_in, cos, sin, ODD_ROPE_RANGES, C_QK_DIM // 2, group, tm_tok)

            w_uq = d_w_uq[j].reshape(q_lora, D_HEADS, D_NOPE + D_ROPE)
            w_abs = bmm_precise(jnp.moveaxis(w_uq[:, :, :D_NOPE], 1, 0), jnp.transpose(d_w_uk[j], (1, 2, 0)))
            w_abs = jnp.moveaxis(w_abs, 0, 1).reshape(q_lora, D_HEADS * R)
            w_rope = jnp.pad(w_uq[:, :, D_NOPE:], ((0, 0), (0, 0), (0, LANES - D_ROPE))).reshape(q_lora, D_HEADS * LANES)
            q_abs, q_rope, lat = mla_prepare(proj, d_q_norm[j][None], d_kv_norm[j][None],
                                             _mx(jnp.concatenate([w_abs, w_rope], axis=1)), cos, sin, tm_close)
            n1 = C_HEADS * C_V_DIM
            w_od = w_out_odd[j][n1:].reshape(D_HEADS, D_V, D)
            w_lat_out = bmm_precise(jnp.transpose(d_w_uv[j], (1, 0, 2)), w_od).reshape(D_HEADS * R, D)

            outs_a, outs_b = [], []
            for grp, store in ((grp_p, od_p), (grp_s, od_s)):
                if grp.lp:
                    outs_a.append(diff_attention_past(grp, proj, lam, c_subln[j][None], 1.0 - lam_init,
                                                      flat_past(cache_c_k[j]), cache_c_v, j))
                    outs_b.append(mla_attention_past(grp, q_abs, q_rope, lat, proj, cache_d_latent, cache_d_krope, j))
                else:
                    outs_a.append(diff_attention(grp, proj, lam, c_subln[j][None], 1.0 - lam_init))
                    outs_b.append(mla_attention(grp, q_abs, q_rope, lat, proj))
                r0, r1 = grp.row0, grp.row0 + grp.batch * grp.lq
                shp = (grp.batch, grp.lq)
                oc = ODD_COLS
                store.append((proj[r0:r1, oc["kc"]:oc["kc"] + C_HEADS * 2 * C_QK_DIM].reshape(shp + (C_HEADS, 2, C_QK_DIM)),
                              proj[r0:r1, oc["vc"]:oc["vc"] + n1].reshape(shp + (C_HEADS, C_V_DIM)),
                              lat[r0:r1].reshape(shp + (R,)),
                              proj[r0:r1, oc["kr"]:oc["kr"] + D_ROPE].reshape(shp + (D_ROPE,))))
            a1, a2 = jnp.concatenate(outs_a, axis=0), jnp.concatenate(outs_b, axis=0)
            w1o, w2o = _mx(w_out_odd[j][:n1]), _mx(w_lat_out)
        x = close_mixer(a1, a2, w1o, w2o, x, gate_g, ln_mix_g[i][None], ln_mix_b[i][None], group, alpha, tm_close)

        shift_g, scale_g, gate_g = conditioning(ada_ffn_w, ada_ffn_b, i)
        x = _moe(x, scale_g, shift_g, gate_g, ln_ffn_g[i][None], ln_ffn_b[i][None], router_w[i], router_bias[i],
                 moe_w1, moe_w3, moe_w2, shared_w1[i], shared_w3[i], shared_w2[i], i, group, alpha)

    stack = lambda rows, idx: jnp.stack([r[idx] for r in rows])
    outs = [x[:Tp].reshape(B, L, D), x[Tp:].reshape(Bs, Ls, D)]
    for idx in range(6):
        outs += [stack(ev_p, idx), stack(ev_s, idx)]
    for idx in range(4):
        outs += [stack(od_p, idx), stack(od_s, idx)]
    return tuple(outs)
```

```python
import functools
import math

import jax
import jax.numpy as jnp
from jax import lax
from jax.experimental import pallas as pl
from jax.experimental.pallas import tpu as pltpu

CHUNK = 64
ROPE_THETA = 10000.0
HEAD_DIM = 128
A_HEADS = 8
A_KV_HEADS = 2
IDX_HEADS = 16
IDX_DIM = 128
A_TOPK_MAX = 256
B_HEADS = 8
C_HEADS = 8
C_QK_DIM = 64
C_V_DIM = 128
D_HEADS = 8
D_NOPE = 128
D_ROPE = 64
D_V = 128
N_GROUPS = 8
TOPK_GROUPS = 4
TOP_K = 8
ROUTED_SCALE = 2.5
NORM_EPS = 1e-6

LANES = 128
SUBLANES = 8
MXU_WIDTH = 256
VMEM_LIMIT_BYTES = 56 * 2**20
MXU_DTYPE = jnp.bfloat16

F32 = jnp.float32
I32 = jnp.int32
NEG = -1e30
INT_MIN = -2**31
KEY_BLOCK = 512
QUERY_BLOCK = 512
HEADS_PER_STEP = 2

EVEN_COLS = dict(qi=0, qa=2048, qf=3072, kf=4096, vf=5120, ka=6144, va=6400, ki=6656, misc=6912)
EVEN_WIDTH = 7168
EVEN_ROPE_RANGES = ((0, 3072), (6144, 6400), (6656, 6912))
MISC_WI, MISC_FL = 0, IDX_HEADS
ODD_COLS = dict(qc=0, kc=1024, vc=2048, qd=3072, ckv=3584, kr=3840)
ODD_WIDTH = 4096
ODD_ROPE_RANGES = ((0, 2048), (3840, 4096))


def _pick(n, target, mult):
    best = None
    for d in range(mult, min(n, target) + 1, mult):
        if n % d == 0:
            best = d
    return n if best is None else best


def _params(n_axes):
    return pltpu.CompilerParams(dimension_semantics=("arbitrary",) * n_axes,
                                vmem_limit_bytes=VMEM_LIMIT_BYTES)


def _mx(a):
    return a.astype(MXU_DTYPE)


def _dot(a, b):
    return jnp.dot(_mx(a), _mx(b), preferred_element_type=F32)


def _dot_nt(a, b, precision=None):
    if precision is None:
        a, b = _mx(a), _mx(b)
    return lax.dot_general(a, b, (((1,), (1,)), ((), ())), preferred_element_type=F32, precision=precision)


def _mm_body(a_ref, b_ref, o_ref, *, precise):
    if precise:
        o = jnp.dot(a_ref[...], b_ref[...], preferred_element_type=F32, precision=lax.Precision.HIGHEST)
    else:
        o = _dot(a_ref[...], b_ref[...])
    o_ref[...] = o.astype(o_ref.dtype)


def matmul(a, b, out_dtype=F32, precise=False, tm_target=1024, tn_target=512, layer=None):
    M, K = a.shape
    N = b.shape[-1]
    tm = _pick(M, tm_target, SUBLANES)
    tn = _pick(N, tn_target, LANES)
    if layer is None:
        b_spec = pl.BlockSpec((K, tn), lambda i, j: (0, j))
    else:
        b_spec = pl.BlockSpec((None, K, tn), lambda i, j: (layer, 0, j))
    return pl.pallas_call(
        functools.partial(_mm_body, precise=precise),
        grid=(M // tm, N // tn),
        in_specs=[pl.BlockSpec((tm, K), lambda i, j: (i, 0)), b_spec],
        out_specs=pl.BlockSpec((tm, tn), lambda i, j: (i, j)),
        out_shape=jax.ShapeDtypeStruct((M, N), out_dtype),
        compiler_params=_params(2),
        name="matmul",
    )(a, b)


def _bmm_body(a_ref, b_ref, o_ref):
    o_ref[0] = jnp.dot(a_ref[0], b_ref[0], preferred_element_type=F32,
                       precision=lax.Precision.HIGHEST).astype(o_ref.dtype)


def bmm_precise(a, b):
    H, M, K = a.shape
    N = b.shape[2]
    return pl.pallas_call(
        _bmm_body,
        grid=(H,),
        in_specs=[pl.BlockSpec((1, M, K), lambda h: (h, 0, 0)),
                  pl.BlockSpec((1, K, N), lambda h: (h, 0, 0))],
        out_specs=pl.BlockSpec((1, M, N), lambda h: (h, 0, 0)),
        out_shape=jax.ShapeDtypeStruct((H, M, N), F32),
        compiler_params=_params(1),
        name="bmm_precise",
    )(a, b)


def _per_tile(table, tm, group):
    return table.reshape(table.shape[0] * group // tm, tm // group, table.shape[1])


def _modulated(x_ref, sc_ref, sh_ref, group):
    parts = []
    for g in range(x_ref.shape[0] // group):
        rows = slice(g * group, (g + 1) * group)
        parts.append(x_ref[rows, :] * (1.0 + sc_ref[0, g:g + 1, :]) + sh_ref[0, g:g + 1, :])
    return parts


def _pack_rows(x):
    bits = pltpu.bitcast(x, I32)
    rounded = bits + 0x7FFF + ((bits >> 16) & 1)
    half = x.shape[1] // 2
    return ((rounded[:, :half] >> 16) & 0xFFFF) | (rounded[:, half:] & jnp.int32(-0x10000))


def _unpack_rows(w):
    lo = pltpu.bitcast(jnp.left_shift(w, 16), F32)
    hi = pltpu.bitcast(w & jnp.int32(-0x10000), F32)
    return jnp.concatenate([lo, hi], axis=1)


def _load_words(ref, first, n, per):
    return jnp.concatenate([ref[pl.ds(first * per + s, n, stride=per), :] for s in range(per)], axis=1)


def _store_words(ref, first, words, per):
    for s in range(per):
        ref[pl.ds(first * per + s, words.shape[0], stride=per), :] = words[:, s * LANES:(s + 1) * LANES]


def _rope_lanes(x, cos, sin, half):
    if 2 * half == LANES:
        partner = pltpu.roll(x, half, axis=1)
    else:
        lane = lax.broadcasted_iota(I32, x.shape, 1)
        lower = (lane % (2 * half)) < half
        partner = jnp.where(lower, pltpu.roll(x, LANES - half, axis=1), pltpu.roll(x, half, axis=1))
    return x * cos + partner * sin


def _layer_norm_rows(r, g, b):
    rc = r - jnp.mean(r, axis=-1, keepdims=True)
    var = jnp.mean(rc * rc, axis=-1, keepdims=True)
    return rc * lax.rsqrt(var + NORM_EPS) * g + b


def _gated_residual(x_ref, y, gate_ref, group, alpha):
    parts = []
    for g in range(x_ref.shape[0] // group):
        rows = slice(g * group, (g + 1) * group)
        parts.append(alpha * x_ref[rows, :] + (1.0 + gate_ref[0, g:g + 1, :]) * y[rows, :])
    return jnp.concatenate(parts, axis=0)


def _proj_body(kind_ref, x_ref, sc_ref, sh_ref, w_ref, *rest, group, half, n_sub):
    tables, (o_ref, h_ref) = rest[:2 * n_sub], rest[2 * n_sub:]
    j = pl.program_id(1)

    @pl.when(j == 0)
    def _():
        for g, part in enumerate(_modulated(x_ref, sc_ref, sh_ref, group)):
            h_ref[g * group:(g + 1) * group, :] = part.astype(h_ref.dtype)

    for c in range(n_sub):
        cos, sin = tables[2 * c][...], tables[2 * c + 1][...]
        acc = jnp.dot(h_ref[...], w_ref[:, c * MXU_WIDTH:(c + 1) * MXU_WIDTH], preferred_element_type=F32)
        for p in range(MXU_WIDTH // LANES):
            o_ref[:, c * MXU_WIDTH + p * LANES:c * MXU_WIDTH + (p + 1) * LANES] = _rope_lanes(
                acc[:, p * LANES:(p + 1) * LANES], cos, sin, half)


def project(x, scale_g, shift_g, w, cos, sin, rope_ranges, half, group, tm):
    T, D = x.shape
    N = w.shape[1]
    n_sub = 4
    tn = n_sub * MXU_WIDTH
    start = jnp.arange(N // MXU_WIDTH) * MXU_WIDTH
    kinds = sum(((start >= lo) & (start < hi)).astype(I32) for lo, hi in rope_ranges)
    row = lambda i, j, k: (i, 0)
    tile = lambda i, j, k: (i, 0, 0)
    table_specs, table_args = [], []
    for c in range(n_sub):
        pick = lambda i, j, k, c=c: (k[j * n_sub + c], i, 0)
        table_specs += [pl.BlockSpec((None, tm, LANES), pick)] * 2
        table_args += [cos, sin]
    return pl.pallas_call(
        functools.partial(_proj_body, group=group, half=half, n_sub=n_sub),
        grid_spec=pltpu.PrefetchScalarGridSpec(
            num_scalar_prefetch=1,
            grid=(T // tm, N // tn),
            in_specs=[pl.BlockSpec((tm, D), row),
                      pl.BlockSpec((1, tm // group, D), tile),
                      pl.BlockSpec((1, tm // group, D), tile),
                      pl.BlockSpec((D, tn), lambda i, j, k: (0, j))] + table_specs,
            out_specs=pl.BlockSpec((tm, tn), lambda i, j, k: (i, j)),
            scratch_shapes=[pltpu.VMEM((tm, D), MXU_DTYPE)]),
        out_shape=jax.ShapeDtypeStruct((T, N), F32),
        compiler_params=_params(2),
        name="project",
    )(kinds, x, _per_tile(scale_g, tm, group), _per_tile(shift_g, tm, group), w, *table_args)


def _close_mix_body(a1_ref, a2_ref, w1_ref, w2_ref, x_ref, gate_ref, g_ref, b_ref, o_ref, *, group, alpha):
    y = _dot(a1_ref[...], w1_ref[...]) + _dot(a2_ref[...], w2_ref[...])
    r = _gated_residual(x_ref, y, gate_ref, group, alpha)
    o_ref[...] = _layer_norm_rows(r, g_ref[...], b_ref[...])


def close_mixer(a1, a2, w1, w2, x, gate_g, ln_g, ln_b, group, alpha, tm):
    T, D = x.shape
    row = lambda i: (i, 0)
    fixed = lambda i: (0, 0)
    return pl.pallas_call(
        functools.partial(_close_mix_body, group=group, alpha=alpha),
        grid=(T // tm,),
        in_specs=[pl.BlockSpec((tm, a1.shape[1]), row),
                  pl.BlockSpec((tm, a2.shape[1]), row),
                  pl.BlockSpec(w1.shape, fixed),
                  pl.BlockSpec(w2.shape, fixed),
                  pl.BlockSpec((tm, D), row),
                  pl.BlockSpec((1, tm // group, D), lambda i: (i, 0, 0)),
                  pl.BlockSpec((1, D), fixed),
                  pl.BlockSpec((1, D), fixed)],
        out_specs=pl.BlockSpec((tm, D), row),
        out_shape=jax.ShapeDtypeStruct((T, D), F32),
        compiler_params=_params(1),
        name="close_mixer",
    )(a1, a2, w1, w2, x, _per_tile(gate_g, tm, group), ln_g, ln_b)


def _online_step(s, v, carry):
    m, l, acc = carry
    m_new = jnp.maximum(m, jnp.max(s, axis=-1, keepdims=True))
    p = jnp.exp(s - m_new)
    alpha = jnp.exp(m - m_new)
    l = alpha * l + jnp.sum(p, axis=-1, keepdims=True)
    acc = alpha * acc + _dot(p, v)
    return m_new, l, acc


def _init_carry(tq, dv):
    return (jnp.full((tq, 1), NEG, F32), jnp.zeros((tq, 1), F32), jnp.zeros((tq, dv), F32))


def _chunk_last_key(i, tq):
    return ((i + 1) * tq - 1) // CHUNK * CHUNK + CHUNK - 1


class _Group:
    def __init__(self, batch, lq, lp, row0):
        self.batch, self.lq, self.lp, self.row0 = batch, lq, lp, row0
        self.lk = lp + lq
        self.tq = _pick(lq, QUERY_BLOCK, SUBLANES)
        self.tq_dsa = _pick(lq, 128, SUBLANES)
        self.tk = _pick(lq, KEY_BLOCK, SUBLANES)
        self.tkp = _pick(lp, KEY_BLOCK, LANES) if lp else 0
        self.n_past_blocks = lp // self.tkp if lp else 0
        self.topk = min(A_TOPK_MAX, self.lk // 4)
        assert lp % CHUNK == 0 and row0 % lq == 0 and lq % self.tk == 0

    def qrow(self, tq):
        base, per = self.row0 // tq, self.lq // tq
        return lambda b, i: base + b * per + i

    def krow(self):
        base = self.row0 // self.lq
        return lambda b: base + b


def _new_key_blocks(i, tq, grp_lq, tk, causal_last):
    return jnp.minimum(causal_last, grp_lq - 1) // tk + 1


def _fox_body(q_ref, k_ref, v_ref, fq_ref, fk_ref, o_ref, *, tq, tk, lq, hp):
    i = pl.program_id(2)
    d = HEAD_DIM
    head = lambda x, u: x[:, u * d:(u + 1) * d]
    q = _mx(q_ref[...] * d ** -0.5)
    carry = tuple(_init_carry(tq, d) for _ in range(hp))
    qpos = i * tq + lax.broadcasted_iota(I32, (tq, tk), 0)

    def new_step(j, c):
        off = pl.multiple_of(j * tk, tk)
        k, v = k_ref[pl.ds(off, tk), :], v_ref[pl.ds(off, tk), :]
        causal = off + lax.broadcasted_iota(I32, (tq, tk), 1) <= qpos
        out = []
        for u in range(hp):
            s = _dot_nt(head(q, u), head(k, u)) + fq_ref[u] - fk_ref[u, j][:, :tk]
            out.append(_online_step(jnp.where(causal, s, NEG), head(v, u), c[u]))
        return tuple(out)

    nb = _new_key_blocks(i, tq, lq, tk, (i + 1) * tq - 1)
    carry = lax.fori_loop(0, nb, new_step, carry)
    for u, (m, l, acc) in enumerate(carry):
        o_ref[:, u * d:(u + 1) * d] = (acc / l).astype(o_ref.dtype)


def fox_attention(grp, proj, fq, fk):
    assert grp.lp == 0
    B, H, hp = grp.batch, B_HEADS, HEADS_PER_STEP
    w = hp * HEAD_DIM
    tq = grp.tq
    qrow, krow = grp.qrow(tq), grp.krow()
    cq, ck, cv = (EVEN_COLS[n] // w for n in ("qf", "kf", "vf"))
    in_specs = [pl.BlockSpec((tq, w), lambda b, h, i: (qrow(b, i), cq + h)),
                pl.BlockSpec((grp.lq, w), lambda b, h, i: (krow(b), ck + h)),
                pl.BlockSpec((grp.lq, w), lambda b, h, i: (krow(b), cv + h))]
    in_specs += [pl.BlockSpec((hp, tq, 1), lambda b, h, i: (b * (H // hp) + h, i, 0)),
                 pl.BlockSpec((hp,) + fk.shape[1:], lambda b, h, i: (b * (H // hp) + h, 0, 0, 0))]
    return pl.pallas_call(
        functools.partial(_fox_body, tq=tq, tk=grp.tk, lq=grp.lq, hp=hp),
        grid=(B, H // hp, grp.lq // tq),
        in_specs=in_specs,
        out_specs=pl.BlockSpec((tq, w), lambda b, h, i: (b * (grp.lq // tq) + i, h)),
        out_shape=jax.ShapeDtypeStruct((B * grp.lq, H * HEAD_DIM), MXU_DTYPE),
        compiler_params=_params(3),
        name="fox_attention",
    )(proj, proj, proj, fq, fk)


def _diff_body(lam_ref, q_ref, k_ref, v_ref, g_ref, o_ref, *, tq, tk, lq, hp, out_scale):
    i = pl.program_id(2)
    d = C_V_DIM
    head = lambda x, u: x[:, u * d:(u + 1) * d]
    q = q_ref[...] * C_QK_DIM ** -0.5
    lane = lax.broadcasted_iota(I32, q.shape, 1) % (2 * C_QK_DIM)
    qs = (_mx(jnp.where(lane < C_QK_DIM, q, 0.0)), _mx(jnp.where(lane >= C_QK_DIM, q, 0.0)))
    carry = tuple(_init_carry(tq, d) for _ in range(2 * hp))

    def step(k, v, ok, c):
        out = []
        for u in range(hp):
            ku, vu = _mx(head(k, u)), _mx(head(v, u))
            for t in range(2):
                s = _dot_nt(head(qs[t], u), ku)
                out.append(_online_step(jnp.where(ok, s, NEG), vu, c[2 * u + t]))
        return tuple(out)

    qchunk = (i * tq + lax.broadcasted_iota(I32, (tq, tk), 0)) // CHUNK

    def new_step(j, c):
        off = pl.multiple_of(j * tk, tk)
        ok = (off + lax.broadcasted_iota(I32, (tq, tk), 1)) // CHUNK <= qchunk
        return step(k_ref[pl.ds(off, tk), :], v_ref[pl.ds(off, tk), :], ok, c)

    nb = _new_key_blocks(i, tq, lq, tk, _chunk_last_key(i, tq))
    carry = lax.fori_loop(0, nb, new_step, carry)
    for u in range(hp):
        (_, l0, a0), (_, l1, a1) = carry[2 * u], carry[2 * u + 1]
        o = a0 / l0 - lam_ref[0] * (a1 / l1)
        o = o * lax.rsqrt(jnp.mean(o * o, axis=-1, keepdims=True) + NORM_EPS)
        o_ref[:, u * d:(u + 1) * d] = (o * g_ref[...] * out_scale).astype(o_ref.dtype)


def diff_attention(grp, proj, lam, subln, out_scale):
    assert grp.lp == 0
    B, H, hp = grp.batch, C_HEADS, HEADS_PER_STEP
    w = hp * C_V_DIM
    tq = grp.tq
    qrow, krow = grp.qrow(tq), grp.krow()
    cq, ck, cv = (ODD_COLS[n] // w for n in ("qc", "kc", "vc"))
    in_specs = [pl.BlockSpec(memory_space=pltpu.SMEM),
                pl.BlockSpec((tq, w), lambda b, h, i: (qrow(b, i), cq + h)),
                pl.BlockSpec((grp.lq, w), lambda b, h, i: (krow(b), ck + h)),
                pl.BlockSpec((grp.lq, w), lambda b, h, i: (krow(b), cv + h))]
    in_specs += [pl.BlockSpec((1, C_V_DIM), lambda b, h, i: (0, 0))]
    return pl.pallas_call(
        functools.partial(_diff_body, tq=tq, tk=grp.tk, lq=grp.lq, hp=hp, out_scale=out_scale),
        grid=(B, H // hp, grp.lq // tq),
        in_specs=in_specs,
        out_specs=pl.BlockSpec((tq, w), lambda b, h, i: (b * (grp.lq // tq) + i, h)),
        out_shape=jax.ShapeDtypeStruct((B * grp.lq, H * C_V_DIM), MXU_DTYPE),
        compiler_params=_params(3),
        name="diff_attention",
    )(lam, proj, proj, proj, subln)


def _mla_prep_body(qd_ref, ckv_ref, qg_ref, kg_ref, w_ref, cos_ref, sin_ref, qa_ref, qr_ref, lat_ref, *, n_abs):
    qd = qd_ref[...]
    qn = qd * lax.rsqrt(jnp.mean(qd * qd, axis=-1, keepdims=True) + NORM_EPS) * qg_ref[...]
    q = _dot(qn, w_ref[...]) * (D_NOPE + D_ROPE) ** -0.5
    qa_ref[...] = q[:, :n_abs].astype(qa_ref.dtype)
    for h in range((q.shape[1] - n_abs) // LANES):
        lanes = slice(n_abs + h * LANES, n_abs + (h + 1) * LANES)
        qr_ref[:, h * LANES:(h + 1) * LANES] = _rope_lanes(
            q[:, lanes], cos_ref[...], sin_ref[...], D_ROPE // 2).astype(qr_ref.dtype)
    ckv = ckv_ref[...]
    lat_ref[...] = ckv * lax.rsqrt(jnp.mean(ckv * ckv, axis=-1, keepdims=True) + NORM_EPS) * kg_ref[...]


def mla_prepare(proj, q_norm, kv_norm, w_q, cos, sin, tm):
    T = proj.shape[0]
    q_lora, R = q_norm.shape[1], kv_norm.shape[1]
    n_abs = D_HEADS * R
    n_rope = D_HEADS * LANES
    row = lambda i: (i, 0)
    fixed = lambda i: (0, 0)
    return pl.pallas_call(
        functools.partial(_mla_prep_body, n_abs=n_abs),
        grid=(T // tm,),
        in_specs=[pl.BlockSpec((tm, q_lora), lambda i: (i, ODD_COLS["qd"] // q_lora)),
                  pl.BlockSpec((tm, R), lambda i: (i, ODD_COLS["ckv"] // R)),
                  pl.BlockSpec((1, q_lora), fixed),
                  pl.BlockSpec((1, R), fixed),
                  pl.BlockSpec(w_q.shape, fixed),
                  pl.BlockSpec((None, tm, LANES), lambda i: (1, i, 0)),
                  pl.BlockSpec((None, tm, LANES), lambda i: (1, i, 0))],
        out_specs=[pl.BlockSpec((tm, n_abs), row), pl.BlockSpec((tm, n_rope), row), pl.BlockSpec((tm, R), row)],
        out_shape=[jax.ShapeDtypeStruct((T, n_abs), MXU_DTYPE), jax.ShapeDtypeStruct((T, n_rope), MXU_DTYPE),
                   jax.ShapeDtypeStruct((T, R), F32)],
        compiler_params=_params(1),
        name="mla_prepare",
    )(proj, proj, q_norm, kv_norm, w_q, cos, sin)


def _mla_body(qa_ref, qr_ref, lat_ref, kr_ref, o_ref, *, tq, tk, lq, hp):
    i = pl.program_id(2)
    R = lat_ref.shape[-1]
    qa = [qa_ref[:, u * R:(u + 1) * R] for u in range(hp)]
    qr = [qr_ref[:, u * LANES:(u + 1) * LANES] for u in range(hp)]
    carry = tuple(_init_carry(tq, R) for _ in range(hp))

    qchunk = (i * tq + lax.broadcasted_iota(I32, (tq, tk), 0)) // CHUNK

    def new_step(j, c):
        off = pl.multiple_of(j * tk, tk)
        lat = _mx(lat_ref[pl.ds(off, tk), :])
        kr = _mx(kr_ref[pl.ds(off, tk), :])
        ok = (off + lax.broadcasted_iota(I32, (tq, tk), 1)) // CHUNK <= qchunk
        return tuple(_online_step(jnp.where(ok, _dot_nt(qa[u], lat) + _dot_nt(qr[u], kr), NEG), lat, c[u])
                     for u in range(hp))

    nb = _new_key_blocks(i, tq, lq, tk, _chunk_last_key(i, tq))
    carry = lax.fori_loop(0, nb, new_step, carry)
    for u, (m, l, acc) in enumerate(carry):
        o_ref[:, u * R:(u + 1) * R] = (acc / l).astype(o_ref.dtype)


def mla_attention(grp, q_abs, q_rope, lat, proj):
    assert grp.lp == 0
    B, H, hp = grp.batch, D_HEADS, HEADS_PER_STEP
    R = lat.shape[1]
    tq = grp.tq
    qrow, krow = grp.qrow(tq), grp.krow()
    in_specs = [pl.BlockSpec((tq, hp * R), lambda b, h, i: (qrow(b, i), h)),
                pl.BlockSpec((tq, hp * LANES), lambda b, h, i: (qrow(b, i), h)),
                pl.BlockSpec((grp.lq, R), lambda b, h, i: (krow(b), 0)),
                pl.BlockSpec((grp.lq, LANES), lambda b, h, i: (krow(b), ODD_COLS["kr"] // LANES))]
    return pl.pallas_call(
        functools.partial(_mla_body, tq=tq, tk=grp.tk, lq=grp.lq, hp=hp),
        grid=(B, H // hp, grp.lq // tq),
        in_specs=in_specs,
        out_specs=pl.BlockSpec((tq, hp * R), lambda b, h, i: (b * (grp.lq // tq) + i, h)),
        out_shape=jax.ShapeDtypeStruct((B * grp.lq, H * R), MXU_DTYPE),
        compiler_params=_params(3),
        name="mla_attention",
    )(q_abs, q_rope, lat, proj)


def _head_cols(ref, h, d, heads):
    if ref.shape[-1] == d:
        return ref[pl.ds(h, ref.shape[0] // heads, stride=heads), :]
    return ref[:, h * d:(h + 1) * d]


def _chain_reset(j, m_ref, l_ref, acc_ref):
    @pl.when(j == 0)
    def _():
        m_ref[...] = jnp.full(m_ref.shape, NEG, F32)
        l_ref[...] = jnp.zeros(l_ref.shape, F32)
        acc_ref[...] = jnp.zeros(acc_ref.shape, F32)


def _chains_advance(scores_values, m_ref, l_ref, acc_ref):
    states = [(m_ref[c], l_ref[c], acc_ref[c]) for c in range(len(scores_values))]
    new = [_online_step(s, v, st) for (s, v), st in zip(scores_values, states)]
    for c, (m, l, acc) in enumerate(new):
        m_ref[c], l_ref[c], acc_ref[c] = m, l, acc
    return [(l, acc) for _, l, acc in new]


def _past_key_spec(cache, layer, tkp, n_past, rows_per_key=1):
    blk = lambda j: jnp.minimum(j, n_past - 1)
    rows, w = tkp * rows_per_key, cache.shape[-1]
    if layer is None:
        return pl.BlockSpec((None, rows, w), lambda b, j: (b, blk(j), 0))
    return pl.BlockSpec((None, None, rows, w), lambda b, j: (layer, b, blk(j), 0))


def _keys_by_head(cache):
    n, b, p, h, d = cache.shape
    return cache.reshape(n, b, p * h, d)


def _past_call(body, grp, heads, dv, chains, in_specs, args, name):
    lq = grp.lq
    assert grp.lq == grp.tk
    return pl.pallas_call(
        body,
        grid=(grp.batch, grp.n_past_blocks + 1),
        in_specs=in_specs,
        out_specs=pl.BlockSpec((lq, heads * dv), lambda b, j: (b, 0)),
        out_shape=jax.ShapeDtypeStruct((grp.batch * lq, heads * dv), MXU_DTYPE),
        scratch_shapes=[pltpu.VMEM((chains, lq, 1), F32), pltpu.VMEM((chains, lq, 1), F32),
                        pltpu.VMEM((chains, lq, dv), F32)],
        compiler_params=_params(2),
        name=name,
    )(*args)


def _fox_past_body(q_ref, k_ref, v_ref, pk_ref, pv_ref, fq_ref, fk_ref, o_ref, m_ref, l_ref, acc_ref, *, n_past):
    j = pl.program_id(1)
    d, lq = HEAD_DIM, q_ref.shape[0]
    _chain_reset(j, m_ref, l_ref, acc_ref)
    q = lambda h: _mx(q_ref[:, h * d:(h + 1) * d] * d ** -0.5)

    @pl.when(j < n_past)
    def _():
        _chains_advance([(_dot_nt(q(h), _head_cols(pk_ref, h, d, B_HEADS)) + fq_ref[h] - fk_ref[h, 0],
                          _head_cols(pv_ref, h, d, B_HEADS)) for h in range(B_HEADS)], m_ref, l_ref, acc_ref)

    @pl.when(j == n_past)
    def _():
        causal = lax.broadcasted_iota(I32, (lq, lq), 1) <= lax.broadcasted_iota(I32, (lq, lq), 0)
        logits = lambda h: _dot_nt(q(h), k_ref[:, h * d:(h + 1) * d]) + fq_ref[h] - fk_ref[h, 0][:, :lq]
        done = _chains_advance([(jnp.where(causal, logits(h), NEG), v_ref[:, h * d:(h + 1) * d])
                                for h in range(B_HEADS)], m_ref, l_ref, acc_ref)
        for h, (l, acc) in enumerate(done):
            o_ref[:, h * d:(h + 1) * d] = (acc / l).astype(o_ref.dtype)


def fox_attention_past(grp, proj, fq, fk, cache_k, cache_v, layer):
    H, d, lq = B_HEADS, HEAD_DIM, grp.lq
    row = grp.krow()
    w = H * d
    in_specs = [pl.BlockSpec((lq, w), lambda b, j, n=n: (row(b), EVEN_COLS[n] // w)) for n in ("qf", "kf", "vf")]
    cache_k, cache_v = _keys_by_head(cache_k), _keys_by_head(cache_v)
    in_specs += [_past_key_spec(cache_k, layer, grp.tkp, grp.n_past_blocks, H),
                 _past_key_spec(cache_v, layer, grp.tkp, grp.n_past_blocks, H),
                 pl.BlockSpec((H, lq, 1), lambda b, j: (b, 0, 0)),
                 pl.BlockSpec((H, 1, 1, fk.shape[-1]), lambda b, j: (b, j, 0, 0))]
    return _past_call(functools.partial(_fox_past_body, n_past=grp.n_past_blocks), grp, H, d, H, in_specs,
                      [proj, proj, proj, cache_k, cache_v, fq, fk], "fox_attention_past")


def _diff_past_body(lam_ref, q_ref, k_ref, v_ref, pk_ref, pv_ref, g_ref, o_ref, m_ref, l_ref, acc_ref,
                    *, n_past, out_scale):
    j = pl.program_id(1)
    d = C_V_DIM
    _chain_reset(j, m_ref, l_ref, acc_ref)
    lane = lax.broadcasted_iota(I32, (q_ref.shape[0], d), 1)

    def maps(h):
        q = q_ref[:, h * d:(h + 1) * d] * C_QK_DIM ** -0.5
        return _mx(jnp.where(lane < C_QK_DIM, q, 0.0)), _mx(jnp.where(lane >= C_QK_DIM, q, 0.0))

    def both_maps(key_of, value_of):
        items = []
        for h in range(C_HEADS):
            k, v = _mx(key_of(h)), _mx(value_of(h))
            items += [(_dot_nt(qt, k), v) for qt in maps(h)]
        return _chains_advance(items, m_ref, l_ref, acc_ref)

    @pl.when(j < n_past)
    def _():
        both_maps(lambda h: _head_cols(pk_ref, h, d, C_HEADS), lambda h: _head_cols(pv_ref, h, d, C_HEADS))

    @pl.when(j == n_past)
    def _():
        done = both_maps(lambda h: k_ref[:, h * d:(h + 1) * d], lambda h: v_ref[:, h * d:(h + 1) * d])
        for h in range(C_HEADS):
            (l0, a0), (l1, a1) = done[2 * h], done[2 * h + 1]
            o = a0 / l0 - lam_ref[0] * (a1 / l1)
            o = o * lax.rsqrt(jnp.mean(o * o, axis=-1, keepdims=True) + NORM_EPS)
            o_ref[:, h * d:(h + 1) * d] = (o * g_ref[...] * out_scale).astype(o_ref.dtype)


def diff_attention_past(grp, proj, lam, subln, out_scale, cache_k, cache_v, layer):
    H, d, lq = C_HEADS, C_V_DIM, grp.lq
    assert lq <= CHUNK and grp.lp % CHUNK == 0
    row = grp.krow()
    w = H * d
    in_specs = [pl.BlockSpec(memory_space=pltpu.SMEM)]
    in_specs += [pl.BlockSpec((lq, w), lambda b, j, n=n: (row(b), ODD_COLS[n] // w)) for n in ("qc", "kc", "vc")]
    cache_v = _keys_by_head(cache_v)
    in_specs += [_past_key_spec(cache_k, None, grp.tkp, grp.n_past_blocks),
                 _past_key_spec(cache_v, layer, grp.tkp, grp.n_past_blocks, H),
                 pl.BlockSpec((1, d), lambda b, j: (0, 0))]
    return _past_call(functools.partial(_diff_past_body, n_past=grp.n_past_blocks, out_scale=out_scale), grp, H, d,
                      2 * H, in_specs, [lam, proj, proj, proj, cache_k, cache_v, subln], "diff_attention_past")


def _mla_past_body(qa_ref, qr_ref, lat_ref, kr_ref, plat_ref, pkr_ref, o_ref, m_ref, l_ref, acc_ref, *, n_past):
    j = pl.program_id(1)
    R = lat_ref.shape[-1]
    _chain_reset(j, m_ref, l_ref, acc_ref)

    @pl.when(j < n_past)
    def _():
        lat, kr = _mx(plat_ref[...]), _mx(pkr_ref[...])
        _chains_advance([(_dot_nt(qa_ref[:, h * R:(h + 1) * R], lat)
                          + _dot_nt(qr_ref[:, h * LANES:h * LANES + D_ROPE], kr), lat) for h in range(D_HEADS)],
                        m_ref, l_ref, acc_ref)

    @pl.when(j == n_past)
    def _():
        lat, kr = _mx(lat_ref[...]), _mx(kr_ref[...])
        done = _chains_advance([(_dot_nt(qa_ref[:, h * R:(h + 1) * R], lat)
                                 + _dot_nt(qr_ref[:, h * LANES:(h + 1) * LANES], kr), lat) for h in range(D_HEADS)],
                               m_ref, l_ref, acc_ref)
        for h, (l, acc) in enumerate(done):
            o_ref[:, h * R:(h + 1) * R] = (acc / l).astype(o_ref.dtype)


def mla_attention_past(grp, q_abs, q_rope, lat, proj, cache_lat, cache_kr, layer):
    H, lq = D_HEADS, grp.lq
    assert lq <= CHUNK and grp.lp % CHUNK == 0
    R = lat.shape[1]
    row = grp.krow()
    in_specs = [pl.BlockSpec((lq, H * R), lambda b, j: (row(b), 0)),
                pl.BlockSpec((lq, H * LANES), lambda b, j: (row(b), 0)),
                pl.BlockSpec((lq, R), lambda b, j: (row(b), 0)),
                pl.BlockSpec((lq, LANES), lambda b, j: (row(b), ODD_COLS["kr"] // LANES)),
                _past_key_spec(cache_lat, layer, grp.tkp, grp.n_past_blocks),
                _past_key_spec(cache_kr, layer, grp.tkp, grp.n_past_blocks)]
    return _past_call(functools.partial(_mla_past_body, n_past=grp.n_past_blocks), grp, H, R, H, in_specs,
                      [q_abs, q_rope, lat, proj, cache_lat, cache_kr], "mla_attention_past")


def _sortable(score):
    bits = pltpu.bitcast(score, I32)
    return jnp.where(bits < 0, bits ^ jnp.int32(0x7FFFFFFF), bits)


def _dsa_body(*refs, tq, i0, nk, topk, has_past):
    if has_past:
        qi_ref, misc_ref, q_ref, ki_ref, k_ref, v_ref, pki_ref, pk_ref, pv_ref, o_ref, *sel_refs = refs
    else:
        qi_ref, misc_ref, q_ref, ki_ref, k_ref, v_ref, o_ref, *sel_refs = refs
    i = i0 + pl.program_id(1)
    wi = misc_ref[:, MISC_WI:MISC_WI + IDX_HEADS] * (IDX_HEADS * IDX_DIM) ** -0.5

    qchunk = (i * tq + lax.broadcasted_iota(I32, (tq, nk), 0)) // CHUNK
    adm_new = lax.broadcasted_iota(I32, (tq, nk), 1) // CHUNK <= qchunk
    segs = [(_mx(ki_ref[...]), k_ref, v_ref, adm_new)]
    if has_past:
        segs.insert(0, (_mx(pki_ref[0]), pk_ref.at[0], pv_ref.at[0], None))

    keys = []
    for ki, _, _, adm in segs:
        score = jnp.zeros((tq, ki.shape[0]), F32)
        for h in range(IDX_HEADS):
            s = _dot_nt(qi_ref[:, h * IDX_DIM:(h + 1) * IDX_DIM], ki)
            score = score + jnp.maximum(s, 0.0) * wi[:, h:h + 1]
        key = _sortable(jnp.where(score == 0.0, 0.0, score))
        keys.append(key if adm is None else jnp.where(adm, key, INT_MIN))

    def count(pred):
        return sum(jnp.sum(jnp.where(pred(key), 1.0, 0.0), axis=-1, keepdims=True) for key in keys)

    kf = float(topk)
    n_ge = count(lambda k: k >= 0)
    t = jnp.where(n_ge >= kf, 0, INT_MIN).astype(I32)

    def bit_step(it, carry):
        t, n_ge = carry
        cand = t | jnp.left_shift(jnp.int32(1), 30 - it)
        n_cand = count(lambda k: k >= cand)
        fits = n_cand >= kf
        return jnp.where(fits, cand, t), jnp.where(fits, n_cand, n_ge)

    t, n_ge = lax.fori_loop(0, 31, bit_step, (t, n_ge))
    for sel_ref, key, (_, _, _, adm) in zip(sel_refs, keys, segs):
        sel = (key >= t) if adm is None else (adm & (key >= t))
        sel_ref[...] = jnp.where(sel, 1, 0)

    bounded = t > INT_MIN
    crowded = jnp.logical_and(bounded, n_ge > kf)

    @pl.when(jnp.max(jnp.where(crowded, 1.0, 0.0)) > 0.0)
    def _():
        places = jnp.where(bounded, kf - count(lambda k: k > t), float(2 ** 30))
        seen = jnp.zeros((tq, 1), F32)
        for sel_ref, key, (_, _, _, adm) in zip(sel_refs, keys, segs):
            n = key.shape[1]
            blk = min(n, KEY_BLOCK)
            upper = jnp.where(lax.broadcasted_iota(I32, (blk, blk), 0) <= lax.broadcasted_iota(I32, (blk, blk), 1),
                              1.0, 0.0)
            for c in range(n // blk):
                kb = key[:, c * blk:(c + 1) * blk]
                tie = kb == t
                tie_f = jnp.where(tie, 1.0, 0.0)
                ahead = _dot(tie_f, upper) + seen
                sel = (kb > t) | (tie & (ahead <= places))
                if adm is not None:
                    sel = sel & adm[:, c * blk:(c + 1) * blk]
                sel_ref[:, c * blk:(c + 1) * blk] = jnp.where(sel, 1, 0)
                seen = seen + jnp.sum(tie_f, axis=-1, keepdims=True)

    sels = [sel_ref[...] > 0 for sel_ref in sel_refs]

    rep = A_HEADS // A_KV_HEADS
    for g in range(A_KV_HEADS):
        kv = [(_mx(k[:, g * HEAD_DIM:(g + 1) * HEAD_DIM]), _mx(v[:, g * HEAD_DIM:(g + 1) * HEAD_DIM]))
              for _, k, v, _ in segs]
        for r in range(rep):
            h = g * rep + r
            q = _mx(q_ref[:, h * HEAD_DIM:(h + 1) * HEAD_DIM] * HEAD_DIM ** -0.5)
            ss = [jnp.where(sel, _dot_nt(q, k), NEG) for sel, (k, _) in zip(sels, kv)]
            m = functools.reduce(jnp.maximum, [jnp.max(s, axis=-1, keepdims=True) for s in ss])
            ps = [jnp.exp(s - m) for s in ss]
            l = sum(jnp.sum(p, axis=-1, keepdims=True) for p in ps)
            o = sum(_dot(p, v) for p, (_, v) in zip(ps, kv))
            o_ref[:, h * HEAD_DIM:(h + 1) * HEAD_DIM] = (o / l).astype(o_ref.dtype)


def dsa_attention(grp, proj, past_ki, past_k, past_v):
    B = grp.batch
    tq = grp.tq_dsa
    n_tiles = grp.lq // tq
    wq, wkv = A_HEADS * HEAD_DIM, A_KV_HEADS * HEAD_DIM
    wqi = IDX_HEADS * IDX_DIM
    c = EVEN_COLS

    extents = sorted({grp.lq // f for f in (1, 2, 4) if grp.lq % f == 0 and (grp.lq // f) % tq == 0
                      and grp.row0 % (grp.lq // f) == 0})
    classes = []
    for i in range(n_tiles):
        need = min(grp.lq, _chunk_last_key(i, tq) + 1)
        nk = min(e for e in extents if e >= need)
        if classes and classes[-1][2] == nk:
            classes[-1][1] += 1
        else:
            classes.append([i, 1, nk])

    outs = []
    for i0, n_i, nk in classes:
        qrow = lambda b, i, i0=i0: grp.row0 // tq + b * n_tiles + i0 + i
        krow = lambda b, nk=nk: (grp.row0 + b * grp.lq) // nk
        in_specs = [pl.BlockSpec((tq, wqi), lambda b, i, f=qrow: (f(b, i), c["qi"] // wqi)),
                    pl.BlockSpec((tq, LANES), lambda b, i, f=qrow: (f(b, i), c["misc"] // LANES)),
                    pl.BlockSpec((tq, wq), lambda b, i, f=qrow: (f(b, i), c["qa"] // wq)),
                    pl.BlockSpec((nk, IDX_DIM), lambda b, i, f=krow: (f(b), c["ki"] // IDX_DIM)),
                    pl.BlockSpec((nk, wkv), lambda b, i, f=krow: (f(b), c["ka"] // wkv)),
                    pl.BlockSpec((nk, wkv), lambda b, i, f=krow: (f(b), c["va"] // wkv))]
        args = [proj] * 6
        if grp.lp:
            in_specs += [pl.BlockSpec((1, grp.lp, IDX_DIM), lambda b, i: (b, 0, 0)),
                         pl.BlockSpec((1, grp.lp, wkv), lambda b, i: (b, 0, 0)),
                         pl.BlockSpec((1, grp.lp, wkv), lambda b, i: (b, 0, 0))]
            args += [past_ki, past_k, past_v]
        o = pl.pallas_call(
            functools.partial(_dsa_body, tq=tq, i0=i0, nk=nk, topk=grp.topk, has_past=bool(grp.lp)),
            grid=(B, n_i),
            in_specs=in_specs,
            out_specs=pl.BlockSpec((tq, wq), lambda b, i, n_i=n_i: (b * n_i + i, 0)),
            out_shape=jax.ShapeDtypeStruct((B * n_i * tq, wq), MXU_DTYPE),
            scratch_shapes=[pltpu.VMEM((tq, n), I32) for n in ([grp.lp] if grp.lp else []) + [nk]],
            compiler_params=_params(2),
            name="dsa_attention",
        )(*args)
        outs.append(o.reshape(B, n_i * tq, wq))
    return jnp.concatenate(outs, axis=1).reshape(B * grp.lq, wq)


def _first_index_of_max(x, iota, n):
    mx = jnp.max(x, axis=0, keepdims=True)
    return mx, jnp.min(jnp.where(x == mx, iota, n), axis=0, keepdims=True)


def _route_body(x_ref, sc_ref, sh_ref, wr_ref, bias_ref, hp_ref, gate_ref, rank_ref, cnt_ref, h_ref, carry_ref,
                *, group):
    step = pl.program_id(0)
    tm = x_ref.shape[0]
    E = wr_ref.shape[0]
    per = E // N_GROUPS
    for g, part in enumerate(_modulated(x_ref, sc_ref, sh_ref, group)):
        h_ref[g * group:(g + 1) * group, :] = part
    h = h_ref[...]
    _store_words(hp_ref, 0, _pack_rows(h), hp_ref.shape[0] // tm)

    scores = jax.nn.sigmoid(_dot_nt(wr_ref[...], h, precision=lax.Precision.HIGHEST))
    biased = scores + bias_ref[...]
    member = lax.broadcasted_iota(I32, (per, tm), 0).astype(F32)
    gscore = []
    for g in range(N_GROUPS):
        blk = biased[g * per:(g + 1) * per, :]
        m1, i1 = _first_index_of_max(blk, member, per)
        m2 = jnp.max(jnp.where(member == i1, -jnp.inf, blk), axis=0, keepdims=True)
        gscore.append(m1 + m2)
    gscore = jnp.concatenate(gscore, axis=0)
    giota = lax.broadcasted_iota(I32, (N_GROUPS, tm), 0).astype(F32)
    gsel = jnp.zeros((N_GROUPS, tm), F32)
    for _ in range(TOPK_GROUPS):
        _, gi = _first_index_of_max(gscore, giota, N_GROUPS)
        hit = giota == gi
        gsel = jnp.where(hit, 1.0, gsel)
        gscore = jnp.where(hit, -jnp.inf, gscore)
    emask = jnp.concatenate([jnp.broadcast_to(gsel[g:g + 1, :], (per, tm)) for g in range(N_GROUPS)], axis=0)
    masked = jnp.where(emask > 0.0, biased, -jnp.inf)
    eiota = lax.broadcasted_iota(I32, (E, tm), 0).astype(F32)
    sel = jnp.zeros((E, tm), jnp.bool_)
    for _ in range(TOP_K):
        _, ei = _first_index_of_max(masked, eiota, E)
        hit = eiota == ei
        sel = sel | hit
        masked = jnp.where(hit, -jnp.inf, masked)
    w = jnp.where(sel, scores, 0.0)
    gate_ref[...] = w / jnp.sum(w, axis=0, keepdims=True) * ROUTED_SCALE

    @pl.when(step == 0)
    def _():
        carry_ref[...] = jnp.zeros(carry_ref.shape, F32)
    upper = (lax.broadcasted_iota(I32, (tm, tm), 0) <= lax.broadcasted_iota(I32, (tm, tm), 1))
    self = jnp.where(sel, 1.0, 0.0)
    incl = _dot(self, jnp.where(upper, 1.0, 0.0)) + carry_ref[...]
    rank_ref[...] = jnp.where(sel, incl - 1.0, -1.0).astype(I32)
    carry_ref[...] = incl[:, tm - 1:tm]
    cnt_ref[...] = incl[:, tm - 1:tm].astype(I32)


def route(x, scale_g, shift_g, w_router_t, r_bias, group, tm):
    T, D = x.shape
    E = w_router_t.shape[0]
    row = lambda i: (i, 0)
    col = lambda i: (0, i)
    fixed = lambda i: (0, 0)
    return pl.pallas_call(
        functools.partial(_route_body, group=group),
        grid=(T // tm,),
        in_specs=[pl.BlockSpec((tm, D), row),
                  pl.BlockSpec((1, tm // group, D), lambda i: (i, 0, 0)),
                  pl.BlockSpec((1, tm // group, D), lambda i: (i, 0, 0)),
                  pl.BlockSpec((E, D), fixed),
                  pl.BlockSpec((E, 1), fixed)],
        out_specs=[pl.BlockSpec((tm * (D // 2 // LANES), LANES), row), pl.BlockSpec((E, tm), col),
                   pl.BlockSpec((E, tm), col), pl.BlockSpec((E, 1), fixed)],
        out_shape=[jax.ShapeDtypeStruct((T * (D // 2 // LANES), LANES), I32), jax.ShapeDtypeStruct((E, T), F32),
                   jax.ShapeDtypeStruct((E, T), I32), jax.ShapeDtypeStruct((E, 1), I32)],
        scratch_shapes=[pltpu.VMEM((tm, D), F32), pltpu.VMEM((E, 1), F32)],
        compiler_params=_params(1),
        name="route",
    )(x, _per_tile(scale_g, tm, group), _per_tile(shift_g, tm, group), w_router_t, r_bias)


def _compact_body(gate_ref, rank_ref, off_ref, dest_ref, g8_ref):
    E, tm = gate_ref.shape
    rank = rank_ref[...]
    sel = rank >= 0
    dest = (rank + off_ref[...]).astype(F32)
    gate = gate_ref[...]
    eiota = lax.broadcasted_iota(I32, (E, tm), 0).astype(F32)
    dests, gates = [], []
    for _ in range(TOP_K):
        ei = jnp.min(jnp.where(sel, eiota, E), axis=0, keepdims=True)
        hit = eiota == ei
        dests.append(jnp.sum(jnp.where(hit, dest, 0.0), axis=0, keepdims=True))
        gates.append(jnp.sum(jnp.where(hit, gate, 0.0), axis=0, keepdims=True))
        sel = sel & jnp.logical_not(hit)
    dest_ref[...] = jnp.concatenate(dests, axis=0).astype(I32)
    g8_ref[...] = jnp.concatenate(gates, axis=0)


def compact_routes(gate, rank, seg_off, tm):
    E, T = gate.shape
    col = lambda i: (0, i)
    return pl.pallas_call(
        _compact_body,
        grid=(T // tm,),
        in_specs=[pl.BlockSpec((E, tm), col), pl.BlockSpec((E, tm), col), pl.BlockSpec((E, 1), lambda i: (0, 0))],
        out_specs=[pl.BlockSpec((TOP_K, tm), col), pl.BlockSpec((TOP_K, tm), col)],
        out_shape=[jax.ShapeDtypeStruct((TOP_K, T), I32), jax.ShapeDtypeStruct((TOP_K, T), F32)],
        compiler_params=_params(1),
        name="compact_routes",
    )(gate, rank, seg_off)


def _ffn(x, w1, w3, w2):
    a = _dot(x, w1)
    return _dot(a * jax.nn.sigmoid(a) * _dot(x, w3), w2)


def _gmm_body(te_ref, nt_ref, tok_ref, tok_next_ref, slot_ref, x_hbm, w1_ref, w3_ref, w2_ref, y_hbm,
              w1b, w3b, w2b, xbuf, ybuf, in_sem, out_sem, *, tm, parts, per):
    i = pl.program_id(0)
    n_used = nt_ref[0]
    cur, nxt = i % 2, (i + 1) % 2

    def token_rows(t):
        return pl.ds(pl.multiple_of(t * per, per), per)

    def row_in(tok, r, buf):
        return pltpu.make_async_copy(x_hbm.at[token_rows(tok[0, 0, r])], xbuf.at[buf, pl.ds(r * per, per)],
                                     in_sem.at[buf])

    def row_out(r, buf):
        return pltpu.make_async_copy(ybuf.at[buf, pl.ds(r * per, per)], y_hbm.at[token_rows(slot_ref[0, 0, r])],
                                     out_sem.at[buf])

    def wait_in(buf):
        pltpu.make_async_copy(x_hbm.at[pl.ds(0, tm * per)], xbuf.at[buf], in_sem.at[buf]).wait()

    def wait_out(buf):
        pltpu.make_async_copy(ybuf.at[buf], y_hbm.at[pl.ds(0, tm * per)], out_sem.at[buf]).wait()

    @pl.when(i == 0)
    def _():
        for r in range(tm):
            row_in(tok_ref, r, 0).start()
        ybuf[1] = jnp.zeros(ybuf.shape[1:], I32)
        spare = pltpu.make_async_copy(ybuf.at[1], y_hbm.at[pl.ds(y_hbm.shape[0] - tm * per, tm * per)], out_sem.at[1])
        spare.start()
        spare.wait()

    @pl.when(jnp.logical_or(i == 0, te_ref[i] != te_ref[jnp.maximum(i - 1, 0)]))
    def _():
        w1b[...] = _mx(w1_ref[...])
        w3b[...] = _mx(w3_ref[...])
        w2b[...] = _mx(w2_ref[...])

    @pl.when(i < n_used)
    def _():
        wait_in(cur)
        for r in range(tm):
            row_in(tok_next_ref, r, nxt).start()
        rows = tm // parts
        for p in range(parts):
            x = _unpack_rows(_load_words(xbuf.at[cur], p * rows, rows, per))
            _store_words(ybuf.at[cur], p * rows, _pack_rows(_ffn(x, w1b[...], w3b[...], w2b[...])), per)
            for r in range(p * rows, (p + 1) * rows):
                row_out(r, cur).start()

    @pl.when(jnp.logical_and(i >= 1, i < n_used))
    def _():
        wait_out(nxt)

    @pl.when(i == n_used - 1)
    def _():
        wait_in(nxt)
        wait_out(cur)


def grouped_ffn(tile_expert, n_tiles, row_token, row_slot, x, w1, w3, w2, layer, n_out, tm):
    n_rows = row_token.shape[0] * tm
    D, F = w1.shape[-2], w1.shape[-1]
    per = D // 2 // LANES
    clamp = lambda i, nt: jnp.maximum(jnp.minimum(i, nt[0] - 1), 0)
    wsel = lambda i, te, nt: (layer, te[i], 0, 0)
    smem_rows = lambda f: pl.BlockSpec((1, 1, tm), f, memory_space=pltpu.SMEM)
    return pl.pallas_call(
        functools.partial(_gmm_body, tm=tm, parts=2, per=per),
        grid_spec=pltpu.PrefetchScalarGridSpec(
            num_scalar_prefetch=2,
            grid=(n_rows // tm,),
            in_specs=[smem_rows(lambda i, te, nt: (clamp(i, nt), 0, 0)),
                      smem_rows(lambda i, te, nt: (clamp(i + 1, nt), 0, 0)),
                      smem_rows(lambda i, te, nt: (clamp(i, nt), 0, 0)),
                      pl.BlockSpec(memory_space=pl.ANY),
                      pl.BlockSpec((None, None, D, F), wsel),
                      pl.BlockSpec((None, None, D, F), wsel),
                      pl.BlockSpec((None, None, F, D), wsel)],
            out_specs=pl.BlockSpec(memory_space=pl.ANY),
            scratch_shapes=[pltpu.VMEM((D, F), MXU_DTYPE), pltpu.VMEM((D, F), MXU_DTYPE),
                            pltpu.VMEM((F, D), MXU_DTYPE),
                            pltpu.VMEM((2, tm * per, LANES), I32), pltpu.VMEM((2, tm * per, LANES), I32),
                            pltpu.SemaphoreType.DMA((2,)), pltpu.SemaphoreType.DMA((2,))]),
        out_shape=jax.ShapeDtypeStruct((n_out * per, LANES), I32),
        compiler_params=pltpu.CompilerParams(dimension_semantics=("arbitrary",), vmem_limit_bytes=VMEM_LIMIT_BYTES,
                                             disable_bounds_checks=True),
        name="grouped_ffn",
    )(tile_expert, n_tiles, row_token, row_token, row_slot, x, w1, w3, w2)


def _close_moe_body(h_ref, *rest, group, alpha, n_k):
    yk_refs, (g8_ref, ws1_ref, ws3_ref, ws2_ref, x_ref, gate_ref, g_ref, b_ref, o_ref) = rest[:n_k], rest[n_k:]
    tm = x_ref.shape[0]
    per = h_ref.shape[0] // tm
    words = lambda ref: _unpack_rows(_load_words(ref, 0, tm, per))
    y = _ffn(words(h_ref), ws1_ref[...], ws3_ref[...], ws2_ref[...])
    g8 = g8_ref[...]
    for k, yk_ref in enumerate(yk_refs):
        y = y + g8[:, k:k + 1] * words(yk_ref)
    r = _gated_residual(x_ref, y, gate_ref, group, alpha)
    o_ref[...] = _layer_norm_rows(r, g_ref[...], b_ref[...])


def close_moe(h, y_rows, g8, ws1, ws3, ws2, x, gate_g, ln_g, ln_b, group, alpha, tm):
    T, D = x.shape
    K = g8.shape[1]
    per = h.shape[0] // T
    row = lambda i: (i, 0)
    fixed = lambda i: (0, 0)
    return pl.pallas_call(
        functools.partial(_close_moe_body, group=group, alpha=alpha, n_k=K),
        grid=(T // tm,),
        in_specs=[pl.BlockSpec((tm * per, LANES), row)] +
                 [pl.BlockSpec((tm * per, LANES), lambda i, k=k: (k * (T // tm) + i, 0)) for k in range(K)] +
                 [pl.BlockSpec((tm, K), row),
                  pl.BlockSpec(ws1.shape, fixed),
                  pl.BlockSpec(ws3.shape, fixed),
                  pl.BlockSpec(ws2.shape, fixed),
                  pl.BlockSpec((tm, D), row),
                  pl.BlockSpec((1, tm // group, D), lambda i: (i, 0, 0)),
                  pl.BlockSpec((1, D), fixed),
                  pl.BlockSpec((1, D), fixed)],
        out_specs=pl.BlockSpec((tm, D), row),
        out_shape=jax.ShapeDtypeStruct((T, D), F32),
        compiler_params=_params(1),
        name="close_moe",
    )(h, *([y_rows] * K), g8, ws1, ws3, ws2, x, _per_tile(gate_g, tm, group), ln_g, ln_b)


def _rope_tables(pos, half):
    inv_freq = jnp.power(ROPE_THETA, -jnp.arange(half, dtype=F32) / half)
    ang = pos.astype(F32)[:, None] * inv_freq[None, :]
    cos, sin = jnp.cos(ang), jnp.sin(ang)
    reps = LANES // (2 * half)
    cos = jnp.tile(jnp.concatenate([cos, cos], axis=1), (1, reps))
    sin = jnp.tile(jnp.concatenate([-sin, sin], axis=1), (1, reps))
    return jnp.stack([jnp.ones_like(cos), cos]), jnp.stack([jnp.zeros_like(sin), sin])


def _place_cols(w, sizes, names, cols, width):
    out = jnp.zeros((w.shape[0], width), w.dtype)
    o = 0
    for size, name in zip(sizes, names):
        if name is not None:
            dst, sub = name
            out = lax.dynamic_update_slice(out, w[:, o:o + size], (0, cols[dst] + sub))
        o += size
    return out


def _moe(x, scale_g, shift_g, gate_g, ln_g, ln_b, w_router, r_bias, w1, w3, w2, ws1, ws3, ws2, layer, group, alpha):
    T, D = x.shape
    E = w1.shape[1]
    tm = 256
    t_route = _pick(T, 768, math.lcm(LANES, group))
    h, gate, rank, counts = route(x, scale_g, shift_g, w_router.T, r_bias.reshape(E, 1), group, t_route)

    counts = counts[:, 0]
    tiles_per = (counts + tm - 1) // tm
    tile_end = jnp.cumsum(tiles_per)
    seg_off = ((tile_end - tiles_per) * tm).astype(I32)
    n_tiles = (T * TOP_K) // tm + E
    used = tile_end[-1].astype(I32)
    tile_ids = jnp.minimum(jnp.arange(n_tiles, dtype=I32), used - 1)
    tile_expert = jnp.sum((tile_end[None, :] <= tile_ids[:, None]).astype(I32), axis=1)

    dest, g8 = compact_routes(gate, rank, seg_off.reshape(E, 1), t_route)
    n_pairs = TOP_K * T
    pair_slot = jnp.arange(n_pairs, dtype=I32).reshape(TOP_K, T)
    spare = n_pairs + jnp.arange(n_tiles * tm, dtype=I32) % tm
    row_slot = spare.at[dest.reshape(-1)].set(pair_slot.reshape(-1), unique_indices=True, mode="promise_in_bounds")
    row_token = jnp.where(row_slot < n_pairs, row_slot % T, 0)
    y_rows = grouped_ffn(tile_expert, used.reshape(1), row_token.reshape(n_tiles, 1, tm),
                         row_slot.reshape(n_tiles, 1, tm), h, w1, w3, w2, layer, n_pairs + tm, tm)
    return close_moe(h, y_rows, g8.T, _mx(ws1), _mx(ws3), _mx(ws2), x, gate_g, ln_g, ln_b, group, alpha,
                     _pick(T, 256, math.lcm(SUBLANES, group)))


def _fox_gates(grp, proj, b_f, past_logf):
    B, H = grp.batch, B_HEADS
    c0 = EVEN_COLS["misc"] + MISC_FL
    fl = proj[grp.row0:grp.row0 + B * grp.lq, c0:c0 + H].reshape(B, grp.lq, H)
    logf = jax.nn.log_sigmoid(fl + b_f)
    logf_all = logf if past_logf is None else jnp.concatenate([past_logf, logf], axis=1)
    cum = jnp.moveaxis(jnp.cumsum(logf_all, axis=1), 1, 2).reshape(B * H, grp.lk)
    fq = cum[:, grp.lp:, None]
    new = cum[:, grp.lp:].reshape(B * H, grp.lq // grp.tk, 1, grp.tk)
    new = jnp.pad(new, ((0, 0), (0, 0), (0, 0), (0, KEY_BLOCK - grp.tk)))
    if grp.lp:
        past = cum[:, :grp.lp].reshape(B * H, grp.n_past_blocks, 1, grp.tkp)
        past = jnp.pad(past, ((0, 0), (0, 0), (0, 0), (0, KEY_BLOCK - grp.tkp)))
        new = jnp.concatenate([past, new], axis=1)
    return logf, fq, new


def kernel(x_prompt, x_sample, c_prompt, c_sample, cache_a_k, cache_a_v, cache_a_kidx, cache_b_k, cache_b_v, cache_b_logf, cache_c_k, cache_c_v, cache_d_latent, cache_d_krope, w_in_even, b_forget, w_out_even, w_in_odd, c_lambda, c_subln, d_q_norm, d_w_uq, d_kv_norm, d_w_uk, d_w_uv, w_out_odd, ada_mix_w, ada_mix_b, ln_mix_g, ln_mix_b, ada_ffn_w, ada_ffn_b, ln_ffn_g, ln_ffn_b, router_w, router_bias, moe_w1, moe_w3, moe_w2, shared_w1, shared_w3, shared_w2):
    B, L, D = x_prompt.shape
    Bs, Ls, _ = x_sample.shape
    depth = ada_mix_w.shape[0]
    alpha = (2 * depth) ** 0.25
    past_len = cache_a_k.shape[2]
    Tp = B * L
    T = Tp + Bs * Ls
    grp_p = _Group(B, L, 0, 0)
    grp_s = _Group(Bs, Ls, past_len, Tp)
    group = math.gcd(L, Ls)
    assert group % (2 * SUBLANES) == 0
    tm_tok = _pick(T, 768, math.lcm(LANES, group))
    tm_close = _pick(T, 512, math.lcm(2 * SUBLANES, group))
    pos = jnp.concatenate([jnp.tile(jnp.arange(L), B), jnp.tile(past_len + jnp.arange(Ls), Bs)])
    c_act = jax.nn.silu(jnp.concatenate([c_prompt, c_sample], axis=0))

    def conditioning(ada_w, ada_b, layer):
        mod = matmul(c_act, ada_w, layer=layer) + ada_b[layer]
        per_group = jnp.concatenate([jnp.repeat(mod[:B], L // group, axis=0),
                                     jnp.repeat(mod[B:], Ls // group, axis=0)], axis=0)
        return jnp.split(per_group, 3, axis=-1)

    def flat_past(c):
        return c.reshape(c.shape[0], c.shape[1], -1)

    x = jnp.concatenate([x_prompt.reshape(Tp, D), x_sample.reshape(Bs * Ls, D)], axis=0)
    ev_p, ev_s, od_p, od_s = [], [], [], []
    for i in range(depth):
        j = i // 2
        shift_g, scale_g, gate_g = conditioning(ada_mix_w, ada_mix_b, i)
        if i % 2 == 0:
            sizes = (A_HEADS * HEAD_DIM, A_KV_HEADS * HEAD_DIM, A_KV_HEADS * HEAD_DIM, IDX_HEADS * IDX_DIM, IDX_DIM,
                     IDX_HEADS, B_HEADS * HEAD_DIM, B_HEADS * HEAD_DIM, B_HEADS * HEAD_DIM, B_HEADS)
            names = (("qa", 0), ("ka", 0), ("va", 0), ("qi", 0), ("ki", 0), ("misc", MISC_WI),
                     ("qf", 0), ("kf", 0), ("vf", 0), ("misc", MISC_FL))
            w_in = _mx(_place_cols(w_in_even[j], sizes, names, EVEN_COLS, EVEN_WIDTH))
            cos, sin = _rope_tables(pos, HEAD_DIM // 2)
            proj = project(x, scale_g, shift_g, w_in, cos, sin, EVEN_ROPE_RANGES, HEAD_DIM // 2, group, tm_tok)
            outs_a, outs_b = [], []
            for grp, store in ((grp_p, ev_p), (grp_s, ev_s)):
                if grp.lp:
                    logf, fq, fk = _fox_gates(grp, proj, b_forget[j], cache_b_logf[j])
                    outs_a.append(dsa_attention(grp, proj, cache_a_kidx[j], flat_past(cache_a_k[j]),
                                                flat_past(cache_a_v[j])))
                    outs_b.append(fox_attention_past(grp, proj, fq, fk, cache_b_k, cache_b_v, j))
                else:
                    logf, fq, fk = _fox_gates(grp, proj, b_forget[j], None)
                    outs_a.append(dsa_attention(grp, proj, None, None, None))
                    outs_b.append(fox_attention(grp, proj, fq, fk))
                r0, r1 = grp.row0, grp.row0 + grp.batch * grp.lq
                cut = lambda name, heads, d: proj[r0:r1, EVEN_COLS[name]:EVEN_COLS[name] + heads * d].reshape(
                    (grp.batch, grp.lq) + ((heads, d) if heads > 1 else (d,)))
                store.append((cut("ka", A_KV_HEADS, HEAD_DIM), cut("va", A_KV_HEADS, HEAD_DIM), cut("ki", 1, IDX_DIM),
                              cut("kf", B_HEADS, HEAD_DIM), cut("vf", B_HEADS, HEAD_DIM), logf))
            a1, a2 = jnp.concatenate(outs_a, axis=0), jnp.concatenate(outs_b, axis=0)
            n1 = A_HEADS * HEAD_DIM
            w1o, w2o = _mx(w_out_even[j][:n1]), _mx(w_out_even[j][n1:])
        else:
            lam_init = 0.8 - 0.6 * math.exp(-0.3 * i)
            lam_f = c_lambda[j]
            lam = (jnp.exp(jnp.sum(lam_f[0] * lam_f[1])) - jnp.exp(jnp.sum(lam_f[2] * lam_f[3])) + lam_init).reshape(1)
            q_lora, R = d_q_norm.shape[1], d_kv_norm.shape[1]
            sizes = (C_HEADS * 2 * C_QK_DIM, C_HEADS * 2 * C_QK_DIM, C_HEADS * C_V_DIM, q_lora, R, D_ROPE)
            names = (("qc", 0), ("kc", 0), ("vc", 0), ("qd", 0), ("ckv", 0), ("kr", 0))
            w_in = _mx(_place_cols(w_in_odd[j], sizes, names, ODD_COLS, ODD_WIDTH))
            cos, sin = _rope_tables(pos, C_QK_DIM // 2)
            proj = project(x, scale_g, shift_g, w_in, cos, sin, ODD_ROPE_RANGES, C_QK_DIM // 2, group, tm_tok)

            w_uq = d_w_uq[j].reshape(q_lora, D_HEADS, D_NOPE + D_ROPE)
            w_abs = bmm_precise(jnp.moveaxis(w_uq[:, :, :D_NOPE], 1, 0), jnp.transpose(d_w_uk[j], (1, 2, 0)))
            w_abs = jnp.moveaxis(w_abs, 0, 1).reshape(q_lora, D_HEADS * R)
            w_rope = jnp.pad(w_uq[:, :, D_NOPE:], ((0, 0), (0, 0), (0, LANES - D_ROPE))).reshape(q_lora, D_HEADS * LANES)
            q_abs, q_rope, lat = mla_prepare(proj, d_q_norm[j][None], d_kv_norm[j][None],
                                             _mx(jnp.concatenate([w_abs, w_rope], axis=1)), cos, sin, tm_close)
            n1 = C_HEADS * C_V_DIM
            w_od = w_out_odd[j][n1:].reshape(D_HEADS, D_V, D)
            w_lat_out = bmm_precise(jnp.transpose(d_w_uv[j], (1, 0, 2)), w_od).reshape(D_HEADS * R, D)

            outs_a, outs_b = [], []
            for grp, store in ((grp_p, od_p), (grp_s, od_s)):
                if grp.lp:
                    outs_a.append(diff_attention_past(grp, proj, lam, c_subln[j][None], 1.0 - lam_init,
                                                      flat_past(cache_c_k[j]), cache_c_v, j))
                    outs_b.append(mla_attention_past(grp, q_abs, q_rope, lat, proj, cache_d_latent, cache_d_krope, j))
                else:
                    outs_a.append(diff_attention(grp, proj, lam, c_subln[j][None], 1.0 - lam_init))
                    outs_b.append(mla_attention(grp, q_abs, q_rope, lat, proj))
                r0, r1 = grp.row0, grp.row0 + grp.batch * grp.lq
                shp = (grp.batch, grp.lq)
                oc = ODD_COLS
                store.append((proj[r0:r1, oc["kc"]:oc["kc"] + C_HEADS * 2 * C_QK_DIM].reshape(shp + (C_HEADS, 2, C_QK_DIM)),
                              proj[r0:r1, oc["vc"]:oc["vc"] + n1].reshape(shp + (C_HEADS, C_V_DIM)),
                              lat[r0:r1].reshape(shp + (R,)),
                              proj[r0:r1, oc["kr"]:oc["kr"] + D_ROPE].reshape(shp + (D_ROPE,))))
            a1, a2 = jnp.concatenate(outs_a, axis=0), jnp.concatenate(outs_b, axis=0)
            w1o, w2o = _mx(w_out_odd[j][:n1]), _mx(w_lat_out)
        x = close_mixer(a1, a2, w1o, w2o, x, gate_g, ln_mix_g[i][None], ln_mix_b[i][None], group, alpha, tm_close)

        shift_g, scale_g, gate_g = conditioning(ada_ffn_w, ada_ffn_b, i)
        x = _moe(x, scale_g, shift_g, gate_g, ln_ffn_g[i][None], ln_ffn_b[i][None], router_w[i], router_bias[i],
                 moe_w1, moe_w3, moe_w2, shared_w1[i], shared_w3[i], shared_w2[i], i, group, alpha)

    stack = lambda rows, idx: jnp.stack([r[idx] for r in rows])
    outs = [x[:Tp].reshape(B, L, D), x[Tp:].reshape(Bs, Ls, D)]
    for idx in range(6):
        outs += [stack(ev_p, idx), stack(ev_s, idx)]
    for idx in range(4):
        outs += [stack(od_p, idx), stack(od_s, idx)]
    return tuple(outs)
```

```python
import functools
import math

import jax
import jax.numpy as jnp
from jax import lax
from jax.experimental import pallas as pl
from jax.experimental.pallas import tpu as pltpu

CHUNK = 64
ROPE_THETA = 10000.0
HEAD_DIM = 128
A_HEADS = 8
A_KV_HEADS = 2
IDX_HEADS = 16
IDX_DIM = 128
A_TOPK_MAX = 256
B_HEADS = 8
C_HEADS = 8
C_QK_DIM = 64
C_V_DIM = 128
D_HEADS = 8
D_NOPE = 128
D_ROPE = 64
D_V = 128
N_GROUPS = 8
TOPK_GROUPS = 4
TOP_K = 8
ROUTED_SCALE = 2.5
NORM_EPS = 1e-6

LANES = 128
SUBLANES = 8
MXU_WIDTH = 256
VMEM_LIMIT_BYTES = 56 * 2**20
MXU_DTYPE = jnp.bfloat16

F32 = jnp.float32
I32 = jnp.int32
NEG = -1e30
INT_MIN = -2**31
KEY_BLOCK = 512
QUERY_BLOCK = 512
HEADS_PER_STEP = 2

EVEN_COLS = dict(qi=0, qa=2048, qf=3072, kf=4096, vf=5120, ka=6144, va=6400, ki=6656, misc=6912)
EVEN_WIDTH = 7168
EVEN_ROPE_RANGES = ((0, 3072), (6144, 6400), (6656, 6912))
MISC_WI, MISC_FL = 0, IDX_HEADS
ODD_COLS = dict(qc=0, kc=1024, vc=2048, qd=3072, ckv=3584, kr=3840)
ODD_WIDTH = 4096
ODD_ROPE_RANGES = ((0, 2048), (3840, 4096))


def _pick(n, target, mult):
    best = None
    for d in range(mult, min(n, target) + 1, mult):
        if n % d == 0:
            best = d
    return n if best is None else best


def _params(n_axes):
    return pltpu.CompilerParams(dimension_semantics=("arbitrary",) * n_axes,
                                vmem_limit_bytes=VMEM_LIMIT_BYTES)


def _mx(a):
    return a.astype(MXU_DTYPE)


def _dot(a, b):
    return jnp.dot(_mx(a), _mx(b), preferred_element_type=F32)


def _dot_nt(a, b, precision=None):
    if precision is None:
        a, b = _mx(a), _mx(b)
    return lax.dot_general(a, b, (((1,), (1,)), ((), ())), preferred_element_type=F32, precision=precision)


def _mm_body(a_ref, b_ref, o_ref, *, precise):
    if precise:
        o = jnp.dot(a_ref[...], b_ref[...], preferred_element_type=F32, precision=lax.Precision.HIGHEST)
    else:
        o = _dot(a_ref[...], b_ref[...])
    o_ref[...] = o.astype(o_ref.dtype)


def matmul(a, b, out_dtype=F32, precise=False, tm_target=1024, tn_target=512, layer=None):
    M, K = a.shape
    N = b.shape[-1]
    tm = _pick(M, tm_target, SUBLANES)
    tn = _pick(N, tn_target, LANES)
    if layer is None:
        b_spec = pl.BlockSpec((K, tn), lambda i, j: (0, j))
    else:
        b_spec = pl.BlockSpec((None, K, tn), lambda i, j: (layer, 0, j))
    return pl.pallas_call(
        functools.partial(_mm_body, precise=precise),
        grid=(M // tm, N // tn),
        in_specs=[pl.BlockSpec((tm, K), lambda i, j: (i, 0)), b_spec],
        out_specs=pl.BlockSpec((tm, tn), lambda i, j: (i, j)),
        out_shape=jax.ShapeDtypeStruct((M, N), out_dtype),
        compiler_params=_params(2),
        name="matmul",
    )(a, b)


def _bmm_body(a_ref, b_ref, o_ref):
    o_ref[0] = jnp.dot(a_ref[0], b_ref[0], preferred_element_type=F32,
                       precision=lax.Precision.HIGHEST).astype(o_ref.dtype)


def bmm_precise(a, b):
    H, M, K = a.shape
    N = b.shape[2]
    return pl.pallas_call(
        _bmm_body,
        grid=(H,),
        in_specs=[pl.BlockSpec((1, M, K), lambda h: (h, 0, 0)),
                  pl.BlockSpec((1, K, N), lambda h: (h, 0, 0))],
        out_specs=pl.BlockSpec((1, M, N), lambda h: (h, 0, 0)),
        out_shape=jax.ShapeDtypeStruct((H, M, N), F32),
        compiler_params=_params(1),
        name="bmm_precise",
    )(a, b)


def _per_tile(table, tm, group):
    return table.reshape(table.shape[0] * group // tm, tm // group, table.shape[1])


def _modulated(x_ref, sc_ref, sh_ref, group):
    parts = []
    for g in range(x_ref.shape[0] // group):
        rows = slice(g * group, (g + 1) * group)
        parts.append(x_ref[rows, :] * (1.0 + sc_ref[0, g:g + 1, :]) + sh_ref[0, g:g + 1, :])
    return parts


def _pack_rows(x):
    bits = pltpu.bitcast(x, I32)
    rounded = bits + 0x7FFF + ((bits >> 16) & 1)
    half = x.shape[1] // 2
    return ((rounded[:, :half] >> 16) & 0xFFFF) | (rounded[:, half:] & jnp.int32(-0x10000))


def _unpack_rows(w):
    lo = pltpu.bitcast(jnp.left_shift(w, 16), F32)
    hi = pltpu.bitcast(w & jnp.int32(-0x10000), F32)
    return jnp.concatenate([lo, hi], axis=1)


def _load_words(ref, first, n, per):
    return jnp.concatenate([ref[pl.ds(first * per + s, n, stride=per), :] for s in range(per)], axis=1)


def _store_words(ref, first, words, per):
    for s in range(per):
        ref[pl.ds(first * per + s, words.shape[0], stride=per), :] = words[:, s * LANES:(s + 1) * LANES]


def _rope_lanes(x, cos, sin, half):
    if 2 * half == LANES:
        partner = pltpu.roll(x, half, axis=1)
    else:
        lane = lax.broadcasted_iota(I32, x.shape, 1)
        lower = (lane % (2 * half)) < half
        partner = jnp.where(lower, pltpu.roll(x, LANES - half, axis=1), pltpu.roll(x, half, axis=1))
    return x * cos + partner * sin


def _layer_norm_rows(r, g, b):
    rc = r - jnp.mean(r, axis=-1, keepdims=True)
    var = jnp.mean(rc * rc, axis=-1, keepdims=True)
    return rc * lax.rsqrt(var + NORM_EPS) * g + b


def _gated_residual(x_ref, y, gate_ref, group, alpha):
    parts = []
    for g in range(x_ref.shape[0] // group):
        rows = slice(g * group, (g + 1) * group)
        parts.append(alpha * x_ref[rows, :] + (1.0 + gate_ref[0, g:g + 1, :]) * y[rows, :])
    return jnp.concatenate(parts, axis=0)


def _proj_body(kind_ref, x_ref, sc_ref, sh_ref, w_ref, *rest, group, half, n_sub):
    tables, (o_ref, h_ref) = rest[:2 * n_sub], rest[2 * n_sub:]
    j = pl.program_id(1)

    @pl.when(j == 0)
    def _():
        for g, part in enumerate(_modulated(x_ref, sc_ref, sh_ref, group)):
            h_ref[g * group:(g + 1) * group, :] = part.astype(h_ref.dtype)

    for c in range(n_sub):
        cos, sin = tables[2 * c][...], tables[2 * c + 1][...]
        acc = jnp.dot(h_ref[...], w_ref[:, c * MXU_WIDTH:(c + 1) * MXU_WIDTH], preferred_element_type=F32)
        for p in range(MXU_WIDTH // LANES):
            o_ref[:, c * MXU_WIDTH + p * LANES:c * MXU_WIDTH + (p + 1) * LANES] = _rope_lanes(
                acc[:, p * LANES:(p + 1) * LANES], cos, sin, half)


def project(x, scale_g, shift_g, w, cos, sin, rope_ranges, half, group, tm):
    T, D = x.shape
    N = w.shape[1]
    n_sub = 4
    tn = n_sub * MXU_WIDTH
    start = jnp.arange(N // MXU_WIDTH) * MXU_WIDTH
    kinds = sum(((start >= lo) & (start < hi)).astype(I32) for lo, hi in rope_ranges)
    row = lambda i, j, k: (i, 0)
    tile = lambda i, j, k: (i, 0, 0)
    table_specs, table_args = [], []
    for c in range(n_sub):
        pick = lambda i, j, k, c=c: (k[j * n_sub + c], i, 0)
        table_specs += [pl.BlockSpec((None, tm, LANES), pick)] * 2
        table_args += [cos, sin]
    return pl.pallas_call(
        functools.partial(_proj_body, group=group, half=half, n_sub=n_sub),
        grid_spec=pltpu.PrefetchScalarGridSpec(
            num_scalar_prefetch=1,
            grid=(T // tm, N // tn),
            in_specs=[pl.BlockSpec((tm, D), row),
                      pl.BlockSpec((1, tm // group, D), tile),
                      pl.BlockSpec((1, tm // group, D), tile),
                      pl.BlockSpec((D, tn), lambda i, j, k: (0, j))] + table_specs,
            out_specs=pl.BlockSpec((tm, tn), lambda i, j, k: (i, j)),
            scratch_shapes=[pltpu.VMEM((tm, D), MXU_DTYPE)]),
        out_shape=jax.ShapeDtypeStruct((T, N), F32),
        compiler_params=_params(2),
        name="project",
    )(kinds, x, _per_tile(scale_g, tm, group), _per_tile(shift_g, tm, group), w, *table_args)


def _close_mix_body(a1_ref, a2_ref, w1_ref, w2_ref, x_ref, gate_ref, g_ref, b_ref, o_ref, *, group, alpha):
    y = _dot(a1_ref[...], w1_ref[...]) + _dot(a2_ref[...], w2_ref[...])
    r = _gated_residual(x_ref, y, gate_ref, group, alpha)
    o_ref[...] = _layer_norm_rows(r, g_ref[...], b_ref[...])


def close_mixer(a1, a2, w1, w2, x, gate_g, ln_g, ln_b, group, alpha, tm):
    T, D = x.shape
    row = lambda i: (i, 0)
    fixed = lambda i: (0, 0)
    return pl.pallas_call(
        functools.partial(_close_mix_body, group=group, alpha=alpha),
        grid=(T // tm,),
        in_specs=[pl.BlockSpec((tm, a1.shape[1]), row),
                  pl.BlockSpec((tm, a2.shape[1]), row),
                  pl.BlockSpec(w1.shape, fixed),
                  pl.BlockSpec(w2.shape, fixed),
                  pl.BlockSpec((tm, D), row),
                  pl.BlockSpec((1, tm // group, D), lambda i: (i, 0, 0)),
                  pl.BlockSpec((1, D), fixed),
                  pl.BlockSpec((1, D), fixed)],
        out_specs=pl.BlockSpec((tm, D), row),
        out_shape=jax.ShapeDtypeStruct((T, D), F32),
        compiler_params=_params(1),
        name="close_mixer",
    )(a1, a2, w1, w2, x, _per_tile(gate_g, tm, group), ln_g, ln_b)


def _online_step(s, v, carry):
    m, l, acc = carry
    m_new = jnp.maximum(m, jnp.max(s, axis=-1, keepdims=True))
    p = jnp.exp(s - m_new)
    alpha = jnp.exp(m - m_new)
    l = alpha * l + jnp.sum(p, axis=-1, keepdims=True)
    acc = alpha * acc + _dot(p, v)
    return m_new, l, acc


def _init_carry(tq, dv):
    return (jnp.full((tq, 1), NEG, F32), jnp.zeros((tq, 1), F32), jnp.zeros((tq, dv), F32))


def _chunk_last_key(i, tq):
    return ((i + 1) * tq - 1) // CHUNK * CHUNK + CHUNK - 1


class _Group:
    def __init__(self, batch, lq, lp, row0):
        self.batch, self.lq, self.lp, self.row0 = batch, lq, lp, row0
        self.lk = lp + lq
        self.tq = _pick(lq, QUERY_BLOCK, SUBLANES)
        self.tq_dsa = _pick(lq, 128, SUBLANES)
        self.tk = _pick(lq, KEY_BLOCK, SUBLANES)
        self.tkp = _pick(lp, KEY_BLOCK, LANES) if lp else 0
        self.n_past_blocks = lp // self.tkp if lp else 0
        self.topk = min(A_TOPK_MAX, self.lk // 4)
        assert lp % CHUNK == 0 and row0 % lq == 0 and lq % self.tk == 0

    def qrow(self, tq):
        base, per = self.row0 // tq, self.lq // tq
        return lambda b, i: base + b * per + i

    def krow(self):
        base = self.row0 // self.lq
        return lambda b: base + b


def _new_key_blocks(i, tq, grp_lq, tk, causal_last):
    return jnp.minimum(causal_last, grp_lq - 1) // tk + 1


def _fox_body(q_ref, k_ref, v_ref, fq_ref, fk_ref, o_ref, *, tq, tk, lq, hp):
    i = pl.program_id(2)
    d = HEAD_DIM
    head = lambda x, u: x[:, u * d:(u + 1) * d]
    q = _mx(q_ref[...] * d ** -0.5)
    carry = tuple(_init_carry(tq, d) for _ in range(hp))
    qpos = i * tq + lax.broadcasted_iota(I32, (tq, tk), 0)

    def new_step(j, c):
        off = pl.multiple_of(j * tk, tk)
        k, v = k_ref[pl.ds(off, tk), :], v_ref[pl.ds(off, tk), :]
        causal = off + lax.broadcasted_iota(I32, (tq, tk), 1) <= qpos
        out = []
        for u in range(hp):
            s = _dot_nt(head(q, u), head(k, u)) + fq_ref[u] - fk_ref[u, j][:, :tk]
            out.append(_online_step(jnp.where(causal, s, NEG), head(v, u), c[u]))
        return tuple(out)

    nb = _new_key_blocks(i, tq, lq, tk, (i + 1) * tq - 1)
    carry = lax.fori_loop(0, nb, new_step, carry)
    for u, (m, l, acc) in enumerate(carry):
        o_ref[:, u * d:(u + 1) * d] = (acc / l).astype(o_ref.dtype)


def fox_attention(grp, proj, fq, fk):
    assert grp.lp == 0
    B, H, hp = grp.batch, B_HEADS, HEADS_PER_STEP
    w = hp * HEAD_DIM
    tq = grp.tq
    qrow, krow = grp.qrow(tq), grp.krow()
    cq, ck, cv = (EVEN_COLS[n] // w for n in ("qf", "kf", "vf"))
    in_specs = [pl.BlockSpec((tq, w), lambda b, h, i: (qrow(b, i), cq + h)),
                pl.BlockSpec((grp.lq, w), lambda b, h, i: (krow(b), ck + h)),
                pl.BlockSpec((grp.lq, w), lambda b, h, i: (krow(b), cv + h))]
    in_specs += [pl.BlockSpec((hp, tq, 1), lambda b, h, i: (b * (H // hp) + h, i, 0)),
                 pl.BlockSpec((hp,) + fk.shape[1:], lambda b, h, i: (b * (H // hp) + h, 0, 0, 0))]
    return pl.pallas_call(
        functools.partial(_fox_body, tq=tq, tk=grp.tk, lq=grp.lq, hp=hp),
        grid=(B, H // hp, grp.lq // tq),
        in_specs=in_specs,
        out_specs=pl.BlockSpec((tq, w), lambda b, h, i: (b * (grp.lq // tq) + i, h)),
        out_shape=jax.ShapeDtypeStruct((B * grp.lq, H * HEAD_DIM), MXU_DTYPE),
        compiler_params=_params(3),
        name="fox_attention",
    )(proj, proj, proj, fq, fk)


def _diff_body(lam_ref, q_ref, k_ref, v_ref, g_ref, o_ref, *, tq, tk, lq, hp, out_scale):
    i = pl.program_id(2)
    d = C_V_DIM
    head = lambda x, u: x[:, u * d:(u + 1) * d]
    q = q_ref[...] * C_QK_DIM ** -0.5
    lane = lax.broadcasted_iota(I32, q.shape, 1) % (2 * C_QK_DIM)
    qs = (_mx(jnp.where(lane < C_QK_DIM, q, 0.0)), _mx(jnp.where(lane >= C_QK_DIM, q, 0.0)))
    carry = tuple(_init_carry(tq, d) for _ in range(2 * hp))

    def step(k, v, ok, c):
        out = []
        for u in range(hp):
            ku, vu = _mx(head(k, u)), _mx(head(v, u))
            for t in range(2):
                s = _dot_nt(head(qs[t], u), ku)
                out.append(_online_step(jnp.where(ok, s, NEG), vu, c[2 * u + t]))
        return tuple(out)

    qchunk = (i * tq + lax.broadcasted_iota(I32, (tq, tk), 0)) // CHUNK

    def new_step(j, c):
        off = pl.multiple_of(j * tk, tk)
        ok = (off + lax.broadcasted_iota(I32, (tq, tk), 1)) // CHUNK <= qchunk
        return step(k_ref[pl.ds(off, tk), :], v_ref[pl.ds(off, tk), :], ok, c)

    nb = _new_key_blocks(i, tq, lq, tk, _chunk_last_key(i, tq))
    carry = lax.fori_loop(0, nb, new_step, carry)
    for u in range(hp):
        (_, l0, a0), (_, l1, a1) = carry[2 * u], carry[2 * u + 1]
        o = a0 / l0 - lam_ref[0] * (a1 / l1)
        o = o * lax.rsqrt(jnp.mean(o * o, axis=-1, keepdims=True) + NORM_EPS)
        o_ref[:, u * d:(u + 1) * d] = (o * g_ref[...] * out_scale).astype(o_ref.dtype)


def diff_attention(grp, proj, lam, subln, out_scale):
    assert grp.lp == 0
    B, H, hp = grp.batch, C_HEADS, HEADS_PER_STEP
    w = hp * C_V_DIM
    tq = grp.tq
    qrow, krow = grp.qrow(tq), grp.krow()
    cq, ck, cv = (ODD_COLS[n] // w for n in ("qc", "kc", "vc"))
    in_specs = [pl.BlockSpec(memory_space=pltpu.SMEM),
                pl.BlockSpec((tq, w), lambda b, h, i: (qrow(b, i), cq + h)),
                pl.BlockSpec((grp.lq, w), lambda b, h, i: (krow(b), ck + h)),
                pl.BlockSpec((grp.lq, w), lambda b, h, i: (krow(b), cv + h))]
    in_specs += [pl.BlockSpec((1, C_V_DIM), lambda b, h, i: (0, 0))]
    return pl.pallas_call(
        functools.partial(_diff_body, tq=tq, tk=grp.tk, lq=grp.lq, hp=hp, out_scale=out_scale),
        grid=(B, H // hp, grp.lq // tq),
        in_specs=in_specs,
        out_specs=pl.BlockSpec((tq, w), lambda b, h, i: (b * (grp.lq // tq) + i, h)),
        out_shape=jax.ShapeDtypeStruct((B * grp.lq, H * C_V_DIM), MXU_DTYPE),
        compiler_params=_params(3),
        name="diff_attention",
    )(lam, proj, proj, proj, subln)


def _mla_prep_body(qd_ref, ckv_ref, qg_ref, kg_ref, w_ref, cos_ref, sin_ref, qa_ref, qr_ref, lat_ref, *, n_abs):
    qd = qd_ref[...]
    qn = qd * lax.rsqrt(jnp.mean(qd * qd, axis=-1, keepdims=True) + NORM_EPS) * qg_ref[...]
    q = _dot(qn, w_ref[...]) * (D_NOPE + D_ROPE) ** -0.5
    qa_ref[...] = q[:, :n_abs].astype(qa_ref.dtype)
    for h in range((q.shape[1] - n_abs) // LANES):
        lanes = slice(n_abs + h * LANES, n_abs + (h + 1) * LANES)
        qr_ref[:, h * LANES:(h + 1) * LANES] = _rope_lanes(
            q[:, lanes], cos_ref[...], sin_ref[...], D_ROPE // 2).astype(qr_ref.dtype)
    ckv = ckv_ref[...]
    lat_ref[...] = ckv * lax.rsqrt(jnp.mean(ckv * ckv, axis=-1, keepdims=True) + NORM_EPS) * kg_ref[...]


def mla_prepare(proj, q_norm, kv_norm, w_q, cos, sin, tm):
    T = proj.shape[0]
    q_lora, R = q_norm.shape[1], kv_norm.shape[1]
    n_abs = D_HEADS * R
    n_rope = D_HEADS * LANES
    row = lambda i: (i, 0)
    fixed = lambda i: (0, 0)
    return pl.pallas_call(
        functools.partial(_mla_prep_body, n_abs=n_abs),
        grid=(T // tm,),
        in_specs=[pl.BlockSpec((tm, q_lora), lambda i: (i, ODD_COLS["qd"] // q_lora)),
                  pl.BlockSpec((tm, R), lambda i: (i, ODD_COLS["ckv"] // R)),
                  pl.BlockSpec((1, q_lora), fixed),
                  pl.BlockSpec((1, R), fixed),
                  pl.BlockSpec(w_q.shape, fixed),
                  pl.BlockSpec((None, tm, LANES), lambda i: (1, i, 0)),
                  pl.BlockSpec((None, tm, LANES), lambda i: (1, i, 0))],
        out_specs=[pl.BlockSpec((tm, n_abs), row), pl.BlockSpec((tm, n_rope), row), pl.BlockSpec((tm, R), row)],
        out_shape=[jax.ShapeDtypeStruct((T, n_abs), MXU_DTYPE), jax.ShapeDtypeStruct((T, n_rope), MXU_DTYPE),
                   jax.ShapeDtypeStruct((T, R), F32)],
        compiler_params=_params(1),
        name="mla_prepare",
    )(proj, proj, q_norm, kv_norm, w_q, cos, sin)


def _mla_body(qa_ref, qr_ref, lat_ref, kr_ref, o_ref, *, tq, tk, lq, hp):
    i = pl.program_id(2)
    R = lat_ref.shape[-1]
    qa = [qa_ref[:, u * R:(u + 1) * R] for u in range(hp)]
    qr = [qr_ref[:, u * LANES:(u + 1) * LANES] for u in range(hp)]
    carry = tuple(_init_carry(tq, R) for _ in range(hp))

    qchunk = (i * tq + lax.broadcasted_iota(I32, (tq, tk), 0)) // CHUNK

    def new_step(j, c):
        off = pl.multiple_of(j * tk, tk)
        lat = _mx(lat_ref[pl.ds(off, tk), :])
        kr = _mx(kr_ref[pl.ds(off, tk), :])
        ok = (off + lax.broadcasted_iota(I32, (tq, tk), 1)) // CHUNK <= qchunk
        return tuple(_online_step(jnp.where(ok, _dot_nt(qa[u], lat) + _dot_nt(qr[u], kr), NEG), lat, c[u])
                     for u in range(hp))

    nb = _new_key_blocks(i, tq, lq, tk, _chunk_last_key(i, tq))
    carry = lax.fori_loop(0, nb, new_step, carry)
    for u, (m, l, acc) in enumerate(carry):
        o_ref[:, u * R:(u + 1) * R] = (acc / l).astype(o_ref.dtype)


def mla_attention(grp, q_abs, q_rope, lat, proj):
    assert grp.lp == 0
    B, H, hp = grp.batch, D_HEADS, HEADS_PER_STEP
    R = lat.shape[1]
    tq = grp.tq
    qrow, krow = grp.qrow(tq), grp.krow()
    in_specs = [pl.BlockSpec((tq, hp * R), lambda b, h, i: (qrow(b, i), h)),
                pl.BlockSpec((tq, hp * LANES), lambda b, h, i: (qrow(b, i), h)),
                pl.BlockSpec((grp.lq, R), lambda b, h, i: (krow(b), 0)),
                pl.BlockSpec((grp.lq, LANES), lambda b, h, i: (krow(b), ODD_COLS["kr"] // LANES))]
    return pl.pallas_call(
        functools.partial(_mla_body, tq=tq, tk=grp.tk, lq=grp.lq, hp=hp),
        grid=(B, H // hp, grp.lq // tq),
        in_specs=in_specs,
        out_specs=pl.BlockSpec((tq, hp * R), lambda b, h, i: (b * (grp.lq // tq) + i, h)),
        out_shape=jax.ShapeDtypeStruct((B * grp.lq, H * R), MXU_DTYPE),
        compiler_params=_params(3),
        name="mla_attention",
    )(q_abs, q_rope, lat, proj)


def _head_cols(ref, h, d, heads):
    if ref.shape[-1] == d:
        return ref[pl.ds(h, ref.shape[0] // heads, stride=heads), :]
    return ref[:, h * d:(h + 1) * d]


def _chain_reset(j, m_ref, l_ref, acc_ref):
    @pl.when(j == 0)
    def _():
        m_ref[...] = jnp.full(m_ref.shape, NEG, F32)
        l_ref[...] = jnp.zeros(l_ref.shape, F32)
        acc_ref[...] = jnp.zeros(acc_ref.shape, F32)


def _chains_advance(scores_values, m_ref, l_ref, acc_ref):
    states = [(m_ref[c], l_ref[c], acc_ref[c]) for c in range(len(scores_values))]
    new = [_online_step(s, v, st) for (s, v), st in zip(scores_values, states)]
    for c, (m, l, acc) in enumerate(new):
        m_ref[c], l_ref[c], acc_ref[c] = m, l, acc
    return [(l, acc) for _, l, acc in new]


def _past_key_spec(cache, layer, tkp, n_past, rows_per_key=1):
    blk = lambda j: jnp.minimum(j, n_past - 1)
    rows, w = tkp * rows_per_key, cache.shape[-1]
    if layer is None:
        return pl.BlockSpec((None, rows, w), lambda b, j: (b, blk(j), 0))
    return pl.BlockSpec((None, None, rows, w), lambda b, j: (layer, b, blk(j), 0))


def _keys_by_head(cache):
    n, b, p, h, d = cache.shape
    return cache.reshape(n, b, p * h, d)


def _past_call(body, grp, heads, dv, chains, in_specs, args, name):
    lq = grp.lq
    assert grp.lq == grp.tk
    return pl.pallas_call(
        body,
        grid=(grp.batch, grp.n_past_blocks + 1),
        in_specs=in_specs,
        out_specs=pl.BlockSpec((lq, heads * dv), lambda b, j: (b, 0)),
        out_shape=jax.ShapeDtypeStruct((grp.batch * lq, heads * dv), MXU_DTYPE),
        scratch_shapes=[pltpu.VMEM((chains, lq, 1), F32), pltpu.VMEM((chains, lq, 1), F32),
                        pltpu.VMEM((chains, lq, dv), F32)],
        compiler_params=_params(2),
        name=name,
    )(*args)


def _fox_past_body(q_ref, k_ref, v_ref, pk_ref, pv_ref, fq_ref, fk_ref, o_ref, m_ref, l_ref, acc_ref, *, n_past):
    j = pl.program_id(1)
    d, lq = HEAD_DIM, q_ref.shape[0]
    _chain_reset(j, m_ref, l_ref, acc_ref)
    q = lambda h: _mx(q_ref[:, h * d:(h + 1) * d] * d ** -0.5)

    @pl.when(j < n_past)
    def _():
        _chains_advance([(_dot_nt(q(h), _head_cols(pk_ref, h, d, B_HEADS)) + fq_ref[h] - fk_ref[h, 0],
                          _head_cols(pv_ref, h, d, B_HEADS)) for h in range(B_HEADS)], m_ref, l_ref, acc_ref)

    @pl.when(j == n_past)
    def _():
        causal = lax.broadcasted_iota(I32, (lq, lq), 1) <= lax.broadcasted_iota(I32, (lq, lq), 0)
        logits = lambda h: _dot_nt(q(h), k_ref[:, h * d:(h + 1) * d]) + fq_ref[h] - fk_ref[h, 0][:, :lq]
        done = _chains_advance([(jnp.where(causal, logits(h), NEG), v_ref[:, h * d:(h + 1) * d])
                                for h in range(B_HEADS)], m_ref, l_ref, acc_ref)
        for h, (l, acc) in enumerate(done):
            o_ref[:, h * d:(h + 1) * d] = (acc / l).astype(o_ref.dtype)


def fox_attention_past(grp, proj, fq, fk, cache_k, cache_v, layer):
    H, d, lq = B_HEADS, HEAD_DIM, grp.lq
    row = grp.krow()
    w = H * d
    in_specs = [pl.BlockSpec((lq, w), lambda b, j, n=n: (row(b), EVEN_COLS[n] // w)) for n in ("qf", "kf", "vf")]
    cache_k, cache_v = _keys_by_head(cache_k), _keys_by_head(cache_v)
    in_specs += [_past_key_spec(cache_k, layer, grp.tkp, grp.n_past_blocks, H),
                 _past_key_spec(cache_v, layer, grp.tkp, grp.n_past_blocks, H),
                 pl.BlockSpec((H, lq, 1), lambda b, j: (b, 0, 0)),
                 pl.BlockSpec((H, 1, 1, fk.shape[-1]), lambda b, j: (b, j, 0, 0))]
    return _past_call(functools.partial(_fox_past_body, n_past=grp.n_past_blocks), grp, H, d, H, in_specs,
                      [proj, proj, proj, cache_k, cache_v, fq, fk], "fox_attention_past")


def _diff_past_body(lam_ref, q_ref, k_ref, v_ref, pk_ref, pv_ref, g_ref, o_ref, m_ref, l_ref, acc_ref,
                    *, n_past, out_scale):
    j = pl.program_id(1)
    d = C_V_DIM
    _chain_reset(j, m_ref, l_ref, acc_ref)
    lane = lax.broadcasted_iota(I32, (q_ref.shape[0], d), 1)

    def maps(h):
        q = q_ref[:, h * d:(h + 1) * d] * C_QK_DIM ** -0.5
        return _mx(jnp.where(lane < C_QK_DIM, q, 0.0)), _mx(jnp.where(lane >= C_QK_DIM, q, 0.0))

    def both_maps(key_of, value_of):
        items = []
        for h in range(C_HEADS):
            k, v = _mx(key_of(h)), _mx(value_of(h))
            items += [(_dot_nt(qt, k), v) for qt in maps(h)]
        return _chains_advance(items, m_ref, l_ref, acc_ref)

    @pl.when(j < n_past)
    def _():
        both_maps(lambda h: _head_cols(pk_ref, h, d, C_HEADS), lambda h: _head_cols(pv_ref, h, d, C_HEADS))

    @pl.when(j == n_past)
    def _():
        done = both_maps(lambda h: k_ref[:, h * d:(h + 1) * d], lambda h: v_ref[:, h * d:(h + 1) * d])
        for h in range(C_HEADS):
            (l0, a0), (l1, a1) = done[2 * h], done[2 * h + 1]
            o = a0 / l0 - lam_ref[0] * (a1 / l1)
            o = o * lax.rsqrt(jnp.mean(o * o, axis=-1, keepdims=True) + NORM_EPS)
            o_ref[:, h * d:(h + 1) * d] = (o * g_ref[...] * out_scale).astype(o_ref.dtype)


def diff_attention_past(grp, proj, lam, subln, out_scale, cache_k, cache_v, layer):
    H, d, lq = C_HEADS, C_V_DIM, grp.lq
    assert lq <= CHUNK and grp.lp % CHUNK == 0
    row = grp.krow()
    w = H * d
    in_specs = [pl.BlockSpec(memory_space=pltpu.SMEM)]
    in_specs += [pl.BlockSpec((lq, w), lambda b, j, n=n: (row(b), ODD_COLS[n] // w)) for n in ("qc", "kc", "vc")]
    cache_v = _keys_by_head(cache_v)
    in_specs += [_past_key_spec(cache_k, None, grp.tkp, grp.n_past_blocks),
                 _past_key_spec(cache_v, layer, grp.tkp, grp.n_past_blocks, H),
                 pl.BlockSpec((1, d), lambda b, j: (0, 0))]
    return _past_call(functools.partial(_diff_past_body, n_past=grp.n_past_blocks, out_scale=out_scale), grp, H, d,
                      2 * H, in_specs, [lam, proj, proj, proj, cache_k, cache_v, subln], "diff_attention_past")


def _mla_past_body(qa_ref, qr_ref, lat_ref, kr_ref, plat_ref, pkr_ref, o_ref, m_ref, l_ref, acc_ref, *, n_past):
    j = pl.program_id(1)
    R = lat_ref.shape[-1]
    _chain_reset(j, m_ref, l_ref, acc_ref)

    @pl.when(j < n_past)
    def _():
        lat, kr = _mx(plat_ref[...]), _mx(pkr_ref[...])
        _chains_advance([(_dot_nt(qa_ref[:, h * R:(h + 1) * R], lat)
                          + _dot_nt(qr_ref[:, h * LANES:h * LANES + D_ROPE], kr), lat) for h in range(D_HEADS)],
                        m_ref, l_ref, acc_ref)

    @pl.when(j == n_past)
    def _():
        lat, kr = _mx(lat_ref[...]), _mx(kr_ref[...])
        done = _chains_advance([(_dot_nt(qa_ref[:, h * R:(h + 1) * R], lat)
                                 + _dot_nt(qr_ref[:, h * LANES:(h + 1) * LANES], kr), lat) for h in range(D_HEADS)],
                               m_ref, l_ref, acc_ref)
        for h, (l, acc) in enumerate(done):
            o_ref[:, h * R:(h + 1) * R] = (acc / l).astype(o_ref.dtype)


def mla_attention_past(grp, q_abs, q_rope, lat, proj, cache_lat, cache_kr, layer):
    H, lq = D_HEADS, grp.lq
    assert lq <= CHUNK and grp.lp % CHUNK == 0
    R = lat.shape[1]
    row = grp.krow()
    in_specs = [pl.BlockSpec((lq, H * R), lambda b, j: (row(b), 0)),
                pl.BlockSpec((lq, H * LANES), lambda b, j: (row(b), 0)),
                pl.BlockSpec((lq, R), lambda b, j: (row(b), 0)),
                pl.BlockSpec((lq, LANES), lambda b, j: (row(b), ODD_COLS["kr"] // LANES)),
                _past_key_spec(cache_lat, layer, grp.tkp, grp.n_past_blocks),
                _past_key_spec(cache_kr, layer, grp.tkp, grp.n_past_blocks)]
    return _past_call(functools.partial(_mla_past_body, n_past=grp.n_past_blocks), grp, H, R, H, in_specs,
                      [q_abs, q_rope, lat, proj, cache_lat, cache_kr], "mla_attention_past")


def _sortable(score):
    bits = pltpu.bitcast(score, I32)
    return jnp.where(bits < 0, bits ^ jnp.int32(0x7FFFFFFF), bits)


def _dsa_body(*refs, tq, i0, nk, topk, has_past):
    if has_past:
        qi_ref, misc_ref, q_ref, ki_ref, k_ref, v_ref, pki_ref, pk_ref, pv_ref, o_ref, *sel_refs = refs
    else:
        qi_ref, misc_ref, q_ref, ki_ref, k_ref, v_ref, o_ref, *sel_refs = refs
    i = i0 + pl.program_id(1)
    wi = misc_ref[:, MISC_WI:MISC_WI + IDX_HEADS] * (IDX_HEADS * IDX_DIM) ** -0.5

    qchunk = (i * tq + lax.broadcasted_iota(I32, (tq, nk), 0)) // CHUNK
    adm_new = lax.broadcasted_iota(I32, (tq, nk), 1) // CHUNK <= qchunk
    segs = [(_mx(ki_ref[...]), k_ref, v_ref, adm_new)]
    if has_past:
        segs.insert(0, (_mx(pki_ref[0]), pk_ref.at[0], pv_ref.at[0], None))

    keys = []
    for ki, _, _, adm in segs:
        score = jnp.zeros((tq, ki.shape[0]), F32)
        for h in range(IDX_HEADS):
            s = _dot_nt(qi_ref[:, h * IDX_DIM:(h + 1) * IDX_DIM], ki)
            score = score + jnp.maximum(s, 0.0) * wi[:, h:h + 1]
        key = _sortable(jnp.where(score == 0.0, 0.0, score))
        keys.append(key if adm is None else jnp.where(adm, key, INT_MIN))

    def count(pred):
        return sum(jnp.sum(jnp.where(pred(key), 1.0, 0.0), axis=-1, keepdims=True) for key in keys)

    kf = float(topk)
    n_ge = count(lambda k: k >= 0)
    t = jnp.where(n_ge >= kf, 0, INT_MIN).astype(I32)

    def bit_step(it, carry):
        t, n_ge = carry
        cand = t | jnp.left_shift(jnp.int32(1), 30 - it)
        n_cand = count(lambda k: k >= cand)
        fits = n_cand >= kf
        return jnp.where(fits, cand, t), jnp.where(fits, n_cand, n_ge)

    t, n_ge = lax.fori_loop(0, 31, bit_step, (t, n_ge))
    for sel_ref, key, (_, _, _, adm) in zip(sel_refs, keys, segs):
        sel = (key >= t) if adm is None else (adm & (key >= t))
        sel_ref[...] = jnp.where(sel, 1, 0)

    bounded = t > INT_MIN
    crowded = jnp.logical_and(bounded, n_ge > kf)

    @pl.when(jnp.max(jnp.where(crowded, 1.0, 0.0)) > 0.0)
    def _():
        places = jnp.where(bounded, kf - count(lambda k: k > t), float(2 ** 30))
        seen = jnp.zeros((tq, 1), F32)
        for sel_ref, key, (_, _, _, adm) in zip(sel_refs, keys, segs):
            n = key.shape[1]
            blk = min(n, KEY_BLOCK)
            upper = jnp.where(lax.broadcasted_iota(I32, (blk, blk), 0) <= lax.broadcasted_iota(I32, (blk, blk), 1),
                              1.0, 0.0)
            for c in range(n // blk):
                kb = key[:, c * blk:(c + 1) * blk]
                tie = kb == t
                tie_f = jnp.where(tie, 1.0, 0.0)
                ahead = _dot(tie_f, upper) + seen
                sel = (kb > t) | (tie & (ahead <= places))
                if adm is not None:
                    sel = sel & adm[:, c * blk:(c + 1) * blk]
                sel_ref[:, c * blk:(c + 1) * blk] = jnp.where(sel, 1, 0)
                seen = seen + jnp.sum(tie_f, axis=-1, keepdims=True)

    sels = [sel_ref[...] > 0 for sel_ref in sel_refs]

    rep = A_HEADS // A_KV_HEADS
    for g in range(A_KV_HEADS):
        kv = [(_mx(k[:, g * HEAD_DIM:(g + 1) * HEAD_DIM]), _mx(v[:, g * HEAD_DIM:(g + 1) * HEAD_DIM]))
              for _, k, v, _ in segs]
        for r in range(rep):
            h = g * rep + r
            q = _mx(q_ref[:, h * HEAD_DIM:(h + 1) * HEAD_DIM] * HEAD_DIM ** -0.5)
            ss = [jnp.where(sel, _dot_nt(q, k), NEG) for sel, (k, _) in zip(sels, kv)]
            m = functools.reduce(jnp.maximum, [jnp.max(s, axis=-1, keepdims=True) for s in ss])
            ps = [jnp.exp(s - m) for s in ss]
            l = sum(jnp.sum(p, axis=-1, keepdims=True) for p in ps)
            o = sum(_dot(p, v) for p, (_, v) in zip(ps, kv))
            o_ref[:, h * HEAD_DIM:(h + 1) * HEAD_DIM] = (o / l).astype(o_ref.dtype)


def dsa_attention(grp, proj, past_ki, past_k, past_v):
    B = grp.batch
    tq = grp.tq_dsa
    n_tiles = grp.lq // tq
    wq, wkv = A_HEADS * HEAD_DIM, A_KV_HEADS * HEAD_DIM
    wqi = IDX_HEADS * IDX_DIM
    c = EVEN_COLS

    extents = sorted({grp.lq // f for f in (1, 2, 4) if grp.lq % f == 0 and (grp.lq // f) % tq == 0
                      and grp.row0 % (grp.lq // f) == 0})
    classes = []
    for i in range(n_tiles):
        need = min(grp.lq, _chunk_last_key(i, tq) + 1)
        nk = min(e for e in extents if e >= need)
        if classes and classes[-1][2] == nk:
            classes[-1][1] += 1
        else:
            classes.append([i, 1, nk])

    outs = []
    for i0, n_i, nk in classes:
        qrow = lambda b, i, i0=i0: grp.row0 // tq + b * n_tiles + i0 + i
        krow = lambda b, nk=nk: (grp.row0 + b * grp.lq) // nk
        in_specs = [pl.BlockSpec((tq, wqi), lambda b, i, f=qrow: (f(b, i), c["qi"] // wqi)),
                    pl.BlockSpec((tq, LANES), lambda b, i, f=qrow: (f(b, i), c["misc"] // LANES)),
                    pl.BlockSpec((tq, wq), lambda b, i, f=qrow: (f(b, i), c["qa"] // wq)),
                    pl.BlockSpec((nk, IDX_DIM), lambda b, i, f=krow: (f(b), c["ki"] // IDX_DIM)),
                    pl.BlockSpec((nk, wkv), lambda b, i, f=krow: (f(b), c["ka"] // wkv)),
                    pl.BlockSpec((nk, wkv), lambda b, i, f=krow: (f(b), c["va"] // wkv))]
        args = [proj] * 6
        if grp.lp:
            in_specs += [pl.BlockSpec((1, grp.lp, IDX_DIM), lambda b, i: (b, 0, 0)),
                         pl.BlockSpec((1, grp.lp, wkv), lambda b, i: (b, 0, 0)),
                         pl.BlockSpec((1, grp.lp, wkv), lambda b, i: (b, 0, 0))]
            args += [past_ki, past_k, past_v]
        o = pl.pallas_call(
            functools.partial(_dsa_body, tq=tq, i0=i0, nk=nk, topk=grp.topk, has_past=bool(grp.lp)),
            grid=(B, n_i),
            in_specs=in_specs,
            out_specs=pl.BlockSpec((tq, wq), lambda b, i, n_i=n_i: (b * n_i + i, 0)),
            out_shape=jax.ShapeDtypeStruct((B * n_i * tq, wq), MXU_DTYPE),
            scratch_shapes=[pltpu.VMEM((tq, n), I32) for n in ([grp.lp] if grp.lp else []) + [nk]],
            compiler_params=_params(2),
            name="dsa_attention",
        )(*args)
        outs.append(o.reshape(B, n_i * tq, wq))
    return jnp.concatenate(outs, axis=1).reshape(B * grp.lq, wq)


def _first_index_of_max(x, iota, n):
    mx = jnp.max(x, axis=0, keepdims=True)
    return mx, jnp.min(jnp.where(x == mx, iota, n), axis=0, keepdims=True)


def _route_body(x_ref, sc_ref, sh_ref, wr_ref, bias_ref, hp_ref, gate_ref, rank_ref, cnt_ref, h_ref, carry_ref,
                *, group):
    step = pl.program_id(0)
    tm = x_ref.shape[0]
    E = wr_ref.shape[0]
    per = E // N_GROUPS
    for g, part in enumerate(_modulated(x_ref, sc_ref, sh_ref, group)):
        h_ref[g * group:(g + 1) * group, :] = part
    h = h_ref[...]
    _store_words(hp_ref, 0, _pack_rows(h), hp_ref.shape[0] // tm)

    scores = jax.nn.sigmoid(_dot_nt(wr_ref[...], h, precision=lax.Precision.HIGHEST))
    biased = scores + bias_ref[...]
    member = lax.broadcasted_iota(I32, (per, tm), 0).astype(F32)
    gscore = []
    for g in range(N_GROUPS):
        blk = biased[g * per:(g + 1) * per, :]
        m1, i1 = _first_index_of_max(blk, member, per)
        m2 = jnp.max(jnp.where(member == i1, -jnp.inf, blk), axis=0, keepdims=True)
        gscore.append(m1 + m2)
    gscore = jnp.concatenate(gscore, axis=0)
    giota = lax.broadcasted_iota(I32, (N_GROUPS, tm), 0).astype(F32)
    gsel = jnp.zeros((N_GROUPS, tm), F32)
    for _ in range(TOPK_GROUPS):
        _, gi = _first_index_of_max(gscore, giota, N_GROUPS)
        hit = giota == gi
        gsel = jnp.where(hit, 1.0, gsel)
        gscore = jnp.where(hit, -jnp.inf, gscore)
    emask = jnp.concatenate([jnp.broadcast_to(gsel[g:g + 1, :], (per, tm)) for g in range(N_GROUPS)], axis=0)
    masked = jnp.where(emask > 0.0, biased, -jnp.inf)
    eiota = lax.broadcasted_iota(I32, (E, tm), 0).astype(F32)
    sel = jnp.zeros((E, tm), jnp.bool_)
    for _ in range(TOP_K):
        _, ei = _first_index_of_max(masked, eiota, E)
        hit = eiota == ei
        sel = sel | hit
        masked = jnp.where(hit, -jnp.inf, masked)
    w = jnp.where(sel, scores, 0.0)
    gate_ref[...] = w / jnp.sum(w, axis=0, keepdims=True) * ROUTED_SCALE

    @pl.when(step == 0)
    def _():
        carry_ref[...] = jnp.zeros(carry_ref.shape, F32)
    upper = (lax.broadcasted_iota(I32, (tm, tm), 0) <= lax.broadcasted_iota(I32, (tm, tm), 1))
    self = jnp.where(sel, 1.0, 0.0)
    incl = _dot(self, jnp.where(upper, 1.0, 0.0)) + carry_ref[...]
    rank_ref[...] = jnp.where(sel, incl - 1.0, -1.0).astype(I32)
    carry_ref[...] = incl[:, tm - 1:tm]
    cnt_ref[...] = incl[:, tm - 1:tm].astype(I32)


def route(x, scale_g, shift_g, w_router_t, r_bias, group, tm):
    T, D = x.shape
    E = w_router_t.shape[0]
    row = lambda i: (i, 0)
    col = lambda i: (0, i)
    fixed = lambda i: (0, 0)
    return pl.pallas_call(
        functools.partial(_route_body, group=group),
        grid=(T // tm,),
        in_specs=[pl.BlockSpec((tm, D), row),
                  pl.BlockSpec((1, tm // group, D), lambda i: (i, 0, 0)),
                  pl.BlockSpec((1, tm // group, D), lambda i: (i, 0, 0)),
                  pl.BlockSpec((E, D), fixed),
                  pl.BlockSpec((E, 1), fixed)],
        out_specs=[pl.BlockSpec((tm * (D // 2 // LANES), LANES), row), pl.BlockSpec((E, tm), col),
                   pl.BlockSpec((E, tm), col), pl.BlockSpec((E, 1), fixed)],
        out_shape=[jax.ShapeDtypeStruct((T * (D // 2 // LANES), LANES), I32), jax.ShapeDtypeStruct((E, T), F32),
                   jax.ShapeDtypeStruct((E, T), I32), jax.ShapeDtypeStruct((E, 1), I32)],
        scratch_shapes=[pltpu.VMEM((tm, D), F32), pltpu.VMEM((E, 1), F32)],
        compiler_params=_params(1),
        name="route",
    )(x, _per_tile(scale_g, tm, group), _per_tile(shift_g, tm, group), w_router_t, r_bias)


def _compact_body(gate_ref, rank_ref, off_ref, dest_ref, g8_ref):
    E, tm = gate_ref.shape
    rank = rank_ref[...]
    sel = rank >= 0
    dest = (rank + off_ref[...]).astype(F32)
    gate = gate_ref[...]
    eiota = lax.broadcasted_iota(I32, (E, tm), 0).astype(F32)
    dests, gates = [], []
    for _ in range(TOP_K):
        ei = jnp.min(jnp.where(sel, eiota, E), axis=0, keepdims=True)
        hit = eiota == ei
        dests.append(jnp.sum(jnp.where(hit, dest, 0.0), axis=0, keepdims=True))
        gates.append(jnp.sum(jnp.where(hit, gate, 0.0), axis=0, keepdims=True))
        sel = sel & jnp.logical_not(hit)
    dest_ref[...] = jnp.concatenate(dests, axis=0).astype(I32)
    g8_ref[...] = jnp.concatenate(gates, axis=0)


def compact_routes(gate, rank, seg_off, tm):
    E, T = gate.shape
    col = lambda i: (0, i)
    return pl.pallas_call(
        _compact_body,
        grid=(T // tm,),
        in_specs=[pl.BlockSpec((E, tm), col), pl.BlockSpec((E, tm), col), pl.BlockSpec((E, 1), lambda i: (0, 0))],
        out_specs=[pl.BlockSpec((TOP_K, tm), col), pl.BlockSpec((TOP_K, tm), col)],
        out_shape=[jax.ShapeDtypeStruct((TOP_K, T), I32), jax.ShapeDtypeStruct((TOP_K, T), F32)],
        compiler_params=_params(1),
        name="compact_routes",
    )(gate, rank, seg_off)


def _ffn(x, w1, w3, w2):
    a = _dot(x, w1)
    return _dot(a * jax.nn.sigmoid(a) * _dot(x, w3), w2)


def _gmm_body(te_ref, nt_ref, tok_ref, tok_next_ref, slot_ref, x_hbm, w1_ref, w3_ref, w2_ref, y_hbm,
              w1b, w3b, w2b, xbuf, ybuf, in_sem, out_sem, *, tm, parts, per):
    i = pl.program_id(0)
    n_used = nt_ref[0]
    cur, nxt = i % 2, (i + 1) % 2

    def token_rows(t):
        return pl.ds(pl.multiple_of(t * per, per), per)

    def row_in(tok, r, buf):
        return pltpu.make_async_copy(x_hbm.at[token_rows(tok[0, 0, r])], xbuf.at[buf, pl.ds(r * per, per)],
                                     in_sem.at[buf])

    def row_out(r, buf):
        return pltpu.make_async_copy(ybuf.at[buf, pl.ds(r * per, per)], y_hbm.at[token_rows(slot_ref[0, 0, r])],
                                     out_sem.at[buf])

    def wait_in(buf):
        pltpu.make_async_copy(x_hbm.at[pl.ds(0, tm * per)], xbuf.at[buf], in_sem.at[buf]).wait()

    def wait_out(buf):
        pltpu.make_async_copy(ybuf.at[buf], y_hbm.at[pl.ds(0, tm * per)], out_sem.at[buf]).wait()

    @pl.when(i == 0)
    def _():
        for r in range(tm):
            row_in(tok_ref, r, 0).start(priority=r % 2)
        ybuf[1] = jnp.zeros(ybuf.shape[1:], I32)
        spare = pltpu.make_async_copy(ybuf.at[1], y_hbm.at[pl.ds(y_hbm.shape[0] - tm * per, tm * per)], out_sem.at[1])
        spare.start()
        spare.wait()

    @pl.when(jnp.logical_or(i == 0, te_ref[i] != te_ref[jnp.maximum(i - 1, 0)]))
    def _():
        w1b[...] = _mx(w1_ref[...])
        w3b[...] = _mx(w3_ref[...])
        w2b[...] = _mx(w2_ref[...])

    @pl.when(i < n_used)
    def _():
        wait_in(cur)
        for r in range(tm):
            row_in(tok_next_ref, r, nxt).start(priority=r % 2)
        rows = tm // parts
        for p in range(parts):
            x = _unpack_rows(_load_words(xbuf.at[cur], p * rows, rows, per))
            _store_words(ybuf.at[cur], p * rows, _pack_rows(_ffn(x, w1b[...], w3b[...], w2b[...])), per)
            for r in range(p * rows, (p + 1) * rows):
                row_out(r, cur).start(priority=r % 2)

    @pl.when(jnp.logical_and(i >= 1, i < n_used))
    def _():
        wait_out(nxt)

    @pl.when(i == n_used - 1)
    def _():
        wait_in(nxt)
        wait_out(cur)


def grouped_ffn(tile_expert, n_tiles, row_token, row_slot, x, w1, w3, w2, layer, n_out, tm):
    n_rows = row_token.shape[0] * tm
    D, F = w1.shape[-2], w1.shape[-1]
    per = D // 2 // LANES
    clamp = lambda i, nt: jnp.maximum(jnp.minimum(i, nt[0] - 1), 0)
    wsel = lambda i, te, nt: (layer, te[i], 0, 0)
    smem_rows = lambda f: pl.BlockSpec((1, 1, tm), f, memory_space=pltpu.SMEM)
    return pl.pallas_call(
        functools.partial(_gmm_body, tm=tm, parts=2, per=per),
        grid_spec=pltpu.PrefetchScalarGridSpec(
            num_scalar_prefetch=2,
            grid=(n_rows // tm,),
            in_specs=[smem_rows(lambda i, te, nt: (clamp(i, nt), 0, 0)),
                      smem_rows(lambda i, te, nt: (clamp(i + 1, nt), 0, 0)),
                      smem_rows(lambda i, te, nt: (clamp(i, nt), 0, 0)),
                      pl.BlockSpec(memory_space=pl.ANY),
                      pl.BlockSpec((None, None, D, F), wsel),
                      pl.BlockSpec((None, None, D, F), wsel),
                      pl.BlockSpec((None, None, F, D), wsel)],
            out_specs=pl.BlockSpec(memory_space=pl.ANY),
            scratch_shapes=[pltpu.VMEM((D, F), MXU_DTYPE), pltpu.VMEM((D, F), MXU_DTYPE),
                            pltpu.VMEM((F, D), MXU_DTYPE),
                            pltpu.VMEM((2, tm * per, LANES), I32), pltpu.VMEM((2, tm * per, LANES), I32),
                            pltpu.SemaphoreType.DMA((2,)), pltpu.SemaphoreType.DMA((2,))]),
        out_shape=jax.ShapeDtypeStruct((n_out * per, LANES), I32),
        compiler_params=pltpu.CompilerParams(dimension_semantics=("arbitrary",), vmem_limit_bytes=VMEM_LIMIT_BYTES,
                                             disable_bounds_checks=True),
        name="grouped_ffn",
    )(tile_expert, n_tiles, row_token, row_token, row_slot, x, w1, w3, w2)


def _close_moe_body(h_ref, *rest, group, alpha, n_k):
    yk_refs, (g8_ref, ws1_ref, ws3_ref, ws2_ref, x_ref, gate_ref, g_ref, b_ref, o_ref) = rest[:n_k], rest[n_k:]
    tm = x_ref.shape[0]
    per = h_ref.shape[0] // tm
    words = lambda ref: _unpack_rows(_load_words(ref, 0, tm, per))
    y = _ffn(words(h_ref), ws1_ref[...], ws3_ref[...], ws2_ref[...])
    g8 = g8_ref[...]
    for k, yk_ref in enumerate(yk_refs):
        y = y + g8[:, k:k + 1] * words(yk_ref)
    r = _gated_residual(x_ref, y, gate_ref, group, alpha)
    o_ref[...] = _layer_norm_rows(r, g_ref[...], b_ref[...])


def close_moe(h, y_rows, g8, ws1, ws3, ws2, x, gate_g, ln_g, ln_b, group, alpha, tm):
    T, D = x.shape
    K = g8.shape[1]
    per = h.shape[0] // T
    row = lambda i: (i, 0)
    fixed = lambda i: (0, 0)
    return pl.pallas_call(
        functools.partial(_close_moe_body, group=group, alpha=alpha, n_k=K),
        grid=(T // tm,),
        in_specs=[pl.BlockSpec((tm * per, LANES), row)] +
                 [pl.BlockSpec((tm * per, LANES), lambda i, k=k: (k * (T // tm) + i, 0)) for k in range(K)] +
                 [pl.BlockSpec((tm, K), row),
                  pl.BlockSpec(ws1.shape, fixed),
                  pl.BlockSpec(ws3.shape, fixed),
                  pl.BlockSpec(ws2.shape, fixed),
                  pl.BlockSpec((tm, D), row),
                  pl.BlockSpec((1, tm // group, D), lambda i: (i, 0, 0)),
                  pl.BlockSpec((1, D), fixed),
                  pl.BlockSpec((1, D), fixed)],
        out_specs=pl.BlockSpec((tm, D), row),
        out_shape=jax.ShapeDtypeStruct((T, D), F32),
        compiler_params=_params(1),
        name="close_moe",
    )(h, *([y_rows] * K), g8, ws1, ws3, ws2, x, _per_tile(gate_g, tm, group), ln_g, ln_b)


def _rope_tables(pos, half):
    inv_freq = jnp.power(ROPE_THETA, -jnp.arange(half, dtype=F32) / half)
    ang = pos.astype(F32)[:, None] * inv_freq[None, :]
    cos, sin = jnp.cos(ang), jnp.sin(ang)
    reps = LANES // (2 * half)
    cos = jnp.tile(jnp.concatenate([cos, cos], axis=1), (1, reps))
    sin = jnp.tile(jnp.concatenate([-sin, sin], axis=1), (1, reps))
    return jnp.stack([jnp.ones_like(cos), cos]), jnp.stack([jnp.zeros_like(sin), sin])


def _place_cols(w, sizes, names, cols, width):
    out = jnp.zeros((w.shape[0], width), w.dtype)
    o = 0
    for size, name in zip(sizes, names):
        if name is not None:
            dst, sub = name
            out = lax.dynamic_update_slice(out, w[:, o:o + size], (0, cols[dst] + sub))
        o += size
    return out


def _moe(x, scale_g, shift_g, gate_g, ln_g, ln_b, w_router, r_bias, w1, w3, w2, ws1, ws3, ws2, layer, group, alpha):
    T, D = x.shape
    E = w1.shape[1]
    tm = 256
    t_route = _pick(T, 768, math.lcm(LANES, group))
    h, gate, rank, counts = route(x, scale_g, shift_g, w_router.T, r_bias.reshape(E, 1), group, t_route)

    counts = counts[:, 0]
    tiles_per = (counts + tm - 1) // tm
    tile_end = jnp.cumsum(tiles_per)
    seg_off = ((tile_end - tiles_per) * tm).astype(I32)
    n_tiles = (T * TOP_K) // tm + E
    used = tile_end[-1].astype(I32)
    tile_ids = jnp.minimum(jnp.arange(n_tiles, dtype=I32), used - 1)
    tile_expert = jnp.sum((tile_end[None, :] <= tile_ids[:, None]).astype(I32), axis=1)

    dest, g8 = compact_routes(gate, rank, seg_off.reshape(E, 1), t_route)
    n_pairs = TOP_K * T
    pair_slot = jnp.arange(n_pairs, dtype=I32).reshape(TOP_K, T)
    spare = n_pairs + jnp.arange(n_tiles * tm, dtype=I32) % tm
    row_slot = spare.at[dest.reshape(-1)].set(pair_slot.reshape(-1), unique_indices=True, mode="promise_in_bounds")
    row_token = jnp.where(row_slot < n_pairs, row_slot % T, 0)
    y_rows = grouped_ffn(tile_expert, used.reshape(1), row_token.reshape(n_tiles, 1, tm),
                         row_slot.reshape(n_tiles, 1, tm), h, w1, w3, w2, layer, n_pairs + tm, tm)
    return close_moe(h, y_rows, g8.T, _mx(ws1), _mx(ws3), _mx(ws2), x, gate_g, ln_g, ln_b, group, alpha,
                     _pick(T, 256, math.lcm(SUBLANES, group)))


def _fox_gates(grp, proj, b_f, past_logf):
    B, H = grp.batch, B_HEADS
    c0 = EVEN_COLS["misc"] + MISC_FL
    fl = proj[grp.row0:grp.row0 + B * grp.lq, c0:c0 + H].reshape(B, grp.lq, H)
    logf = jax.nn.log_sigmoid(fl + b_f)
    logf_all = logf if past_logf is None else jnp.concatenate([past_logf, logf], axis=1)
    cum = jnp.moveaxis(jnp.cumsum(logf_all, axis=1), 1, 2).reshape(B * H, grp.lk)
    fq = cum[:, grp.lp:, None]
    new = cum[:, grp.lp:].reshape(B * H, grp.lq // grp.tk, 1, grp.tk)
    new = jnp.pad(new, ((0, 0), (0, 0), (0, 0), (0, KEY_BLOCK - grp.tk)))
    if grp.lp:
        past = cum[:, :grp.lp].reshape(B * H, grp.n_past_blocks, 1, grp.tkp)
        past = jnp.pad(past, ((0, 0), (0, 0), (0, 0), (0, KEY_BLOCK - grp.tkp)))
        new = jnp.concatenate([past, new], axis=1)
    return logf, fq, new


def kernel(x_prompt, x_sample, c_prompt, c_sample, cache_a_k, cache_a_v, cache_a_kidx, cache_b_k, cache_b_v, cache_b_logf, cache_c_k, cache_c_v, cache_d_latent, cache_d_krope, w_in_even, b_forget, w_out_even, w_in_odd, c_lambda, c_subln, d_q_norm, d_w_uq, d_kv_norm, d_w_uk, d_w_uv, w_out_odd, ada_mix_w, ada_mix_b, ln_mix_g, ln_mix_b, ada_ffn_w, ada_ffn_b, ln_ffn_g, ln_ffn_b, router_w, router_bias, moe_w1, moe_w3, moe_w2, shared_w1, shared_w3, shared_w2):
    B, L, D = x_prompt.shape
    Bs, Ls, _ = x_sample.shape
    depth = ada_mix_w.shape[0]
    alpha = (2 * depth) ** 0.25
    past_len = cache_a_k.shape[2]
    Tp = B * L
    T = Tp + Bs * Ls
    grp_p = _Group(B, L, 0, 0)
    grp_s = _Group(Bs, Ls, past_len, Tp)
    group = math.gcd(L, Ls)
    assert group % (2 * SUBLANES) == 0
    tm_tok = _pick(T, 768, math.lcm(LANES, group))
    tm_close = _pick(T, 512, math.lcm(2 * SUBLANES, group))
    pos = jnp.concatenate([jnp.tile(jnp.arange(L), B), jnp.tile(past_len + jnp.arange(Ls), Bs)])
    c_act = jax.nn.silu(jnp.concatenate([c_prompt, c_sample], axis=0))

    def conditioning(ada_w, ada_b, layer):
        mod = matmul(c_act, ada_w, layer=layer) + ada_b[layer]
        per_group = jnp.concatenate([jnp.repeat(mod[:B], L // group, axis=0),
                                     jnp.repeat(mod[B:], Ls // group, axis=0)], axis=0)
        return jnp.split(per_group, 3, axis=-1)

    def flat_past(c):
        return c.reshape(c.shape[0], c.shape[1], -1)

    x = jnp.concatenate([x_prompt.reshape(Tp, D), x_sample.reshape(Bs * Ls, D)], axis=0)
    ev_p, ev_s, od_p, od_s = [], [], [], []
    for i in range(depth):
        j = i // 2
        shift_g, scale_g, gate_g = conditioning(ada_mix_w, ada_mix_b, i)
        if i % 2 == 0:
            sizes = (A_HEADS * HEAD_DIM, A_KV_HEADS * HEAD_DIM, A_KV_HEADS * HEAD_DIM, IDX_HEADS * IDX_DIM, IDX_DIM,
                     IDX_HEADS, B_HEADS * HEAD_DIM, B_HEADS * HEAD_DIM, B_HEADS * HEAD_DIM, B_HEADS)
            names = (("qa", 0), ("ka", 0), ("va", 0), ("qi", 0), ("ki", 0), ("misc", MISC_WI),
                     ("qf", 0), ("kf", 0), ("vf", 0), ("misc", MISC_FL))
            w_in = _mx(_place_cols(w_in_even[j], sizes, names, EVEN_COLS, EVEN_WIDTH))
            cos, sin = _rope_tables(pos, HEAD_DIM // 2)
            proj = project(x, scale_g, shift_g, w_in, cos, sin, EVEN_ROPE_RANGES, HEAD_DIM // 2, group, tm_tok)
            outs_a, outs_b = [], []
            for grp, store in ((grp_p, ev_p), (grp_s, ev_s)):
                if grp.lp:
                    logf, fq, fk = _fox_gates(grp, proj, b_forget[j], cache_b_logf[j])
                    outs_a.append(dsa_attention(grp, proj, cache_a_kidx[j], flat_past(cache_a_k[j]),
                                                flat_past(cache_a_v[j])))
                    outs_b.append(fox_attention_past(grp, proj, fq, fk, cache_b_k, cache_b_v, j))
                else:
                    logf, fq, fk = _fox_gates(grp, proj, b_forget[j], None)
                    outs_a.append(dsa_attention(grp, proj, None, None, None))
                    outs_b.append(fox_attention(grp, proj, fq, fk))
                r0, r1 = grp.row0, grp.row0 + grp.batch * grp.lq
                cut = lambda name, heads, d: proj[r0:r1, EVEN_COLS[name]:EVEN_COLS[name] + heads * d].reshape(
                    (grp.batch, grp.lq) + ((heads, d) if heads > 1 else (d,)))
                store.append((cut("ka", A_KV_HEADS, HEAD_DIM), cut("va", A_KV_HEADS, HEAD_DIM), cut("ki", 1, IDX_DIM),
                              cut("kf", B_HEADS, HEAD_DIM), cut("vf", B_HEADS, HEAD_DIM), logf))
            a1, a2 = jnp.concatenate(outs_a, axis=0), jnp.concatenate(outs_b, axis=0)
            n1 = A_HEADS * HEAD_DIM
            w1o, w2o = _mx(w_out_even[j][:n1]), _mx(w_out_even[j][n1:])
        else:
            lam_init = 0.8 - 0.6 * math.exp(-0.3 * i)
            lam_f = c_lambda[j]
            lam = (jnp.exp(jnp.sum(lam_f[0] * lam_f[1])) - jnp.exp(jnp.sum(lam_f[2] * lam_f[3])) + lam_init).reshape(1)
            q_lora, R = d_q_norm.shape[1], d_kv_norm.shape[1]
            sizes = (C_HEADS * 2 * C_QK_DIM, C_HEADS * 2 * C_QK_DIM, C_HEADS * C_V_DIM, q_lora, R, D_ROPE)
            names = (("qc", 0), ("kc", 0), ("vc", 0), ("qd", 0), ("ckv", 0), ("kr", 0))
            w_in = _mx(_place_cols(w_in_odd[j], sizes, names, ODD_COLS, ODD_WIDTH))
            cos, sin = _rope_tables(pos, C_QK_DIM // 2)
            proj = project(x, scale_g, shift_g, w_in, cos, sin, ODD_ROPE_RANGES, C_QK_DIM // 2, group, tm_tok)

            w_uq = d_w_uq[j].reshape(q_lora, D_HEADS, D_NOPE + D_ROPE)
            w_abs = bmm_precise(jnp.moveaxis(w_uq[:, :, :D_NOPE], 1, 0), jnp.transpose(d_w_uk[j], (1, 2, 0)))
            w_abs = jnp.moveaxis(w_abs, 0, 1).reshape(q_lora, D_HEADS * R)
            w_rope = jnp.pad(w_uq[:, :, D_NOPE:], ((0, 0), (0, 0), (0, LANES - D_ROPE))).reshape(q_lora, D_HEADS * LANES)
            q_abs, q_rope, lat = mla_prepare(proj, d_q_norm[j][None], d_kv_norm[j][None],
                                             _mx(jnp.concatenate([w_abs, w_rope], axis=1)), cos, sin, tm_close)
            n1 = C_HEADS * C_V_DIM
            w_od = w_out_odd[j][n1:].reshape(D_HEADS, D_V, D)
            w_lat_out = bmm_precise(jnp.transpose(d_w_uv[j], (1, 0, 2)), w_od).reshape(D_HEADS * R, D)

            outs_a, outs_b = [], []
            for grp, store in ((grp_p, od_p), (grp_s, od_s)):
                if grp.lp:
                    outs_a.append(diff_attention_past(grp, proj, lam, c_subln[j][None], 1.0 - lam_init,
                                                      flat_past(cache_c_k[j]), cache_c_v, j))
                    outs_b.append(mla_attention_past(grp, q_abs, q_rope, lat, proj, cache_d_latent, cache_d_krope, j))
                else:
                    outs_a.append(diff_attention(grp, proj, lam, c_subln[j][None], 1.0 - lam_init))
                    outs_b.append(mla_attention(grp, q_abs, q_rope, lat, proj))
                r0, r1 = grp.row0, grp.row0 + grp.batch * grp.lq
                shp = (grp.batch, grp.lq)
                oc = ODD_COLS
                store.append((proj[r0:r1, oc["kc"]:oc["kc"] + C_HEADS * 2 * C_QK_DIM].reshape(shp + (C_HEADS, 2, C_QK_DIM)),
                              proj[r0:r1, oc["vc"]:oc["vc"] + n1].reshape(shp + (C_HEADS, C_V_DIM)),
                              lat[r0:r1].reshape(shp + (R,)),
                              proj[r0:r1, oc["kr"]:oc["kr"] + D_ROPE].reshape(shp + (D_ROPE,))))
            a1, a2 = jnp.concatenate(outs_a, axis=0), jnp.concatenate(outs_b, axis=0)
            w1o, w2o = _mx(w_out_odd[j][:n1]), _mx(w_lat_out)
        x = close_mixer(a1, a2, w1o, w2o, x, gate_g, ln_mix_g[i][None], ln_mix_b[i][None], group, alpha, tm_close)

        shift_g, scale_g, gate_g = conditioning(ada_ffn_w, ada_ffn_b, i)
        x = _moe(x, scale_g, shift_g, gate_g, ln_ffn_g[i][None], ln_ffn_b[i][None], router_w[i], router_bias[i],
                 moe_w1, moe_w3, moe_w2, shared_w1[i], shared_w3[i], shared_w2[i], i, group, alpha)

    stack = lambda rows, idx: jnp.stack([r[idx] for r in rows])
    outs = [x[:Tp].reshape(B, L, D), x[Tp:].reshape(Bs, Ls, D)]
    for idx in range(6):
        outs += [stack(ev_p, idx), stack(ev_s, idx)]
    for idx in range(4):
        outs += [stack(od_p, idx), stack(od_s, idx)]
    return tuple(outs)
```
